```python
import math
import jax, jax.numpy as jnp
from jax import lax
import numpy as np

D_MODEL = 1024
BATCH = 2
SEQ = 8192
DEPTH = 1

MIX_WIDTH = D_MODEL
MLA_WIDTH = MIX_WIDTH // 2
HYENA_WIDTH = MIX_WIDTH - MLA_WIDTH
MLA_HEADS = 8
MLA_V_DIM = MLA_WIDTH // MLA_HEADS
MLA_NOPE_DIM = 64
MLA_ROPE_DIM = 32
Q_LORA_RANK = D_MODEL // 4
KV_LORA_RANK = D_MODEL // 8
ROPE_THETA = 10000.0
Q_BLOCK = 128
HYENA_ORDER = 2
SHORT_CONV = 3
FILTER_EMB = 33
FILTER_ORDER = 64
DECAY_TARGET = 1e-2
FAST_DECAY_PCT = 0.3
SLOW_DECAY_PCT = 1.5
Q_A_COLS = Q_LORA_RANK
KV_A_COLS = KV_LORA_RANK + MLA_ROPE_DIM
HY_COLS = (HYENA_ORDER + 1) * HYENA_WIDTH
IN_COLS = Q_A_COLS + KV_A_COLS + HY_COLS
N_EXPERTS = 16
EC_FACTOR = 2
EXPERT_FF = D_MODEL
DN_ALPHA = (2.0 * DEPTH) ** 0.25
DN_BETA = (8.0 * DEPTH) ** -0.25
EPS = 1e-5

kernel_name = "hybrid_mla_hyena_ecmoe_deepnorm_adaln"


def layer_norm(x, g, b):
    xf = x.astype(jnp.float32)
    mu = jnp.mean(xf, axis=-1, keepdims=True)
    var = jnp.mean(jnp.square(xf - mu), axis=-1, keepdims=True)
    return ((xf - mu) * lax.rsqrt(var + EPS) * g + b).astype(x.dtype)


def rms_norm(x, g):
    xf = x.astype(jnp.float32)
    return (xf * lax.rsqrt(jnp.mean(xf * xf, axis=-1, keepdims=True) + EPS) * g).astype(x.dtype)


def apply_rope(x, cos, sin):
    half = MLA_ROPE_DIM // 2
    x1, x2 = x[..., :half], x[..., half:]
    return jnp.concatenate([x1 * cos - x2 * sin, x2 * cos + x1 * sin], axis=-1).astype(x.dtype)


def mla_group(cq_raw, ckv_raw, positions, q_norm_g, w_qb, kv_norm_g, w_kvb):
    B, S, _ = cq_raw.shape
    H = MLA_HEADS
    q = (rms_norm(cq_raw, q_norm_g) @ w_qb).reshape(B, S, H, MLA_NOPE_DIM + MLA_ROPE_DIM)
    q_nope, q_rope = q[..., :MLA_NOPE_DIM], q[..., MLA_NOPE_DIM:]
    c_kv, k_rope = ckv_raw[..., :KV_LORA_RANK], ckv_raw[..., KV_LORA_RANK:]
    kv = (rms_norm(c_kv, kv_norm_g) @ w_kvb).reshape(B, S, H, MLA_NOPE_DIM + MLA_V_DIM)
    k_nope, v = kv[..., :MLA_NOPE_DIM], kv[..., MLA_NOPE_DIM:]
    half = MLA_ROPE_DIM // 2
    inv_freq = ROPE_THETA ** (-jnp.arange(half, dtype=jnp.float32) / half)
    ang = positions.astype(jnp.float32)[..., None] * inv_freq
    cos, sin = jnp.cos(ang), jnp.sin(ang)
    q_rope = apply_rope(q_rope, cos[:, :, None, :], sin[:, :, None, :])
    k_rope = apply_rope(k_rope, cos, sin)
    scale = (MLA_NOPE_DIM + MLA_ROPE_DIM) ** -0.5
    nb = S // Q_BLOCK
    qn_b = q_nope.reshape(B, nb, Q_BLOCK, H, MLA_NOPE_DIM).transpose(1, 0, 2, 3, 4)
    qr_b = q_rope.reshape(B, nb, Q_BLOCK, H, MLA_ROPE_DIM).transpose(1, 0, 2, 3, 4)

    def attend(blk):
        qn, qr = blk
        s = jnp.einsum('bqhd,bkhd->bhqk', qn, k_nope) + jnp.einsum('bqhr,bkr->bhqk', qr, k_rope)
        p = jax.nn.softmax(s.astype(jnp.float32) * scale, axis=-1).astype(v.dtype)
        return jnp.einsum('bhqk,bkhd->bqhd', p, v)

    o = lax.map(attend, (qn_b, qr_b))
    return o.transpose(1, 0, 2, 3, 4).reshape(B, S, H * MLA_V_DIM)


def hyena_filter_spectrum(L, w1, b1, freq, w2, b2, w3, b3, decay):
    f32 = jnp.float32
    pos = jnp.arange(L, dtype=f32)
    t = pos[:, None] / (L - 1)
    bands = (FILTER_EMB - 1) // 2
    freqs = jnp.linspace(1e-4, bands - 1, bands, dtype=f32)
    phase = (2.0 * math.pi / L) * pos[:, None] * freqs[None, :]
    feats = jnp.concatenate([t, jnp.cos(phase), -jnp.sin(phase)], axis=-1)
    fr = freq.astype(f32)
    h = jnp.sin(fr * (feats @ w1.astype(f32) + b1.astype(f32)))
    h = jnp.sin(fr * (h @ w2.astype(f32) + b2.astype(f32)))
    h = (h @ w3.astype(f32) + b3.astype(f32)).reshape(L, 2, HYENA_ORDER, HYENA_WIDTH)
    h = h * jnp.exp(-t[:, :, None, None] * jnp.abs(decay.astype(f32)))
    h_fwd, h_bwd = h[:, 0], h[:, 1]
    two_sided = jnp.concatenate([h_fwd, jnp.zeros_like(h_fwd[:1]), h_bwd[:0:-1]], axis=0)
    return jnp.fft.rfft(two_sided, axis=0)


def hyena_group(hy_raw, conv_w, conv_b, filt_f, hyena_bias):
    B, S, _ = hy_raw.shape
    pad = SHORT_CONV // 2
    xp = jnp.pad(hy_raw, ((0, 0), (pad, pad), (0, 0)))
    u = conv_b + sum(xp[:, j:j + S] * conv_w[j] for j in range(SHORT_CONV))
    parts = jnp.split(u, HYENA_ORDER + 1, axis=-1)
    z = parts[0]
    for o in range(HYENA_ORDER):
        zf = z.astype(jnp.float32)
        y = jnp.fft.irfft(jnp.fft.rfft(zf, n=2 * S, axis=1) * filt_f[:, o], n=2 * S, axis=1)[:, :S]
        z = (parts[o + 1].astype(jnp.float32) * (y + zf * hyena_bias[o])).astype(hy_raw.dtype)
    return z


def expert_choice_moe(u, w_router, w_gate, w_up, w_down):
    B, S, D = u.shape
    cap = EC_FACTOR * S // N_EXPERTS
    aff = jax.nn.softmax((u @ w_router).astype(jnp.float32), axis=-1)
    g, idx = lax.top_k(aff.transpose(0, 2, 1), cap)
    xe = jax.vmap(lambda ub, ib: ub[ib])(u, idx)
    h = jax.nn.silu(jnp.einsum('becd,edf->becf', xe, w_gate)) * jnp.einsum('becd,edf->becf', xe, w_up)
    ye = jnp.einsum('becf,efd->becd', h, w_down) * g[..., None].astype(u.dtype)
    flat_idx = (idx + (jnp.arange(B, dtype=idx.dtype) * S)[:, None, None]).reshape(-1)
    out = jnp.zeros((B * S, D), u.dtype).at[flat_idx].add(ye.reshape(-1, D))
    return out.reshape(B, S, D)


def setup_inputs(seed: int = 0) -> dict:
    key = jax.random.key(seed)
    ks = iter(jax.random.split(key, 40))
    f32 = jnp.float32
    L_ = DEPTH

    def nrm(shape, std):
        return jax.random.normal(next(ks), shape, f32) * std

    def gain(shape):
        return 1.0 + nrm(shape, 0.02)

    x = jax.random.normal(next(ks), (BATCH, SEQ, D_MODEL), f32)
    c = jax.random.normal(next(ks), (BATCH, D_MODEL), f32)
    offs = jax.random.randint(next(ks), (BATCH, 1), 0, 4096, dtype=jnp.int32)
    positions = offs + jnp.arange(SEQ, dtype=jnp.int32)[None, :]
    min_decay = abs(math.log(DECAY_TARGET) / SLOW_DECAY_PCT)
    max_decay = abs(math.log(DECAY_TARGET) / FAST_DECAY_PCT)
    base_decay = jnp.linspace(min_decay, max_decay, HYENA_WIDTH, dtype=f32)
    hyena_decay = base_decay * jnp.exp(nrm((L_, 2, HYENA_ORDER, HYENA_WIDTH), 0.1))
    return {
        "x": x,
        "c": c,
        "positions": positions,
        "w_ada": nrm((L_, D_MODEL, 6 * D_MODEL), 0.5 * D_MODEL ** -0.5),
        "b_ada": nrm((L_, 6 * D_MODEL), 0.01),
        "w_in": nrm((L_, D_MODEL, IN_COLS), D_MODEL ** -0.5),
        "q_norm_g": gain((L_, Q_LORA_RANK)),
        "w_qb": nrm((L_, Q_LORA_RANK, MLA_HEADS * (MLA_NOPE_DIM + MLA_ROPE_DIM)), Q_LORA_RANK ** -0.5),
        "kv_norm_g": gain((L_, KV_LORA_RANK)),
        "w_kvb": nrm((L_, KV_LORA_RANK, MLA_HEADS * (MLA_NOPE_DIM + MLA_V_DIM)), KV_LORA_RANK ** -0.5),
        "conv_w": nrm((L_, SHORT_CONV, HY_COLS), SHORT_CONV ** -0.5),
        "conv_b": nrm((L_, HY_COLS), 0.01),
        "filt_w1": nrm((L_, FILTER_EMB, FILTER_ORDER), FILTER_EMB ** -0.5),
        "filt_b1": nrm((L_, FILTER_ORDER), 0.02),
        "filt_freq": gain((L_, FILTER_ORDER)),
        "filt_w2": nrm((L_, FILTER_ORDER, FILTER_ORDER), FILTER_ORDER ** -0.5),
        "filt_b2": nrm((L_, FILTER_ORDER), 0.02),
        "filt_w3": nrm((L_, FILTER_ORDER, 2 * HYENA_ORDER * HYENA_WIDTH), 0.1 * FILTER_ORDER ** -0.5),
        "filt_b3": nrm((L_, 2 * HYENA_ORDER * HYENA_WIDTH), 0.01),
        "hyena_decay": hyena_decay,
        "hyena_bias": nrm((L_, HYENA_ORDER, HYENA_WIDTH), 1.0),
        "attn_out_g": gain((L_, MLA_WIDTH)),
        "hyena_out_g": gain((L_, HYENA_WIDTH)),
        "w_o": nrm((L_, MIX_WIDTH, D_MODEL), DN_BETA * MIX_WIDTH ** -0.5),
        "ln1_g": gain((L_, D_MODEL)),
        "ln1_b": nrm((L_, D_MODEL), 0.01),
        "w_router": nrm((L_, D_MODEL, N_EXPERTS), D_MODEL ** -0.5),
        "w_gate": nrm((L_, N_EXPERTS, D_MODEL, EXPERT_FF), D_MODEL ** -0.5),
        "w_up": nrm((L_, N_EXPERTS, D_MODEL, EXPERT_FF), D_MODEL ** -0.5),
        "w_down": nrm((L_, N_EXPERTS, EXPERT_FF, D_MODEL), DN_BETA * EXPERT_FF ** -0.5),
        "ln2_g": gain((L_, D_MODEL)),
        "ln2_b": nrm((L_, D_MODEL), 0.01),
    }


def reference(x, c, positions, w_ada, b_ada, w_in, q_norm_g, w_qb, kv_norm_g, w_kvb,
              conv_w, conv_b, filt_w1, filt_b1, filt_freq, filt_w2, filt_b2, filt_w3, filt_b3,
              hyena_decay, hyena_bias, attn_out_g, hyena_out_g, w_o, ln1_g, ln1_b,
              w_router, w_gate, w_up, w_down, ln2_g, ln2_b):
    S = x.shape[1]
    for l in range(DEPTH):
        mod = jax.nn.silu(c) @ w_ada[l] + b_ada[l]
        sh1, sc1, g1, sh2, sc2, g2 = jnp.split(mod[:, None, :], 6, axis=-1)
        u = x * (1.0 + sc1) + sh1
        proj = u @ w_in[l]
        cq_raw = proj[..., :Q_A_COLS]
        ckv_raw = proj[..., Q_A_COLS:Q_A_COLS + KV_A_COLS]
        hy_raw = proj[..., Q_A_COLS + KV_A_COLS:]
        a = mla_group(cq_raw, ckv_raw, positions, q_norm_g[l], w_qb[l], kv_norm_g[l], w_kvb[l])
        filt_f = hyena_filter_spectrum(S, filt_w1[l], filt_b1[l], filt_freq[l], filt_w2[l], filt_b2[l],
                                       filt_w3[l], filt_b3[l], hyena_decay[l])
        hy = hyena_group(hy_raw, conv_w[l], conv_b[l], filt_f, hyena_bias[l])
        mixed = jnp.concatenate([rms_norm(a, attn_out_g[l]), rms_norm(hy, hyena_out_g[l])], axis=-1) @ w_o[l]
        x = layer_norm(DN_ALPHA * x + g1 * mixed, ln1_g[l], ln1_b[l])
        u = x * (1.0 + sc2) + sh2
        ffn = expert_choice_moe(u, w_router[l], w_gate[l], w_up[l], w_down[l])
        x = layer_norm(DN_ALPHA * x + g2 * ffn, ln2_g[l], ln2_b[l])
    return x
```

```python
import functools
import math

import jax
import jax.numpy as jnp
from jax import lax
from jax.experimental import pallas as pl
from jax.experimental.pallas import tpu as pltpu

F32 = jnp.float32
BF16 = jnp.bfloat16
HIGHEST = lax.Precision.HIGHEST

LANES = 128
SUBLANES = 8
VMEM_LIMIT = 56 * 1024 * 1024

D_MODEL = 1024
MLA_HEADS = 8
NOPE = 64
ROPE = 32
V_DIM = 64
Q_RANK = 256
KV_RANK = 128
HY_W = 512
HY_COLS = 3 * HY_W
N_EXPERTS = 16
EC_FACTOR = 2
ROPE_THETA = 10000.0
EPS = 1e-5
DN_ALPHA = 2.0 ** 0.25
LOG2E = 1.4426950408889634

TOK_TILE = 512
ATT_TQ = 512
ATT_TK = 512
MOE_ROWS = 512


def _params(*sem):
    return pltpu.CompilerParams(dimension_semantics=sem, vmem_limit_bytes=VMEM_LIMIT)


def _ada_kernel(c_ref, w_ref, b_ref, o_ref):
    c = c_ref[...]
    sc = c * jax.nn.sigmoid(c)
    o_ref[...] = jnp.dot(sc, w_ref[...], precision=HIGHEST, preferred_element_type=F32) + b_ref[...]


def _ada_mod(c8, w_ada, b_ada):
    n = w_ada.shape[1]
    tn = 1024
    return pl.pallas_call(
        _ada_kernel,
        grid=(n // tn,),
        in_specs=[
            pl.BlockSpec((SUBLANES, D_MODEL), lambda j: (0, 0)),
            pl.BlockSpec((D_MODEL, tn), lambda j: (0, j)),
            pl.BlockSpec((1, tn), lambda j: (0, j)),
        ],
        out_specs=pl.BlockSpec((SUBLANES, tn), lambda j: (0, j)),
        out_shape=jax.ShapeDtypeStruct((SUBLANES, n), F32),
        compiler_params=_params("arbitrary"),
        name="ada_mod",
    )(c8, w_ada, b_ada)


def _rms(x, g):
    return x * lax.rsqrt(jnp.mean(x * x, axis=-1, keepdims=True) + EPS) * g


def _inproj_kernel(x_ref, sc_ref, sh_ref, pos_ref, freq_ref, win_ref, qg_ref, wq_ref, wqr_ref,
                   kvg_ref, wk_ref, wv_ref, q_ref, kt_ref, v_ref, hy_ref):
    u = (x_ref[0] * (1.0 + sc_ref[0]) + sh_ref[0]).astype(BF16)
    ang = pos_ref[0].astype(F32) * freq_ref[...]
    cos = jnp.cos(ang)
    sin = jnp.sin(ang)

    def proj(lo, hi):
        return jnp.dot(u, win_ref[:, lo:hi], preferred_element_type=F32)

    cq = _rms(proj(0, Q_RANK), qg_ref[...]).astype(BF16)
    q_main = jnp.dot(cq, wq_ref[...], preferred_element_type=F32)
    q_rot = jnp.dot(cq, wqr_ref[...], preferred_element_type=F32)
    qscale = LOG2E * (NOPE + ROPE) ** -0.5
    cq_s = cos * qscale
    sq_s = sin * qscale
    for h in range(MLA_HEADS):
        sl = slice(h * LANES, (h + 1) * LANES)
        q_ref[0, :, sl] = (q_main[:, sl] * cq_s + q_rot[:, sl] * sq_s).astype(BF16)

    o = Q_RANK
    ckv = _rms(proj(o, o + KV_RANK), kvg_ref[...]).astype(BF16)
    kr = proj(o + KV_RANK, o + 2 * KV_RANK) * cos + proj(o + 2 * KV_RANK, o + 3 * KV_RANK) * sin
    kn = jnp.dot(ckv, wk_ref[...], preferred_element_type=F32)
    for h in range(MLA_HEADS):
        kh = kn[:, h * LANES:(h + 1) * LANES] + kr
        kt_ref[0, h] = kh.T.astype(BF16)
    v_ref[0] = jnp.dot(ckv, wv_ref[...], preferred_element_type=F32).astype(BF16)

    o = Q_RANK + 3 * KV_RANK
    hy_ref[0] = proj(o, o + HY_COLS)


def _inproj(x, sc1, sh1, pos3, freq_row, win_aug, qg, wq, wqr, kvg, wk, wv):
    B, S, D = x.shape
    tm = TOK_TILE
    ncol = win_aug.shape[1]
    full = lambda shape: pl.BlockSpec(shape, lambda b, i: (0,) * len(shape))
    return pl.pallas_call(
        _inproj_kernel,
        grid=(B, S // tm),
        in_specs=[
            pl.BlockSpec((1, tm, D), lambda b, i: (b, i, 0)),
            pl.BlockSpec((1, 1, D), lambda b, i: (b, 0, 0)),
            pl.BlockSpec((1, 1, D), lambda b, i: (b, 0, 0)),
            pl.BlockSpec((1, tm, 1), lambda b, i: (b, i, 0)),
            full((1, LANES)),
            full((D, ncol)),
            full((1, Q_RANK)),
            full((Q_RANK, MLA_HEADS * LANES)),
            full((Q_RANK, MLA_HEADS * LANES)),
            full((1, KV_RANK)),
            full((KV_RANK, MLA_HEADS * LANES)),
            full((KV_RANK, MLA_HEADS * V_DIM)),
        ],
        out_specs=[
            pl.BlockSpec((1, tm, MLA_HEADS * LANES), lambda b, i: (b, i, 0)),
            pl.BlockSpec((1, MLA_HEADS, LANES, tm), lambda b, i: (b, 0, 0, i)),
            pl.BlockSpec((1, tm, MLA_HEADS * V_DIM), lambda b, i: (b, i, 0)),
            pl.BlockSpec((1, tm, HY_COLS), lambda b, i: (b, i, 0)),
        ],
        out_shape=[
            jax.ShapeDtypeStruct((B, S, MLA_HEADS * LANES), BF16),
            jax.ShapeDtypeStruct((B, MLA_HEADS, LANES, S), BF16),
            jax.ShapeDtypeStruct((B, S, MLA_HEADS * V_DIM), BF16),
            jax.ShapeDtypeStruct((B, S, HY_COLS), F32),
        ],
        compiler_params=_params("parallel", "parallel"),
        name="in_proj",
    )(x, sc1, sh1, pos3, freq_row, win_aug, qg, wq, wqr, kvg, wk, wv)


def _attn_kernel(q_ref, kt_ref, v_ref, o_ref, m_scr, l_scr, acc_scr):
    S = kt_ref.shape[3]
    nk = S // ATT_TK
    nj = ATT_TK // LANES
    m_scr[...] = jnp.full(m_scr.shape, -jnp.inf, F32)
    l_scr[...] = jnp.zeros(l_scr.shape, F32)
    acc_scr[...] = jnp.zeros(acc_scr.shape, F32)

    def step(c, carry):
        k0 = pl.multiple_of(c * ATT_TK, ATT_TK)
        vblk = v_ref[0, pl.ds(k0, ATT_TK), :]
        for h in range(2):
            q = q_ref[0, :, h * LANES:(h + 1) * LANES]
            s = jnp.dot(q, kt_ref[0, h, :, pl.ds(k0, ATT_TK)], preferred_element_type=F32)
            cols = [s[:, j * LANES:(j + 1) * LANES] for j in range(nj)]
            mx = cols[0]
            for j in range(1, nj):
                mx = jnp.maximum(mx, cols[j])
            m_prev = m_scr[h]
            m_new = jnp.maximum(m_prev, jnp.max(mx, axis=1, keepdims=True))
            alpha = jnp.exp2(m_prev - m_new)
            ps = [jnp.exp2(cj - m_new) for cj in cols]
            psum = ps[0]
            for j in range(1, nj):
                psum = psum + ps[j]
            l_scr[h] = alpha * l_scr[h] + psum
            p = jnp.concatenate([pj.astype(BF16) for pj in ps], axis=1)
            acc_scr[h] = alpha * acc_scr[h] + jnp.dot(p, vblk, preferred_element_type=F32)
            m_scr[h] = m_new
        return carry

    lax.fori_loop(0, nk, step, 0)
    o0 = acc_scr[0] / jnp.sum(l_scr[0], axis=1, keepdims=True)
    o1 = acc_scr[1] / jnp.sum(l_scr[1], axis=1, keepdims=True)
    lane = lax.broadcasted_iota(jnp.int32, o0.shape, 1)
    o_ref[0] = jnp.where(lane < V_DIM, o0, o1)


def _attention(q, kt, v):
    B, S, _ = q.shape
    tq = ATT_TQ
    return pl.pallas_call(
        _attn_kernel,
        grid=(B, MLA_HEADS // 2, S // tq),
        in_specs=[
            pl.BlockSpec((1, tq, 2 * LANES), lambda b, hp, i: (b, i, hp)),
            pl.BlockSpec((1, 2, LANES, S), lambda b, hp, i: (b, hp, 0, 0)),
            pl.BlockSpec((1, S, LANES), lambda b, hp, i: (b, 0, hp)),
        ],
        out_specs=pl.BlockSpec((1, tq, LANES), lambda b, hp, i: (b, i, hp)),
        out_shape=jax.ShapeDtypeStruct((B, S, MLA_HEADS * V_DIM), F32),
        scratch_shapes=[
            pltpu.VMEM((2, tq, LANES), F32),
            pltpu.VMEM((2, tq, LANES), F32),
            pltpu.VMEM((2, tq, LANES), F32),
        ],
        compiler_params=_params("parallel", "parallel", "parallel"),
        name="mla_attention",
    )(q, kt, v)


def _layer_norm(x, g, b):
    mu = jnp.mean(x, axis=-1, keepdims=True)
    xc = x - mu
    var = jnp.mean(xc * xc, axis=-1, keepdims=True)
    return xc * lax.rsqrt(var + EPS) * g + b


def _outproj_kernel(a_ref, hy_ref, x_ref, g1_ref, sc2_ref, sh2_ref, ag_ref, hg_ref, wo_ref,
                    lg_ref, lb_ref, wr_ref, x1_ref, u2_ref, aff_ref):
    half = a_ref.shape[2]
    an = _rms(a_ref[0], ag_ref[...]).astype(BF16)
    hn = _rms(hy_ref[0], hg_ref[...]).astype(BF16)
    mixed = (jnp.dot(an, wo_ref[:half, :], preferred_element_type=F32)
             + jnp.dot(hn, wo_ref[half:, :], preferred_element_type=F32))
    x1 = _layer_norm(DN_ALPHA * x_ref[0] + g1_ref[0] * mixed, lg_ref[...], lb_ref[...])
    x1_ref[0] = x1
    u2 = x1 * (1.0 + sc2_ref[0]) + sh2_ref[0]
    u2_ref[0] = u2.astype(BF16)
    logits = jnp.dot(u2, wr_ref[...], precision=HIGHEST, preferred_element_type=F32)
    e = jnp.exp(logits - jnp.max(logits, axis=-1, keepdims=True))
    aff_ref[0] = e / jnp.sum(e, axis=-1, keepdims=True)


def _outproj(a, hy, x, g1, sc2, sh2, ag, hg, wo, lg, lb, wr):
    B, S, D = x.shape
    tm = TOK_TILE
    half = a.shape[2]
    E = wr.shape[1]
    full = lambda shape: pl.BlockSpec(shape, lambda b, i: (0,) * len(shape))
    tok = lambda w: pl.BlockSpec((1, tm, w), lambda b, i: (b, i, 0))
    per_b = pl.BlockSpec((1, 1, D), lambda b, i: (b, 0, 0))
    return pl.pallas_call(
        _outproj_kernel,
        grid=(B, S // tm),
        in_specs=[tok(half), tok(half), tok(D), per_b, per_b, per_b,
                  full((1, half)), full((1, half)), full((D, D)), full((1, D)), full((1, D)),
                  full((D, E))],
        out_specs=[tok(D), tok(D), tok(E)],
        out_shape=[
            jax.ShapeDtypeStruct((B, S, D), F32),
            jax.ShapeDtypeStruct((B, S, D), BF16),
            jax.ShapeDtypeStruct((B, S, E), F32),
        ],
        compiler_params=_params("parallel", "parallel"),
        name="out_proj_ln1_router",
    )(a, hy, x, g1, sc2, sh2, ag, hg, wo, lg, lb, wr)


def _moe_kernel(x_ref, g_ref, wg_ref, wu_ref, wd_ref, o_ref, wgb, wub, wdb):
    @pl.when(pl.program_id(1) == 0)
    def _():
        wgb[...] = wg_ref[0].astype(BF16)
        wub[...] = wu_ref[0].astype(BF16)
        wdb[...] = wd_ref[0].astype(BF16)

    x = x_ref[0]
    hg = jnp.dot(x, wgb[...], preferred_element_type=F32)
    hu = jnp.dot(x, wub[...], preferred_element_type=F32)
    h = (hg * jax.nn.sigmoid(hg) * hu).astype(BF16)
    o_ref[0] = jnp.dot(h, wdb[...], preferred_element_type=F32) * g_ref[0]


def _moe_ffn(xe, ge, w_gate, w_up, w_down):
    E, R, D = xe.shape
    Fh = w_gate.shape[2]
    tr = MOE_ROWS
    return pl.pallas_call(
        _moe_kernel,
        grid=(E, R // tr),
        in_specs=[
            pl.BlockSpec((1, tr, D), lambda e, r: (e, r, 0)),
            pl.BlockSpec((1, tr, 1), lambda e, r: (e, r, 0)),
            pl.BlockSpec((1, D, Fh), lambda e, r: (e, 0, 0)),
            pl.BlockSpec((1, D, Fh), lambda e, r: (e, 0, 0)),
            pl.BlockSpec((1, Fh, D), lambda e, r: (e, 0, 0)),
        ],
        out_specs=pl.BlockSpec((1, tr, D), lambda e, r: (e, r, 0)),
        out_shape=jax.ShapeDtypeStruct((E, R, D), F32),
        scratch_shapes=[pltpu.VMEM((D, Fh), BF16), pltpu.VMEM((D, Fh), BF16), pltpu.VMEM((Fh, D), BF16)],
        compiler_params=_params("parallel", "arbitrary"),
        name="moe_swiglu",
    )(xe, ge, w_gate, w_up, w_down)


def _ln2_kernel(x_ref, f_ref, g2_ref, lg_ref, lb_ref, o_ref):
    o_ref[0] = _layer_norm(DN_ALPHA * x_ref[0] + g2_ref[0] * f_ref[0], lg_ref[...], lb_ref[...])


def _ln2(x1, ffn, g2, lg, lb):
    B, S, D = x1.shape
    tm = TOK_TILE
    tok = pl.BlockSpec((1, tm, D), lambda b, i: (b, i, 0))
    vec = pl.BlockSpec((1, D), lambda b, i: (0, 0))
    return pl.pallas_call(
        _ln2_kernel,
        grid=(B, S // tm),
        in_specs=[tok, tok, pl.BlockSpec((1, 1, D), lambda b, i: (b, 0, 0)), vec, vec],
        out_specs=tok,
        out_shape=jax.ShapeDtypeStruct((B, S, D), F32),
        compiler_params=_params("parallel", "parallel"),
        name="residual_ln2",
    )(x1, ffn, g2, lg, lb)


def _hyena_filter_spectrum(L, w1, b1, freq, w2, b2, w3, b3, decay):
    pos = jnp.arange(L, dtype=F32)
    t = pos[:, None] / (L - 1)
    bands = (w1.shape[0] - 1) // 2
    freqs = jnp.linspace(1e-4, bands - 1, bands, dtype=F32)
    phase = (2.0 * math.pi / L) * pos[:, None] * freqs[None, :]
    feats = jnp.concatenate([t, jnp.cos(phase), -jnp.sin(phase)], axis=-1)
    h = jnp.sin(freq * (jnp.dot(feats, w1, precision=HIGHEST) + b1))
    h = jnp.sin(freq * (jnp.dot(h, w2, precision=HIGHEST) + b2))
    h = (jnp.dot(h, w3, precision=HIGHEST) + b3).reshape(L, 2, 2, HY_W)
    h = h * jnp.exp(-t[:, :, None, None] * jnp.abs(decay))
    h_fwd, h_bwd = h[:, 0], h[:, 1]
    two_sided = jnp.concatenate([h_fwd, jnp.zeros_like(h_fwd[:1]), h_bwd[:0:-1]], axis=0)
    return jnp.fft.rfft(two_sided, axis=0)


def _hyena_group(hy_raw, conv_w, conv_b, filt_f, hyena_bias):
    B, S, _ = hy_raw.shape
    xp = jnp.pad(hy_raw, ((0, 0), (1, 1), (0, 0)))
    u = conv_b + sum(xp[:, j:j + S] * conv_w[j] for j in range(3))
    parts = jnp.split(u, 3, axis=-1)
    z = parts[0]
    for o in range(2):
        y = jnp.fft.irfft(jnp.fft.rfft(z, n=2 * S, axis=1) * filt_f[:, o], n=2 * S, axis=1)[:, :S]
        z = parts[o + 1] * (y + z * hyena_bias[o])
    return z


def _rope_rot_cols(w):
    half = ROPE // 2
    return jnp.concatenate([-w[..., half:], w[..., :half]], axis=-1)


def _prep_weights(w_in, w_qb, w_kvb):
    D = w_in.shape[0]
    o = Q_RANK + KV_RANK
    w_kr = w_in[:, o:o + ROPE]
    z64 = jnp.zeros((D, NOPE), F32)
    z32 = jnp.zeros((D, LANES - NOPE - ROPE), F32)
    win_aug = jnp.concatenate([
        w_in[:, :o],
        z64, w_kr, z32,
        z64, _rope_rot_cols(w_kr), z32,
        w_in[:, o + ROPE:],
    ], axis=1).astype(BF16)
    wq = w_qb.reshape(Q_RANK, MLA_HEADS, NOPE + ROPE)
    zq = jnp.zeros((Q_RANK, MLA_HEADS, LANES - NOPE - ROPE), F32)
    wq_main = jnp.concatenate([wq, zq], axis=-1).reshape(Q_RANK, MLA_HEADS * LANES).astype(BF16)
    wq_rot = jnp.concatenate([jnp.zeros((Q_RANK, MLA_HEADS, NOPE), F32), _rope_rot_cols(wq[..., NOPE:]), zq],
                             axis=-1).reshape(Q_RANK, MLA_HEADS * LANES).astype(BF16)
    wkv = w_kvb.reshape(KV_RANK, MLA_HEADS, NOPE + V_DIM)
    wk = jnp.concatenate([wkv[..., :NOPE], jnp.zeros((KV_RANK, MLA_HEADS, LANES - NOPE), F32)],
                         axis=-1).reshape(KV_RANK, MLA_HEADS * LANES).astype(BF16)
    wv = wkv[..., NOPE:].reshape(KV_RANK, MLA_HEADS * V_DIM).astype(BF16)
    return win_aug, wq_main, wq_rot, wk, wv


def kernel(x, c, positions, w_ada, b_ada, w_in, q_norm_g, w_qb, kv_norm_g, w_kvb, conv_w, conv_b, filt_w1, filt_b1, filt_freq, filt_w2, filt_b2, filt_w3, filt_b3, hyena_decay, hyena_bias, attn_out_g, hyena_out_g, w_o, ln1_g, ln1_b, w_router, w_gate, w_up, w_down, ln2_g, ln2_b):
    B, S, D = x.shape
    l = 0
    c8 = jnp.zeros((SUBLANES, D), F32).at[:B].set(c)
    mod = _ada_mod(c8, w_ada[l], b_ada[l][None, :])[:B]
    sh1, sc1, g1, sh2, sc2, g2 = [m[:, None, :] for m in jnp.split(mod, 6, axis=-1)]

    win_aug, wq_main, wq_rot, wk, wv = _prep_weights(w_in[l], w_qb[l], w_kvb[l])
    half = ROPE // 2
    inv_freq = ROPE_THETA ** (-jnp.arange(half, dtype=F32) / half)
    freq_row = jnp.concatenate([jnp.zeros((NOPE,), F32), inv_freq, inv_freq,
                                jnp.zeros((LANES - NOPE - ROPE,), F32)])[None, :]
    q, kt, v, hy_raw = _inproj(x, sc1, sh1, positions[:, :, None], freq_row, win_aug,
                               q_norm_g[l][None, :], wq_main, wq_rot, kv_norm_g[l][None, :], wk, wv)
    a = _attention(q, kt, v)

    filt_f = _hyena_filter_spectrum(S, filt_w1[l], filt_b1[l], filt_freq[l], filt_w2[l], filt_b2[l],
                                    filt_w3[l], filt_b3[l], hyena_decay[l])
    hy = _hyena_group(hy_raw, conv_w[l], conv_b[l], filt_f, hyena_bias[l])

    x1, u2, aff = _outproj(a, hy, x, g1, sc2, sh2, attn_out_g[l][None, :], hyena_out_g[l][None, :],
                           w_o[l].astype(BF16), ln1_g[l][None, :], ln1_b[l][None, :], w_router[l])

    cap = EC_FACTOR * S // N_EXPERTS
    gsel, idx = lax.top_k(aff.transpose(0, 2, 1), cap)
    xe = jax.vmap(lambda ub, ib: ub[ib])(u2, idx)
    xe = xe.transpose(1, 0, 2, 3).reshape(N_EXPERTS, B * cap, D)
    ge = gsel.transpose(1, 0, 2).reshape(N_EXPERTS, B * cap, 1)
    ye = _moe_ffn(xe, ge, w_gate[l], w_up[l], w_down[l])
    flat_idx = (idx + (jnp.arange(B, dtype=idx.dtype) * S)[:, None, None]).transpose(1, 0, 2).reshape(-1)
    ffn = jnp.zeros((B * S, D), F32).at[flat_idx].add(ye.reshape(-1, D)).reshape(B, S, D)

    return _ln2(x1, ffn, g2, ln2_g[l][None, :], ln2_b[l][None, :])
```

```python
import functools
import math

import jax
import jax.numpy as jnp
from jax import lax
from jax.experimental import pallas as pl
from jax.experimental.pallas import tpu as pltpu

F32 = jnp.float32
BF16 = jnp.bfloat16
HIGHEST = lax.Precision.HIGHEST

LANES = 128
SUBLANES = 8
VMEM_LIMIT = 56 * 1024 * 1024

D_MODEL = 1024
MLA_HEADS = 8
NOPE = 64
ROPE = 32
V_DIM = 64
Q_RANK = 256
KV_RANK = 128
HY_W = 512
HY_COLS = 3 * HY_W
N_EXPERTS = 16
EC_FACTOR = 2
ROPE_THETA = 10000.0
EPS = 1e-5
DN_ALPHA = 2.0 ** 0.25
LOG2E = 1.4426950408889634

TOK_TILE = 512
ATT_TQ = 512
ATT_TK = 512
MOE_ROWS = 512


def _params(*sem):
    return pltpu.CompilerParams(dimension_semantics=sem, vmem_limit_bytes=VMEM_LIMIT)


def _ada_kernel(c_ref, w_ref, b_ref, o_ref):
    c = c_ref[...]
    sc = c * jax.nn.sigmoid(c)
    o_ref[...] = jnp.dot(sc, w_ref[...], precision=HIGHEST, preferred_element_type=F32) + b_ref[...]


def _ada_mod(c8, w_ada, b_ada):
    n = w_ada.shape[1]
    tn = 1024
    return pl.pallas_call(
        _ada_kernel,
        grid=(n // tn,),
        in_specs=[
            pl.BlockSpec((SUBLANES, D_MODEL), lambda j: (0, 0)),
            pl.BlockSpec((D_MODEL, tn), lambda j: (0, j)),
            pl.BlockSpec((1, tn), lambda j: (0, j)),
        ],
        out_specs=pl.BlockSpec((SUBLANES, tn), lambda j: (0, j)),
        out_shape=jax.ShapeDtypeStruct((SUBLANES, n), F32),
        compiler_params=_params("arbitrary"),
        name="ada_mod",
    )(c8, w_ada, b_ada)


def _rms(x, g):
    return x * lax.rsqrt(jnp.mean(x * x, axis=-1, keepdims=True) + EPS) * g


def _inproj_kernel(x_ref, sc_ref, sh_ref, pos_ref, freq_ref, win_ref, qg_ref, wq_ref, wqr_ref,
                   kvg_ref, wk_ref, wv_ref, q_ref, kt_ref, v_ref, hy_ref):
    u = (x_ref[0] * (1.0 + sc_ref[0]) + sh_ref[0]).astype(BF16)
    ang = pos_ref[0].astype(F32) * freq_ref[...]
    cos = jnp.cos(ang)
    sin = jnp.sin(ang)

    def proj(lo, hi):
        return jnp.dot(u, win_ref[:, lo:hi], preferred_element_type=F32)

    cq = _rms(proj(0, Q_RANK), qg_ref[...]).astype(BF16)
    q_main = jnp.dot(cq, wq_ref[...], preferred_element_type=F32)
    q_rot = jnp.dot(cq, wqr_ref[...], preferred_element_type=F32)
    qscale = LOG2E * (NOPE + ROPE) ** -0.5
    cq_s = cos * qscale
    sq_s = sin * qscale
    for h in range(MLA_HEADS):
        sl = slice(h * LANES, (h + 1) * LANES)
        q_ref[0, :, sl] = (q_main[:, sl] * cq_s + q_rot[:, sl] * sq_s).astype(BF16)

    o = Q_RANK
    ckv = _rms(proj(o, o + KV_RANK), kvg_ref[...]).astype(BF16)
    kr = proj(o + KV_RANK, o + 2 * KV_RANK) * cos + proj(o + 2 * KV_RANK, o + 3 * KV_RANK) * sin
    kn = jnp.dot(ckv, wk_ref[...], preferred_element_type=F32)
    for h in range(MLA_HEADS):
        kh = kn[:, h * LANES:(h + 1) * LANES] + kr
        kt_ref[0, h] = kh.T.astype(BF16)
    v_ref[0] = jnp.dot(ckv, wv_ref[...], preferred_element_type=F32).astype(BF16)

    o = Q_RANK + 3 * KV_RANK
    hy_ref[0] = proj(o, o + HY_COLS)


def _inproj(x, sc1, sh1, pos3, freq_row, win_aug, qg, wq, wqr, kvg, wk, wv):
    B, S, D = x.shape
    tm = TOK_TILE
    ncol = win_aug.shape[1]
    full = lambda shape: pl.BlockSpec(shape, lambda b, i: (0,) * len(shape))
    return pl.pallas_call(
        _inproj_kernel,
        grid=(B, S // tm),
        in_specs=[
            pl.BlockSpec((1, tm, D), lambda b, i: (b, i, 0)),
            pl.BlockSpec((1, 1, D), lambda b, i: (b, 0, 0)),
            pl.BlockSpec((1, 1, D), lambda b, i: (b, 0, 0)),
            pl.BlockSpec((1, tm, 1), lambda b, i: (b, i, 0)),
            full((1, LANES)),
            full((D, ncol)),
            full((1, Q_RANK)),
            full((Q_RANK, MLA_HEADS * LANES)),
            full((Q_RANK, MLA_HEADS * LANES)),
            full((1, KV_RANK)),
            full((KV_RANK, MLA_HEADS * LANES)),
            full((KV_RANK, MLA_HEADS * V_DIM)),
        ],
        out_specs=[
            pl.BlockSpec((1, tm, MLA_HEADS * LANES), lambda b, i: (b, i, 0)),
            pl.BlockSpec((1, MLA_HEADS, LANES, tm), lambda b, i: (b, 0, 0, i)),
            pl.BlockSpec((1, tm, MLA_HEADS * V_DIM), lambda b, i: (b, i, 0)),
            pl.BlockSpec((1, tm, HY_COLS), lambda b, i: (b, i, 0)),
        ],
        out_shape=[
            jax.ShapeDtypeStruct((B, S, MLA_HEADS * LANES), BF16),
            jax.ShapeDtypeStruct((B, MLA_HEADS, LANES, S), BF16),
            jax.ShapeDtypeStruct((B, S, MLA_HEADS * V_DIM), BF16),
            jax.ShapeDtypeStruct((B, S, HY_COLS), F32),
        ],
        compiler_params=_params("parallel", "parallel"),
        name="in_proj",
    )(x, sc1, sh1, pos3, freq_row, win_aug, qg, wq, wqr, kvg, wk, wv)


def _attn_kernel(q_ref, kt_ref, v_ref, o_ref, m_scr, l_scr, acc_scr):
    S = kt_ref.shape[3]
    nk = S // ATT_TK
    nj = ATT_TK // LANES
    m_scr[...] = jnp.full(m_scr.shape, -jnp.inf, F32)
    l_scr[...] = jnp.zeros(l_scr.shape, F32)
    acc_scr[...] = jnp.zeros(acc_scr.shape, F32)

    def step(c, carry):
        k0 = pl.multiple_of(c * ATT_TK, ATT_TK)
        vblk = v_ref[0, pl.ds(k0, ATT_TK), :]
        for h in range(2):
            q = q_ref[0, :, h * LANES:(h + 1) * LANES]
            s = jnp.dot(q, kt_ref[0, h, :, pl.ds(k0, ATT_TK)], preferred_element_type=F32)
            cols = [s[:, j * LANES:(j + 1) * LANES] for j in range(nj)]
            mx = cols[0]
            for j in range(1, nj):
                mx = jnp.maximum(mx, cols[j])
            m_prev = m_scr[h]
            m_new = jnp.maximum(m_prev, jnp.max(mx, axis=1, keepdims=True))
            alpha = jnp.exp2(m_prev - m_new)
            ps = [jnp.exp2(cj - m_new) for cj in cols]
            psum = ps[0]
            for j in range(1, nj):
                psum = psum + ps[j]
            l_scr[h] = alpha * l_scr[h] + psum
            p = jnp.concatenate([pj.astype(BF16) for pj in ps], axis=1)
            acc_scr[h] = alpha * acc_scr[h] + jnp.dot(p, vblk, preferred_element_type=F32)
            m_scr[h] = m_new
        return carry

    lax.fori_loop(0, nk, step, 0)
    o0 = acc_scr[0] / jnp.sum(l_scr[0], axis=1, keepdims=True)
    o1 = acc_scr[1] / jnp.sum(l_scr[1], axis=1, keepdims=True)
    lane = lax.broadcasted_iota(jnp.int32, o0.shape, 1)
    o_ref[0] = jnp.where(lane < V_DIM, o0, o1)


def _attention(q, kt, v):
    B, S, _ = q.shape
    tq = ATT_TQ
    return pl.pallas_call(
        _attn_kernel,
        grid=(B, MLA_HEADS // 2, S // tq),
        in_specs=[
            pl.BlockSpec((1, tq, 2 * LANES), lambda b, hp, i: (b, i, hp)),
            pl.BlockSpec((1, 2, LANES, S), lambda b, hp, i: (b, hp, 0, 0)),
            pl.BlockSpec((1, S, LANES), lambda b, hp, i: (b, 0, hp)),
        ],
        out_specs=pl.BlockSpec((1, tq, LANES), lambda b, hp, i: (b, i, hp)),
        out_shape=jax.ShapeDtypeStruct((B, S, MLA_HEADS * V_DIM), F32),
        scratch_shapes=[
            pltpu.VMEM((2, tq, LANES), F32),
            pltpu.VMEM((2, tq, LANES), F32),
            pltpu.VMEM((2, tq, LANES), F32),
        ],
        compiler_params=_params("parallel", "parallel", "parallel"),
        name="mla_attention",
    )(q, kt, v)


def _layer_norm(x, g, b):
    mu = jnp.mean(x, axis=-1, keepdims=True)
    xc = x - mu
    var = jnp.mean(xc * xc, axis=-1, keepdims=True)
    return xc * lax.rsqrt(var + EPS) * g + b


def _outproj_kernel(a_ref, hy_ref, x_ref, g1_ref, sc2_ref, sh2_ref, ag_ref, hg_ref, wo_ref,
                    lg_ref, lb_ref, wr_ref, x1_ref, u2_ref, aff_ref):
    half = a_ref.shape[2]
    an = _rms(a_ref[0], ag_ref[...]).astype(BF16)
    hn = _rms(hy_ref[0], hg_ref[...]).astype(BF16)
    mixed = (jnp.dot(an, wo_ref[:half, :], preferred_element_type=F32)
             + jnp.dot(hn, wo_ref[half:, :], preferred_element_type=F32))
    x1 = _layer_norm(DN_ALPHA * x_ref[0] + g1_ref[0] * mixed, lg_ref[...], lb_ref[...])
    x1_ref[0] = x1
    u2 = x1 * (1.0 + sc2_ref[0]) + sh2_ref[0]
    u2_ref[0] = u2.astype(BF16)
    logits = jnp.dot(u2, wr_ref[...], precision=HIGHEST, preferred_element_type=F32)
    e = jnp.exp(logits - jnp.max(logits, axis=-1, keepdims=True))
    aff_ref[0] = e / jnp.sum(e, axis=-1, keepdims=True)


def _outproj(a, hy, x, g1, sc2, sh2, ag, hg, wo, lg, lb, wr):
    B, S, D = x.shape
    tm = TOK_TILE
    half = a.shape[2]
    E = wr.shape[1]
    full = lambda shape: pl.BlockSpec(shape, lambda b, i: (0,) * len(shape))
    tok = lambda w: pl.BlockSpec((1, tm, w), lambda b, i: (b, i, 0))
    per_b = pl.BlockSpec((1, 1, D), lambda b, i: (b, 0, 0))
    return pl.pallas_call(
        _outproj_kernel,
        grid=(B, S // tm),
        in_specs=[tok(half), tok(half), tok(D), per_b, per_b, per_b,
                  full((1, half)), full((1, half)), full((D, D)), full((1, D)), full((1, D)),
                  full((D, E))],
        out_specs=[tok(D), tok(D), tok(E)],
        out_shape=[
            jax.ShapeDtypeStruct((B, S, D), F32),
            jax.ShapeDtypeStruct((B, S, D), BF16),
            jax.ShapeDtypeStruct((B, S, E), F32),
        ],
        compiler_params=_params("parallel", "parallel"),
        name="out_proj_ln1_router",
    )(a, hy, x, g1, sc2, sh2, ag, hg, wo, lg, lb, wr)


def _moe_kernel(x_ref, g_ref, wg_ref, wu_ref, wd_ref, o_ref, wgb, wub, wdb):
    @pl.when(pl.program_id(1) == 0)
    def _():
        wgb[...] = wg_ref[0].astype(BF16)
        wub[...] = wu_ref[0].astype(BF16)
        wdb[...] = wd_ref[0].astype(BF16)

    x = x_ref[0]
    hg = jnp.dot(x, wgb[...], preferred_element_type=F32)
    hu = jnp.dot(x, wub[...], preferred_element_type=F32)
    h = (hg * jax.nn.sigmoid(hg) * hu).astype(BF16)
    o_ref[0] = jnp.dot(h, wdb[...], preferred_element_type=F32) * g_ref[0]


def _moe_ffn(xe, ge, w_gate, w_up, w_down):
    E, R, D = xe.shape
    Fh = w_gate.shape[2]
    tr = MOE_ROWS
    return pl.pallas_call(
        _moe_kernel,
        grid=(E, R // tr),
        in_specs=[
            pl.BlockSpec((1, tr, D), lambda e, r: (e, r, 0)),
            pl.BlockSpec((1, tr, 1), lambda e, r: (e, r, 0)),
            pl.BlockSpec((1, D, Fh), lambda e, r: (e, 0, 0)),
            pl.BlockSpec((1, D, Fh), lambda e, r: (e, 0, 0)),
            pl.BlockSpec((1, Fh, D), lambda e, r: (e, 0, 0)),
        ],
        out_specs=pl.BlockSpec((1, tr, D), lambda e, r: (e, r, 0)),
        out_shape=jax.ShapeDtypeStruct((E, R, D), F32),
        scratch_shapes=[pltpu.VMEM((D, Fh), BF16), pltpu.VMEM((D, Fh), BF16), pltpu.VMEM((Fh, D), BF16)],
        compiler_params=_params("parallel", "arbitrary"),
        name="moe_swiglu",
    )(xe, ge, w_gate, w_up, w_down)


def _ln2_kernel(x_ref, f_ref, g2_ref, lg_ref, lb_ref, o_ref):
    o_ref[0] = _layer_norm(DN_ALPHA * x_ref[0] + g2_ref[0] * f_ref[0], lg_ref[...], lb_ref[...])


def _ln2(x1, ffn, g2, lg, lb):
    B, S, D = x1.shape
    tm = TOK_TILE
    tok = pl.BlockSpec((1, tm, D), lambda b, i: (b, i, 0))
    vec = pl.BlockSpec((1, D), lambda b, i: (0, 0))
    return pl.pallas_call(
        _ln2_kernel,
        grid=(B, S // tm),
        in_specs=[tok, tok, pl.BlockSpec((1, 1, D), lambda b, i: (b, 0, 0)), vec, vec],
        out_specs=tok,
        out_shape=jax.ShapeDtypeStruct((B, S, D), F32),
        compiler_params=_params("parallel", "parallel"),
        name="residual_ln2",
    )(x1, ffn, g2, lg, lb)


DFT_R = 128
K1_HALF = 40
K1_VALID = DFT_R // 2 + 1
K1_COUNTS = (K1_HALF, K1_VALID - K1_HALF)
HY_CT = 128


def _dft_tables(S):
    N = 2 * S
    na = S // DFT_R
    r = jnp.arange(2 * K1_HALF, dtype=jnp.int32)
    k1 = (jnp.arange(2, dtype=jnp.int32)[:, None] * K1_HALF + r[None, :] % K1_HALF)
    valid = (k1 < K1_VALID).astype(F32)
    n = DFT_R * jnp.arange(na, dtype=jnp.int32)[None, :] + jnp.arange(DFT_R, dtype=jnp.int32)[:, None]
    prod = (n[None, :, None, :] * k1[:, None, :, None]) % N
    theta = prod.astype(F32) * (2.0 * math.pi / N)
    is_im = (r >= K1_HALF)[None, None, :, None]
    g = jnp.where(is_im, -jnp.sin(theta), jnp.cos(theta)) * valid[:, None, :, None]
    cw = jnp.where((k1 == 0) | (k1 == DFT_R // 2), 1.0, 2.0) * valid / N
    gi = jnp.transpose(g * cw[:, None, :, None], (0, 1, 3, 2))
    bk = (jnp.arange(DFT_R, dtype=jnp.int32)[:, None] * jnp.arange(DFT_R, dtype=jnp.int32)[None, :]) % DFT_R
    ph = bk.astype(F32) * (2.0 * math.pi / DFT_R)
    C, Sn = jnp.cos(ph), jnp.sin(ph)
    f3f = jnp.block([[C, Sn], [-Sn, C]])
    f3i = jnp.block([[C, -Sn], [Sn, C]])
    return g.astype(BF16), gi.astype(BF16), f3f.astype(BF16), f3i.astype(BF16)


def _dft_stage1(load_rows, gf_ref, hf, a_scr):
    rows = 2 * K1_HALF

    def body(b, carry):
        zb = load_rows(b).astype(BF16)
        a_scr[pl.ds(pl.multiple_of(b * rows, SUBLANES), rows), :] = jnp.dot(
            gf_ref[hf, b], zb, preferred_element_type=F32)
        return carry

    lax.fori_loop(0, DFT_R, body, 0, unroll=8)


def _dft_stage3_operand(a_scr, j):
    rows = 2 * K1_HALF
    mr = a_scr[pl.ds(j, DFT_R, stride=rows), :]
    mi = a_scr[pl.ds(K1_HALF + j, DFT_R, stride=rows), :]
    return jnp.concatenate([mr, mi], axis=0).astype(BF16)


def _filter_mlp_kernel(feat_ref, w1_ref, b1_ref, fr_ref, w2_ref, b2_ref, w3_ref, b3_ref, dec_ref, o_ref):
    feats = feat_ref[...]
    fr = fr_ref[...]
    h = jnp.sin(fr * (jnp.dot(feats, w1_ref[...], precision=HIGHEST, preferred_element_type=F32) + b1_ref[...]))
    h = jnp.sin(fr * (jnp.dot(h, w2_ref[...], precision=HIGHEST, preferred_element_type=F32) + b2_ref[...]))
    h = jnp.dot(h, w3_ref[...], precision=HIGHEST, preferred_element_type=F32) + b3_ref[...]
    h = h * jnp.exp(-feats[:, 0:1] * jnp.abs(dec_ref[...]))
    row = lax.broadcasted_iota(jnp.int32, h.shape, 0) + pl.program_id(0) * h.shape[0]
    col = lax.broadcasted_iota(jnp.int32, h.shape, 1)
    o_ref[...] = jnp.where((row == 0) & (col >= h.shape[1] // 2), 0.0, h)


def _filter_taps(L, w1, b1, freq, w2, b2, w3, b3, decay):
    pos = jnp.arange(L, dtype=F32)
    t = pos[:, None] / (L - 1)
    bands = (w1.shape[0] - 1) // 2
    freqs = jnp.linspace(1e-4, bands - 1, bands, dtype=F32)
    phase = (2.0 * math.pi / L) * pos[:, None] * freqs[None, :]
    nf = w1.shape[0]
    feats = jnp.concatenate([t, jnp.cos(phase), -jnp.sin(phase), jnp.zeros((L, LANES - nf), F32)], axis=-1)
    w1p = jnp.concatenate([w1, jnp.zeros((LANES - nf, w1.shape[1]), F32)], axis=0)
    fo = w1.shape[1]
    ncol = w3.shape[1]
    tl = 512
    full = lambda shape: pl.BlockSpec(shape, lambda i: (0,) * len(shape))
    return pl.pallas_call(
        _filter_mlp_kernel,
        grid=(L // tl,),
        in_specs=[pl.BlockSpec((tl, LANES), lambda i: (i, 0)), full((LANES, fo)), full((1, fo)), full((1, fo)),
                  full((fo, fo)), full((1, fo)), full((fo, ncol)), full((1, ncol)), full((1, ncol))],
        out_specs=pl.BlockSpec((tl, ncol), lambda i: (i, 0)),
        out_shape=jax.ShapeDtypeStruct((L, ncol), F32),
        compiler_params=_params("parallel"),
        name="hyena_filter_mlp",
    )(feats, w1p, b1[None, :], freq[None, :], w2, b2[None, :], w3, b3[None, :], decay.reshape(1, ncol))


def _filter_spectrum_kernel(hf_ref, hb_ref, gf_ref, f3f_ref, o_ref, a_scr, x_scr):
    na = hf_ref.shape[0] // DFT_R
    for t, src in enumerate((hf_ref, hb_ref)):
        for hf in range(2):
            _dft_stage1(lambda b: src[pl.ds(b, na, stride=DFT_R), :], gf_ref, hf, a_scr)

            def body(j, carry):
                x = jnp.dot(f3f_ref[...], _dft_stage3_operand(a_scr, j), preferred_element_type=F32)
                k1 = hf * K1_HALF + j
                if t == 0:
                    x_scr[k1] = x
                else:
                    xf = x_scr[k1]
                    o_ref[0, k1] = jnp.concatenate([xf[:DFT_R] + x[:DFT_R], xf[DFT_R:] - x[DFT_R:]],
                                                   axis=0).astype(BF16)
                return carry

            lax.fori_loop(0, K1_COUNTS[hf], body, 0, unroll=5)


def _filter_spectrum(taps, gf, f3f):
    L, ncol = taps.shape
    C = ncol // 4
    nct = C // HY_CT
    ct = HY_CT
    return pl.pallas_call(
        _filter_spectrum_kernel,
        grid=(2, nct),
        in_specs=[
            pl.BlockSpec((L, ct), lambda o, j: (0, o * nct + j)),
            pl.BlockSpec((L, ct), lambda o, j: (0, 2 * nct + o * nct + j)),
            pl.BlockSpec(gf.shape, lambda o, j: (0, 0, 0, 0), pipeline_mode=pl.Buffered(1)),
            pl.BlockSpec(f3f.shape, lambda o, j: (0, 0), pipeline_mode=pl.Buffered(1)),
        ],
        out_specs=pl.BlockSpec((1, K1_VALID, 2 * DFT_R, ct), lambda o, j: (o, 0, 0, j)),
        out_shape=jax.ShapeDtypeStruct((2, K1_VALID, 2 * DFT_R, C), BF16),
        scratch_shapes=[pltpu.VMEM((DFT_R * 2 * K1_HALF, ct), F32), pltpu.VMEM((K1_VALID, 2 * DFT_R, ct), F32)],
        compiler_params=_params("parallel", "parallel"),
        name="hyena_filter_spectrum",
    )(taps, taps, gf, f3f)


def _short_conv_kernel(x_ref, w_ref, b_ref, o_ref):
    S = x_ref.shape[1]
    R = 256
    nchunk = S // R
    w0, w1, w2 = w_ref[0:1, :], w_ref[1:2, :], w_ref[2:3, :]
    bias = b_ref[...]
    row = lax.broadcasted_iota(jnp.int32, (R, x_ref.shape[2]), 0)

    def body(i, carry):
        r0 = pl.multiple_of(i * R, R)
        x = x_ref[0, pl.ds(r0, R), :]
        prev = x_ref[0, pl.ds(pl.multiple_of(jnp.maximum(r0 - SUBLANES, 0), SUBLANES), SUBLANES), :]
        nxt = x_ref[0, pl.ds(pl.multiple_of(jnp.minimum(r0 + R, S - SUBLANES), SUBLANES), SUBLANES), :]
        prev = jnp.where(i == 0, 0.0, prev[SUBLANES - 1:SUBLANES, :])
        nxt = jnp.where(i == nchunk - 1, 0.0, nxt[0:1, :])
        up = jnp.where(row == 0, prev, pltpu.roll(x, 1, 0))
        dn = jnp.where(row == R - 1, nxt, pltpu.roll(x, R - 1, 0))
        o_ref[0, pl.ds(r0, R), :] = bias + w0 * up + w1 * x + w2 * dn
        return carry

    lax.fori_loop(0, nchunk, body, 0)


def _short_conv(hy_raw, conv_w, conv_b):
    B, S, C = hy_raw.shape
    ct = LANES
    return pl.pallas_call(
        _short_conv_kernel,
        grid=(B, C // ct),
        in_specs=[pl.BlockSpec((1, S, ct), lambda b, j: (b, 0, j)),
                  pl.BlockSpec((conv_w.shape[0], ct), lambda b, j: (0, j)),
                  pl.BlockSpec((1, ct), lambda b, j: (0, j))],
        out_specs=pl.BlockSpec((1, S, ct), lambda b, j: (b, 0, j)),
        out_shape=jax.ShapeDtypeStruct((B, S, C), F32),
        compiler_params=_params("parallel", "parallel"),
        name="hyena_short_conv",
    )(hy_raw, conv_w, conv_b[None, :])


def _long_conv_kernel(z_ref, xg_ref, h_ref, bias_ref, gf_ref, f3f_ref, f3i_ref, gi_ref, o_ref, a_scr, b_scr):
    na = z_ref.shape[1] // DFT_R
    rows_b = lambda ref, b: ref[0, pl.ds(b, na, stride=DFT_R), :]
    bias = bias_ref[0]
    for hf in range(2):
        _dft_stage1(lambda b: rows_b(z_ref, b), gf_ref, hf, a_scr)
        if K1_COUNTS[hf] < K1_HALF:
            pad0 = K1_COUNTS[hf] * 2 * DFT_R
            b_scr[pad0:, :] = jnp.zeros((b_scr.shape[0] - pad0, b_scr.shape[1]), F32)

        def spectral(j, carry):
            x = jnp.dot(f3f_ref[...], _dft_stage3_operand(a_scr, j), preferred_element_type=F32)
            h = h_ref[0, hf * K1_HALF + j].astype(F32)
            xr, xi, hr, hi = x[:DFT_R], x[DFT_R:], h[:DFT_R], h[DFT_R:]
            y = jnp.concatenate([xr * hr - xi * hi, xr * hi + xi * hr], axis=0).astype(BF16)
            b_scr[pl.ds(pl.multiple_of(j * 2 * DFT_R, 2 * DFT_R), 2 * DFT_R), :] = jnp.dot(
                f3i_ref[...], y, preferred_element_type=F32)
            return carry

        lax.fori_loop(0, K1_COUNTS[hf], spectral, 0, unroll=5)

        def inverse(b, carry):
            br = b_scr[pl.ds(b, K1_HALF, stride=2 * DFT_R), :]
            bi = b_scr[pl.ds(DFT_R + b, K1_HALF, stride=2 * DFT_R), :]
            bm = jnp.concatenate([br, bi], axis=0).astype(BF16)
            yb = jnp.dot(gi_ref[hf, b], bm, preferred_element_type=F32)
            if hf == 0:
                o_ref[0, pl.ds(b, na, stride=DFT_R), :] = yb
            else:
                y = rows_b(o_ref, b) + yb
                o_ref[0, pl.ds(b, na, stride=DFT_R), :] = rows_b(xg_ref, b) * (y + rows_b(z_ref, b) * bias)
            return carry

        lax.fori_loop(0, DFT_R, inverse, 0, unroll=8)


def _long_conv(z_arr, z_off, xg_arr, xg_off, spec, order, bias3, gf, f3f, f3i, gi):
    B, S, _ = z_arr.shape
    C = spec.shape[3]
    ct = HY_CT
    const = lambda arr: pl.BlockSpec(arr.shape, lambda j, b: (0,) * arr.ndim, pipeline_mode=pl.Buffered(1))
    return pl.pallas_call(
        _long_conv_kernel,
        grid=(C // ct, B),
        in_specs=[
            pl.BlockSpec((1, S, ct), lambda j, b: (b, 0, z_off + j)),
            pl.BlockSpec((1, S, ct), lambda j, b: (b, 0, xg_off + j)),
            pl.BlockSpec((1, K1_VALID, 2 * DFT_R, ct), lambda j, b: (order, 0, 0, j), pipeline_mode=pl.Buffered(1)),
            pl.BlockSpec((1, 1, ct), lambda j, b: (order, 0, j)),
            const(gf), const(f3f), const(f3i), const(gi),
        ],
        out_specs=pl.BlockSpec((1, S, ct), lambda j, b: (b, 0, j)),
        out_shape=jax.ShapeDtypeStruct((B, S, C), F32),
        scratch_shapes=[pltpu.VMEM((DFT_R * 2 * K1_HALF, ct), F32), pltpu.VMEM((K1_HALF * 2 * DFT_R, ct), F32)],
        compiler_params=_params("parallel", "parallel"),
        name="hyena_long_conv",
    )(z_arr, xg_arr, spec, bias3, gf, f3f, f3i, gi)


def _hyena_group(hy_raw, conv_w, conv_b, fw1, fb1, ffreq, fw2, fb2, fw3, fb3, decay, hyena_bias):
    B, S, _ = hy_raw.shape
    gf, gi, f3f, f3i = _dft_tables(S)
    taps = _filter_taps(S, fw1, fb1, ffreq, fw2, fb2, fw3, fb3, decay)
    spec = _filter_spectrum(taps, gf, f3f)
    u = _short_conv(hy_raw, conv_w, conv_b)
    nct = HY_W // HY_CT
    bias3 = hyena_bias[:, None, :]
    z1 = _long_conv(u, 0, u, nct, spec, 0, bias3, gf, f3f, f3i, gi)
    return _long_conv(z1, 0, u, 2 * nct, spec, 1, bias3, gf, f3f, f3i, gi)


def _rope_rot_cols(w):
    half = ROPE // 2
    return jnp.concatenate([-w[..., half:], w[..., :half]], axis=-1)


def _prep_weights(w_in, w_qb, w_kvb):
    D = w_in.shape[0]
    o = Q_RANK + KV_RANK
    w_kr = w_in[:, o:o + ROPE]
    z64 = jnp.zeros((D, NOPE), F32)
    z32 = jnp.zeros((D, LANES - NOPE - ROPE), F32)
    win_aug = jnp.concatenate([
        w_in[:, :o],
        z64, w_kr, z32,
        z64, _rope_rot_cols(w_kr), z32,
        w_in[:, o + ROPE:],
    ], axis=1).astype(BF16)
    wq = w_qb.reshape(Q_RANK, MLA_HEADS, NOPE + ROPE)
    zq = jnp.zeros((Q_RANK, MLA_HEADS, LANES - NOPE - ROPE), F32)
    wq_main = jnp.concatenate([wq, zq], axis=-1).reshape(Q_RANK, MLA_HEADS * LANES).astype(BF16)
    wq_rot = jnp.concatenate([jnp.zeros((Q_RANK, MLA_HEADS, NOPE), F32), _rope_rot_cols(wq[..., NOPE:]), zq],
                             axis=-1).reshape(Q_RANK, MLA_HEADS * LANES).astype(BF16)
    wkv = w_kvb.reshape(KV_RANK, MLA_HEADS, NOPE + V_DIM)
    wk = jnp.concatenate([wkv[..., :NOPE], jnp.zeros((KV_RANK, MLA_HEADS, LANES - NOPE), F32)],
                         axis=-1).reshape(KV_RANK, MLA_HEADS * LANES).astype(BF16)
    wv = wkv[..., NOPE:].reshape(KV_RANK, MLA_HEADS * V_DIM).astype(BF16)
    return win_aug, wq_main, wq_rot, wk, wv


def kernel(x, c, positions, w_ada, b_ada, w_in, q_norm_g, w_qb, kv_norm_g, w_kvb, conv_w, conv_b, filt_w1, filt_b1, filt_freq, filt_w2, filt_b2, filt_w3, filt_b3, hyena_decay, hyena_bias, attn_out_g, hyena_out_g, w_o, ln1_g, ln1_b, w_router, w_gate, w_up, w_down, ln2_g, ln2_b):
    B, S, D = x.shape
    l = 0
    c8 = jnp.zeros((SUBLANES, D), F32).at[:B].set(c)
    mod = _ada_mod(c8, w_ada[l], b_ada[l][None, :])[:B]
    sh1, sc1, g1, sh2, sc2, g2 = [m[:, None, :] for m in jnp.split(mod, 6, axis=-1)]

    win_aug, wq_main, wq_rot, wk, wv = _prep_weights(w_in[l], w_qb[l], w_kvb[l])
    half = ROPE // 2
    inv_freq = ROPE_THETA ** (-jnp.arange(half, dtype=F32) / half)
    freq_row = jnp.concatenate([jnp.zeros((NOPE,), F32), inv_freq, inv_freq,
                                jnp.zeros((LANES - NOPE - ROPE,), F32)])[None, :]
    q, kt, v, hy_raw = _inproj(x, sc1, sh1, positions[:, :, None], freq_row, win_aug,
                               q_norm_g[l][None, :], wq_main, wq_rot, kv_norm_g[l][None, :], wk, wv)
    a = _attention(q, kt, v)

    hy = _hyena_group(hy_raw, conv_w[l], conv_b[l], filt_w1[l], filt_b1[l], filt_freq[l], filt_w2[l],
                      filt_b2[l], filt_w3[l], filt_b3[l], hyena_decay[l], hyena_bias[l])

    x1, u2, aff = _outproj(a, hy, x, g1, sc2, sh2, attn_out_g[l][None, :], hyena_out_g[l][None, :],
                           w_o[l].astype(BF16), ln1_g[l][None, :], ln1_b[l][None, :], w_router[l])

    cap = EC_FACTOR * S // N_EXPERTS
    gsel, idx = lax.top_k(aff.transpose(0, 2, 1), cap)
    xe = jax.vmap(lambda ub, ib: ub[ib])(u2, idx)
    xe = xe.transpose(1, 0, 2, 3).reshape(N_EXPERTS, B * cap, D)
    ge = gsel.transpose(1, 0, 2).reshape(N_EXPERTS, B * cap, 1)
    ye = _moe_ffn(xe, ge, w_gate[l], w_up[l], w_down[l])
    flat_idx = (idx + (jnp.arange(B, dtype=idx.dtype) * S)[:, None, None]).transpose(1, 0, 2).reshape(-1)
    ffn = jnp.zeros((B * S, D), F32).at[flat_idx].add(ye.reshape(-1, D)).reshape(B, S, D)

    return _ln2(x1, ffn, g2, ln2_g[l][None, :], ln2_b[l][None, :])
```

```python
import functools
import math

import jax
import jax.numpy as jnp
from jax import lax
from jax.experimental import pallas as pl
from jax.experimental.pallas import tpu as pltpu

F32 = jnp.float32
BF16 = jnp.bfloat16
HIGHEST = lax.Precision.HIGHEST

LANES = 128
SUBLANES = 8
VMEM_LIMIT = 56 * 1024 * 1024

D_MODEL = 1024
MLA_HEADS = 8
NOPE = 64
ROPE = 32
V_DIM = 64
Q_RANK = 256
KV_RANK = 128
HY_W = 512
HY_COLS = 3 * HY_W
N_EXPERTS = 16
EC_FACTOR = 2
ROPE_THETA = 10000.0
EPS = 1e-5
DN_ALPHA = 2.0 ** 0.25
LOG2E = 1.4426950408889634

TOK_TILE = 512
ATT_TQ = 512
ATT_TK = 512
MOE_ROWS = 512


def _params(*sem):
    return pltpu.CompilerParams(dimension_semantics=sem, vmem_limit_bytes=VMEM_LIMIT)


def _ada_kernel(c_ref, w_ref, b_ref, o_ref):
    c = c_ref[...]
    sc = c * jax.nn.sigmoid(c)
    o_ref[...] = jnp.dot(sc, w_ref[...], precision=HIGHEST, preferred_element_type=F32) + b_ref[...]


def _ada_mod(c8, w_ada, b_ada):
    n = w_ada.shape[1]
    tn = 1024
    return pl.pallas_call(
        _ada_kernel,
        grid=(n // tn,),
        in_specs=[
            pl.BlockSpec((SUBLANES, D_MODEL), lambda j: (0, 0)),
            pl.BlockSpec((D_MODEL, tn), lambda j: (0, j)),
            pl.BlockSpec((1, tn), lambda j: (0, j)),
        ],
        out_specs=pl.BlockSpec((SUBLANES, tn), lambda j: (0, j)),
        out_shape=jax.ShapeDtypeStruct((SUBLANES, n), F32),
        compiler_params=_params("arbitrary"),
        name="ada_mod",
    )(c8, w_ada, b_ada)


def _rms(x, g):
    return x * lax.rsqrt(jnp.mean(x * x, axis=-1, keepdims=True) + EPS) * g


def _inproj_kernel(x_ref, sc_ref, sh_ref, pos_ref, freq_ref, win_ref, qg_ref, wq_ref, wqr_ref,
                   kvg_ref, wk_ref, wv_ref, q_ref, k_ref, v_ref, hy_ref):
    u = (x_ref[0] * (1.0 + sc_ref[0]) + sh_ref[0]).astype(BF16)
    ang = pos_ref[0].astype(F32) * freq_ref[...]
    cos = jnp.cos(ang)
    sin = jnp.sin(ang)

    def proj(lo, hi):
        return jnp.dot(u, win_ref[:, lo:hi], preferred_element_type=F32)

    cq = _rms(proj(0, Q_RANK), qg_ref[...]).astype(BF16)
    q_main = jnp.dot(cq, wq_ref[...], preferred_element_type=F32)
    q_rot = jnp.dot(cq, wqr_ref[...], preferred_element_type=F32)
    qscale = LOG2E * (NOPE + ROPE) ** -0.5
    cq_s = cos * qscale
    sq_s = sin * qscale
    for h in range(MLA_HEADS):
        sl = slice(h * LANES, (h + 1) * LANES)
        q_ref[0, h] = (q_main[:, sl] * cq_s + q_rot[:, sl] * sq_s).T.astype(BF16)

    o = Q_RANK
    ckv = _rms(proj(o, o + KV_RANK), kvg_ref[...]).astype(BF16)
    kr = proj(o + KV_RANK, o + 2 * KV_RANK) * cos + proj(o + 2 * KV_RANK, o + 3 * KV_RANK) * sin
    kn = jnp.dot(ckv, wk_ref[...], preferred_element_type=F32)
    for h in range(MLA_HEADS):
        k_ref[0, h] = (kn[:, h * LANES:(h + 1) * LANES] + kr).astype(BF16)
    v_ref[0] = jnp.dot(ckv, wv_ref[...], preferred_element_type=F32).T.astype(BF16)

    o = Q_RANK + 3 * KV_RANK
    hy_ref[0] = proj(o, o + HY_COLS)


def _inproj(x, sc1, sh1, pos3, freq_row, win_aug, qg, wq, wqr, kvg, wk, wv):
    B, S, D = x.shape
    tm = TOK_TILE
    ncol = win_aug.shape[1]
    full = lambda shape: pl.BlockSpec(shape, lambda b, i: (0,) * len(shape))
    return pl.pallas_call(
        _inproj_kernel,
        grid=(B, S // tm),
        in_specs=[
            pl.BlockSpec((1, tm, D), lambda b, i: (b, i, 0)),
            pl.BlockSpec((1, 1, D), lambda b, i: (b, 0, 0)),
            pl.BlockSpec((1, 1, D), lambda b, i: (b, 0, 0)),
            pl.BlockSpec((1, tm, 1), lambda b, i: (b, i, 0)),
            full((1, LANES)),
            full((D, ncol)),
            full((1, Q_RANK)),
            full((Q_RANK, MLA_HEADS * LANES)),
            full((Q_RANK, MLA_HEADS * LANES)),
            full((1, KV_RANK)),
            full((KV_RANK, MLA_HEADS * LANES)),
            full((KV_RANK, MLA_HEADS * V_DIM)),
        ],
        out_specs=[
            pl.BlockSpec((1, MLA_HEADS, LANES, tm), lambda b, i: (b, 0, 0, i)),
            pl.BlockSpec((1, MLA_HEADS, tm, LANES), lambda b, i: (b, 0, i, 0)),
            pl.BlockSpec((1, MLA_HEADS * V_DIM, tm), lambda b, i: (b, 0, i)),
            pl.BlockSpec((1, tm, HY_COLS), lambda b, i: (b, i, 0)),
        ],
        out_shape=[
            jax.ShapeDtypeStruct((B, MLA_HEADS, LANES, S), BF16),
            jax.ShapeDtypeStruct((B, MLA_HEADS, S, LANES), BF16),
            jax.ShapeDtypeStruct((B, MLA_HEADS * V_DIM, S), BF16),
            jax.ShapeDtypeStruct((B, S, HY_COLS), F32),
        ],
        compiler_params=_params("parallel", "parallel"),
        name="in_proj",
    )(x, sc1, sh1, pos3, freq_row, win_aug, qg, wq, wqr, kvg, wk, wv)


def _attn_kernel(qt_ref, k_ref, vt_ref, o_ref):
    S = k_ref.shape[2]
    tq = qt_ref.shape[3]
    nk = S // ATT_TK

    def step(c, carry):
        k0 = pl.multiple_of(c * ATT_TK, ATT_TK)
        s = [jnp.dot(k_ref[0, h, pl.ds(k0, ATT_TK), :], qt_ref[0, h], preferred_element_type=F32)
             for h in range(2)]
        out = []
        for h in range(2):
            m_prev, l_prev, acc = carry[h]
            m_new = jnp.maximum(m_prev, jnp.max(s[h], axis=0, keepdims=True))
            alpha = jnp.exp2(m_prev - m_new)
            p = jnp.exp2(s[h] - m_new)
            l_new = alpha * l_prev + jnp.sum(p, axis=0, keepdims=True)
            pv = jnp.dot(vt_ref[0, h * V_DIM:(h + 1) * V_DIM, pl.ds(k0, ATT_TK)], p.astype(BF16),
                         preferred_element_type=F32)
            out.append((m_new, l_new, alpha * acc + pv))
        return tuple(out)

    init = tuple((jnp.full((1, tq), -jnp.inf, F32), jnp.zeros((1, tq), F32), jnp.zeros((V_DIM, tq), F32))
                 for _ in range(2))
    fin = lax.fori_loop(0, nk, step, init, unroll=2)
    o = jnp.concatenate([acc * (1.0 / l) for (_, l, acc) in fin], axis=0)
    o_ref[0] = o.T


def _attention(qt, k, vt):
    B, H, _, S = qt.shape
    tq = ATT_TQ
    return pl.pallas_call(
        _attn_kernel,
        grid=(B, H // 2, S // tq),
        in_specs=[
            pl.BlockSpec((1, 2, LANES, tq), lambda b, hp, i: (b, hp, 0, i)),
            pl.BlockSpec((1, 2, S, LANES), lambda b, hp, i: (b, hp, 0, 0)),
            pl.BlockSpec((1, 2 * V_DIM, S), lambda b, hp, i: (b, hp, 0)),
        ],
        out_specs=pl.BlockSpec((1, tq, 2 * V_DIM), lambda b, hp, i: (b, i, hp)),
        out_shape=jax.ShapeDtypeStruct((B, S, H * V_DIM), F32),
        compiler_params=_params("parallel", "parallel", "parallel"),
        name="mla_attention",
    )(qt, k, vt)


def _layer_norm(x, g, b):
    mu = jnp.mean(x, axis=-1, keepdims=True)
    xc = x - mu
    var = jnp.mean(xc * xc, axis=-1, keepdims=True)
    return xc * lax.rsqrt(var + EPS) * g + b


def _outproj_kernel(a_ref, hy_ref, x_ref, g1_ref, sc2_ref, sh2_ref, ag_ref, hg_ref, wo_ref,
                    lg_ref, lb_ref, wr_ref, x1_ref, u2_ref, aff_ref):
    half = a_ref.shape[2]
    an = _rms(a_ref[0], ag_ref[...]).astype(BF16)
    hn = _rms(hy_ref[0], hg_ref[...]).astype(BF16)
    mixed = (jnp.dot(an, wo_ref[:half, :], preferred_element_type=F32)
             + jnp.dot(hn, wo_ref[half:, :], preferred_element_type=F32))
    x1 = _layer_norm(DN_ALPHA * x_ref[0] + g1_ref[0] * mixed, lg_ref[...], lb_ref[...])
    x1_ref[0] = x1
    u2 = x1 * (1.0 + sc2_ref[0]) + sh2_ref[0]
    u2_ref[0] = u2.astype(BF16)
    logits = jnp.dot(u2, wr_ref[...], precision=HIGHEST, preferred_element_type=F32)
    lane = lax.broadcasted_iota(jnp.int32, logits.shape, 1)
    logits = jnp.where(lane < N_EXPERTS, logits, -jnp.inf)
    e = jnp.exp(logits - jnp.max(logits, axis=-1, keepdims=True))
    aff = e / jnp.sum(e, axis=-1, keepdims=True)
    aff_ref[0] = aff.T[:N_EXPERTS, :]


def _outproj(a, hy, x, g1, sc2, sh2, ag, hg, wo, lg, lb, wr_pad):
    B, S, D = x.shape
    tm = TOK_TILE
    half = a.shape[2]
    full = lambda shape: pl.BlockSpec(shape, lambda b, i: (0,) * len(shape))
    tok = lambda w: pl.BlockSpec((1, tm, w), lambda b, i: (b, i, 0))
    per_b = pl.BlockSpec((1, 1, D), lambda b, i: (b, 0, 0))
    return pl.pallas_call(
        _outproj_kernel,
        grid=(B, S // tm),
        in_specs=[tok(half), tok(half), tok(D), per_b, per_b, per_b,
                  full((1, half)), full((1, half)), full((D, D)), full((1, D)), full((1, D)),
                  full((D, LANES))],
        out_specs=[tok(D), tok(D), pl.BlockSpec((1, N_EXPERTS, tm), lambda b, i: (b, 0, i))],
        out_shape=[
            jax.ShapeDtypeStruct((B, S, D), F32),
            jax.ShapeDtypeStruct((B, S, D), BF16),
            jax.ShapeDtypeStruct((B, N_EXPERTS, S), F32),
        ],
        compiler_params=_params("parallel", "parallel"),
        name="out_proj_ln1_router",
    )(a, hy, x, g1, sc2, sh2, ag, hg, wo, lg, lb, wr_pad)


def _select_kernel(aff_ref, ones_ref, utri_ref, bsum_ref, ltri_ref, pos_ref, posall_ref, g_ref, *, cap):
    aff = aff_ref[0]
    ones = ones_ref[...]

    def rowsum(mf):
        return jnp.dot(mf, ones, preferred_element_type=F32)

    def expert_total(mf):
        return jnp.dot(bsum_ref[...], rowsum(mf).astype(BF16), preferred_element_type=F32)

    def prefix(mf):
        return (jnp.dot(mf, utri_ref[...], preferred_element_type=F32)
                + jnp.dot(ltri_ref[...], rowsum(mf).astype(BF16), preferred_element_type=F32))

    def ind(m):
        return jnp.where(m, 1.0, 0.0).astype(BF16)

    def search(i, thr_bits):
        cand = thr_bits | jnp.left_shift(jnp.int32(1), 30 - i)
        enough = expert_total(ind(aff >= lax.bitcast_convert_type(cand, F32))) >= cap
        return jnp.where(enough, cand, thr_bits)

    thr = lax.bitcast_convert_type(lax.fori_loop(0, 31, search, jnp.zeros(aff.shape, jnp.int32)), F32)
    gt = aff > thr
    eq = aff == thr
    room = cap - expert_total(ind(gt))
    take_eq = jnp.where(eq, jnp.where(prefix(ind(eq)) < room, 1.0, 0.0), 0.0)
    sel = jnp.where(gt, 1.0, take_eq)
    pos = prefix(sel.astype(BF16)).astype(jnp.int32)
    posall_ref[0] = pos
    pos_ref[0] = jnp.where(sel > 0.0, pos, -1)
    g_ref[0] = jnp.where(sel > 0.0, aff, 0.0)


def _select(aff2, cap):
    B, R, _ = aff2.shape
    rpe = R // N_EXPERTS
    li = jnp.arange(LANES, dtype=jnp.int32)
    ri = jnp.arange(R, dtype=jnp.int32)
    same = (ri[:, None] // rpe) == (ri[None, :] // rpe)
    ones = jnp.ones((LANES, LANES), BF16)
    utri = (li[:, None] < li[None, :]).astype(BF16)
    bsum = same.astype(BF16)
    ltri = (same & (ri[None, :] < ri[:, None])).astype(BF16)
    blk = pl.BlockSpec((1, R, LANES), lambda b: (b, 0, 0))
    full = lambda arr: pl.BlockSpec(arr.shape, lambda b: (0, 0))
    return pl.pallas_call(
        functools.partial(_select_kernel, cap=cap),
        grid=(B,),
        in_specs=[blk, full(ones), full(utri), full(bsum), full(ltri)],
        out_specs=[blk, blk, blk],
        out_shape=[jax.ShapeDtypeStruct((B, R, LANES), jnp.int32), jax.ShapeDtypeStruct((B, R, LANES), jnp.int32),
                   jax.ShapeDtypeStruct((B, R, LANES), F32)],
        compiler_params=_params("parallel"),
        name="expert_choice_select",
    )(aff2, ones, utri, bsum, ltri)


GATHER_TOK = 1024
GATHER_WIN = 256


def _gather_kernel(offs_ref, u_ref, pos_ref, xe_ref, acc, *, cap, nblk):
    b, e = pl.program_id(0), pl.program_id(1)
    base = (b * N_EXPERTS + e) * (nblk + 1)
    rows = GATHER_TOK // LANES
    acc[...] = jnp.zeros(acc.shape, F32)
    slot = lax.broadcasted_iota(jnp.int32, (GATHER_WIN, LANES), 0)

    def block(j, carry):
        off, end = offs_ref[base + j], offs_ref[base + j + 1]
        w = (off // SUBLANES) * SUBLANES
        nsub = jnp.where(end > off, (end - w + GATHER_WIN - 1) // GATHER_WIN, 0)
        prow = pos_ref[0, pl.ds(pl.multiple_of(j * rows, rows), rows), :]
        ublk = u_ref[0, pl.ds(pl.multiple_of(j * GATHER_TOK, GATHER_TOK), GATHER_TOK), :]

        def window(i, c):
            s0 = pl.multiple_of(w + i * GATHER_WIN, SUBLANES)
            onehot = jnp.concatenate(
                [jnp.where(slot == prow[r:r + 1, :] - s0, 1.0, 0.0).astype(BF16) for r in range(rows)], axis=1)
            acc[pl.ds(s0, GATHER_WIN), :] += jnp.dot(onehot, ublk, preferred_element_type=F32)
            return c

        lax.fori_loop(0, nsub, window, 0)
        return carry

    lax.fori_loop(0, nblk, block, 0)
    xe_ref[0] = acc[:cap, :].astype(BF16)


def _gather(offs, u2, pos_sel, cap):
    B, S, D = u2.shape
    rpe = S // LANES
    nblk = S // GATHER_TOK
    return pl.pallas_call(
        functools.partial(_gather_kernel, cap=cap, nblk=nblk),
        grid_spec=pltpu.PrefetchScalarGridSpec(
            num_scalar_prefetch=1,
            grid=(B, N_EXPERTS),
            in_specs=[pl.BlockSpec((1, S, D), lambda b, e, o: (b, 0, 0)),
                      pl.BlockSpec((1, rpe, LANES), lambda b, e, o: (b, e, 0))],
            out_specs=pl.BlockSpec((1, cap, D), lambda b, e, o: (e, b, 0)),
            scratch_shapes=[pltpu.VMEM((cap + GATHER_WIN + SUBLANES, D), F32)],
        ),
        out_shape=jax.ShapeDtypeStruct((N_EXPERTS, B * cap, D), BF16),
        compiler_params=_params("parallel", "parallel"),
        name="expert_gather",
    )(offs, u2, pos_sel)


def _moe_kernel(x_ref, wg_ref, wu_ref, wd_ref, o_ref, wgb, wub, wdb):
    @pl.when(pl.program_id(1) == 0)
    def _():
        wgb[...] = wg_ref[0].astype(BF16)
        wub[...] = wu_ref[0].astype(BF16)
        wdb[...] = wd_ref[0].astype(BF16)

    x = x_ref[0]
    hg = jnp.dot(x, wgb[...], preferred_element_type=F32)
    hu = jnp.dot(x, wub[...], preferred_element_type=F32)
    h = (hg * jax.nn.sigmoid(hg) * hu).astype(BF16)
    o_ref[0] = jnp.dot(h, wdb[...], preferred_element_type=F32).astype(BF16)


def _moe_ffn(xe, w_gate, w_up, w_down):
    E, R, D = xe.shape
    Fh = w_gate.shape[2]
    tr = MOE_ROWS
    return pl.pallas_call(
        _moe_kernel,
        grid=(E, R // tr),
        in_specs=[
            pl.BlockSpec((1, tr, D), lambda e, r: (e, r, 0)),
            pl.BlockSpec((1, D, Fh), lambda e, r: (e, 0, 0)),
            pl.BlockSpec((1, D, Fh), lambda e, r: (e, 0, 0)),
            pl.BlockSpec((1, Fh, D), lambda e, r: (e, 0, 0)),
        ],
        out_specs=pl.BlockSpec((1, tr, D), lambda e, r: (e, r, 0)),
        out_shape=jax.ShapeDtypeStruct((E, R, D), BF16),
        scratch_shapes=[pltpu.VMEM((D, Fh), BF16), pltpu.VMEM((D, Fh), BF16), pltpu.VMEM((Fh, D), BF16)],
        compiler_params=_params("parallel", "arbitrary"),
        name="moe_swiglu",
    )(xe, w_gate, w_up, w_down)


CMB_TOK = 512
CMB_WIN = 128
BF16_ROWS = 16


def _combine_kernel(offs_ref, x1_ref, pos_ref, g_ref, g2_ref, lg_ref, lb_ref, ye_hbm, o_ref,
                    rhs, lhs, acc, sem, *, cap, nblk, nbatch):
    b, i = pl.program_id(0), pl.program_id(1)
    W = CMB_WIN
    total = ye_hbm.shape[0]
    row_base, first, npass = [], [], jnp.int32(0)
    for e in range(N_EXPERTS):
        k = (b * N_EXPERTS + e) * (nblk + 1) + i
        off, end = offs_ref[k], offs_ref[k + 1]
        lo = (off // BF16_ROWS) * BF16_ROWS
        row_base.append((e * nbatch + b) * cap)
        first.append(lo)
        npass = jnp.maximum(npass, jnp.where(end > off, (end - lo + W - 1) // W, 0))
    acc[...] = jnp.zeros(acc.shape, F32)
    lane_e = lax.broadcasted_iota(jnp.int32, (1, N_EXPERTS), 1)
    lane_w = lax.broadcasted_iota(jnp.int32, (CMB_TOK, W), 1)

    def window_copy(e, start):
        return pltpu.make_async_copy(ye_hbm.at[pl.ds(pl.multiple_of(start, BF16_ROWS), W), :],
                                     rhs.at[pl.ds(e * W, W), :], sem.at[e])

    def one_pass(p, carry):
        lo_vec = jnp.zeros((1, N_EXPERTS), jnp.int32)
        shift_vec = jnp.zeros((1, N_EXPERTS), jnp.int32)
        starts = []
        for e in range(N_EXPERTS):
            lo = first[e] + p * W
            start = jnp.minimum(row_base[e] + lo, total - W)
            starts.append(start)
            lo_vec = jnp.where(lane_e == e, lo, lo_vec)
            shift_vec = jnp.where(lane_e == e, row_base[e] + lo - start, shift_vec)
        for e in range(N_EXPERTS):
            window_copy(e, starts[e]).start()
        d = pos_ref[0] - lo_vec
        rel = jnp.where((d >= 0) & (d < W), d + shift_vec, -1)
        g = g_ref[0]
        for e in range(N_EXPERTS):
            lhs[:, e * W:(e + 1) * W] = jnp.where(lane_w == rel[:, e:e + 1], g[:, e:e + 1], 0.0).astype(BF16)
        for e in range(N_EXPERTS):
            window_copy(e, starts[e]).wait()
        acc[...] += jnp.dot(lhs[...], rhs[...], preferred_element_type=F32)
        return carry

    lax.fori_loop(0, npass, one_pass, 0)
    o_ref[0] = _layer_norm(DN_ALPHA * x1_ref[0] + g2_ref[0] * acc[...], lg_ref[...], lb_ref[...])


def _combine_ln2(offs, x1, pos_tok, g_tok, g2, lg, lb, ye_rows, cap):
    B, S, D = x1.shape
    T = CMB_TOK
    nblk = S // T
    tok = lambda w: pl.BlockSpec((1, T, w), lambda b, i, o: (b, i, 0))
    vec = pl.BlockSpec((1, D), lambda b, i, o: (0, 0))
    return pl.pallas_call(
        functools.partial(_combine_kernel, cap=cap, nblk=nblk, nbatch=B),
        grid_spec=pltpu.PrefetchScalarGridSpec(
            num_scalar_prefetch=1,
            grid=(B, nblk),
            in_specs=[tok(D), tok(N_EXPERTS), tok(N_EXPERTS),
                      pl.BlockSpec((1, 1, D), lambda b, i, o: (b, 0, 0)), vec, vec,
                      pl.BlockSpec(memory_space=pl.ANY)],
            out_specs=tok(D),
            scratch_shapes=[pltpu.VMEM((N_EXPERTS * CMB_WIN, D), BF16),
                            pltpu.VMEM((T, N_EXPERTS * CMB_WIN), BF16),
                            pltpu.VMEM((T, D), F32),
                            pltpu.SemaphoreType.DMA((N_EXPERTS,))],
        ),
        out_shape=jax.ShapeDtypeStruct((B, S, D), F32),
        compiler_params=_params("parallel", "parallel"),
        name="combine_residual_ln2",
    )(offs, x1, pos_tok, g_tok, g2, lg, lb, ye_rows)


DFT_R = 128
K1_HALF = 40
K1_VALID = DFT_R // 2 + 1
K1_COUNTS = (K1_HALF, K1_VALID - K1_HALF)
HY_CT = 128


def _dft_tables(S):
    N = 2 * S
    na = S // DFT_R
    r = jnp.arange(2 * K1_HALF, dtype=jnp.int32)
    k1 = (jnp.arange(2, dtype=jnp.int32)[:, None] * K1_HALF + r[None, :] % K1_HALF)
    valid = (k1 < K1_VALID).astype(F32)
    n = DFT_R * jnp.arange(na, dtype=jnp.int32)[None, :] + jnp.arange(DFT_R, dtype=jnp.int32)[:, None]
    prod = (n[None, :, None, :] * k1[:, None, :, None]) % N
    theta = prod.astype(F32) * (2.0 * math.pi / N)
    is_im = (r >= K1_HALF)[None, None, :, None]
    g = jnp.where(is_im, -jnp.sin(theta), jnp.cos(theta)) * valid[:, None, :, None]
    cw = jnp.where((k1 == 0) | (k1 == DFT_R // 2), 1.0, 2.0) * valid / N
    gi = jnp.transpose(g * cw[:, None, :, None], (0, 1, 3, 2))
    bk = (jnp.arange(DFT_R, dtype=jnp.int32)[:, None] * jnp.arange(DFT_R, dtype=jnp.int32)[None, :]) % DFT_R
    ph = bk.astype(F32) * (2.0 * math.pi / DFT_R)
    C, Sn = jnp.cos(ph), jnp.sin(ph)
    f3f = jnp.block([[C, Sn], [-Sn, C]])
    f3i = jnp.block([[C, -Sn], [Sn, C]])
    return g.astype(BF16), gi.astype(BF16), f3f.astype(BF16), f3i.astype(BF16)


def _dft_stage1(load_rows, gf_ref, hf, a_scr):
    rows = 2 * K1_HALF

    def body(b, carry):
        zb = load_rows(b).astype(BF16)
        a_scr[pl.ds(pl.multiple_of(b * rows, SUBLANES), rows), :] = jnp.dot(
            gf_ref[hf, b], zb, preferred_element_type=F32)
        return carry

    lax.fori_loop(0, DFT_R, body, 0, unroll=8)


def _dft_stage3_operand(a_scr, j):
    rows = 2 * K1_HALF
    mr = a_scr[pl.ds(j, DFT_R, stride=rows), :]
    mi = a_scr[pl.ds(K1_HALF + j, DFT_R, stride=rows), :]
    return jnp.concatenate([mr, mi], axis=0).astype(BF16)


def _filter_mlp_kernel(feat_ref, w1_ref, b1_ref, fr_ref, w2_ref, b2_ref, w3_ref, b3_ref, dec_ref, o_ref):
    feats = feat_ref[...]
    fr = fr_ref[...]
    h = jnp.sin(fr * (jnp.dot(feats, w1_ref[...], precision=HIGHEST, preferred_element_type=F32) + b1_ref[...]))
    h = jnp.sin(fr * (jnp.dot(h, w2_ref[...], precision=HIGHEST, preferred_element_type=F32) + b2_ref[...]))
    h = jnp.dot(h, w3_ref[...], precision=HIGHEST, preferred_element_type=F32) + b3_ref[...]
    h = h * jnp.exp(-feats[:, 0:1] * jnp.abs(dec_ref[...]))
    row = lax.broadcasted_iota(jnp.int32, h.shape, 0) + pl.program_id(0) * h.shape[0]
    col = lax.broadcasted_iota(jnp.int32, h.shape, 1)
    o_ref[...] = jnp.where((row == 0) & (col >= h.shape[1] // 2), 0.0, h)


def _filter_taps(L, w1, b1, freq, w2, b2, w3, b3, decay):
    pos = jnp.arange(L, dtype=F32)
    t = pos[:, None] / (L - 1)
    bands = (w1.shape[0] - 1) // 2
    freqs = jnp.linspace(1e-4, bands - 1, bands, dtype=F32)
    phase = (2.0 * math.pi / L) * pos[:, None] * freqs[None, :]
    nf = w1.shape[0]
    feats = jnp.concatenate([t, jnp.cos(phase), -jnp.sin(phase), jnp.zeros((L, LANES - nf), F32)], axis=-1)
    w1p = jnp.concatenate([w1, jnp.zeros((LANES - nf, w1.shape[1]), F32)], axis=0)
    fo = w1.shape[1]
    ncol = w3.shape[1]
    tl = 512
    full = lambda shape: pl.BlockSpec(shape, lambda i: (0,) * len(shape))
    return pl.pallas_call(
        _filter_mlp_kernel,
        grid=(L // tl,),
        in_specs=[pl.BlockSpec((tl, LANES), lambda i: (i, 0)), full((LANES, fo)), full((1, fo)), full((1, fo)),
                  full((fo, fo)), full((1, fo)), full((fo, ncol)), full((1, ncol)), full((1, ncol))],
        out_specs=pl.BlockSpec((tl, ncol), lambda i: (i, 0)),
        out_shape=jax.ShapeDtypeStruct((L, ncol), F32),
        compiler_params=_params("parallel"),
        name="hyena_filter_mlp",
    )(feats, w1p, b1[None, :], freq[None, :], w2, b2[None, :], w3, b3[None, :], decay.reshape(1, ncol))


def _filter_spectrum_kernel(hf_ref, hb_ref, gf_ref, f3f_ref, o_ref, a_scr, x_scr):
    na = hf_ref.shape[0] // DFT_R
    for t, src in enumerate((hf_ref, hb_ref)):
        for hf in range(2):
            _dft_stage1(lambda b: src[pl.ds(b, na, stride=DFT_R), :], gf_ref, hf, a_scr)

            def body(j, carry):
                x = jnp.dot(f3f_ref[...], _dft_stage3_operand(a_scr, j), preferred_element_type=F32)
                k1 = hf * K1_HALF + j
                if t == 0:
                    x_scr[k1] = x
                else:
                    xf = x_scr[k1]
                    o_ref[0, k1] = jnp.concatenate([xf[:DFT_R] + x[:DFT_R], xf[DFT_R:] - x[DFT_R:]],
                                                   axis=0).astype(BF16)
                return carry

            lax.fori_loop(0, K1_COUNTS[hf], body, 0, unroll=5)


def _filter_spectrum(taps, gf, f3f):
    L, ncol = taps.shape
    C = ncol // 4
    nct = C // HY_CT
    ct = HY_CT
    return pl.pallas_call(
        _filter_spectrum_kernel,
        grid=(2, nct),
        in_specs=[
            pl.BlockSpec((L, ct), lambda o, j: (0, o * nct + j)),
            pl.BlockSpec((L, ct), lambda o, j: (0, 2 * nct + o * nct + j)),
            pl.BlockSpec(gf.shape, lambda o, j: (0, 0, 0, 0), pipeline_mode=pl.Buffered(1)),
            pl.BlockSpec(f3f.shape, lambda o, j: (0, 0), pipeline_mode=pl.Buffered(1)),
        ],
        out_specs=pl.BlockSpec((1, K1_VALID, 2 * DFT_R, ct), lambda o, j: (o, 0, 0, j)),
        out_shape=jax.ShapeDtypeStruct((2, K1_VALID, 2 * DFT_R, C), BF16),
        scratch_shapes=[pltpu.VMEM((DFT_R * 2 * K1_HALF, ct), F32), pltpu.VMEM((K1_VALID, 2 * DFT_R, ct), F32)],
        compiler_params=_params("parallel", "parallel"),
        name="hyena_filter_spectrum",
    )(taps, taps, gf, f3f)


def _short_conv_kernel(x_ref, w_ref, b_ref, o_ref):
    S = x_ref.shape[1]
    R = 256
    nchunk = S // R
    w0, w1, w2 = w_ref[0:1, :], w_ref[1:2, :], w_ref[2:3, :]
    bias = b_ref[...]
    row = lax.broadcasted_iota(jnp.int32, (R, x_ref.shape[2]), 0)

    def body(i, carry):
        r0 = pl.multiple_of(i * R, R)
        x = x_ref[0, pl.ds(r0, R), :]
        prev = x_ref[0, pl.ds(pl.multiple_of(jnp.maximum(r0 - SUBLANES, 0), SUBLANES), SUBLANES), :]
        nxt = x_ref[0, pl.ds(pl.multiple_of(jnp.minimum(r0 + R, S - SUBLANES), SUBLANES), SUBLANES), :]
        prev = jnp.where(i == 0, 0.0, prev[SUBLANES - 1:SUBLANES, :])
        nxt = jnp.where(i == nchunk - 1, 0.0, nxt[0:1, :])
        up = jnp.where(row == 0, prev, pltpu.roll(x, 1, 0))
        dn = jnp.where(row == R - 1, nxt, pltpu.roll(x, R - 1, 0))
        o_ref[0, pl.ds(r0, R), :] = bias + w0 * up + w1 * x + w2 * dn
        return carry

    lax.fori_loop(0, nchunk, body, 0)


def _short_conv(hy_raw, conv_w, conv_b):
    B, S, C = hy_raw.shape
    ct = LANES
    return pl.pallas_call(
        _short_conv_kernel,
        grid=(B, C // ct),
        in_specs=[pl.BlockSpec((1, S, ct), lambda b, j: (b, 0, j)),
                  pl.BlockSpec((conv_w.shape[0], ct), lambda b, j: (0, j)),
                  pl.BlockSpec((1, ct), lambda b, j: (0, j))],
        out_specs=pl.BlockSpec((1, S, ct), lambda b, j: (b, 0, j)),
        out_shape=jax.ShapeDtypeStruct((B, S, C), F32),
        compiler_params=_params("parallel", "parallel"),
        name="hyena_short_conv",
    )(hy_raw, conv_w, conv_b[None, :])


def _long_conv_kernel(z_ref, xg_ref, h_ref, bias_ref, gf_ref, f3f_ref, f3i_ref, gi_ref, o_ref, a_scr, b_scr):
    na = z_ref.shape[1] // DFT_R
    rows_b = lambda ref, b: ref[0, pl.ds(b, na, stride=DFT_R), :]
    bias = bias_ref[0]
    for hf in range(2):
        _dft_stage1(lambda b: rows_b(z_ref, b), gf_ref, hf, a_scr)
        if K1_COUNTS[hf] < K1_HALF:
            pad0 = K1_COUNTS[hf] * 2 * DFT_R
            b_scr[pad0:, :] = jnp.zeros((b_scr.shape[0] - pad0, b_scr.shape[1]), F32)

        def spectral(j, carry):
            x = jnp.dot(f3f_ref[...], _dft_stage3_operand(a_scr, j), preferred_element_type=F32)
            h = h_ref[0, hf * K1_HALF + j].astype(F32)
            xr, xi, hr, hi = x[:DFT_R], x[DFT_R:], h[:DFT_R], h[DFT_R:]
            y = jnp.concatenate([xr * hr - xi * hi, xr * hi + xi * hr], axis=0).astype(BF16)
            b_scr[pl.ds(pl.multiple_of(j * 2 * DFT_R, 2 * DFT_R), 2 * DFT_R), :] = jnp.dot(
                f3i_ref[...], y, preferred_element_type=F32)
            return carry

        lax.fori_loop(0, K1_COUNTS[hf], spectral, 0, unroll=5)

        def inverse(b, carry):
            br = b_scr[pl.ds(b, K1_HALF, stride=2 * DFT_R), :]
            bi = b_scr[pl.ds(DFT_R + b, K1_HALF, stride=2 * DFT_R), :]
            bm = jnp.concatenate([br, bi], axis=0).astype(BF16)
            yb = jnp.dot(gi_ref[hf, b], bm, preferred_element_type=F32)
            if hf == 0:
                o_ref[0, pl.ds(b, na, stride=DFT_R), :] = yb
            else:
                y = rows_b(o_ref, b) + yb
                o_ref[0, pl.ds(b, na, stride=DFT_R), :] = rows_b(xg_ref, b) * (y + rows_b(z_ref, b) * bias)
            return carry

        lax.fori_loop(0, DFT_R, inverse, 0, unroll=8)


def _long_conv(z_arr, z_off, xg_arr, xg_off, spec, order, bias3, gf, f3f, f3i, gi):
    B, S, _ = z_arr.shape
    C = spec.shape[3]
    ct = HY_CT
    const = lambda arr: pl.BlockSpec(arr.shape, lambda j, b: (0,) * arr.ndim, pipeline_mode=pl.Buffered(1))
    return pl.pallas_call(
        _long_conv_kernel,
        grid=(C // ct, B),
        in_specs=[
            pl.BlockSpec((1, S, ct), lambda j, b: (b, 0, z_off + j)),
            pl.BlockSpec((1, S, ct), lambda j, b: (b, 0, xg_off + j)),
            pl.BlockSpec((1, K1_VALID, 2 * DFT_R, ct), lambda j, b: (order, 0, 0, j), pipeline_mode=pl.Buffered(1)),
            pl.BlockSpec((1, 1, ct), lambda j, b: (order, 0, j)),
            const(gf), const(f3f), const(f3i), const(gi),
        ],
        out_specs=pl.BlockSpec((1, S, ct), lambda j, b: (b, 0, j)),
        out_shape=jax.ShapeDtypeStruct((B, S, C), F32),
        scratch_shapes=[pltpu.VMEM((DFT_R * 2 * K1_HALF, ct), F32), pltpu.VMEM((K1_HALF * 2 * DFT_R, ct), F32)],
        compiler_params=_params("parallel", "parallel"),
        name="hyena_long_conv",
    )(z_arr, xg_arr, spec, bias3, gf, f3f, f3i, gi)


def _hyena_group(hy_raw, conv_w, conv_b, fw1, fb1, ffreq, fw2, fb2, fw3, fb3, decay, hyena_bias):
    B, S, _ = hy_raw.shape
    gf, gi, f3f, f3i = _dft_tables(S)
    taps = _filter_taps(S, fw1, fb1, ffreq, fw2, fb2, fw3, fb3, decay)
    spec = _filter_spectrum(taps, gf, f3f)
    u = _short_conv(hy_raw, conv_w, conv_b)
    nct = HY_W // HY_CT
    bias3 = hyena_bias[:, None, :]
    z1 = _long_conv(u, 0, u, nct, spec, 0, bias3, gf, f3f, f3i, gi)
    return _long_conv(z1, 0, u, 2 * nct, spec, 1, bias3, gf, f3f, f3i, gi)


def _rope_rot_cols(w):
    half = ROPE // 2
    return jnp.concatenate([-w[..., half:], w[..., :half]], axis=-1)


def _prep_weights(w_in, w_qb, w_kvb):
    D = w_in.shape[0]
    o = Q_RANK + KV_RANK
    w_kr = w_in[:, o:o + ROPE]
    z64 = jnp.zeros((D, NOPE), F32)
    z32 = jnp.zeros((D, LANES - NOPE - ROPE), F32)
    win_aug = jnp.concatenate([
        w_in[:, :o],
        z64, w_kr, z32,
        z64, _rope_rot_cols(w_kr), z32,
        w_in[:, o + ROPE:],
    ], axis=1).astype(BF16)
    wq = w_qb.reshape(Q_RANK, MLA_HEADS, NOPE + ROPE)
    zq = jnp.zeros((Q_RANK, MLA_HEADS, LANES - NOPE - ROPE), F32)
    wq_main = jnp.concatenate([wq, zq], axis=-1).reshape(Q_RANK, MLA_HEADS * LANES).astype(BF16)
    wq_rot = jnp.concatenate([jnp.zeros((Q_RANK, MLA_HEADS, NOPE), F32), _rope_rot_cols(wq[..., NOPE:]), zq],
                             axis=-1).reshape(Q_RANK, MLA_HEADS * LANES).astype(BF16)
    wkv = w_kvb.reshape(KV_RANK, MLA_HEADS, NOPE + V_DIM)
    wk = jnp.concatenate([wkv[..., :NOPE], jnp.zeros((KV_RANK, MLA_HEADS, LANES - NOPE), F32)],
                         axis=-1).reshape(KV_RANK, MLA_HEADS * LANES).astype(BF16)
    wv = wkv[..., NOPE:].reshape(KV_RANK, MLA_HEADS * V_DIM).astype(BF16)
    return win_aug, wq_main, wq_rot, wk, wv


def kernel(x, c, positions, w_ada, b_ada, w_in, q_norm_g, w_qb, kv_norm_g, w_kvb, conv_w, conv_b, filt_w1, filt_b1, filt_freq, filt_w2, filt_b2, filt_w3, filt_b3, hyena_decay, hyena_bias, attn_out_g, hyena_out_g, w_o, ln1_g, ln1_b, w_router, w_gate, w_up, w_down, ln2_g, ln2_b):
    B, S, D = x.shape
    l = 0
    c8 = jnp.zeros((SUBLANES, D), F32).at[:B].set(c)
    mod = _ada_mod(c8, w_ada[l], b_ada[l][None, :])[:B]
    sh1, sc1, g1, sh2, sc2, g2 = [m[:, None, :] for m in jnp.split(mod, 6, axis=-1)]

    win_aug, wq_main, wq_rot, wk, wv = _prep_weights(w_in[l], w_qb[l], w_kvb[l])
    half = ROPE // 2
    inv_freq = ROPE_THETA ** (-jnp.arange(half, dtype=F32) / half)
    freq_row = jnp.concatenate([jnp.zeros((NOPE,), F32), inv_freq, inv_freq,
                                jnp.zeros((LANES - NOPE - ROPE,), F32)])[None, :]
    q, kt, v, hy_raw = _inproj(x, sc1, sh1, positions[:, :, None], freq_row, win_aug,
                               q_norm_g[l][None, :], wq_main, wq_rot, kv_norm_g[l][None, :], wk, wv)
    a = _attention(q, kt, v)

    hy = _hyena_group(hy_raw, conv_w[l], conv_b[l], filt_w1[l], filt_b1[l], filt_freq[l], filt_w2[l],
                      filt_b2[l], filt_w3[l], filt_b3[l], hyena_decay[l], hyena_bias[l])

    wr_pad = jnp.concatenate([w_router[l], jnp.zeros((D, LANES - N_EXPERTS), F32)], axis=1)
    x1, u2, aff_t = _outproj(a, hy, x, g1, sc2, sh2, attn_out_g[l][None, :], hyena_out_g[l][None, :],
                             w_o[l].astype(BF16), ln1_g[l][None, :], ln1_b[l][None, :], wr_pad)

    cap = EC_FACTOR * S // N_EXPERTS
    rpe = S // LANES
    aff2 = aff_t.reshape(B, N_EXPERTS * rpe, LANES)
    pos_sel, pos_all, gsel = _select(aff2, cap)
    row_start = pos_all.reshape(B, N_EXPERTS, rpe, LANES)[:, :, :, 0]

    def block_offsets(tokens):
        ends = jnp.full((B, N_EXPERTS, 1), cap, jnp.int32)
        return jnp.concatenate([row_start[:, :, ::tokens // LANES], ends], axis=-1).reshape(-1)

    xe = _gather(block_offsets(GATHER_TOK), u2, pos_sel, cap)
    ye = _moe_ffn(xe, w_gate[l], w_up[l], w_down[l])
    pos_tok = pos_sel.reshape(B, N_EXPERTS, S).transpose(0, 2, 1)
    g_tok = gsel.reshape(B, N_EXPERTS, S).transpose(0, 2, 1)
    return _combine_ln2(block_offsets(CMB_TOK), x1, pos_tok, g_tok, g2, ln2_g[l][None, :], ln2_b[l][None, :],
                        ye.reshape(N_EXPERTS * B * cap, D), cap)
```

```python
import functools
import math

import jax
import jax.numpy as jnp
from jax import lax
from jax.experimental import pallas as pl
from jax.experimental.pallas import tpu as pltpu

F32 = jnp.float32
BF16 = jnp.bfloat16
HIGHEST = lax.Precision.HIGHEST

LANES = 128
SUBLANES = 8
VMEM_LIMIT = 56 * 1024 * 1024

D_MODEL = 1024
MLA_HEADS = 8
NOPE = 64
ROPE = 32
V_DIM = 64
Q_RANK = 256
KV_RANK = 128
HY_W = 512
HY_COLS = 3 * HY_W
N_EXPERTS = 16
EC_FACTOR = 2
ROPE_THETA = 10000.0
EPS = 1e-5
DN_ALPHA = 2.0 ** 0.25
LOG2E = 1.4426950408889634

TOK_TILE = 512
ATT_TQ = 512
ATT_TK = 512
MOE_ROWS = 512


def _params(*sem):
    return pltpu.CompilerParams(dimension_semantics=sem, vmem_limit_bytes=VMEM_LIMIT)


def _ada_kernel(c_ref, w_ref, b_ref, o_ref):
    c = c_ref[...]
    sc = c * jax.nn.sigmoid(c)
    o_ref[...] = jnp.dot(sc, w_ref[...], precision=HIGHEST, preferred_element_type=F32) + b_ref[...]


def _ada_mod(c8, w_ada, b_ada):
    n = w_ada.shape[1]
    tn = 1024
    return pl.pallas_call(
        _ada_kernel,
        grid=(n // tn,),
        in_specs=[
            pl.BlockSpec((SUBLANES, D_MODEL), lambda j: (0, 0)),
            pl.BlockSpec((D_MODEL, tn), lambda j: (0, j)),
            pl.BlockSpec((1, tn), lambda j: (0, j)),
        ],
        out_specs=pl.BlockSpec((SUBLANES, tn), lambda j: (0, j)),
        out_shape=jax.ShapeDtypeStruct((SUBLANES, n), F32),
        compiler_params=_params("arbitrary"),
        name="ada_mod",
    )(c8, w_ada, b_ada)


def _rms(x, g):
    return x * lax.rsqrt(jnp.mean(x * x, axis=-1, keepdims=True) + EPS) * g


def _inproj_kernel(x_ref, sc_ref, sh_ref, pos_ref, freq_ref, win_ref, qg_ref, wq_ref, wqr_ref,
                   kvg_ref, wk_ref, wv_ref, q_ref, k_ref, v_ref, hy_ref):
    u = (x_ref[0] * (1.0 + sc_ref[0]) + sh_ref[0]).astype(BF16)
    ang = pos_ref[0].astype(F32) * freq_ref[...]
    cos = jnp.cos(ang)
    sin = jnp.sin(ang)

    def proj(lo, hi):
        return jnp.dot(u, win_ref[:, lo:hi], preferred_element_type=F32)

    cq = _rms(proj(0, Q_RANK), qg_ref[...]).astype(BF16)
    q_main = jnp.dot(cq, wq_ref[...], preferred_element_type=F32)
    q_rot = jnp.dot(cq, wqr_ref[...], preferred_element_type=F32)
    qscale = LOG2E * (NOPE + ROPE) ** -0.5
    cq_s = cos * qscale
    sq_s = sin * qscale
    for h in range(MLA_HEADS):
        sl = slice(h * LANES, (h + 1) * LANES)
        q_ref[0, h] = (q_main[:, sl] * cq_s + q_rot[:, sl] * sq_s).T.astype(BF16)

    o = Q_RANK
    ckv = _rms(proj(o, o + KV_RANK), kvg_ref[...]).astype(BF16)
    kr = proj(o + KV_RANK, o + 2 * KV_RANK) * cos + proj(o + 2 * KV_RANK, o + 3 * KV_RANK) * sin
    kn = jnp.dot(ckv, wk_ref[...], preferred_element_type=F32)
    for h in range(MLA_HEADS):
        k_ref[0, h] = (kn[:, h * LANES:(h + 1) * LANES] + kr).astype(BF16)
    v_ref[0] = jnp.dot(ckv, wv_ref[...], preferred_element_type=F32).T.astype(BF16)

    o = Q_RANK + 3 * KV_RANK
    hy_ref[0] = proj(o, o + HY_COLS)


def _inproj(x, sc1, sh1, pos3, freq_row, win_aug, qg, wq, wqr, kvg, wk, wv):
    B, S, D = x.shape
    tm = TOK_TILE
    ncol = win_aug.shape[1]
    full = lambda shape: pl.BlockSpec(shape, lambda b, i: (0,) * len(shape))
    return pl.pallas_call(
        _inproj_kernel,
        grid=(B, S // tm),
        in_specs=[
            pl.BlockSpec((1, tm, D), lambda b, i: (b, i, 0)),
            pl.BlockSpec((1, 1, D), lambda b, i: (b, 0, 0)),
            pl.BlockSpec((1, 1, D), lambda b, i: (b, 0, 0)),
            pl.BlockSpec((1, tm, 1), lambda b, i: (b, i, 0)),
            full((1, LANES)),
            full((D, ncol)),
            full((1, Q_RANK)),
            full((Q_RANK, MLA_HEADS * LANES)),
            full((Q_RANK, MLA_HEADS * LANES)),
            full((1, KV_RANK)),
            full((KV_RANK, MLA_HEADS * LANES)),
            full((KV_RANK, MLA_HEADS * V_DIM)),
        ],
        out_specs=[
            pl.BlockSpec((1, MLA_HEADS, LANES, tm), lambda b, i: (b, 0, 0, i)),
            pl.BlockSpec((1, MLA_HEADS, tm, LANES), lambda b, i: (b, 0, i, 0)),
            pl.BlockSpec((1, MLA_HEADS * V_DIM, tm), lambda b, i: (b, 0, i)),
            pl.BlockSpec((1, tm, HY_COLS), lambda b, i: (b, i, 0)),
        ],
        out_shape=[
            jax.ShapeDtypeStruct((B, MLA_HEADS, LANES, S), BF16),
            jax.ShapeDtypeStruct((B, MLA_HEADS, S, LANES), BF16),
            jax.ShapeDtypeStruct((B, MLA_HEADS * V_DIM, S), BF16),
            jax.ShapeDtypeStruct((B, S, HY_COLS), F32),
        ],
        compiler_params=_params("parallel", "parallel"),
        name="in_proj",
    )(x, sc1, sh1, pos3, freq_row, win_aug, qg, wq, wqr, kvg, wk, wv)


ATT_AUG = 16


def _attn_kernel(qt_ref, k_ref, vt_ref, o_ref, s_scr):
    S = k_ref.shape[2]
    tq = qt_ref.shape[3]
    nk = S // ATT_TK
    ones_rows = jnp.where(lax.broadcasted_iota(jnp.int32, (ATT_AUG, ATT_TK), 0) == 0, 1.0, 0.0).astype(BF16)

    def scores(c, slot):
        k0 = pl.multiple_of(c * ATT_TK, ATT_TK)
        for h in range(2):
            s_scr[slot, h] = jnp.dot(k_ref[0, h, pl.ds(k0, ATT_TK), :], qt_ref[0, h], preferred_element_type=F32)

    def absorb(c, slot, carry):
        k0 = pl.multiple_of(c * ATT_TK, ATT_TK)
        out = []
        for h in range(2):
            m_prev, acc = carry[h]
            s = s_scr[slot, h]
            m_new = jnp.maximum(m_prev, jnp.max(s, axis=0, keepdims=True))
            alpha = jnp.exp2(m_prev - m_new)
            p = jnp.exp2(s - m_new).astype(BF16)
            v_aug = jnp.concatenate([vt_ref[0, h * V_DIM:(h + 1) * V_DIM, pl.ds(k0, ATT_TK)], ones_rows], axis=0)
            out.append((m_new, alpha * acc + jnp.dot(v_aug, p, preferred_element_type=F32)))
        return tuple(out)

    def pair(i, carry):
        c0 = 2 * i
        scores(c0 + 1, 1)
        carry = absorb(c0, 0, carry)
        scores(jnp.minimum(c0 + 2, nk - 1), 0)
        return absorb(c0 + 1, 1, carry)

    scores(0, 0)
    init = tuple((jnp.full((1, tq), -jnp.inf, F32), jnp.zeros((V_DIM + ATT_AUG, tq), F32)) for _ in range(2))
    fin = lax.fori_loop(0, nk // 2, pair, init)
    o = jnp.concatenate([acc[:V_DIM] * (1.0 / acc[V_DIM:V_DIM + 1]) for (_, acc) in fin], axis=0)
    o_ref[0] = o.T


def _attention(qt, k, vt):
    B, H, _, S = qt.shape
    tq = ATT_TQ
    return pl.pallas_call(
        _attn_kernel,
        grid=(B, H // 2, S // tq),
        in_specs=[
            pl.BlockSpec((1, 2, LANES, tq), lambda b, hp, i: (b, hp, 0, i)),
            pl.BlockSpec((1, 2, S, LANES), lambda b, hp, i: (b, hp, 0, 0)),
            pl.BlockSpec((1, 2 * V_DIM, S), lambda b, hp, i: (b, hp, 0)),
        ],
        out_specs=pl.BlockSpec((1, tq, 2 * V_DIM), lambda b, hp, i: (b, i, hp)),
        out_shape=jax.ShapeDtypeStruct((B, S, H * V_DIM), F32),
        scratch_shapes=[pltpu.VMEM((2, 2, ATT_TK, tq), F32)],
        compiler_params=_params("parallel", "parallel", "parallel"),
        name="mla_attention",
    )(qt, k, vt)


def _layer_norm(x, g, b):
    mu = jnp.mean(x, axis=-1, keepdims=True)
    xc = x - mu
    var = jnp.mean(xc * xc, axis=-1, keepdims=True)
    return xc * lax.rsqrt(var + EPS) * g + b


def _outproj_kernel(a_ref, hy_ref, x_ref, g1_ref, sc2_ref, sh2_ref, ag_ref, hg_ref, wo_ref,
                    lg_ref, lb_ref, wr_ref, x1_ref, u2_ref, aff_ref):
    half = a_ref.shape[2]
    an = _rms(a_ref[0], ag_ref[...]).astype(BF16)
    hn = _rms(hy_ref[0], hg_ref[...]).astype(BF16)
    mixed = (jnp.dot(an, wo_ref[:half, :], preferred_element_type=F32)
             + jnp.dot(hn, wo_ref[half:, :], preferred_element_type=F32))
    x1 = _layer_norm(DN_ALPHA * x_ref[0] + g1_ref[0] * mixed, lg_ref[...], lb_ref[...])
    x1_ref[0] = x1
    u2 = x1 * (1.0 + sc2_ref[0]) + sh2_ref[0]
    u2_ref[0] = u2.astype(BF16)
    logits = jnp.dot(u2, wr_ref[...], precision=HIGHEST, preferred_element_type=F32)
    lane = lax.broadcasted_iota(jnp.int32, logits.shape, 1)
    logits = jnp.where(lane < N_EXPERTS, logits, -jnp.inf)
    e = jnp.exp(logits - jnp.max(logits, axis=-1, keepdims=True))
    aff = e / jnp.sum(e, axis=-1, keepdims=True)
    aff_ref[0] = aff.T[:N_EXPERTS, :]


def _outproj(a, hy, x, g1, sc2, sh2, ag, hg, wo, lg, lb, wr_pad):
    B, S, D = x.shape
    tm = TOK_TILE
    half = a.shape[2]
    full = lambda shape: pl.BlockSpec(shape, lambda b, i: (0,) * len(shape))
    tok = lambda w: pl.BlockSpec((1, tm, w), lambda b, i: (b, i, 0))
    per_b = pl.BlockSpec((1, 1, D), lambda b, i: (b, 0, 0))
    return pl.pallas_call(
        _outproj_kernel,
        grid=(B, S // tm),
        in_specs=[tok(half), tok(half), tok(D), per_b, per_b, per_b,
                  full((1, half)), full((1, half)), full((D, D)), full((1, D)), full((1, D)),
                  full((D, LANES))],
        out_specs=[tok(D), tok(D), pl.BlockSpec((1, N_EXPERTS, tm), lambda b, i: (b, 0, i))],
        out_shape=[
            jax.ShapeDtypeStruct((B, S, D), F32),
            jax.ShapeDtypeStruct((B, S, D), BF16),
            jax.ShapeDtypeStruct((B, N_EXPERTS, S), F32),
        ],
        compiler_params=_params("parallel", "parallel"),
        name="out_proj_ln1_router",
    )(a, hy, x, g1, sc2, sh2, ag, hg, wo, lg, lb, wr_pad)


def _select_kernel(aff_ref, ones_ref, utri_ref, bsum_ref, ltri_ref, pos_ref, posall_ref, g_ref, *, cap):
    aff = aff_ref[0]
    ones = ones_ref[...]

    def rowsum(mf):
        return jnp.dot(mf, ones, preferred_element_type=F32)

    def expert_total(mf):
        return jnp.dot(bsum_ref[...], rowsum(mf).astype(BF16), preferred_element_type=F32)

    def prefix(mf):
        return (jnp.dot(mf, utri_ref[...], preferred_element_type=F32)
                + jnp.dot(ltri_ref[...], rowsum(mf).astype(BF16), preferred_element_type=F32))

    def ind(m):
        return jnp.where(m, 1.0, 0.0).astype(BF16)

    def search(i, thr_bits):
        cand = thr_bits | jnp.left_shift(jnp.int32(1), 30 - i)
        enough = expert_total(ind(aff >= lax.bitcast_convert_type(cand, F32))) >= cap
        return jnp.where(enough, cand, thr_bits)

    thr = lax.bitcast_convert_type(lax.fori_loop(0, 31, search, jnp.zeros(aff.shape, jnp.int32)), F32)
    gt = aff > thr
    eq = aff == thr
    room = cap - expert_total(ind(gt))
    take_eq = jnp.where(eq, jnp.where(prefix(ind(eq)) < room, 1.0, 0.0), 0.0)
    sel = jnp.where(gt, 1.0, take_eq)
    pos = prefix(sel.astype(BF16)).astype(jnp.int32)
    posall_ref[0] = pos
    pos_ref[0] = jnp.where(sel > 0.0, pos, -1)
    g_ref[0] = jnp.where(sel > 0.0, aff, 0.0)


def _select(aff2, cap):
    B, R, _ = aff2.shape
    rpe = R // N_EXPERTS
    li = jnp.arange(LANES, dtype=jnp.int32)
    ri = jnp.arange(R, dtype=jnp.int32)
    same = (ri[:, None] // rpe) == (ri[None, :] // rpe)
    ones = jnp.ones((LANES, LANES), BF16)
    utri = (li[:, None] < li[None, :]).astype(BF16)
    bsum = same.astype(BF16)
    ltri = (same & (ri[None, :] < ri[:, None])).astype(BF16)
    blk = pl.BlockSpec((1, R, LANES), lambda b: (b, 0, 0))
    full = lambda arr: pl.BlockSpec(arr.shape, lambda b: (0, 0))
    return pl.pallas_call(
        functools.partial(_select_kernel, cap=cap),
        grid=(B,),
        in_specs=[blk, full(ones), full(utri), full(bsum), full(ltri)],
        out_specs=[blk, blk, blk],
        out_shape=[jax.ShapeDtypeStruct((B, R, LANES), jnp.int32), jax.ShapeDtypeStruct((B, R, LANES), jnp.int32),
                   jax.ShapeDtypeStruct((B, R, LANES), F32)],
        compiler_params=_params("parallel"),
        name="expert_choice_select",
    )(aff2, ones, utri, bsum, ltri)


GATHER_TOK = 1024
GATHER_WIN = 256


def _gather_kernel(offs_ref, u_ref, pos_ref, xe_ref, acc, *, cap, nblk):
    b, e = pl.program_id(0), pl.program_id(1)
    base = (b * N_EXPERTS + e) * (nblk + 1)
    rows = GATHER_TOK // LANES
    acc[...] = jnp.zeros(acc.shape, F32)
    slot = lax.broadcasted_iota(jnp.int32, (GATHER_WIN, LANES), 0)

    def block(j, carry):
        off, end = offs_ref[base + j], offs_ref[base + j + 1]
        w = (off // SUBLANES) * SUBLANES
        nsub = jnp.where(end > off, (end - w + GATHER_WIN - 1) // GATHER_WIN, 0)
        prow = pos_ref[0, pl.ds(pl.multiple_of(j * rows, rows), rows), :]
        ublk = u_ref[0, pl.ds(pl.multiple_of(j * GATHER_TOK, GATHER_TOK), GATHER_TOK), :]

        def window(i, c):
            s0 = pl.multiple_of(w + i * GATHER_WIN, SUBLANES)
            onehot = jnp.concatenate(
                [jnp.where(slot == prow[r:r + 1, :] - s0, 1.0, 0.0).astype(BF16) for r in range(rows)], axis=1)
            acc[pl.ds(s0, GATHER_WIN), :] += jnp.dot(onehot, ublk, preferred_element_type=F32)
            return c

        lax.fori_loop(0, nsub, window, 0)
        return carry

    lax.fori_loop(0, nblk, block, 0)
    xe_ref[0] = acc[:cap, :].astype(BF16)


def _gather(offs, u2, pos_sel, cap):
    B, S, D = u2.shape
    rpe = S // LANES
    nblk = S // GATHER_TOK
    return pl.pallas_call(
        functools.partial(_gather_kernel, cap=cap, nblk=nblk),
        grid_spec=pltpu.PrefetchScalarGridSpec(
            num_scalar_prefetch=1,
            grid=(B, N_EXPERTS),
            in_specs=[pl.BlockSpec((1, S, D), lambda b, e, o: (b, 0, 0)),
                      pl.BlockSpec((1, rpe, LANES), lambda b, e, o: (b, e, 0))],
            out_specs=pl.BlockSpec((1, cap, D), lambda b, e, o: (e, b, 0)),
            scratch_shapes=[pltpu.VMEM((cap + GATHER_WIN + SUBLANES, D), F32)],
        ),
        out_shape=jax.ShapeDtypeStruct((N_EXPERTS, B * cap, D), BF16),
        compiler_params=_params("parallel", "parallel"),
        name="expert_gather",
    )(offs, u2, pos_sel)


def _moe_kernel(x_ref, wg_ref, wu_ref, wd_ref, o_ref, wgb, wub, wdb):
    @pl.when(pl.program_id(1) == 0)
    def _():
        wgb[...] = wg_ref[0].astype(BF16)
        wub[...] = wu_ref[0].astype(BF16)
        wdb[...] = wd_ref[0].astype(BF16)

    x = x_ref[0]
    hg = jnp.dot(x, wgb[...], preferred_element_type=F32)
    hu = jnp.dot(x, wub[...], preferred_element_type=F32)
    h = (hg * jax.nn.sigmoid(hg) * hu).astype(BF16)
    o_ref[0] = jnp.dot(h, wdb[...], preferred_element_type=F32).astype(BF16)


def _moe_ffn(xe, w_gate, w_up, w_down):
    E, R, D = xe.shape
    Fh = w_gate.shape[2]
    tr = MOE_ROWS
    return pl.pallas_call(
        _moe_kernel,
        grid=(E, R // tr),
        in_specs=[
            pl.BlockSpec((1, tr, D), lambda e, r: (e, r, 0)),
            pl.BlockSpec((1, D, Fh), lambda e, r: (e, 0, 0)),
            pl.BlockSpec((1, D, Fh), lambda e, r: (e, 0, 0)),
            pl.BlockSpec((1, Fh, D), lambda e, r: (e, 0, 0)),
        ],
        out_specs=pl.BlockSpec((1, tr, D), lambda e, r: (e, r, 0)),
        out_shape=jax.ShapeDtypeStruct((E, R, D), BF16),
        scratch_shapes=[pltpu.VMEM((D, Fh), BF16), pltpu.VMEM((D, Fh), BF16), pltpu.VMEM((Fh, D), BF16)],
        compiler_params=_params("parallel", "arbitrary"),
        name="moe_swiglu",
    )(xe, w_gate, w_up, w_down)


CMB_TOK = 512
CMB_WIN = 128
BF16_ROWS = 16


def _combine_kernel(offs_ref, x1_ref, pos_ref, g_ref, g2_ref, lg_ref, lb_ref, ye_hbm, o_ref,
                    rhs, lhs, acc, sem, *, cap, nblk, nbatch):
    b, i = pl.program_id(0), pl.program_id(1)
    W = CMB_WIN
    total = ye_hbm.shape[0]
    row_base, first, npass = [], [], jnp.int32(0)
    for e in range(N_EXPERTS):
        k = (b * N_EXPERTS + e) * (nblk + 1) + i
        off, end = offs_ref[k], offs_ref[k + 1]
        lo = (off // BF16_ROWS) * BF16_ROWS
        row_base.append((e * nbatch + b) * cap)
        first.append(lo)
        npass = jnp.maximum(npass, jnp.where(end > off, (end - lo + W - 1) // W, 0))
    acc[...] = jnp.zeros(acc.shape, F32)
    lane_e = lax.broadcasted_iota(jnp.int32, (1, N_EXPERTS), 1)
    lane_w = lax.broadcasted_iota(jnp.int32, (CMB_TOK, W), 1)

    def window_copy(e, start):
        return pltpu.make_async_copy(ye_hbm.at[pl.ds(pl.multiple_of(start, BF16_ROWS), W), :],
                                     rhs.at[pl.ds(e * W, W), :], sem.at[e])

    def one_pass(p, carry):
        lo_vec = jnp.zeros((1, N_EXPERTS), jnp.int32)
        shift_vec = jnp.zeros((1, N_EXPERTS), jnp.int32)
        starts = []
        for e in range(N_EXPERTS):
            lo = first[e] + p * W
            start = jnp.minimum(row_base[e] + lo, total - W)
            starts.append(start)
            lo_vec = jnp.where(lane_e == e, lo, lo_vec)
            shift_vec = jnp.where(lane_e == e, row_base[e] + lo - start, shift_vec)
        for e in range(N_EXPERTS):
            window_copy(e, starts[e]).start()
        d = pos_ref[0] - lo_vec
        rel = jnp.where((d >= 0) & (d < W), d + shift_vec, -1)
        g = g_ref[0]
        for e in range(N_EXPERTS):
            lhs[:, e * W:(e + 1) * W] = jnp.where(lane_w == rel[:, e:e + 1], g[:, e:e + 1], 0.0).astype(BF16)
        for e in range(N_EXPERTS):
            window_copy(e, starts[e]).wait()
        acc[...] += jnp.dot(lhs[...], rhs[...], preferred_element_type=F32)
        return carry

    lax.fori_loop(0, npass, one_pass, 0)
    o_ref[0] = _layer_norm(DN_ALPHA * x1_ref[0] + g2_ref[0] * acc[...], lg_ref[...], lb_ref[...])


def _combine_ln2(offs, x1, pos_tok, g_tok, g2, lg, lb, ye_rows, cap):
    B, S, D = x1.shape
    T = CMB_TOK
    nblk = S // T
    tok = lambda w: pl.BlockSpec((1, T, w), lambda b, i, o: (b, i, 0))
    vec = pl.BlockSpec((1, D), lambda b, i, o: (0, 0))
    return pl.pallas_call(
        functools.partial(_combine_kernel, cap=cap, nblk=nblk, nbatch=B),
        grid_spec=pltpu.PrefetchScalarGridSpec(
            num_scalar_prefetch=1,
            grid=(B, nblk),
            in_specs=[tok(D), tok(N_EXPERTS), tok(N_EXPERTS),
                      pl.BlockSpec((1, 1, D), lambda b, i, o: (b, 0, 0)), vec, vec,
                      pl.BlockSpec(memory_space=pl.ANY)],
            out_specs=tok(D),
            scratch_shapes=[pltpu.VMEM((N_EXPERTS * CMB_WIN, D), BF16),
                            pltpu.VMEM((T, N_EXPERTS * CMB_WIN), BF16),
                            pltpu.VMEM((T, D), F32),
                            pltpu.SemaphoreType.DMA((N_EXPERTS,))],
        ),
        out_shape=jax.ShapeDtypeStruct((B, S, D), F32),
        compiler_params=_params("parallel", "parallel"),
        name="combine_residual_ln2",
    )(offs, x1, pos_tok, g_tok, g2, lg, lb, ye_rows)


DFT_R = 128
K1_HALF = 40
K1_VALID = DFT_R // 2 + 1
K1_COUNTS = (K1_HALF, K1_VALID - K1_HALF)
HY_CT = 128


def _dft_tables(S):
    N = 2 * S
    na = S // DFT_R
    r = jnp.arange(2 * K1_HALF, dtype=jnp.int32)
    k1 = (jnp.arange(2, dtype=jnp.int32)[:, None] * K1_HALF + r[None, :] % K1_HALF)
    valid = (k1 < K1_VALID).astype(F32)
    n = DFT_R * jnp.arange(na, dtype=jnp.int32)[None, :] + jnp.arange(DFT_R, dtype=jnp.int32)[:, None]
    prod = (n[None, :, None, :] * k1[:, None, :, None]) % N
    theta = prod.astype(F32) * (2.0 * math.pi / N)
    is_im = (r >= K1_HALF)[None, None, :, None]
    g = jnp.where(is_im, -jnp.sin(theta), jnp.cos(theta)) * valid[:, None, :, None]
    cw = jnp.where((k1 == 0) | (k1 == DFT_R // 2), 1.0, 2.0) * valid / N
    gi = jnp.transpose(g * cw[:, None, :, None], (0, 1, 3, 2))
    bk = (jnp.arange(DFT_R, dtype=jnp.int32)[:, None] * jnp.arange(DFT_R, dtype=jnp.int32)[None, :]) % DFT_R
    ph = bk.astype(F32) * (2.0 * math.pi / DFT_R)
    C, Sn = jnp.cos(ph), jnp.sin(ph)
    f3f = jnp.block([[C, Sn], [-Sn, C]])
    f3i = jnp.block([[C, -Sn], [Sn, C]])
    return g.astype(BF16), gi.astype(BF16), f3f.astype(BF16), f3i.astype(BF16)


def _dft_stage1(load_rows, gf_ref, hf, a_scr):
    rows = 2 * K1_HALF

    def body(b, carry):
        zb = load_rows(b).astype(BF16)
        a_scr[pl.ds(pl.multiple_of(b * rows, SUBLANES), rows), :] = jnp.dot(
            gf_ref[hf, b], zb, preferred_element_type=F32)
        return carry

    lax.fori_loop(0, DFT_R, body, 0, unroll=8)


def _dft_stage3_operand(a_scr, j):
    rows = 2 * K1_HALF
    mr = a_scr[pl.ds(j, DFT_R, stride=rows), :]
    mi = a_scr[pl.ds(K1_HALF + j, DFT_R, stride=rows), :]
    return jnp.concatenate([mr, mi], axis=0).astype(BF16)


def _filter_mlp_kernel(feat_ref, w1_ref, b1_ref, fr_ref, w2_ref, b2_ref, w3_ref, b3_ref, dec_ref, o_ref):
    feats = feat_ref[...]
    fr = fr_ref[...]
    h = jnp.sin(fr * (jnp.dot(feats, w1_ref[...], precision=HIGHEST, preferred_element_type=F32) + b1_ref[...]))
    h = jnp.sin(fr * (jnp.dot(h, w2_ref[...], precision=HIGHEST, preferred_element_type=F32) + b2_ref[...]))
    h = jnp.dot(h, w3_ref[...], precision=HIGHEST, preferred_element_type=F32) + b3_ref[...]
    h = h * jnp.exp(-feats[:, 0:1] * jnp.abs(dec_ref[...]))
    row = lax.broadcasted_iota(jnp.int32, h.shape, 0) + pl.program_id(0) * h.shape[0]
    col = lax.broadcasted_iota(jnp.int32, h.shape, 1)
    o_ref[...] = jnp.where((row == 0) & (col >= h.shape[1] // 2), 0.0, h)


def _filter_taps(L, w1, b1, freq, w2, b2, w3, b3, decay):
    pos = jnp.arange(L, dtype=F32)
    t = pos[:, None] / (L - 1)
    bands = (w1.shape[0] - 1) // 2
    freqs = jnp.linspace(1e-4, bands - 1, bands, dtype=F32)
    phase = (2.0 * math.pi / L) * pos[:, None] * freqs[None, :]
    nf = w1.shape[0]
    feats = jnp.concatenate([t, jnp.cos(phase), -jnp.sin(phase), jnp.zeros((L, LANES - nf), F32)], axis=-1)
    w1p = jnp.concatenate([w1, jnp.zeros((LANES - nf, w1.shape[1]), F32)], axis=0)
    fo = w1.shape[1]
    ncol = w3.shape[1]
    tl = 512
    full = lambda shape: pl.BlockSpec(shape, lambda i: (0,) * len(shape))
    return pl.pallas_call(
        _filter_mlp_kernel,
        grid=(L // tl,),
        in_specs=[pl.BlockSpec((tl, LANES), lambda i: (i, 0)), full((LANES, fo)), full((1, fo)), full((1, fo)),
                  full((fo, fo)), full((1, fo)), full((fo, ncol)), full((1, ncol)), full((1, ncol))],
        out_specs=pl.BlockSpec((tl, ncol), lambda i: (i, 0)),
        out_shape=jax.ShapeDtypeStruct((L, ncol), F32),
        compiler_params=_params("parallel"),
        name="hyena_filter_mlp",
    )(feats, w1p, b1[None, :], freq[None, :], w2, b2[None, :], w3, b3[None, :], decay.reshape(1, ncol))


def _filter_spectrum_kernel(hf_ref, hb_ref, gf_ref, f3f_ref, o_ref, a_scr, x_scr):
    na = hf_ref.shape[0] // DFT_R
    for t, src in enumerate((hf_ref, hb_ref)):
        for hf in range(2):
            _dft_stage1(lambda b: src[pl.ds(b, na, stride=DFT_R), :], gf_ref, hf, a_scr)

            def body(j, carry):
                x = jnp.dot(f3f_ref[...], _dft_stage3_operand(a_scr, j), preferred_element_type=F32)
                k1 = hf * K1_HALF + j
                if t == 0:
                    x_scr[k1] = x
                else:
                    xf = x_scr[k1]
                    o_ref[0, k1] = jnp.concatenate([xf[:DFT_R] + x[:DFT_R], xf[DFT_R:] - x[DFT_R:]],
                                                   axis=0).astype(BF16)
                return carry

            lax.fori_loop(0, K1_COUNTS[hf], body, 0, unroll=5)


def _filter_spectrum(taps, gf, f3f):
    L, ncol = taps.shape
    C = ncol // 4
    nct = C // HY_CT
    ct = HY_CT
    return pl.pallas_call(
        _filter_spectrum_kernel,
        grid=(2, nct),
        in_specs=[
            pl.BlockSpec((L, ct), lambda o, j: (0, o * nct + j)),
            pl.BlockSpec((L, ct), lambda o, j: (0, 2 * nct + o * nct + j)),
            pl.BlockSpec(gf.shape, lambda o, j: (0, 0, 0, 0), pipeline_mode=pl.Buffered(1)),
            pl.BlockSpec(f3f.shape, lambda o, j: (0, 0), pipeline_mode=pl.Buffered(1)),
        ],
        out_specs=pl.BlockSpec((1, K1_VALID, 2 * DFT_R, ct), lambda o, j: (o, 0, 0, j)),
        out_shape=jax.ShapeDtypeStruct((2, K1_VALID, 2 * DFT_R, C), BF16),
        scratch_shapes=[pltpu.VMEM((DFT_R * 2 * K1_HALF, ct), F32), pltpu.VMEM((K1_VALID, 2 * DFT_R, ct), F32)],
        compiler_params=_params("parallel", "parallel"),
        name="hyena_filter_spectrum",
    )(taps, taps, gf, f3f)


def _short_conv_kernel(x_ref, w_ref, b_ref, o_ref):
    S = x_ref.shape[1]
    R = 256
    nchunk = S // R
    w0, w1, w2 = w_ref[0:1, :], w_ref[1:2, :], w_ref[2:3, :]
    bias = b_ref[...]
    row = lax.broadcasted_iota(jnp.int32, (R, x_ref.shape[2]), 0)

    def body(i, carry):
        r0 = pl.multiple_of(i * R, R)
        x = x_ref[0, pl.ds(r0, R), :]
        prev = x_ref[0, pl.ds(pl.multiple_of(jnp.maximum(r0 - SUBLANES, 0), SUBLANES), SUBLANES), :]
        nxt = x_ref[0, pl.ds(pl.multiple_of(jnp.minimum(r0 + R, S - SUBLANES), SUBLANES), SUBLANES), :]
        prev = jnp.where(i == 0, 0.0, prev[SUBLANES - 1:SUBLANES, :])
        nxt = jnp.where(i == nchunk - 1, 0.0, nxt[0:1, :])
        up = jnp.where(row == 0, prev, pltpu.roll(x, 1, 0))
        dn = jnp.where(row == R - 1, nxt, pltpu.roll(x, R - 1, 0))
        o_ref[0, pl.ds(r0, R), :] = bias + w0 * up + w1 * x + w2 * dn
        return carry

    lax.fori_loop(0, nchunk, body, 0)


def _short_conv(hy_raw, conv_w, conv_b):
    B, S, C = hy_raw.shape
    ct = LANES
    return pl.pallas_call(
        _short_conv_kernel,
        grid=(B, C // ct),
        in_specs=[pl.BlockSpec((1, S, ct), lambda b, j: (b, 0, j)),
                  pl.BlockSpec((conv_w.shape[0], ct), lambda b, j: (0, j)),
                  pl.BlockSpec((1, ct), lambda b, j: (0, j))],
        out_specs=pl.BlockSpec((1, S, ct), lambda b, j: (b, 0, j)),
        out_shape=jax.ShapeDtypeStruct((B, S, C), F32),
        compiler_params=_params("parallel", "parallel"),
        name="hyena_short_conv",
    )(hy_raw, conv_w, conv_b[None, :])


def _long_conv_kernel(z_ref, xg_ref, h_ref, bias_ref, gf_ref, f3f_ref, f3i_ref, gi_ref, o_ref, a_scr, b_scr):
    na = z_ref.shape[1] // DFT_R
    rows_b = lambda ref, b: ref[0, pl.ds(b, na, stride=DFT_R), :]
    bias = bias_ref[0]
    for hf in range(2):
        _dft_stage1(lambda b: rows_b(z_ref, b), gf_ref, hf, a_scr)
        if K1_COUNTS[hf] < K1_HALF:
            pad0 = K1_COUNTS[hf] * 2 * DFT_R
            b_scr[pad0:, :] = jnp.zeros((b_scr.shape[0] - pad0, b_scr.shape[1]), F32)

        def spectral(j, carry):
            x = jnp.dot(f3f_ref[...], _dft_stage3_operand(a_scr, j), preferred_element_type=F32)
            h = h_ref[0, hf * K1_HALF + j].astype(F32)
            xr, xi, hr, hi = x[:DFT_R], x[DFT_R:], h[:DFT_R], h[DFT_R:]
            y = jnp.concatenate([xr * hr - xi * hi, xr * hi + xi * hr], axis=0).astype(BF16)
            b_scr[pl.ds(pl.multiple_of(j * 2 * DFT_R, 2 * DFT_R), 2 * DFT_R), :] = jnp.dot(
                f3i_ref[...], y, preferred_element_type=F32)
            return carry

        lax.fori_loop(0, K1_COUNTS[hf], spectral, 0, unroll=5)

        def inverse(b, carry):
            br = b_scr[pl.ds(b, K1_HALF, stride=2 * DFT_R), :]
            bi = b_scr[pl.ds(DFT_R + b, K1_HALF, stride=2 * DFT_R), :]
            bm = jnp.concatenate([br, bi], axis=0).astype(BF16)
            yb = jnp.dot(gi_ref[hf, b], bm, preferred_element_type=F32)
            if hf == 0:
                o_ref[0, pl.ds(b, na, stride=DFT_R), :] = yb
            else:
                y = rows_b(o_ref, b) + yb
                o_ref[0, pl.ds(b, na, stride=DFT_R), :] = rows_b(xg_ref, b) * (y + rows_b(z_ref, b) * bias)
            return carry

        lax.fori_loop(0, DFT_R, inverse, 0, unroll=8)


def _long_conv(z_arr, z_off, xg_arr, xg_off, spec, order, bias3, gf, f3f, f3i, gi):
    B, S, _ = z_arr.shape
    C = spec.shape[3]
    ct = HY_CT
    const = lambda arr: pl.BlockSpec(arr.shape, lambda j, b: (0,) * arr.ndim, pipeline_mode=pl.Buffered(1))
    return pl.pallas_call(
        _long_conv_kernel,
        grid=(C // ct, B),
        in_specs=[
            pl.BlockSpec((1, S, ct), lambda j, b: (b, 0, z_off + j)),
            pl.BlockSpec((1, S, ct), lambda j, b: (b, 0, xg_off + j)),
            pl.BlockSpec((1, K1_VALID, 2 * DFT_R, ct), lambda j, b: (order, 0, 0, j), pipeline_mode=pl.Buffered(1)),
            pl.BlockSpec((1, 1, ct), lambda j, b: (order, 0, j)),
            const(gf), const(f3f), const(f3i), const(gi),
        ],
        out_specs=pl.BlockSpec((1, S, ct), lambda j, b: (b, 0, j)),
        out_shape=jax.ShapeDtypeStruct((B, S, C), F32),
        scratch_shapes=[pltpu.VMEM((DFT_R * 2 * K1_HALF, ct), F32), pltpu.VMEM((K1_HALF * 2 * DFT_R, ct), F32)],
        compiler_params=_params("parallel", "parallel"),
        name="hyena_long_conv",
    )(z_arr, xg_arr, spec, bias3, gf, f3f, f3i, gi)


def _hyena_group(hy_raw, conv_w, conv_b, fw1, fb1, ffreq, fw2, fb2, fw3, fb3, decay, hyena_bias):
    B, S, _ = hy_raw.shape
    gf, gi, f3f, f3i = _dft_tables(S)
    taps = _filter_taps(S, fw1, fb1, ffreq, fw2, fb2, fw3, fb3, decay)
    spec = _filter_spectrum(taps, gf, f3f)
    u = _short_conv(hy_raw, conv_w, conv_b)
    nct = HY_W // HY_CT
    bias3 = hyena_bias[:, None, :]
    z1 = _long_conv(u, 0, u, nct, spec, 0, bias3, gf, f3f, f3i, gi)
    return _long_conv(z1, 0, u, 2 * nct, spec, 1, bias3, gf, f3f, f3i, gi)


def _rope_rot_cols(w):
    half = ROPE // 2
    return jnp.concatenate([-w[..., half:], w[..., :half]], axis=-1)


def _prep_weights(w_in, w_qb, w_kvb):
    D = w_in.shape[0]
    o = Q_RANK + KV_RANK
    w_kr = w_in[:, o:o + ROPE]
    z64 = jnp.zeros((D, NOPE), F32)
    z32 = jnp.zeros((D, LANES - NOPE - ROPE), F32)
    win_aug = jnp.concatenate([
        w_in[:, :o],
        z64, w_kr, z32,
        z64, _rope_rot_cols(w_kr), z32,
        w_in[:, o + ROPE:],
    ], axis=1).astype(BF16)
    wq = w_qb.reshape(Q_RANK, MLA_HEADS, NOPE + ROPE)
    zq = jnp.zeros((Q_RANK, MLA_HEADS, LANES - NOPE - ROPE), F32)
    wq_main = jnp.concatenate([wq, zq], axis=-1).reshape(Q_RANK, MLA_HEADS * LANES).astype(BF16)
    wq_rot = jnp.concatenate([jnp.zeros((Q_RANK, MLA_HEADS, NOPE), F32), _rope_rot_cols(wq[..., NOPE:]), zq],
                             axis=-1).reshape(Q_RANK, MLA_HEADS * LANES).astype(BF16)
    wkv = w_kvb.reshape(KV_RANK, MLA_HEADS, NOPE + V_DIM)
    wk = jnp.concatenate([wkv[..., :NOPE], jnp.zeros((KV_RANK, MLA_HEADS, LANES - NOPE), F32)],
                         axis=-1).reshape(KV_RANK, MLA_HEADS * LANES).astype(BF16)
    wv = wkv[..., NOPE:].reshape(KV_RANK, MLA_HEADS * V_DIM).astype(BF16)
    return win_aug, wq_main, wq_rot, wk, wv


def kernel(x, c, positions, w_ada, b_ada, w_in, q_norm_g, w_qb, kv_norm_g, w_kvb, conv_w, conv_b, filt_w1, filt_b1, filt_freq, filt_w2, filt_b2, filt_w3, filt_b3, hyena_decay, hyena_bias, attn_out_g, hyena_out_g, w_o, ln1_g, ln1_b, w_router, w_gate, w_up, w_down, ln2_g, ln2_b):
    B, S, D = x.shape
    l = 0
    c8 = jnp.zeros((SUBLANES, D), F32).at[:B].set(c)
    mod = _ada_mod(c8, w_ada[l], b_ada[l][None, :])[:B]
    sh1, sc1, g1, sh2, sc2, g2 = [m[:, None, :] for m in jnp.split(mod, 6, axis=-1)]

    win_aug, wq_main, wq_rot, wk, wv = _prep_weights(w_in[l], w_qb[l], w_kvb[l])
    half = ROPE // 2
    inv_freq = ROPE_THETA ** (-jnp.arange(half, dtype=F32) / half)
    freq_row = jnp.concatenate([jnp.zeros((NOPE,), F32), inv_freq, inv_freq,
                                jnp.zeros((LANES - NOPE - ROPE,), F32)])[None, :]
    q, kt, v, hy_raw = _inproj(x, sc1, sh1, positions[:, :, None], freq_row, win_aug,
                               q_norm_g[l][None, :], wq_main, wq_rot, kv_norm_g[l][None, :], wk, wv)
    a = _attention(q, kt, v)

    hy = _hyena_group(hy_raw, conv_w[l], conv_b[l], filt_w1[l], filt_b1[l], filt_freq[l], filt_w2[l],
                      filt_b2[l], filt_w3[l], filt_b3[l], hyena_decay[l], hyena_bias[l])

    wr_pad = jnp.concatenate([w_router[l], jnp.zeros((D, LANES - N_EXPERTS), F32)], axis=1)
    x1, u2, aff_t = _outproj(a, hy, x, g1, sc2, sh2, attn_out_g[l][None, :], hyena_out_g[l][None, :],
                             w_o[l].astype(BF16), ln1_g[l][None, :], ln1_b[l][None, :], wr_pad)

    cap = EC_FACTOR * S // N_EXPERTS
    rpe = S // LANES
    aff2 = aff_t.reshape(B, N_EXPERTS * rpe, LANES)
    pos_sel, pos_all, gsel = _select(aff2, cap)
    row_start = pos_all.reshape(B, N_EXPERTS, rpe, LANES)[:, :, :, 0]

    def block_offsets(tokens):
        ends = jnp.full((B, N_EXPERTS, 1), cap, jnp.int32)
        return jnp.concatenate([row_start[:, :, ::tokens // LANES], ends], axis=-1).reshape(-1)

    xe = _gather(block_offsets(GATHER_TOK), u2, pos_sel, cap)
    ye = _moe_ffn(xe, w_gate[l], w_up[l], w_down[l])
    pos_tok = pos_sel.reshape(B, N_EXPERTS, S).transpose(0, 2, 1)
    g_tok = gsel.reshape(B, N_EXPERTS, S).transpose(0, 2, 1)
    return _combine_ln2(block_offsets(CMB_TOK), x1, pos_tok, g_tok, g2, ln2_g[l][None, :], ln2_b[l][None, :],
                        ye.reshape(N_EXPERTS * B * cap, D), cap)
```

```python
import functools
import math

import jax
import jax.numpy as jnp
from jax import lax
from jax.experimental import pallas as pl
from jax.experimental.pallas import tpu as pltpu

F32 = jnp.float32
BF16 = jnp.bfloat16
HIGHEST = lax.Precision.HIGHEST

LANES = 128
SUBLANES = 8
VMEM_LIMIT = 56 * 1024 * 1024

D_MODEL = 1024
MLA_HEADS = 8
NOPE = 64
ROPE = 32
V_DIM = 64
Q_RANK = 256
KV_RANK = 128
HY_W = 512
HY_COLS = 3 * HY_W
N_EXPERTS = 16
EC_FACTOR = 2
ROPE_THETA = 10000.0
EPS = 1e-5
DN_ALPHA = 2.0 ** 0.25
LOG2E = 1.4426950408889634

TOK_TILE = 512
ATT_TQ = 512
ATT_TK = 512
MOE_ROWS = 512


def _params(*sem):
    return pltpu.CompilerParams(dimension_semantics=sem, vmem_limit_bytes=VMEM_LIMIT)


def _ada_kernel(c_ref, w_ref, b_ref, o_ref):
    c = c_ref[...]
    sc = c * jax.nn.sigmoid(c)
    o_ref[...] = jnp.dot(sc, w_ref[...], precision=HIGHEST, preferred_element_type=F32) + b_ref[...]


def _ada_mod(c8, w_ada, b_ada):
    n = w_ada.shape[1]
    tn = 1024
    return pl.pallas_call(
        _ada_kernel,
        grid=(n // tn,),
        in_specs=[
            pl.BlockSpec((SUBLANES, D_MODEL), lambda j: (0, 0)),
            pl.BlockSpec((D_MODEL, tn), lambda j: (0, j)),
            pl.BlockSpec((1, tn), lambda j: (0, j)),
        ],
        out_specs=pl.BlockSpec((SUBLANES, tn), lambda j: (0, j)),
        out_shape=jax.ShapeDtypeStruct((SUBLANES, n), F32),
        compiler_params=_params("arbitrary"),
        name="ada_mod",
    )(c8, w_ada, b_ada)


def _rms(x, g):
    return x * lax.rsqrt(jnp.mean(x * x, axis=-1, keepdims=True) + EPS) * g


def _inproj_kernel(x_ref, sc_ref, sh_ref, pos_ref, freq_ref, win_ref, qg_ref, wq_ref, wqr_ref,
                   kvg_ref, wk_ref, wv_ref, q_ref, k_ref, v_ref, hy_ref):
    u = (x_ref[0] * (1.0 + sc_ref[0]) + sh_ref[0]).astype(BF16)
    ang = pos_ref[0].astype(F32) * freq_ref[...]
    cos = jnp.cos(ang)
    sin = jnp.sin(ang)

    def proj(lo, hi):
        return jnp.dot(u, win_ref[:, lo:hi], preferred_element_type=F32)

    cq = _rms(proj(0, Q_RANK), qg_ref[...]).astype(BF16)
    q_main = jnp.dot(cq, wq_ref[...], preferred_element_type=F32)
    q_rot = jnp.dot(cq, wqr_ref[...], preferred_element_type=F32)
    qscale = LOG2E * (NOPE + ROPE) ** -0.5
    cq_s = cos * qscale
    sq_s = sin * qscale
    for h in range(MLA_HEADS):
        sl = slice(h * LANES, (h + 1) * LANES)
        q_ref[0, h] = (q_main[:, sl] * cq_s + q_rot[:, sl] * sq_s).T.astype(BF16)

    o = Q_RANK
    ckv = _rms(proj(o, o + KV_RANK), kvg_ref[...]).astype(BF16)
    kr = proj(o + KV_RANK, o + 2 * KV_RANK) * cos + proj(o + 2 * KV_RANK, o + 3 * KV_RANK) * sin
    kn = jnp.dot(ckv, wk_ref[...], preferred_element_type=F32)
    for h in range(MLA_HEADS):
        k_ref[0, h] = (kn[:, h * LANES:(h + 1) * LANES] + kr).astype(BF16)
    v_ref[0] = jnp.dot(ckv, wv_ref[...], preferred_element_type=F32).T.astype(BF16)

    o = Q_RANK + 3 * KV_RANK
    hy_ref[0] = proj(o, o + HY_COLS)


def _inproj(x, sc1, sh1, pos3, freq_row, win_aug, qg, wq, wqr, kvg, wk, wv):
    B, S, D = x.shape
    tm = TOK_TILE
    ncol = win_aug.shape[1]
    full = lambda shape: pl.BlockSpec(shape, lambda b, i: (0,) * len(shape))
    return pl.pallas_call(
        _inproj_kernel,
        grid=(B, S // tm),
        in_specs=[
            pl.BlockSpec((1, tm, D), lambda b, i: (b, i, 0)),
            pl.BlockSpec((1, 1, D), lambda b, i: (b, 0, 0)),
            pl.BlockSpec((1, 1, D), lambda b, i: (b, 0, 0)),
            pl.BlockSpec((1, tm, 1), lambda b, i: (b, i, 0)),
            full((1, LANES)),
            full((D, ncol)),
            full((1, Q_RANK)),
            full((Q_RANK, MLA_HEADS * LANES)),
            full((Q_RANK, MLA_HEADS * LANES)),
            full((1, KV_RANK)),
            full((KV_RANK, MLA_HEADS * LANES)),
            full((KV_RANK, MLA_HEADS * V_DIM)),
        ],
        out_specs=[
            pl.BlockSpec((1, MLA_HEADS, LANES, tm), lambda b, i: (b, 0, 0, i)),
            pl.BlockSpec((1, MLA_HEADS, tm, LANES), lambda b, i: (b, 0, i, 0)),
            pl.BlockSpec((1, MLA_HEADS * V_DIM, tm), lambda b, i: (b, 0, i)),
            pl.BlockSpec((1, tm, HY_COLS), lambda b, i: (b, i, 0)),
        ],
        out_shape=[
            jax.ShapeDtypeStruct((B, MLA_HEADS, LANES, S), BF16),
            jax.ShapeDtypeStruct((B, MLA_HEADS, S, LANES), BF16),
            jax.ShapeDtypeStruct((B, MLA_HEADS * V_DIM, S), BF16),
            jax.ShapeDtypeStruct((B, S, HY_COLS), F32),
        ],
        compiler_params=_params("parallel", "parallel"),
        name="in_proj",
    )(x, sc1, sh1, pos3, freq_row, win_aug, qg, wq, wqr, kvg, wk, wv)


ATT_AUG = 16


def _attn_kernel(qt_ref, k_ref, vt_ref, o_ref, s_scr):
    S = k_ref.shape[2]
    tq = qt_ref.shape[3]
    nk = S // ATT_TK
    ones_rows = jnp.where(lax.broadcasted_iota(jnp.int32, (ATT_AUG, ATT_TK), 0) == 0, 1.0, 0.0).astype(BF16)

    def scores(c, slot):
        k0 = pl.multiple_of(c * ATT_TK, ATT_TK)
        for h in range(2):
            s_scr[slot, h] = jnp.dot(k_ref[0, h, pl.ds(k0, ATT_TK), :], qt_ref[0, h], preferred_element_type=F32)

    def absorb(c, slot, carry):
        k0 = pl.multiple_of(c * ATT_TK, ATT_TK)
        out = []
        for h in range(2):
            m_prev, acc = carry[h]
            s = s_scr[slot, h]
            m_new = jnp.maximum(m_prev, jnp.max(s, axis=0, keepdims=True))
            alpha = jnp.exp2(m_prev - m_new)
            p = jnp.exp2(s - m_new).astype(BF16)
            v_aug = jnp.concatenate([vt_ref[0, h * V_DIM:(h + 1) * V_DIM, pl.ds(k0, ATT_TK)], ones_rows], axis=0)
            out.append((m_new, alpha * acc + jnp.dot(v_aug, p, preferred_element_type=F32)))
        return tuple(out)

    def pair(i, carry):
        c0 = 2 * i
        scores(c0 + 1, 1)
        carry = absorb(c0, 0, carry)
        scores(jnp.minimum(c0 + 2, nk - 1), 0)
        return absorb(c0 + 1, 1, carry)

    scores(0, 0)
    init = tuple((jnp.full((1, tq), -jnp.inf, F32), jnp.zeros((V_DIM + ATT_AUG, tq), F32)) for _ in range(2))
    fin = lax.fori_loop(0, nk // 2, pair, init)
    o = jnp.concatenate([acc[:V_DIM] * (1.0 / acc[V_DIM:V_DIM + 1]) for (_, acc) in fin], axis=0)
    o_ref[0] = o.T


def _attention(qt, k, vt):
    B, H, _, S = qt.shape
    tq = ATT_TQ
    return pl.pallas_call(
        _attn_kernel,
        grid=(B, H // 2, S // tq),
        in_specs=[
            pl.BlockSpec((1, 2, LANES, tq), lambda b, hp, i: (b, hp, 0, i)),
            pl.BlockSpec((1, 2, S, LANES), lambda b, hp, i: (b, hp, 0, 0)),
            pl.BlockSpec((1, 2 * V_DIM, S), lambda b, hp, i: (b, hp, 0)),
        ],
        out_specs=pl.BlockSpec((1, tq, 2 * V_DIM), lambda b, hp, i: (b, i, hp)),
        out_shape=jax.ShapeDtypeStruct((B, S, H * V_DIM), F32),
        scratch_shapes=[pltpu.VMEM((2, 2, ATT_TK, tq), F32)],
        compiler_params=_params("parallel", "parallel", "parallel"),
        name="mla_attention",
    )(qt, k, vt)


def _layer_norm(x, g, b):
    mu = jnp.mean(x, axis=-1, keepdims=True)
    xc = x - mu
    var = jnp.mean(xc * xc, axis=-1, keepdims=True)
    return xc * lax.rsqrt(var + EPS) * g + b


def _outproj_kernel(a_ref, hy_ref, x_ref, g1_ref, sc2_ref, sh2_ref, ag_ref, hg_ref, wo_ref,
                    lg_ref, lb_ref, wr_ref, x1_ref, u2_ref, aff_ref):
    half = a_ref.shape[2]
    an = _rms(a_ref[0], ag_ref[...]).astype(BF16)
    hn = _rms(hy_ref[0], hg_ref[...]).astype(BF16)
    mixed = (jnp.dot(an, wo_ref[:half, :], preferred_element_type=F32)
             + jnp.dot(hn, wo_ref[half:, :], preferred_element_type=F32))
    x1 = _layer_norm(DN_ALPHA * x_ref[0] + g1_ref[0] * mixed, lg_ref[...], lb_ref[...])
    x1_ref[0] = x1
    u2 = x1 * (1.0 + sc2_ref[0]) + sh2_ref[0]
    u2_ref[0] = u2.astype(BF16)
    logits = jnp.dot(u2, wr_ref[...], precision=HIGHEST, preferred_element_type=F32)
    lane = lax.broadcasted_iota(jnp.int32, logits.shape, 1)
    logits = jnp.where(lane < N_EXPERTS, logits, -jnp.inf)
    e = jnp.exp(logits - jnp.max(logits, axis=-1, keepdims=True))
    aff = e / jnp.sum(e, axis=-1, keepdims=True)
    aff_ref[0] = aff.T[:N_EXPERTS, :]


def _outproj(a, hy, x, g1, sc2, sh2, ag, hg, wo, lg, lb, wr_pad):
    B, S, D = x.shape
    tm = TOK_TILE
    half = a.shape[2]
    full = lambda shape: pl.BlockSpec(shape, lambda b, i: (0,) * len(shape))
    tok = lambda w: pl.BlockSpec((1, tm, w), lambda b, i: (b, i, 0))
    per_b = pl.BlockSpec((1, 1, D), lambda b, i: (b, 0, 0))
    return pl.pallas_call(
        _outproj_kernel,
        grid=(B, S // tm),
        in_specs=[tok(half), tok(half), tok(D), per_b, per_b, per_b,
                  full((1, half)), full((1, half)), full((D, D)), full((1, D)), full((1, D)),
                  full((D, LANES))],
        out_specs=[tok(D), tok(D), pl.BlockSpec((1, N_EXPERTS, tm), lambda b, i: (b, 0, i))],
        out_shape=[
            jax.ShapeDtypeStruct((B, S, D), F32),
            jax.ShapeDtypeStruct((B, S, D), BF16),
            jax.ShapeDtypeStruct((B, N_EXPERTS, S), F32),
        ],
        compiler_params=_params("parallel", "parallel"),
        name="out_proj_ln1_router",
    )(a, hy, x, g1, sc2, sh2, ag, hg, wo, lg, lb, wr_pad)


def _select_kernel(aff_ref, ones_ref, utri_ref, bsum_ref, ltri_ref, pos_ref, posall_ref, g_ref, *, cap):
    aff = aff_ref[0]
    ones = ones_ref[...]

    def rowsum(mf):
        return jnp.dot(mf, ones, preferred_element_type=F32)

    def expert_total(mf):
        return jnp.dot(bsum_ref[...], rowsum(mf).astype(BF16), preferred_element_type=F32)

    def prefix(mf):
        return (jnp.dot(mf, utri_ref[...], preferred_element_type=F32)
                + jnp.dot(ltri_ref[...], rowsum(mf).astype(BF16), preferred_element_type=F32))

    def ind(m):
        return jnp.where(m, 1.0, 0.0).astype(BF16)

    def search(i, thr_bits):
        cand = thr_bits | jnp.left_shift(jnp.int32(1), 30 - i)
        enough = expert_total(ind(aff >= lax.bitcast_convert_type(cand, F32))) >= cap
        return jnp.where(enough, cand, thr_bits)

    thr = lax.bitcast_convert_type(lax.fori_loop(0, 31, search, jnp.zeros(aff.shape, jnp.int32)), F32)
    gt = aff > thr
    eq = aff == thr
    room = cap - expert_total(ind(gt))
    take_eq = jnp.where(eq, jnp.where(prefix(ind(eq)) < room, 1.0, 0.0), 0.0)
    sel = jnp.where(gt, 1.0, take_eq)
    pos = prefix(sel.astype(BF16)).astype(jnp.int32)
    posall_ref[0] = pos
    pos_ref[0] = jnp.where(sel > 0.0, pos, -1)
    g_ref[0] = jnp.where(sel > 0.0, aff, 0.0)


def _select(aff2, cap):
    B, R, _ = aff2.shape
    rpe = R // N_EXPERTS
    li = jnp.arange(LANES, dtype=jnp.int32)
    ri = jnp.arange(R, dtype=jnp.int32)
    same = (ri[:, None] // rpe) == (ri[None, :] // rpe)
    ones = jnp.ones((LANES, LANES), BF16)
    utri = (li[:, None] < li[None, :]).astype(BF16)
    bsum = same.astype(BF16)
    ltri = (same & (ri[None, :] < ri[:, None])).astype(BF16)
    blk = pl.BlockSpec((1, R, LANES), lambda b: (b, 0, 0))
    full = lambda arr: pl.BlockSpec(arr.shape, lambda b: (0, 0))
    return pl.pallas_call(
        functools.partial(_select_kernel, cap=cap),
        grid=(B,),
        in_specs=[blk, full(ones), full(utri), full(bsum), full(ltri)],
        out_specs=[blk, blk, blk],
        out_shape=[jax.ShapeDtypeStruct((B, R, LANES), jnp.int32), jax.ShapeDtypeStruct((B, R, LANES), jnp.int32),
                   jax.ShapeDtypeStruct((B, R, LANES), F32)],
        compiler_params=_params("parallel"),
        name="expert_choice_select",
    )(aff2, ones, utri, bsum, ltri)


GATHER_TOK = 1024
GATHER_WIN = 256


def _gather_kernel(offs_ref, u_ref, pos_ref, xe_ref, acc, *, cap, nblk):
    b, e = pl.program_id(0), pl.program_id(1)
    base = (b * N_EXPERTS + e) * (nblk + 1)
    rows = GATHER_TOK // LANES
    acc[...] = jnp.zeros(acc.shape, F32)
    slot = lax.broadcasted_iota(jnp.int32, (GATHER_WIN, LANES), 0)

    def block(j, carry):
        off, end = offs_ref[base + j], offs_ref[base + j + 1]
        w = (off // SUBLANES) * SUBLANES
        nsub = jnp.where(end > off, (end - w + GATHER_WIN - 1) // GATHER_WIN, 0)
        prow = pos_ref[0, pl.ds(pl.multiple_of(j * rows, rows), rows), :]
        ublk = u_ref[0, pl.ds(pl.multiple_of(j * GATHER_TOK, GATHER_TOK), GATHER_TOK), :]

        def window(i, c):
            s0 = pl.multiple_of(w + i * GATHER_WIN, SUBLANES)
            onehot = jnp.concatenate(
                [jnp.where(slot == prow[r:r + 1, :] - s0, 1.0, 0.0).astype(BF16) for r in range(rows)], axis=1)
            acc[pl.ds(s0, GATHER_WIN), :] += jnp.dot(onehot, ublk, preferred_element_type=F32)
            return c

        lax.fori_loop(0, nsub, window, 0)
        return carry

    lax.fori_loop(0, nblk, block, 0)
    xe_ref[0] = acc[:cap, :].astype(BF16)


def _gather(offs, u2, pos_sel, cap):
    B, S, D = u2.shape
    rpe = S // LANES
    nblk = S // GATHER_TOK
    return pl.pallas_call(
        functools.partial(_gather_kernel, cap=cap, nblk=nblk),
        grid_spec=pltpu.PrefetchScalarGridSpec(
            num_scalar_prefetch=1,
            grid=(B, N_EXPERTS),
            in_specs=[pl.BlockSpec((1, S, D), lambda b, e, o: (b, 0, 0)),
                      pl.BlockSpec((1, rpe, LANES), lambda b, e, o: (b, e, 0))],
            out_specs=pl.BlockSpec((1, cap, D), lambda b, e, o: (e, b, 0)),
            scratch_shapes=[pltpu.VMEM((cap + GATHER_WIN + SUBLANES, D), F32)],
        ),
        out_shape=jax.ShapeDtypeStruct((N_EXPERTS, B * cap, D), BF16),
        compiler_params=_params("parallel", "parallel"),
        name="expert_gather",
    )(offs, u2, pos_sel)


def _moe_kernel(x_ref, wg_ref, wu_ref, wd_ref, o_ref, wgb, wub, wdb):
    @pl.when(pl.program_id(1) == 0)
    def _():
        wgb[...] = wg_ref[0].astype(BF16)
        wub[...] = wu_ref[0].astype(BF16)
        wdb[...] = wd_ref[0].astype(BF16)

    x = x_ref[0]
    hg = jnp.dot(x, wgb[...], preferred_element_type=F32)
    hu = jnp.dot(x, wub[...], preferred_element_type=F32)
    h = (hg * jax.nn.sigmoid(hg) * hu).astype(BF16)
    o_ref[0] = jnp.dot(h, wdb[...], preferred_element_type=F32).astype(BF16)


def _moe_ffn(xe, w_gate, w_up, w_down):
    E, R, D = xe.shape
    Fh = w_gate.shape[2]
    tr = MOE_ROWS
    return pl.pallas_call(
        _moe_kernel,
        grid=(E, R // tr),
        in_specs=[
            pl.BlockSpec((1, tr, D), lambda e, r: (e, r, 0)),
            pl.BlockSpec((1, D, Fh), lambda e, r: (e, 0, 0)),
            pl.BlockSpec((1, D, Fh), lambda e, r: (e, 0, 0)),
            pl.BlockSpec((1, Fh, D), lambda e, r: (e, 0, 0)),
        ],
        out_specs=pl.BlockSpec((1, tr, D), lambda e, r: (e, r, 0)),
        out_shape=jax.ShapeDtypeStruct((E, R, D), BF16),
        scratch_shapes=[pltpu.VMEM((D, Fh), BF16), pltpu.VMEM((D, Fh), BF16), pltpu.VMEM((Fh, D), BF16)],
        compiler_params=_params("parallel", "arbitrary"),
        name="moe_swiglu",
    )(xe, w_gate, w_up, w_down)


CMB_TOK = 512
CMB_WIN = 128
BF16_ROWS = 16


def _combine_kernel(offs_ref, x1_ref, pos_ref, g_ref, g2_ref, lg_ref, lb_ref, ye_hbm, o_ref,
                    rhs, lhs, acc, sem, *, cap, nblk, nbatch):
    b, i = pl.program_id(0), pl.program_id(1)
    W = CMB_WIN
    total = ye_hbm.shape[0]
    row_base, first, npass = [], [], jnp.int32(0)
    for e in range(N_EXPERTS):
        k = (b * N_EXPERTS + e) * (nblk + 1) + i
        off, end = offs_ref[k], offs_ref[k + 1]
        lo = (off // BF16_ROWS) * BF16_ROWS
        row_base.append((e * nbatch + b) * cap)
        first.append(lo)
        npass = jnp.maximum(npass, jnp.where(end > off, (end - lo + W - 1) // W, 0))
    acc[...] = jnp.zeros(acc.shape, F32)
    lane_e = lax.broadcasted_iota(jnp.int32, (1, N_EXPERTS), 1)
    lane_w = lax.broadcasted_iota(jnp.int32, (CMB_TOK, W), 1)

    def window_copy(e, start):
        return pltpu.make_async_copy(ye_hbm.at[pl.ds(pl.multiple_of(start, BF16_ROWS), W), :],
                                     rhs.at[pl.ds(e * W, W), :], sem.at[e])

    def one_pass(p, carry):
        lo_vec = jnp.zeros((1, N_EXPERTS), jnp.int32)
        shift_vec = jnp.zeros((1, N_EXPERTS), jnp.int32)
        starts = []
        for e in range(N_EXPERTS):
            lo = first[e] + p * W
            start = jnp.minimum(row_base[e] + lo, total - W)
            starts.append(start)
            lo_vec = jnp.where(lane_e == e, lo, lo_vec)
            shift_vec = jnp.where(lane_e == e, row_base[e] + lo - start, shift_vec)
        for e in range(N_EXPERTS):
            window_copy(e, starts[e]).start()
        d = pos_ref[0] - lo_vec
        rel = jnp.where((d >= 0) & (d < W), d + shift_vec, -1)
        g = g_ref[0]
        for e in range(N_EXPERTS):
            lhs[:, e * W:(e + 1) * W] = jnp.where(lane_w == rel[:, e:e + 1], g[:, e:e + 1], 0.0).astype(BF16)
        for e in range(N_EXPERTS):
            window_copy(e, starts[e]).wait()
        acc[...] += jnp.dot(lhs[...], rhs[...], preferred_element_type=F32)
        return carry

    lax.fori_loop(0, npass, one_pass, 0)
    o_ref[0] = _layer_norm(DN_ALPHA * x1_ref[0] + g2_ref[0] * acc[...], lg_ref[...], lb_ref[...])


def _combine_ln2(offs, x1, pos_tok, g_tok, g2, lg, lb, ye_rows, cap):
    B, S, D = x1.shape
    T = CMB_TOK
    nblk = S // T
    tok = lambda w: pl.BlockSpec((1, T, w), lambda b, i, o: (b, i, 0))
    vec = pl.BlockSpec((1, D), lambda b, i, o: (0, 0))
    return pl.pallas_call(
        functools.partial(_combine_kernel, cap=cap, nblk=nblk, nbatch=B),
        grid_spec=pltpu.PrefetchScalarGridSpec(
            num_scalar_prefetch=1,
            grid=(B, nblk),
            in_specs=[tok(D), tok(N_EXPERTS), tok(N_EXPERTS),
                      pl.BlockSpec((1, 1, D), lambda b, i, o: (b, 0, 0)), vec, vec,
                      pl.BlockSpec(memory_space=pl.ANY)],
            out_specs=tok(D),
            scratch_shapes=[pltpu.VMEM((N_EXPERTS * CMB_WIN, D), BF16),
                            pltpu.VMEM((T, N_EXPERTS * CMB_WIN), BF16),
                            pltpu.VMEM((T, D), F32),
                            pltpu.SemaphoreType.DMA((N_EXPERTS,))],
        ),
        out_shape=jax.ShapeDtypeStruct((B, S, D), F32),
        compiler_params=_params("parallel", "parallel"),
        name="combine_residual_ln2",
    )(offs, x1, pos_tok, g_tok, g2, lg, lb, ye_rows)


DFT_R = 128
K1_HALF = 40
K1_VALID = DFT_R // 2 + 1
K1_GROUPS = (K1_HALF // SUBLANES, -(-(K1_VALID - K1_HALF) // SUBLANES))
K1_SPEC = K1_HALF + K1_GROUPS[1] * SUBLANES
HY_CT = 128


def _dft_tables(S):
    N = 2 * S
    na = S // DFT_R
    r = jnp.arange(2 * K1_HALF, dtype=jnp.int32)
    k1 = (jnp.arange(2, dtype=jnp.int32)[:, None] * K1_HALF + r[None, :] % K1_HALF)
    valid = (k1 < K1_VALID).astype(F32)
    n = DFT_R * jnp.arange(na, dtype=jnp.int32)[None, :] + jnp.arange(DFT_R, dtype=jnp.int32)[:, None]
    prod = (n[None, :, None, :] * k1[:, None, :, None]) % N
    theta = prod.astype(F32) * (2.0 * math.pi / N)
    is_im = (r >= K1_HALF)[None, None, :, None]
    g = jnp.where(is_im, -jnp.sin(theta), jnp.cos(theta)) * valid[:, None, :, None]
    cw = jnp.where((k1 == 0) | (k1 == DFT_R // 2), 1.0, 2.0) * valid / N
    gi = jnp.transpose(g * cw[:, None, :, None], (0, 1, 3, 2))
    bk = (jnp.arange(DFT_R, dtype=jnp.int32)[:, None] * jnp.arange(DFT_R, dtype=jnp.int32)[None, :]) % DFT_R
    ph = bk.astype(F32) * (2.0 * math.pi / DFT_R)
    C, Sn = jnp.cos(ph), jnp.sin(ph)
    f3f = jnp.block([[C, Sn], [-Sn, C]])
    f3i = jnp.block([[C, -Sn], [Sn, C]])

    def pair_lanes(t):
        h, nb, nr, nc = t.shape
        return t.reshape(h, nb // 2, 2, nr, nc).transpose(0, 1, 3, 2, 4).reshape(h, nb // 2, nr, 2 * nc)

    return pair_lanes(g).astype(BF16), pair_lanes(gi).astype(BF16), f3f.astype(BF16), f3i.astype(BF16)


def _swap01(x):
    return jnp.swapaxes(x, 0, 1)


def _block_diag_lanes(top, bot):
    return jnp.concatenate([jnp.concatenate([top, jnp.zeros((top.shape[0], bot.shape[1]), top.dtype)], axis=1),
                            jnp.concatenate([jnp.zeros((bot.shape[0], top.shape[1]), bot.dtype), bot], axis=1)], axis=0)


def _dft_stage1(z_ref, gf_ref, hf, a_scr):
    ct = a_scr.shape[2]

    def body(g, carry):
        b0 = pl.multiple_of(g * SUBLANES, SUBLANES)
        zt = _swap01(z_ref[:, pl.ds(b0, SUBLANES), :]).astype(BF16)
        for p in range(SUBLANES // 2):
            a2 = jnp.dot(gf_ref[hf, g * (SUBLANES // 2) + p], _block_diag_lanes(zt[2 * p], zt[2 * p + 1]),
                         preferred_element_type=F32)
            a_scr[b0 + 2 * p] = a2[:, :ct]
            a_scr[b0 + 2 * p + 1] = a2[:, ct:]
        return carry

    lax.fori_loop(0, DFT_R // SUBLANES, body, 0)


def _dft_stage3_operands(a_scr, r0):
    mr = _swap01(a_scr[:, pl.ds(r0, SUBLANES), :])
    mi = _swap01(a_scr[:, pl.ds(K1_HALF + r0, SUBLANES), :])
    return [jnp.concatenate([jnp.concatenate([mr[j], mi[j]], axis=0),
                             jnp.concatenate([mr[j + 1], mi[j + 1]], axis=0)], axis=1).astype(BF16)
            for j in range(0, SUBLANES, 2)]


def _filter_mlp_kernel(feat_ref, w1_ref, b1_ref, fr_ref, w2_ref, b2_ref, w3_ref, b3_ref, dec_ref, o_ref):
    feats = feat_ref[...]
    fr = fr_ref[...]
    h = jnp.sin(fr * (jnp.dot(feats, w1_ref[...], precision=HIGHEST, preferred_element_type=F32) + b1_ref[...]))
    h = jnp.sin(fr * (jnp.dot(h, w2_ref[...], precision=HIGHEST, preferred_element_type=F32) + b2_ref[...]))
    h = jnp.dot(h, w3_ref[...], precision=HIGHEST, preferred_element_type=F32) + b3_ref[...]
    h = h * jnp.exp(-feats[:, 0:1] * jnp.abs(dec_ref[...]))
    row = lax.broadcasted_iota(jnp.int32, h.shape, 0) + pl.program_id(0) * h.shape[0]
    col = lax.broadcasted_iota(jnp.int32, h.shape, 1)
    o_ref[...] = jnp.where((row == 0) & (col >= h.shape[1] // 2), 0.0, h)


def _filter_taps(L, w1, b1, freq, w2, b2, w3, b3, decay):
    pos = jnp.arange(L, dtype=F32)
    t = pos[:, None] / (L - 1)
    bands = (w1.shape[0] - 1) // 2
    freqs = jnp.linspace(1e-4, bands - 1, bands, dtype=F32)
    phase = (2.0 * math.pi / L) * pos[:, None] * freqs[None, :]
    nf = w1.shape[0]
    feats = jnp.concatenate([t, jnp.cos(phase), -jnp.sin(phase), jnp.zeros((L, LANES - nf), F32)], axis=-1)
    w1p = jnp.concatenate([w1, jnp.zeros((LANES - nf, w1.shape[1]), F32)], axis=0)
    fo = w1.shape[1]
    ncol = w3.shape[1]
    tl = 512
    full = lambda shape: pl.BlockSpec(shape, lambda i: (0,) * len(shape))
    return pl.pallas_call(
        _filter_mlp_kernel,
        grid=(L // tl,),
        in_specs=[pl.BlockSpec((tl, LANES), lambda i: (i, 0)), full((LANES, fo)), full((1, fo)), full((1, fo)),
                  full((fo, fo)), full((1, fo)), full((fo, ncol)), full((1, ncol)), full((1, ncol))],
        out_specs=pl.BlockSpec((tl, ncol), lambda i: (i, 0)),
        out_shape=jax.ShapeDtypeStruct((L, ncol), F32),
        compiler_params=_params("parallel"),
        name="hyena_filter_mlp",
    )(feats, w1p, b1[None, :], freq[None, :], w2, b2[None, :], w3, b3[None, :], decay.reshape(1, ncol))


def _filter_spectrum_kernel(hf_ref, hb_ref, gf_ref, f3f_ref, o_ref, a_scr, x_scr):
    ct = a_scr.shape[2]
    for t, src in enumerate((hf_ref, hb_ref)):
        for hf in range(2):
            _dft_stage1(src, gf_ref, hf, a_scr)

            def body(g, carry):
                r0 = pl.multiple_of(g * SUBLANES, SUBLANES)
                for p, m in enumerate(_dft_stage3_operands(a_scr, r0)):
                    x2 = jnp.dot(f3f_ref[...], m, preferred_element_type=F32)
                    for q in range(2):
                        k1 = hf * K1_HALF + r0 + 2 * p + q
                        x = x2[:, q * ct:(q + 1) * ct]
                        if t == 0:
                            x_scr[k1] = x
                        else:
                            xf = x_scr[k1]
                            o_ref[0, k1] = jnp.concatenate([xf[:DFT_R] + x[:DFT_R], xf[DFT_R:] - x[DFT_R:]],
                                                           axis=0).astype(BF16)
                return carry

            lax.fori_loop(0, K1_GROUPS[hf], body, 0)


def _filter_spectrum(taps3, gf, f3f):
    na, nb, ncol = taps3.shape
    C = ncol // 4
    nct = C // HY_CT
    ct = HY_CT
    return pl.pallas_call(
        _filter_spectrum_kernel,
        grid=(2, nct),
        in_specs=[
            pl.BlockSpec((na, nb, ct), lambda o, j: (0, 0, o * nct + j)),
            pl.BlockSpec((na, nb, ct), lambda o, j: (0, 0, 2 * nct + o * nct + j)),
            pl.BlockSpec(gf.shape, lambda o, j: (0, 0, 0, 0), pipeline_mode=pl.Buffered(1)),
            pl.BlockSpec(f3f.shape, lambda o, j: (0, 0), pipeline_mode=pl.Buffered(1)),
        ],
        out_specs=pl.BlockSpec((1, K1_SPEC, 2 * DFT_R, ct), lambda o, j: (o, 0, 0, j)),
        out_shape=jax.ShapeDtypeStruct((2, K1_SPEC, 2 * DFT_R, C), BF16),
        scratch_shapes=[pltpu.VMEM((DFT_R, 2 * K1_HALF, ct), F32), pltpu.VMEM((K1_SPEC, 2 * DFT_R, ct), F32)],
        compiler_params=_params("parallel", "parallel"),
        name="hyena_filter_spectrum",
    )(taps3, taps3, gf, f3f)


def _short_conv_kernel(x_ref, w_ref, b_ref, o_ref):
    S = x_ref.shape[1]
    R = 256
    nchunk = S // R
    w0, w1, w2 = w_ref[0:1, :], w_ref[1:2, :], w_ref[2:3, :]
    bias = b_ref[...]
    row = lax.broadcasted_iota(jnp.int32, (R, x_ref.shape[2]), 0)

    def body(i, carry):
        r0 = pl.multiple_of(i * R, R)
        x = x_ref[0, pl.ds(r0, R), :]
        prev = x_ref[0, pl.ds(pl.multiple_of(jnp.maximum(r0 - SUBLANES, 0), SUBLANES), SUBLANES), :]
        nxt = x_ref[0, pl.ds(pl.multiple_of(jnp.minimum(r0 + R, S - SUBLANES), SUBLANES), SUBLANES), :]
        prev = jnp.where(i == 0, 0.0, prev[SUBLANES - 1:SUBLANES, :])
        nxt = jnp.where(i == nchunk - 1, 0.0, nxt[0:1, :])
        up = jnp.where(row == 0, prev, pltpu.roll(x, 1, 0))
        dn = jnp.where(row == R - 1, nxt, pltpu.roll(x, R - 1, 0))
        o_ref[0, pl.ds(r0, R), :] = bias + w0 * up + w1 * x + w2 * dn
        return carry

    lax.fori_loop(0, nchunk, body, 0)


def _short_conv(hy_raw, conv_w, conv_b):
    B, S, C = hy_raw.shape
    ct = LANES
    return pl.pallas_call(
        _short_conv_kernel,
        grid=(B, C // ct),
        in_specs=[pl.BlockSpec((1, S, ct), lambda b, j: (b, 0, j)),
                  pl.BlockSpec((conv_w.shape[0], ct), lambda b, j: (0, j)),
                  pl.BlockSpec((1, ct), lambda b, j: (0, j))],
        out_specs=pl.BlockSpec((1, S, ct), lambda b, j: (b, 0, j)),
        out_shape=jax.ShapeDtypeStruct((B, S, C), F32),
        compiler_params=_params("parallel", "parallel"),
        name="hyena_short_conv",
    )(hy_raw, conv_w, conv_b[None, :])


def _long_conv_kernel(z_ref, xg_ref, h_ref, bias_ref, gf_ref, f3f_ref, f3i_ref, gi_ref, o_ref, a_scr, b_scr):
    ct = a_scr.shape[2]
    bias = bias_ref[0]
    for hf in range(2):
        _dft_stage1(z_ref.at[0], gf_ref, hf, a_scr)
        ngrp = K1_GROUPS[hf]
        if ngrp * SUBLANES < K1_HALF:
            b_scr[ngrp * SUBLANES:] = jnp.zeros((K1_HALF - ngrp * SUBLANES,) + b_scr.shape[1:], F32)

        def spectral(g, carry):
            r0 = pl.multiple_of(g * SUBLANES, SUBLANES)
            for p, m in enumerate(_dft_stage3_operands(a_scr, r0)):
                x = jnp.dot(f3f_ref[...], m, preferred_element_type=F32)
                k1 = hf * K1_HALF + r0 + 2 * p
                h = jnp.concatenate([h_ref[0, k1], h_ref[0, k1 + 1]], axis=1).astype(F32)
                xr, xi, hr, hi = x[:DFT_R], x[DFT_R:], h[:DFT_R], h[DFT_R:]
                y = jnp.concatenate([xr * hr - xi * hi, xr * hi + xi * hr], axis=0).astype(BF16)
                bb = jnp.dot(f3i_ref[...], y, preferred_element_type=F32)
                b_scr[r0 + 2 * p] = bb[:, :ct]
                b_scr[r0 + 2 * p + 1] = bb[:, ct:]
            return carry

        lax.fori_loop(0, ngrp, spectral, 0)

        def inverse(g, carry):
            b0 = pl.multiple_of(g * SUBLANES, SUBLANES)
            br = _swap01(b_scr[:, pl.ds(b0, SUBLANES), :])
            bi = _swap01(b_scr[:, pl.ds(DFT_R + b0, SUBLANES), :])
            ys = []
            for p in range(SUBLANES // 2):
                bm = [jnp.concatenate([br[2 * p + q], bi[2 * p + q]], axis=0).astype(BF16) for q in range(2)]
                y2 = jnp.dot(gi_ref[hf, g * (SUBLANES // 2) + p], _block_diag_lanes(bm[0], bm[1]),
                             preferred_element_type=F32)
                ys += [y2[:, :ct], y2[:, ct:]]
            y = _swap01(jnp.stack(ys, axis=0))
            sl = (0, slice(None), pl.ds(b0, SUBLANES), slice(None))
            if hf == 0:
                o_ref[sl] = y
            else:
                o_ref[sl] = xg_ref[sl] * (o_ref[sl] + y + z_ref[sl] * bias)
            return carry

        lax.fori_loop(0, DFT_R // SUBLANES, inverse, 0)


def _long_conv(z_arr, z_off, xg_arr, xg_off, spec, order, bias3, gf, f3f, f3i, gi):
    B, na, nb, _ = z_arr.shape
    C = spec.shape[3]
    ct = HY_CT
    const = lambda arr: pl.BlockSpec(arr.shape, lambda j, b: (0,) * arr.ndim, pipeline_mode=pl.Buffered(1))
    return pl.pallas_call(
        _long_conv_kernel,
        grid=(C // ct, B),
        in_specs=[
            pl.BlockSpec((1, na, nb, ct), lambda j, b: (b, 0, 0, z_off + j)),
            pl.BlockSpec((1, na, nb, ct), lambda j, b: (b, 0, 0, xg_off + j)),
            pl.BlockSpec((1, K1_SPEC, 2 * DFT_R, ct), lambda j, b: (order, 0, 0, j), pipeline_mode=pl.Buffered(1)),
            pl.BlockSpec((1, 1, ct), lambda j, b: (order, 0, j)),
            const(gf), const(f3f), const(f3i), const(gi),
        ],
        out_specs=pl.BlockSpec((1, na, nb, ct), lambda j, b: (b, 0, 0, j)),
        out_shape=jax.ShapeDtypeStruct((B, na, nb, C), F32),
        scratch_shapes=[pltpu.VMEM((DFT_R, 2 * K1_HALF, ct), F32), pltpu.VMEM((K1_HALF, 2 * DFT_R, ct), F32)],
        compiler_params=_params("parallel", "parallel"),
        name="hyena_long_conv",
    )(z_arr, xg_arr, spec, bias3, gf, f3f, f3i, gi)


def _hyena_group(hy_raw, conv_w, conv_b, fw1, fb1, ffreq, fw2, fb2, fw3, fb3, decay, hyena_bias):
    B, S, _ = hy_raw.shape
    na = S // DFT_R
    gf, gi, f3f, f3i = _dft_tables(S)
    taps = _filter_taps(S, fw1, fb1, ffreq, fw2, fb2, fw3, fb3, decay)
    spec = _filter_spectrum(taps.reshape(na, DFT_R, -1), gf, f3f)
    u = _short_conv(hy_raw, conv_w, conv_b).reshape(B, na, DFT_R, -1)
    nct = HY_W // HY_CT
    bias3 = hyena_bias[:, None, :]
    z1 = _long_conv(u, 0, u, nct, spec, 0, bias3, gf, f3f, f3i, gi)
    return _long_conv(z1, 0, u, 2 * nct, spec, 1, bias3, gf, f3f, f3i, gi).reshape(B, S, HY_W)


def _rope_rot_cols(w):
    half = ROPE // 2
    return jnp.concatenate([-w[..., half:], w[..., :half]], axis=-1)


def _prep_weights(w_in, w_qb, w_kvb):
    D = w_in.shape[0]
    o = Q_RANK + KV_RANK
    w_kr = w_in[:, o:o + ROPE]
    z64 = jnp.zeros((D, NOPE), F32)
    z32 = jnp.zeros((D, LANES - NOPE - ROPE), F32)
    win_aug = jnp.concatenate([
        w_in[:, :o],
        z64, w_kr, z32,
        z64, _rope_rot_cols(w_kr), z32,
        w_in[:, o + ROPE:],
    ], axis=1).astype(BF16)
    wq = w_qb.reshape(Q_RANK, MLA_HEADS, NOPE + ROPE)
    zq = jnp.zeros((Q_RANK, MLA_HEADS, LANES - NOPE - ROPE), F32)
    wq_main = jnp.concatenate([wq, zq], axis=-1).reshape(Q_RANK, MLA_HEADS * LANES).astype(BF16)
    wq_rot = jnp.concatenate([jnp.zeros((Q_RANK, MLA_HEADS, NOPE), F32), _rope_rot_cols(wq[..., NOPE:]), zq],
                             axis=-1).reshape(Q_RANK, MLA_HEADS * LANES).astype(BF16)
    wkv = w_kvb.reshape(KV_RANK, MLA_HEADS, NOPE + V_DIM)
    wk = jnp.concatenate([wkv[..., :NOPE], jnp.zeros((KV_RANK, MLA_HEADS, LANES - NOPE), F32)],
                         axis=-1).reshape(KV_RANK, MLA_HEADS * LANES).astype(BF16)
    wv = wkv[..., NOPE:].reshape(KV_RANK, MLA_HEADS * V_DIM).astype(BF16)
    return win_aug, wq_main, wq_rot, wk, wv


def kernel(x, c, positions, w_ada, b_ada, w_in, q_norm_g, w_qb, kv_norm_g, w_kvb, conv_w, conv_b, filt_w1, filt_b1, filt_freq, filt_w2, filt_b2, filt_w3, filt_b3, hyena_decay, hyena_bias, attn_out_g, hyena_out_g, w_o, ln1_g, ln1_b, w_router, w_gate, w_up, w_down, ln2_g, ln2_b):
    B, S, D = x.shape
    l = 0
    c8 = jnp.zeros((SUBLANES, D), F32).at[:B].set(c)
    mod = _ada_mod(c8, w_ada[l], b_ada[l][None, :])[:B]
    sh1, sc1, g1, sh2, sc2, g2 = [m[:, None, :] for m in jnp.split(mod, 6, axis=-1)]

    win_aug, wq_main, wq_rot, wk, wv = _prep_weights(w_in[l], w_qb[l], w_kvb[l])
    half = ROPE // 2
    inv_freq = ROPE_THETA ** (-jnp.arange(half, dtype=F32) / half)
    freq_row = jnp.concatenate([jnp.zeros((NOPE,), F32), inv_freq, inv_freq,
                                jnp.zeros((LANES - NOPE - ROPE,), F32)])[None, :]
    q, kt, v, hy_raw = _inproj(x, sc1, sh1, positions[:, :, None], freq_row, win_aug,
                               q_norm_g[l][None, :], wq_main, wq_rot, kv_norm_g[l][None, :], wk, wv)
    a = _attention(q, kt, v)

    hy = _hyena_group(hy_raw, conv_w[l], conv_b[l], filt_w1[l], filt_b1[l], filt_freq[l], filt_w2[l],
                      filt_b2[l], filt_w3[l], filt_b3[l], hyena_decay[l], hyena_bias[l])

    wr_pad = jnp.concatenate([w_router[l], jnp.zeros((D, LANES - N_EXPERTS), F32)], axis=1)
    x1, u2, aff_t = _outproj(a, hy, x, g1, sc2, sh2, attn_out_g[l][None, :], hyena_out_g[l][None, :],
                             w_o[l].astype(BF16), ln1_g[l][None, :], ln1_b[l][None, :], wr_pad)

    cap = EC_FACTOR * S // N_EXPERTS
    rpe = S // LANES
    aff2 = aff_t.reshape(B, N_EXPERTS * rpe, LANES)
    pos_sel, pos_all, gsel = _select(aff2, cap)
    row_start = pos_all.reshape(B, N_EXPERTS, rpe, LANES)[:, :, :, 0]

    def block_offsets(tokens):
        ends = jnp.full((B, N_EXPERTS, 1), cap, jnp.int32)
        return jnp.concatenate([row_start[:, :, ::tokens // LANES], ends], axis=-1).reshape(-1)

    xe = _gather(block_offsets(GATHER_TOK), u2, pos_sel, cap)
    ye = _moe_ffn(xe, w_gate[l], w_up[l], w_down[l])
    pos_tok = pos_sel.reshape(B, N_EXPERTS, S).transpose(0, 2, 1)
    g_tok = gsel.reshape(B, N_EXPERTS, S).transpose(0, 2, 1)
    return _combine_ln2(block_offsets(CMB_TOK), x1, pos_tok, g_tok, g2, ln2_g[l][None, :], ln2_b[l][None, :],
                        ye.reshape(N_EXPERTS * B * cap, D), cap)
```

```python
import functools
import math

import jax
import jax.numpy as jnp
from jax import lax
from jax.experimental import pallas as pl
from jax.experimental.pallas import tpu as pltpu

F32 = jnp.float32
BF16 = jnp.bfloat16
HIGHEST = lax.Precision.HIGHEST

LANES = 128
SUBLANES = 8
VMEM_LIMIT = 56 * 1024 * 1024

D_MODEL = 1024
MLA_HEADS = 8
NOPE = 64
ROPE = 32
V_DIM = 64
Q_RANK = 256
KV_RANK = 128
HY_W = 512
HY_COLS = 3 * HY_W
N_EXPERTS = 16
EC_FACTOR = 2
ROPE_THETA = 10000.0
EPS = 1e-5
DN_ALPHA = 2.0 ** 0.25
LOG2E = 1.4426950408889634

TOK_TILE = 512
ATT_TQ = 512
ATT_TK = 512
MOE_ROWS = 512


def _params(*sem):
    return pltpu.CompilerParams(dimension_semantics=sem, vmem_limit_bytes=VMEM_LIMIT)


def _ada_kernel(c_ref, w_ref, b_ref, o_ref):
    c = c_ref[...]
    sc = c * jax.nn.sigmoid(c)
    o_ref[...] = jnp.dot(sc, w_ref[...], precision=HIGHEST, preferred_element_type=F32) + b_ref[...]


def _ada_mod(c8, w_ada, b_ada):
    n = w_ada.shape[1]
    tn = 1024
    return pl.pallas_call(
        _ada_kernel,
        grid=(n // tn,),
        in_specs=[
            pl.BlockSpec((SUBLANES, D_MODEL), lambda j: (0, 0)),
            pl.BlockSpec((D_MODEL, tn), lambda j: (0, j)),
            pl.BlockSpec((1, tn), lambda j: (0, j)),
        ],
        out_specs=pl.BlockSpec((SUBLANES, tn), lambda j: (0, j)),
        out_shape=jax.ShapeDtypeStruct((SUBLANES, n), F32),
        compiler_params=_params("arbitrary"),
        name="ada_mod",
    )(c8, w_ada, b_ada)


def _dot3(a, b):
    ah, bh = a.astype(BF16), b.astype(BF16)
    al = (a - ah.astype(F32)).astype(BF16)
    bl = (b - bh.astype(F32)).astype(BF16)
    dot = functools.partial(jnp.dot, preferred_element_type=F32)
    return dot(ah, bh) + (dot(ah, bl) + dot(al, bh))


def _rms(x, g):
    return x * lax.rsqrt(jnp.mean(x * x, axis=-1, keepdims=True) + EPS) * g


def _rope_kernel(pos_ref, freq_ref, cos_ref, sin_ref):
    ang = freq_ref[...] * pos_ref[0].astype(F32)
    cos_ref[0] = jnp.cos(ang)
    sin_ref[0] = jnp.sin(ang)


def _rope_tables(positions):
    B, S = positions.shape
    half = ROPE // 2
    inv_freq = (ROPE_THETA ** (-jnp.arange(half, dtype=F32) / half))[:, None]
    out = jax.ShapeDtypeStruct((B, half, S), F32)
    blk = pl.BlockSpec((1, half, S), lambda b: (b, 0, 0))
    return pl.pallas_call(
        _rope_kernel,
        grid=(B,),
        in_specs=[pl.BlockSpec((1, 1, S), lambda b: (b, 0, 0)), pl.BlockSpec((half, 1), lambda b: (0, 0))],
        out_specs=[blk, blk],
        out_shape=[out, out],
        compiler_params=_params("parallel"),
        name="rope_tables",
    )(positions[:, None, :], inv_freq)


def _inproj_kernel(x_ref, xp_ref, xn_ref, sc_ref, sh_ref, cos_ref, sin_ref, win_ref, qg_ref, wq_ref, wqr_ref,
                   kvg_ref, wk_ref, wv_ref, cw_ref, cb_ref, q_ref, k_ref, v_ref, hy_ref):
    def modulate(xv):
        return (xv * (1.0 + sc_ref[0]) + sh_ref[0]).astype(BF16)

    u = modulate(x_ref[0])
    cos = cos_ref[0].T
    sin = sin_ref[0].T

    def proj(lo, hi):
        return jnp.dot(u, win_ref[:, lo:hi], preferred_element_type=F32)

    cq = _rms(proj(0, Q_RANK), qg_ref[...]).astype(BF16)
    q_main = jnp.dot(cq, wq_ref[...], preferred_element_type=F32)
    q_rot = jnp.dot(cq, wqr_ref[...], preferred_element_type=F32)
    qscale = LOG2E * (NOPE + ROPE) ** -0.5
    cq_s = cos * qscale
    sq_s = sin * qscale
    for h in range(MLA_HEADS):
        sl = slice(h * LANES, (h + 1) * LANES)
        q_ref[0, h] = (q_main[:, sl] * cq_s + q_rot[:, sl] * sq_s).T.astype(BF16)

    o = Q_RANK
    ckv = _rms(proj(o, o + KV_RANK), kvg_ref[...]).astype(BF16)
    kr = proj(o + KV_RANK, o + 2 * KV_RANK) * cos + proj(o + 2 * KV_RANK, o + 3 * KV_RANK) * sin
    kn = jnp.dot(ckv, wk_ref[...], preferred_element_type=F32)
    for h in range(MLA_HEADS):
        k_ref[0, h] = (kn[:, h * LANES:(h + 1) * LANES] + kr).astype(BF16)
    v_ref[0] = jnp.dot(ckv, wv_ref[...], preferred_element_type=F32).T.astype(BF16)

    o = Q_RANK + 3 * KV_RANK
    hy = proj(o, o + HY_COLS)
    tm = hy.shape[0]
    i, last = pl.program_id(1), pl.num_programs(1) - 1
    edge = jnp.dot(modulate(jnp.concatenate([xp_ref[0], xn_ref[0]], axis=0)), win_ref[:, o:o + HY_COLS],
                   preferred_element_type=F32)
    prev = jnp.where(i == 0, 0.0, edge[SUBLANES - 1:SUBLANES, :])
    nxt = jnp.where(i == last, 0.0, edge[SUBLANES:SUBLANES + 1, :])
    row = lax.broadcasted_iota(jnp.int32, hy.shape, 0)
    up = jnp.where(row == 0, prev, pltpu.roll(hy, 1, 0))
    dn = jnp.where(row == tm - 1, nxt, pltpu.roll(hy, tm - 1, 0))
    hy_ref[0] = cb_ref[...] + cw_ref[0:1, :] * up + cw_ref[1:2, :] * hy + cw_ref[2:3, :] * dn


def _inproj(x, sc1, sh1, cos_t, sin_t, win_aug, qg, wq, wqr, kvg, wk, wv, conv_w, conv_b):
    B, S, D = x.shape
    tm = TOK_TILE
    ncol = win_aug.shape[1]
    g8 = tm // SUBLANES
    n8 = S // SUBLANES
    full = lambda shape: pl.BlockSpec(shape, lambda b, i: (0,) * len(shape))
    return pl.pallas_call(
        _inproj_kernel,
        grid=(B, S // tm),
        in_specs=[
            pl.BlockSpec((1, tm, D), lambda b, i: (b, i, 0)),
            pl.BlockSpec((1, SUBLANES, D), lambda b, i: (b, jnp.maximum(i * g8 - 1, 0), 0)),
            pl.BlockSpec((1, SUBLANES, D), lambda b, i: (b, jnp.minimum((i + 1) * g8, n8 - 1), 0)),
            pl.BlockSpec((1, 1, D), lambda b, i: (b, 0, 0)),
            pl.BlockSpec((1, 1, D), lambda b, i: (b, 0, 0)),
            pl.BlockSpec((1, LANES, tm), lambda b, i: (b, 0, i)),
            pl.BlockSpec((1, LANES, tm), lambda b, i: (b, 0, i)),
            full((D, ncol)),
            full((1, Q_RANK)),
            full((Q_RANK, MLA_HEADS * LANES)),
            full((Q_RANK, MLA_HEADS * LANES)),
            full((1, KV_RANK)),
            full((KV_RANK, MLA_HEADS * LANES)),
            full((KV_RANK, MLA_HEADS * V_DIM)),
            full(conv_w.shape),
            full((1, HY_COLS)),
        ],
        out_specs=[
            pl.BlockSpec((1, MLA_HEADS, LANES, tm), lambda b, i: (b, 0, 0, i)),
            pl.BlockSpec((1, MLA_HEADS, tm, LANES), lambda b, i: (b, 0, i, 0)),
            pl.BlockSpec((1, MLA_HEADS * V_DIM, tm), lambda b, i: (b, 0, i)),
            pl.BlockSpec((1, tm, HY_COLS), lambda b, i: (b, i, 0)),
        ],
        out_shape=[
            jax.ShapeDtypeStruct((B, MLA_HEADS, LANES, S), BF16),
            jax.ShapeDtypeStruct((B, MLA_HEADS, S, LANES), BF16),
            jax.ShapeDtypeStruct((B, MLA_HEADS * V_DIM, S), BF16),
            jax.ShapeDtypeStruct((B, S, HY_COLS), F32),
        ],
        compiler_params=_params("parallel", "parallel"),
        name="in_proj",
    )(x, x, x, sc1, sh1, cos_t, sin_t, win_aug, qg, wq, wqr, kvg, wk, wv, conv_w, conv_b[None, :])


ATT_AUG = 16


def _attn_kernel(qt_ref, k_ref, vt_ref, o_ref, s_scr):
    S = k_ref.shape[2]
    tq = qt_ref.shape[3]
    nk = S // ATT_TK
    ones_rows = jnp.where(lax.broadcasted_iota(jnp.int32, (ATT_AUG, ATT_TK), 0) == 0, 1.0, 0.0).astype(BF16)

    def scores(c, slot):
        k0 = pl.multiple_of(c * ATT_TK, ATT_TK)
        for h in range(2):
            s_scr[slot, h] = jnp.dot(k_ref[0, h, pl.ds(k0, ATT_TK), :], qt_ref[0, h], preferred_element_type=F32)

    def absorb(c, slot, carry):
        k0 = pl.multiple_of(c * ATT_TK, ATT_TK)
        out = []
        for h in range(2):
            m_prev, acc = carry[h]
            s = s_scr[slot, h]
            m_new = jnp.maximum(m_prev, jnp.max(s, axis=0, keepdims=True))
            alpha = jnp.exp2(m_prev - m_new)
            p = jnp.exp2(s - m_new).astype(BF16)
            v_aug = jnp.concatenate([vt_ref[0, h * V_DIM:(h + 1) * V_DIM, pl.ds(k0, ATT_TK)], ones_rows], axis=0)
            out.append((m_new, alpha * acc + jnp.dot(v_aug, p, preferred_element_type=F32)))
        return tuple(out)

    def pair(i, carry):
        c0 = 2 * i
        scores(c0 + 1, 1)
        carry = absorb(c0, 0, carry)
        scores(jnp.minimum(c0 + 2, nk - 1), 0)
        return absorb(c0 + 1, 1, carry)

    scores(0, 0)
    init = tuple((jnp.full((1, tq), -jnp.inf, F32), jnp.zeros((V_DIM + ATT_AUG, tq), F32)) for _ in range(2))
    fin = lax.fori_loop(0, nk // 2, pair, init, unroll=4)
    o = jnp.concatenate([acc[:V_DIM] * (1.0 / acc[V_DIM:V_DIM + 1]) for (_, acc) in fin], axis=0)
    o_ref[0] = o.T


def _attention(qt, k, vt):
    B, H, _, S = qt.shape
    tq = ATT_TQ
    return pl.pallas_call(
        _attn_kernel,
        grid=(B, H // 2, S // tq),
        in_specs=[
            pl.BlockSpec((1, 2, LANES, tq), lambda b, hp, i: (b, hp, 0, i)),
            pl.BlockSpec((1, 2, S, LANES), lambda b, hp, i: (b, hp, 0, 0)),
            pl.BlockSpec((1, 2 * V_DIM, S), lambda b, hp, i: (b, hp, 0)),
        ],
        out_specs=pl.BlockSpec((1, tq, 2 * V_DIM), lambda b, hp, i: (b, i, hp)),
        out_shape=jax.ShapeDtypeStruct((B, S, H * V_DIM), F32),
        scratch_shapes=[pltpu.VMEM((2, 2, ATT_TK, tq), F32)],
        compiler_params=_params("parallel", "parallel", "parallel"),
        name="mla_attention",
    )(qt, k, vt)


def _layer_norm(x, g, b):
    mu = jnp.mean(x, axis=-1, keepdims=True)
    xc = x - mu
    var = jnp.mean(xc * xc, axis=-1, keepdims=True)
    return xc * lax.rsqrt(var + EPS) * g + b


def _outproj_kernel(a_ref, hy_ref, x_ref, g1_ref, sc2_ref, sh2_ref, ag_ref, hg_ref, wo_ref,
                    lg_ref, lb_ref, wr_ref, x1_ref, u2_ref, aff_ref):
    half = a_ref.shape[2]
    an = _rms(a_ref[0], ag_ref[...]).astype(BF16)
    hn = _rms(hy_ref[0], hg_ref[...]).astype(BF16)
    mixed = (jnp.dot(an, wo_ref[:half, :], preferred_element_type=F32)
             + jnp.dot(hn, wo_ref[half:, :], preferred_element_type=F32))
    x1 = _layer_norm(DN_ALPHA * x_ref[0] + g1_ref[0] * mixed, lg_ref[...], lb_ref[...])
    x1_ref[0] = x1
    u2 = x1 * (1.0 + sc2_ref[0]) + sh2_ref[0]
    u2_ref[0] = u2.astype(BF16)
    logits = _dot3(u2, wr_ref[...])
    lane = lax.broadcasted_iota(jnp.int32, logits.shape, 1)
    logits = jnp.where(lane < N_EXPERTS, logits, -jnp.inf)
    e = jnp.exp(logits - jnp.max(logits, axis=-1, keepdims=True))
    aff = e / jnp.sum(e, axis=-1, keepdims=True)
    aff_ref[0] = aff.T[:N_EXPERTS, :]


def _outproj(a, hy, x, g1, sc2, sh2, ag, hg, wo, lg, lb, wr_pad):
    B, S, D = x.shape
    tm = TOK_TILE
    half = a.shape[2]
    full = lambda shape: pl.BlockSpec(shape, lambda b, i: (0,) * len(shape))
    tok = lambda w: pl.BlockSpec((1, tm, w), lambda b, i: (b, i, 0))
    per_b = pl.BlockSpec((1, 1, D), lambda b, i: (b, 0, 0))
    return pl.pallas_call(
        _outproj_kernel,
        grid=(B, S // tm),
        in_specs=[tok(half), tok(half), tok(D), per_b, per_b, per_b,
                  full((1, half)), full((1, half)), full((D, D)), full((1, D)), full((1, D)),
                  full((D, LANES))],
        out_specs=[tok(D), tok(D), pl.BlockSpec((1, N_EXPERTS, tm), lambda b, i: (b, 0, i))],
        out_shape=[
            jax.ShapeDtypeStruct((B, S, D), F32),
            jax.ShapeDtypeStruct((B, S, D), BF16),
            jax.ShapeDtypeStruct((B, N_EXPERTS, S), F32),
        ],
        compiler_params=_params("parallel", "parallel"),
        name="out_proj_ln1_router",
    )(a, hy, x, g1, sc2, sh2, ag, hg, wo, lg, lb, wr_pad)


def _select_kernel(aff_ref, ones_ref, utri_ref, bsum_ref, ltri_ref, pos_ref, posall_ref, g_ref, *, cap):
    aff = aff_ref[0]
    ones = ones_ref[...]

    def rowsum(mf):
        return jnp.dot(mf, ones, preferred_element_type=F32)

    def expert_total(mf):
        return jnp.dot(bsum_ref[...], rowsum(mf).astype(BF16), preferred_element_type=F32)

    def prefix(mf):
        return (jnp.dot(mf, utri_ref[...], preferred_element_type=F32)
                + jnp.dot(ltri_ref[...], rowsum(mf).astype(BF16), preferred_element_type=F32))

    def ind(m):
        return jnp.where(m, 1.0, 0.0).astype(BF16)

    def search(i, thr_bits):
        cand = thr_bits | jnp.left_shift(jnp.int32(1), 30 - i)
        enough = expert_total(ind(aff >= lax.bitcast_convert_type(cand, F32))) >= cap
        return jnp.where(enough, cand, thr_bits)

    thr = lax.bitcast_convert_type(lax.fori_loop(0, 31, search, jnp.zeros(aff.shape, jnp.int32)), F32)
    gt = aff > thr
    eq = aff == thr
    room = cap - expert_total(ind(gt))
    take_eq = jnp.where(eq, jnp.where(prefix(ind(eq)) < room, 1.0, 0.0), 0.0)
    sel = jnp.where(gt, 1.0, take_eq)
    pos = prefix(sel.astype(BF16)).astype(jnp.int32)
    posall_ref[0] = pos
    pos_ref[0] = jnp.where(sel > 0.0, pos, -1)
    g_ref[0] = jnp.where(sel > 0.0, aff, 0.0)


def _select(aff2, cap):
    B, R, _ = aff2.shape
    rpe = R // N_EXPERTS
    li = jnp.arange(LANES, dtype=jnp.int32)
    ri = jnp.arange(R, dtype=jnp.int32)
    same = (ri[:, None] // rpe) == (ri[None, :] // rpe)
    ones = jnp.ones((LANES, LANES), BF16)
    utri = (li[:, None] < li[None, :]).astype(BF16)
    bsum = same.astype(BF16)
    ltri = (same & (ri[None, :] < ri[:, None])).astype(BF16)
    blk = pl.BlockSpec((1, R, LANES), lambda b: (b, 0, 0))
    full = lambda arr: pl.BlockSpec(arr.shape, lambda b: (0, 0))
    return pl.pallas_call(
        functools.partial(_select_kernel, cap=cap),
        grid=(B,),
        in_specs=[blk, full(ones), full(utri), full(bsum), full(ltri)],
        out_specs=[blk, blk, blk],
        out_shape=[jax.ShapeDtypeStruct((B, R, LANES), jnp.int32), jax.ShapeDtypeStruct((B, R, LANES), jnp.int32),
                   jax.ShapeDtypeStruct((B, R, LANES), F32)],
        compiler_params=_params("parallel"),
        name="expert_choice_select",
    )(aff2, ones, utri, bsum, ltri)


GATHER_TOK = 1024
GATHER_WIN = 256


def _gather_kernel(offs_ref, u_ref, pos_ref, xe_ref, acc, *, cap, nblk):
    b, e = pl.program_id(0), pl.program_id(1)
    base = (b * N_EXPERTS + e) * (nblk + 1)
    rows = GATHER_TOK // LANES
    acc[...] = jnp.zeros(acc.shape, F32)
    slot = lax.broadcasted_iota(jnp.int32, (GATHER_WIN, LANES), 0)

    def block(j, carry):
        off, end = offs_ref[base + j], offs_ref[base + j + 1]
        w = (off // SUBLANES) * SUBLANES
        nsub = jnp.where(end > off, (end - w + GATHER_WIN - 1) // GATHER_WIN, 0)
        prow = pos_ref[0, pl.ds(pl.multiple_of(j * rows, rows), rows), :]
        ublk = u_ref[0, pl.ds(pl.multiple_of(j * GATHER_TOK, GATHER_TOK), GATHER_TOK), :]

        def window(i, c):
            s0 = pl.multiple_of(w + i * GATHER_WIN, SUBLANES)
            onehot = jnp.concatenate(
                [jnp.where(slot == prow[r:r + 1, :] - s0, 1.0, 0.0).astype(BF16) for r in range(rows)], axis=1)
            acc[pl.ds(s0, GATHER_WIN), :] += jnp.dot(onehot, ublk, preferred_element_type=F32)
            return c

        lax.fori_loop(0, nsub, window, 0)
        return carry

    lax.fori_loop(0, nblk, block, 0)
    xe_ref[0] = acc[:cap, :].astype(BF16)


def _gather(offs, u2, pos_sel, cap):
    B, S, D = u2.shape
    rpe = S // LANES
    nblk = S // GATHER_TOK
    return pl.pallas_call(
        functools.partial(_gather_kernel, cap=cap, nblk=nblk),
        grid_spec=pltpu.PrefetchScalarGridSpec(
            num_scalar_prefetch=1,
            grid=(B, N_EXPERTS),
            in_specs=[pl.BlockSpec((1, S, D), lambda b, e, o: (b, 0, 0)),
                      pl.BlockSpec((1, rpe, LANES), lambda b, e, o: (b, e, 0))],
            out_specs=pl.BlockSpec((1, cap, D), lambda b, e, o: (e, b, 0)),
            scratch_shapes=[pltpu.VMEM((cap + GATHER_WIN + SUBLANES, D), F32)],
        ),
        out_shape=jax.ShapeDtypeStruct((N_EXPERTS, B * cap, D), BF16),
        compiler_params=_params("parallel", "parallel"),
        name="expert_gather",
    )(offs, u2, pos_sel)


def _moe_kernel(x_ref, wg_ref, wu_ref, wd_ref, o_ref, wgb, wub, wdb):
    @pl.when(pl.program_id(1) == 0)
    def _():
        wgb[...] = wg_ref[0].astype(BF16)
        wub[...] = wu_ref[0].astype(BF16)
        wdb[...] = wd_ref[0].astype(BF16)

    x = x_ref[0]
    hg = jnp.dot(x, wgb[...], preferred_element_type=F32)
    hu = jnp.dot(x, wub[...], preferred_element_type=F32)
    h = (hg * jax.nn.sigmoid(hg) * hu).astype(BF16)
    o_ref[0] = jnp.dot(h, wdb[...], preferred_element_type=F32).astype(BF16)


def _moe_ffn(xe, w_gate, w_up, w_down):
    E, R, D = xe.shape
    Fh = w_gate.shape[2]
    tr = MOE_ROWS
    return pl.pallas_call(
        _moe_kernel,
        grid=(E, R // tr),
        in_specs=[
            pl.BlockSpec((1, tr, D), lambda e, r: (e, r, 0)),
            pl.BlockSpec((1, D, Fh), lambda e, r: (e, 0, 0)),
            pl.BlockSpec((1, D, Fh), lambda e, r: (e, 0, 0)),
            pl.BlockSpec((1, Fh, D), lambda e, r: (e, 0, 0)),
        ],
        out_specs=pl.BlockSpec((1, tr, D), lambda e, r: (e, r, 0)),
        out_shape=jax.ShapeDtypeStruct((E, R, D), BF16),
        scratch_shapes=[pltpu.VMEM((D, Fh), BF16), pltpu.VMEM((D, Fh), BF16), pltpu.VMEM((Fh, D), BF16)],
        compiler_params=_params("parallel", "arbitrary"),
        name="moe_swiglu",
    )(xe, w_gate, w_up, w_down)


CMB_TOK = 512
CMB_WIN = 128
BF16_ROWS = 16


def _combine_kernel(offs_ref, x1_ref, pos_ref, g_ref, g2_ref, lg_ref, lb_ref, ye_hbm, o_ref,
                    rhs, lhs, acc, sem, *, cap, nblk, nbatch):
    b, i = pl.program_id(0), pl.program_id(1)
    W = CMB_WIN
    total = ye_hbm.shape[0]
    row_base, first, npass = [], [], jnp.int32(0)
    for e in range(N_EXPERTS):
        k = (b * N_EXPERTS + e) * (nblk + 1) + i
        off, end = offs_ref[k], offs_ref[k + 1]
        lo = (off // BF16_ROWS) * BF16_ROWS
        row_base.append((e * nbatch + b) * cap)
        first.append(lo)
        npass = jnp.maximum(npass, jnp.where(end > off, (end - lo + W - 1) // W, 0))
    acc[...] = jnp.zeros(acc.shape, F32)
    lane_e = lax.broadcasted_iota(jnp.int32, (1, N_EXPERTS), 1)
    lane_w = lax.broadcasted_iota(jnp.int32, (CMB_TOK, W), 1)

    def window_copy(e, start):
        return pltpu.make_async_copy(ye_hbm.at[pl.ds(pl.multiple_of(start, BF16_ROWS), W), :],
                                     rhs.at[pl.ds(e * W, W), :], sem.at[e])

    def one_pass(p, carry):
        lo_vec = jnp.zeros((1, N_EXPERTS), jnp.int32)
        shift_vec = jnp.zeros((1, N_EXPERTS), jnp.int32)
        starts = []
        for e in range(N_EXPERTS):
            lo = first[e] + p * W
            start = jnp.minimum(row_base[e] + lo, total - W)
            starts.append(start)
            lo_vec = jnp.where(lane_e == e, lo, lo_vec)
            shift_vec = jnp.where(lane_e == e, row_base[e] + lo - start, shift_vec)
        for e in range(N_EXPERTS):
            window_copy(e, starts[e]).start()
        d = pos_ref[0] - lo_vec
        rel = jnp.where((d >= 0) & (d < W), d + shift_vec, -1)
        g = g_ref[0]
        for e in range(N_EXPERTS):
            lhs[:, e * W:(e + 1) * W] = jnp.where(lane_w == rel[:, e:e + 1], g[:, e:e + 1], 0.0).astype(BF16)
        for e in range(N_EXPERTS):
            window_copy(e, starts[e]).wait()
        acc[...] += jnp.dot(lhs[...], rhs[...], preferred_element_type=F32)
        return carry

    lax.fori_loop(0, npass, one_pass, 0)
    o_ref[0] = _layer_norm(DN_ALPHA * x1_ref[0] + g2_ref[0] * acc[...], lg_ref[...], lb_ref[...])


def _combine_ln2(offs, x1, pos_tok, g_tok, g2, lg, lb, ye_rows, cap):
    B, S, D = x1.shape
    T = CMB_TOK
    nblk = S // T
    tok = lambda w: pl.BlockSpec((1, T, w), lambda b, i, o: (b, i, 0))
    vec = pl.BlockSpec((1, D), lambda b, i, o: (0, 0))
    return pl.pallas_call(
        functools.partial(_combine_kernel, cap=cap, nblk=nblk, nbatch=B),
        grid_spec=pltpu.PrefetchScalarGridSpec(
            num_scalar_prefetch=1,
            grid=(B, nblk),
            in_specs=[tok(D), tok(N_EXPERTS), tok(N_EXPERTS),
                      pl.BlockSpec((1, 1, D), lambda b, i, o: (b, 0, 0)), vec, vec,
                      pl.BlockSpec(memory_space=pl.ANY)],
            out_specs=tok(D),
            scratch_shapes=[pltpu.VMEM((N_EXPERTS * CMB_WIN, D), BF16),
                            pltpu.VMEM((T, N_EXPERTS * CMB_WIN), BF16),
                            pltpu.VMEM((T, D), F32),
                            pltpu.SemaphoreType.DMA((N_EXPERTS,))],
        ),
        out_shape=jax.ShapeDtypeStruct((B, S, D), F32),
        compiler_params=_params("parallel", "parallel"),
        name="combine_residual_ln2",
    )(offs, x1, pos_tok, g_tok, g2, lg, lb, ye_rows)


DFT_R = 128
K1_HALF = 40
K1_VALID = DFT_R // 2 + 1
K1_GROUPS = (K1_HALF // SUBLANES, -(-(K1_VALID - K1_HALF) // SUBLANES))
K1_SPEC = K1_HALF + K1_GROUPS[1] * SUBLANES
HY_CT = 128


def _dft_tables(S):
    N = 2 * S
    na = S // DFT_R
    r = jnp.arange(2 * K1_HALF, dtype=jnp.int32)
    k1 = (jnp.arange(2, dtype=jnp.int32)[:, None] * K1_HALF + r[None, :] % K1_HALF)
    valid = (k1 < K1_VALID).astype(F32)
    n = DFT_R * jnp.arange(na, dtype=jnp.int32)[None, :] + jnp.arange(DFT_R, dtype=jnp.int32)[:, None]
    prod = (n[None, :, None, :] * k1[:, None, :, None]) % N
    theta = prod.astype(F32) * (2.0 * math.pi / N)
    is_im = (r >= K1_HALF)[None, None, :, None]
    g = jnp.where(is_im, -jnp.sin(theta), jnp.cos(theta)) * valid[:, None, :, None]
    cw = jnp.where((k1 == 0) | (k1 == DFT_R // 2), 1.0, 2.0) * valid / N
    gi = jnp.transpose(g * cw[:, None, :, None], (0, 1, 3, 2))
    bk = (jnp.arange(DFT_R, dtype=jnp.int32)[:, None] * jnp.arange(DFT_R, dtype=jnp.int32)[None, :]) % DFT_R
    ph = bk.astype(F32) * (2.0 * math.pi / DFT_R)
    C, Sn = jnp.cos(ph), jnp.sin(ph)
    f3f = jnp.block([[C, Sn], [-Sn, C]])
    f3i = jnp.block([[C, -Sn], [Sn, C]])

    def pair_lanes(t):
        h, nb, nr, nc = t.shape
        return t.reshape(h, nb // 2, 2, nr, nc).transpose(0, 1, 3, 2, 4).reshape(h, nb // 2, nr, 2 * nc)

    return pair_lanes(g).astype(BF16), pair_lanes(gi).astype(BF16), f3f.astype(BF16), f3i.astype(BF16)


def _swap01(x):
    return jnp.swapaxes(x, 0, 1)


def _block_diag_lanes(top, bot):
    return jnp.concatenate([jnp.concatenate([top, jnp.zeros((top.shape[0], bot.shape[1]), top.dtype)], axis=1),
                            jnp.concatenate([jnp.zeros((bot.shape[0], top.shape[1]), bot.dtype), bot], axis=1)], axis=0)


def _dft_stage1(z_ref, gf_ref, hf, a_scr):
    ct = a_scr.shape[2]

    def body(g, carry):
        b0 = pl.multiple_of(g * SUBLANES, SUBLANES)
        zt = _swap01(z_ref[:, pl.ds(b0, SUBLANES), :]).astype(BF16)
        for p in range(SUBLANES // 2):
            a2 = jnp.dot(gf_ref[hf, g * (SUBLANES // 2) + p], _block_diag_lanes(zt[2 * p], zt[2 * p + 1]),
                         preferred_element_type=F32)
            a_scr[b0 + 2 * p] = a2[:, :ct]
            a_scr[b0 + 2 * p + 1] = a2[:, ct:]
        return carry

    lax.fori_loop(0, DFT_R // SUBLANES, body, 0)


def _dft_stage3_operands(a_scr, r0):
    mr = _swap01(a_scr[:, pl.ds(r0, SUBLANES), :])
    mi = _swap01(a_scr[:, pl.ds(K1_HALF + r0, SUBLANES), :])
    return [jnp.concatenate([jnp.concatenate([mr[j], mi[j]], axis=0),
                             jnp.concatenate([mr[j + 1], mi[j + 1]], axis=0)], axis=1).astype(BF16)
            for j in range(0, SUBLANES, 2)]


def _filter_mlp_kernel(feat_ref, w1_ref, b1_ref, fr_ref, w2_ref, b2_ref, w3_ref, b3_ref, dec_ref, o_ref):
    feats = feat_ref[...]
    fr = fr_ref[...]
    h = jnp.sin(fr * (jnp.dot(feats, w1_ref[...], precision=HIGHEST, preferred_element_type=F32) + b1_ref[...]))
    h = jnp.sin(fr * (jnp.dot(h, w2_ref[...], precision=HIGHEST, preferred_element_type=F32) + b2_ref[...]))
    h = _dot3(h, w3_ref[...]) + b3_ref[...]
    h = h * jnp.exp(-feats[:, 0:1] * jnp.abs(dec_ref[...]))
    row = lax.broadcasted_iota(jnp.int32, h.shape, 0) + pl.program_id(0) * h.shape[0]
    col = lax.broadcasted_iota(jnp.int32, h.shape, 1)
    o_ref[...] = jnp.where((row == 0) & (col >= h.shape[1] // 2), 0.0, h)


def _filter_taps(L, w1, b1, freq, w2, b2, w3, b3, decay):
    pos = jnp.arange(L, dtype=F32)
    t = pos[:, None] / (L - 1)
    bands = (w1.shape[0] - 1) // 2
    freqs = jnp.linspace(1e-4, bands - 1, bands, dtype=F32)
    phase = (2.0 * math.pi / L) * pos[:, None] * freqs[None, :]
    nf = w1.shape[0]
    feats = jnp.concatenate([t, jnp.cos(phase), -jnp.sin(phase), jnp.zeros((L, LANES - nf), F32)], axis=-1)
    w1p = jnp.concatenate([w1, jnp.zeros((LANES - nf, w1.shape[1]), F32)], axis=0)
    fo = w1.shape[1]
    ncol = w3.shape[1]
    tl = 512
    full = lambda shape: pl.BlockSpec(shape, lambda i: (0,) * len(shape))
    return pl.pallas_call(
        _filter_mlp_kernel,
        grid=(L // tl,),
        in_specs=[pl.BlockSpec((tl, LANES), lambda i: (i, 0)), full((LANES, fo)), full((1, fo)), full((1, fo)),
                  full((fo, fo)), full((1, fo)), full((fo, ncol)), full((1, ncol)), full((1, ncol))],
        out_specs=pl.BlockSpec((tl, ncol), lambda i: (i, 0)),
        out_shape=jax.ShapeDtypeStruct((L, ncol), F32),
        compiler_params=_params("parallel"),
        name="hyena_filter_mlp",
    )(feats, w1p, b1[None, :], freq[None, :], w2, b2[None, :], w3, b3[None, :], decay.reshape(1, ncol))


def _filter_spectrum_kernel(hf_ref, hb_ref, gf_ref, f3f_ref, o_ref, a_scr, x_scr):
    ct = a_scr.shape[2]
    for t, src in enumerate((hf_ref, hb_ref)):
        for hf in range(2):
            _dft_stage1(src, gf_ref, hf, a_scr)

            def body(g, carry):
                r0 = pl.multiple_of(g * SUBLANES, SUBLANES)
                for p, m in enumerate(_dft_stage3_operands(a_scr, r0)):
                    x2 = jnp.dot(f3f_ref[...], m, preferred_element_type=F32)
                    for q in range(2):
                        k1 = hf * K1_HALF + r0 + 2 * p + q
                        x = x2[:, q * ct:(q + 1) * ct]
                        if t == 0:
                            x_scr[k1] = x
                        else:
                            xf = x_scr[k1]
                            o_ref[0, k1] = jnp.concatenate([xf[:DFT_R] + x[:DFT_R], xf[DFT_R:] - x[DFT_R:]],
                                                           axis=0).astype(BF16)
                return carry

            lax.fori_loop(0, K1_GROUPS[hf], body, 0)


def _filter_spectrum(taps3, gf, f3f):
    na, nb, ncol = taps3.shape
    C = ncol // 4
    nct = C // HY_CT
    ct = HY_CT
    return pl.pallas_call(
        _filter_spectrum_kernel,
        grid=(2, nct),
        in_specs=[
            pl.BlockSpec((na, nb, ct), lambda o, j: (0, 0, o * nct + j)),
            pl.BlockSpec((na, nb, ct), lambda o, j: (0, 0, 2 * nct + o * nct + j)),
            pl.BlockSpec(gf.shape, lambda o, j: (0, 0, 0, 0), pipeline_mode=pl.Buffered(1)),
            pl.BlockSpec(f3f.shape, lambda o, j: (0, 0), pipeline_mode=pl.Buffered(1)),
        ],
        out_specs=pl.BlockSpec((1, K1_SPEC, 2 * DFT_R, ct), lambda o, j: (o, 0, 0, j)),
        out_shape=jax.ShapeDtypeStruct((2, K1_SPEC, 2 * DFT_R, C), BF16),
        scratch_shapes=[pltpu.VMEM((DFT_R, 2 * K1_HALF, ct), F32), pltpu.VMEM((K1_SPEC, 2 * DFT_R, ct), F32)],
        compiler_params=_params("parallel", "parallel"),
        name="hyena_filter_spectrum",
    )(taps3, taps3, gf, f3f)


def _long_conv_kernel(z_ref, xg_ref, h_ref, bias_ref, gf_ref, f3f_ref, f3i_ref, gi_ref, o_ref, a_scr, b_scr):
    ct = a_scr.shape[2]
    bias = bias_ref[0]
    for hf in range(2):
        _dft_stage1(z_ref.at[0], gf_ref, hf, a_scr)
        ngrp = K1_GROUPS[hf]
        if ngrp * SUBLANES < K1_HALF:
            b_scr[ngrp * SUBLANES:] = jnp.zeros((K1_HALF - ngrp * SUBLANES,) + b_scr.shape[1:], F32)

        def spectral(g, carry):
            r0 = pl.multiple_of(g * SUBLANES, SUBLANES)
            for p, m in enumerate(_dft_stage3_operands(a_scr, r0)):
                x = jnp.dot(f3f_ref[...], m, preferred_element_type=F32)
                k1 = hf * K1_HALF + r0 + 2 * p
                h = jnp.concatenate([h_ref[0, k1], h_ref[0, k1 + 1]], axis=1).astype(F32)
                xr, xi, hr, hi = x[:DFT_R], x[DFT_R:], h[:DFT_R], h[DFT_R:]
                y = jnp.concatenate([xr * hr - xi * hi, xr * hi + xi * hr], axis=0).astype(BF16)
                bb = jnp.dot(f3i_ref[...], y, preferred_element_type=F32)
                b_scr[r0 + 2 * p] = bb[:, :ct]
                b_scr[r0 + 2 * p + 1] = bb[:, ct:]
            return carry

        lax.fori_loop(0, ngrp, spectral, 0)

        def inverse(g, carry):
            b0 = pl.multiple_of(g * SUBLANES, SUBLANES)
            br = _swap01(b_scr[:, pl.ds(b0, SUBLANES), :])
            bi = _swap01(b_scr[:, pl.ds(DFT_R + b0, SUBLANES), :])
            ys = []
            for p in range(SUBLANES // 2):
                bm = [jnp.concatenate([br[2 * p + q], bi[2 * p + q]], axis=0).astype(BF16) for q in range(2)]
                y2 = jnp.dot(gi_ref[hf, g * (SUBLANES // 2) + p], _block_diag_lanes(bm[0], bm[1]),
                             preferred_element_type=F32)
                ys += [y2[:, :ct], y2[:, ct:]]
            y = _swap01(jnp.stack(ys, axis=0))
            sl = (0, slice(None), pl.ds(b0, SUBLANES), slice(None))
            if hf == 0:
                o_ref[sl] = y
            else:
                o_ref[sl] = xg_ref[sl] * (o_ref[sl] + y + z_ref[sl] * bias)
            return carry

        lax.fori_loop(0, DFT_R // SUBLANES, inverse, 0)


def _long_conv(z_arr, z_off, xg_arr, xg_off, spec, order, bias3, gf, f3f, f3i, gi):
    B, na, nb, _ = z_arr.shape
    C = spec.shape[3]
    ct = HY_CT
    const = lambda arr: pl.BlockSpec(arr.shape, lambda j, b: (0,) * arr.ndim, pipeline_mode=pl.Buffered(1))
    return pl.pallas_call(
        _long_conv_kernel,
        grid=(C // ct, B),
        in_specs=[
            pl.BlockSpec((1, na, nb, ct), lambda j, b: (b, 0, 0, z_off + j)),
            pl.BlockSpec((1, na, nb, ct), lambda j, b: (b, 0, 0, xg_off + j)),
            pl.BlockSpec((1, K1_SPEC, 2 * DFT_R, ct), lambda j, b: (order, 0, 0, j), pipeline_mode=pl.Buffered(1)),
            pl.BlockSpec((1, 1, ct), lambda j, b: (order, 0, j)),
            const(gf), const(f3f), const(f3i), const(gi),
        ],
        out_specs=pl.BlockSpec((1, na, nb, ct), lambda j, b: (b, 0, 0, j)),
        out_shape=jax.ShapeDtypeStruct((B, na, nb, C), F32),
        scratch_shapes=[pltpu.VMEM((DFT_R, 2 * K1_HALF, ct), F32), pltpu.VMEM((K1_HALF, 2 * DFT_R, ct), F32)],
        compiler_params=_params("parallel", "parallel"),
        name="hyena_long_conv",
    )(z_arr, xg_arr, spec, bias3, gf, f3f, f3i, gi)


def _hyena_group(hy_conv, fw1, fb1, ffreq, fw2, fb2, fw3, fb3, decay, hyena_bias):
    B, S, _ = hy_conv.shape
    na = S // DFT_R
    gf, gi, f3f, f3i = _dft_tables(S)
    taps = _filter_taps(S, fw1, fb1, ffreq, fw2, fb2, fw3, fb3, decay)
    spec = _filter_spectrum(taps.reshape(na, DFT_R, -1), gf, f3f)
    u = hy_conv.reshape(B, na, DFT_R, -1)
    nct = HY_W // HY_CT
    bias3 = hyena_bias[:, None, :]
    z1 = _long_conv(u, 0, u, nct, spec, 0, bias3, gf, f3f, f3i, gi)
    return _long_conv(z1, 0, u, 2 * nct, spec, 1, bias3, gf, f3f, f3i, gi).reshape(B, S, HY_W)


def _rope_rot_cols(w):
    half = ROPE // 2
    return jnp.concatenate([-w[..., half:], w[..., :half]], axis=-1)


def _prep_weights(w_in, w_qb, w_kvb):
    D = w_in.shape[0]
    o = Q_RANK + KV_RANK
    w_kr = w_in[:, o:o + ROPE]
    z64 = jnp.zeros((D, NOPE), F32)
    z32 = jnp.zeros((D, LANES - NOPE - ROPE), F32)
    win_aug = jnp.concatenate([
        w_in[:, :o],
        z64, w_kr, z32,
        z64, _rope_rot_cols(w_kr), z32,
        w_in[:, o + ROPE:],
    ], axis=1).astype(BF16)
    wq = w_qb.reshape(Q_RANK, MLA_HEADS, NOPE + ROPE)
    zq = jnp.zeros((Q_RANK, MLA_HEADS, LANES - NOPE - ROPE), F32)
    wq_main = jnp.concatenate([wq, zq], axis=-1).reshape(Q_RANK, MLA_HEADS * LANES).astype(BF16)
    wq_rot = jnp.concatenate([jnp.zeros((Q_RANK, MLA_HEADS, NOPE), F32), _rope_rot_cols(wq[..., NOPE:]), zq],
                             axis=-1).reshape(Q_RANK, MLA_HEADS * LANES).astype(BF16)
    wkv = w_kvb.reshape(KV_RANK, MLA_HEADS, NOPE + V_DIM)
    wk = jnp.concatenate([wkv[..., :NOPE], jnp.zeros((KV_RANK, MLA_HEADS, LANES - NOPE), F32)],
                         axis=-1).reshape(KV_RANK, MLA_HEADS * LANES).astype(BF16)
    wv = wkv[..., NOPE:].reshape(KV_RANK, MLA_HEADS * V_DIM).astype(BF16)
    return win_aug, wq_main, wq_rot, wk, wv


def kernel(x, c, positions, w_ada, b_ada, w_in, q_norm_g, w_qb, kv_norm_g, w_kvb, conv_w, conv_b, filt_w1, filt_b1, filt_freq, filt_w2, filt_b2, filt_w3, filt_b3, hyena_decay, hyena_bias, attn_out_g, hyena_out_g, w_o, ln1_g, ln1_b, w_router, w_gate, w_up, w_down, ln2_g, ln2_b):
    B, S, D = x.shape
    l = 0
    c8 = jnp.zeros((SUBLANES, D), F32).at[:B].set(c)
    mod = _ada_mod(c8, w_ada[l], b_ada[l][None, :])[:B]
    sh1, sc1, g1, sh2, sc2, g2 = [m[:, None, :] for m in jnp.split(mod, 6, axis=-1)]

    win_aug, wq_main, wq_rot, wk, wv = _prep_weights(w_in[l], w_qb[l], w_kvb[l])
    cos_h, sin_h = _rope_tables(positions)
    pad = lambda n, v: jnp.full((B, n, S), v, F32)
    cos_t = jnp.concatenate([pad(NOPE, 1.0), cos_h, cos_h, pad(LANES - NOPE - ROPE, 1.0)], axis=1)
    sin_t = jnp.concatenate([pad(NOPE, 0.0), sin_h, sin_h, pad(LANES - NOPE - ROPE, 0.0)], axis=1)
    q, kt, v, hy_conv = _inproj(x, sc1, sh1, cos_t, sin_t, win_aug, q_norm_g[l][None, :], wq_main, wq_rot,
                                kv_norm_g[l][None, :], wk, wv, conv_w[l], conv_b[l])
    a = _attention(q, kt, v)

    hy = _hyena_group(hy_conv, filt_w1[l], filt_b1[l], filt_freq[l], filt_w2[l],
                      filt_b2[l], filt_w3[l], filt_b3[l], hyena_decay[l], hyena_bias[l])

    wr_pad = jnp.concatenate([w_router[l], jnp.zeros((D, LANES - N_EXPERTS), F32)], axis=1)
    x1, u2, aff_t = _outproj(a, hy, x, g1, sc2, sh2, attn_out_g[l][None, :], hyena_out_g[l][None, :],
                             w_o[l].astype(BF16), ln1_g[l][None, :], ln1_b[l][None, :], wr_pad)

    cap = EC_FACTOR * S // N_EXPERTS
    rpe = S // LANES
    aff2 = aff_t.reshape(B, N_EXPERTS * rpe, LANES)
    pos_sel, pos_all, gsel = _select(aff2, cap)
    row_start = pos_all.reshape(B, N_EXPERTS, rpe, LANES)[:, :, :, 0]

    def block_offsets(tokens):
        ends = jnp.full((B, N_EXPERTS, 1), cap, jnp.int32)
        return jnp.concatenate([row_start[:, :, ::tokens // LANES], ends], axis=-1).reshape(-1)

    xe = _gather(block_offsets(GATHER_TOK), u2, pos_sel, cap)
    ye = _moe_ffn(xe, w_gate[l], w_up[l], w_down[l])
    pos_tok = pos_sel.reshape(B, N_EXPERTS, S).transpose(0, 2, 1)
    g_tok = gsel.reshape(B, N_EXPERTS, S).transpose(0, 2, 1)
    return _combine_ln2(block_offsets(CMB_TOK), x1, pos_tok, g_tok, g2, ln2_g[l][None, :], ln2_b[l][None, :],
                        ye.reshape(N_EXPERTS * B * cap, D), cap)
```

```python
import functools
import math

import jax
import jax.numpy as jnp
import numpy as np
from jax import lax
from jax.experimental import pallas as pl
from jax.experimental.pallas import tpu as pltpu

F32 = jnp.float32
BF16 = jnp.bfloat16
HIGHEST = lax.Precision.HIGHEST

LANES = 128
SUBLANES = 8
VMEM_LIMIT = 56 * 1024 * 1024

D_MODEL = 1024
MLA_HEADS = 8
NOPE = 64
ROPE = 32
V_DIM = 64
Q_RANK = 256
KV_RANK = 128
HY_W = 512
HY_COLS = 3 * HY_W
N_EXPERTS = 16
EC_FACTOR = 2
ROPE_THETA = 10000.0
EPS = 1e-5
DN_ALPHA = 2.0 ** 0.25
LOG2E = 1.4426950408889634

TOK_TILE = 512
ATT_TQ = 512
ATT_TK = 512
MOE_ROWS = 512


def _params(*sem):
    return pltpu.CompilerParams(dimension_semantics=sem, vmem_limit_bytes=VMEM_LIMIT)


def _ada_kernel(c_ref, w_ref, b_ref, o_ref):
    c = c_ref[...]
    sc = c * jax.nn.sigmoid(c)
    o_ref[...] = jnp.dot(sc, w_ref[...], precision=HIGHEST, preferred_element_type=F32) + b_ref[...]


def _ada_mod(c8, w_ada, b_ada):
    n = w_ada.shape[1]
    tn = 1024
    return pl.pallas_call(
        _ada_kernel,
        grid=(n // tn,),
        in_specs=[
            pl.BlockSpec((SUBLANES, D_MODEL), lambda j: (0, 0)),
            pl.BlockSpec((D_MODEL, tn), lambda j: (0, j)),
            pl.BlockSpec((1, tn), lambda j: (0, j)),
        ],
        out_specs=pl.BlockSpec((SUBLANES, tn), lambda j: (0, j)),
        out_shape=jax.ShapeDtypeStruct((SUBLANES, n), F32),
        compiler_params=_params("arbitrary"),
        name="ada_mod",
    )(c8, w_ada, b_ada)


def _dot3(a, b):
    ah, bh = a.astype(BF16), b.astype(BF16)
    al = (a - ah.astype(F32)).astype(BF16)
    bl = (b - bh.astype(F32)).astype(BF16)
    dot = functools.partial(jnp.dot, preferred_element_type=F32)
    return dot(ah, bh) + (dot(ah, bl) + dot(al, bh))


def _rms(x, g):
    return x * lax.rsqrt(jnp.mean(x * x, axis=-1, keepdims=True) + EPS) * g


def _rope_kernel(pos_ref, freq_ref, cos_ref, sin_ref):
    ang = freq_ref[...] * pos_ref[0].astype(F32)
    cos_ref[0] = jnp.cos(ang)
    sin_ref[0] = jnp.sin(ang)


def _rope_tables(positions):
    B, S = positions.shape
    half = ROPE // 2
    inv_freq = (ROPE_THETA ** (-jnp.arange(half, dtype=F32) / half))[:, None]
    out = jax.ShapeDtypeStruct((B, half, S), F32)
    blk = pl.BlockSpec((1, half, S), lambda b: (b, 0, 0))
    return pl.pallas_call(
        _rope_kernel,
        grid=(B,),
        in_specs=[pl.BlockSpec((1, 1, S), lambda b: (b, 0, 0)), pl.BlockSpec((half, 1), lambda b: (0, 0))],
        out_specs=[blk, blk],
        out_shape=[out, out],
        compiler_params=_params("parallel"),
        name="rope_tables",
    )(positions[:, None, :], inv_freq)


def _inproj_kernel(x_ref, xp_ref, xn_ref, sc_ref, sh_ref, cos_ref, sin_ref, win_ref, qg_ref, wq_ref, wqr_ref,
                   kvg_ref, wk_ref, wv_ref, cw_ref, cb_ref, q_ref, k_ref, v_ref, hy_ref):
    def modulate(xv):
        return (xv * (1.0 + sc_ref[0]) + sh_ref[0]).astype(BF16)

    u = modulate(x_ref[0])
    cos = cos_ref[0].T
    sin = sin_ref[0].T

    def proj(lo, hi):
        return jnp.dot(u, win_ref[:, lo:hi], preferred_element_type=F32)

    cq = _rms(proj(0, Q_RANK), qg_ref[...]).astype(BF16)
    q_main = jnp.dot(cq, wq_ref[...], preferred_element_type=F32)
    q_rot = jnp.dot(cq, wqr_ref[...], preferred_element_type=F32)
    qscale = LOG2E * (NOPE + ROPE) ** -0.5
    cq_s = cos * qscale
    sq_s = sin * qscale
    for h in range(MLA_HEADS):
        sl = slice(h * LANES, (h + 1) * LANES)
        q_ref[0, h] = (q_main[:, sl] * cq_s + q_rot[:, sl] * sq_s).T.astype(BF16)

    o = Q_RANK
    ckv = _rms(proj(o, o + KV_RANK), kvg_ref[...]).astype(BF16)
    kr = proj(o + KV_RANK, o + 2 * KV_RANK) * cos + proj(o + 2 * KV_RANK, o + 3 * KV_RANK) * sin
    kn = jnp.dot(ckv, wk_ref[...], preferred_element_type=F32)
    for h in range(MLA_HEADS):
        k_ref[0, h] = (kn[:, h * LANES:(h + 1) * LANES] + kr).astype(BF16)
    v_ref[0] = jnp.dot(ckv, wv_ref[...], preferred_element_type=F32).T.astype(BF16)

    o = Q_RANK + 3 * KV_RANK
    hy = proj(o, o + HY_COLS)
    tm = hy.shape[0]
    i, last = pl.program_id(1), pl.num_programs(1) - 1
    edge = jnp.dot(modulate(jnp.concatenate([xp_ref[0], xn_ref[0]], axis=0)), win_ref[:, o:o + HY_COLS],
                   preferred_element_type=F32)
    prev = jnp.where(i == 0, 0.0, edge[SUBLANES - 1:SUBLANES, :])
    nxt = jnp.where(i == last, 0.0, edge[SUBLANES:SUBLANES + 1, :])
    row = lax.broadcasted_iota(jnp.int32, hy.shape, 0)
    up = jnp.where(row == 0, prev, pltpu.roll(hy, 1, 0))
    dn = jnp.where(row == tm - 1, nxt, pltpu.roll(hy, tm - 1, 0))
    hy_ref[0] = cb_ref[...] + cw_ref[0:1, :] * up + cw_ref[1:2, :] * hy + cw_ref[2:3, :] * dn


def _inproj(x, sc1, sh1, cos_t, sin_t, win_aug, qg, wq, wqr, kvg, wk, wv, conv_w, conv_b):
    B, S, D = x.shape
    tm = TOK_TILE
    ncol = win_aug.shape[1]
    g8 = tm // SUBLANES
    n8 = S // SUBLANES
    full = lambda shape: pl.BlockSpec(shape, lambda b, i: (0,) * len(shape))
    return pl.pallas_call(
        _inproj_kernel,
        grid=(B, S // tm),
        in_specs=[
            pl.BlockSpec((1, tm, D), lambda b, i: (b, i, 0)),
            pl.BlockSpec((1, SUBLANES, D), lambda b, i: (b, jnp.maximum(i * g8 - 1, 0), 0)),
            pl.BlockSpec((1, SUBLANES, D), lambda b, i: (b, jnp.minimum((i + 1) * g8, n8 - 1), 0)),
            pl.BlockSpec((1, 1, D), lambda b, i: (b, 0, 0)),
            pl.BlockSpec((1, 1, D), lambda b, i: (b, 0, 0)),
            pl.BlockSpec((1, LANES, tm), lambda b, i: (b, 0, i)),
            pl.BlockSpec((1, LANES, tm), lambda b, i: (b, 0, i)),
            full((D, ncol)),
            full((1, Q_RANK)),
            full((Q_RANK, MLA_HEADS * LANES)),
            full((Q_RANK, MLA_HEADS * LANES)),
            full((1, KV_RANK)),
            full((KV_RANK, MLA_HEADS * LANES)),
            full((KV_RANK, MLA_HEADS * V_DIM)),
            full(conv_w.shape),
            full((1, HY_COLS)),
        ],
        out_specs=[
            pl.BlockSpec((1, MLA_HEADS, LANES, tm), lambda b, i: (b, 0, 0, i)),
            pl.BlockSpec((1, MLA_HEADS, tm, LANES), lambda b, i: (b, 0, i, 0)),
            pl.BlockSpec((1, MLA_HEADS * V_DIM, tm), lambda b, i: (b, 0, i)),
            pl.BlockSpec((1, tm, HY_COLS), lambda b, i: (b, i, 0)),
        ],
        out_shape=[
            jax.ShapeDtypeStruct((B, MLA_HEADS, LANES, S), BF16),
            jax.ShapeDtypeStruct((B, MLA_HEADS, S, LANES), BF16),
            jax.ShapeDtypeStruct((B, MLA_HEADS * V_DIM, S), BF16),
            jax.ShapeDtypeStruct((B, S, HY_COLS), F32),
        ],
        compiler_params=_params("parallel", "parallel"),
        name="in_proj",
    )(x, x, x, sc1, sh1, cos_t, sin_t, win_aug, qg, wq, wqr, kvg, wk, wv, conv_w, conv_b[None, :])


ATT_AUG = 16


def _attn_kernel(qt_ref, k_ref, vt_ref, o_ref, s_scr):
    S = k_ref.shape[2]
    tq = qt_ref.shape[3]
    nk = S // ATT_TK
    ones_rows = jnp.where(lax.broadcasted_iota(jnp.int32, (ATT_AUG, ATT_TK), 0) == 0, 1.0, 0.0).astype(BF16)

    def scores(c, slot):
        k0 = pl.multiple_of(c * ATT_TK, ATT_TK)
        for h in range(2):
            s_scr[slot, h] = jnp.dot(k_ref[0, h, pl.ds(k0, ATT_TK), :], qt_ref[0, h], preferred_element_type=F32)

    def absorb(c, slot, carry):
        k0 = pl.multiple_of(c * ATT_TK, ATT_TK)
        out = []
        for h in range(2):
            m_prev, acc = carry[h]
            s = s_scr[slot, h]
            m_new = jnp.maximum(m_prev, jnp.max(s, axis=0, keepdims=True))
            alpha = jnp.exp2(m_prev - m_new)
            p = jnp.exp2(s - m_new).astype(BF16)
            v_aug = jnp.concatenate([vt_ref[0, h * V_DIM:(h + 1) * V_DIM, pl.ds(k0, ATT_TK)], ones_rows], axis=0)
            out.append((m_new, alpha * acc + jnp.dot(v_aug, p, preferred_element_type=F32)))
        return tuple(out)

    def pair(i, carry):
        c0 = 2 * i
        scores(c0 + 1, 1)
        carry = absorb(c0, 0, carry)
        scores(jnp.minimum(c0 + 2, nk - 1), 0)
        return absorb(c0 + 1, 1, carry)

    scores(0, 0)
    init = tuple((jnp.full((1, tq), -jnp.inf, F32), jnp.zeros((V_DIM + ATT_AUG, tq), F32)) for _ in range(2))
    fin = lax.fori_loop(0, nk // 2, pair, init, unroll=4)
    o = jnp.concatenate([acc[:V_DIM] * (1.0 / acc[V_DIM:V_DIM + 1]) for (_, acc) in fin], axis=0)
    o_ref[0] = o.T


def _attention(qt, k, vt):
    B, H, _, S = qt.shape
    tq = ATT_TQ
    return pl.pallas_call(
        _attn_kernel,
        grid=(B, H // 2, S // tq),
        in_specs=[
            pl.BlockSpec((1, 2, LANES, tq), lambda b, hp, i: (b, hp, 0, i)),
            pl.BlockSpec((1, 2, S, LANES), lambda b, hp, i: (b, hp, 0, 0)),
            pl.BlockSpec((1, 2 * V_DIM, S), lambda b, hp, i: (b, hp, 0)),
        ],
        out_specs=pl.BlockSpec((1, tq, 2 * V_DIM), lambda b, hp, i: (b, i, hp)),
        out_shape=jax.ShapeDtypeStruct((B, S, H * V_DIM), F32),
        scratch_shapes=[pltpu.VMEM((2, 2, ATT_TK, tq), F32)],
        compiler_params=_params("parallel", "parallel", "parallel"),
        name="mla_attention",
    )(qt, k, vt)


def _layer_norm(x, g, b):
    mu = jnp.mean(x, axis=-1, keepdims=True)
    xc = x - mu
    var = jnp.mean(xc * xc, axis=-1, keepdims=True)
    return xc * lax.rsqrt(var + EPS) * g + b


def _outproj_kernel(a_ref, hy_ref, x_ref, g1_ref, sc2_ref, sh2_ref, ag_ref, hg_ref, wo_ref,
                    lg_ref, lb_ref, wr_ref, x1_ref, u2_ref, aff_ref):
    half = a_ref.shape[2]
    an = _rms(a_ref[0], ag_ref[...]).astype(BF16)
    hn = _rms(hy_ref[0], hg_ref[...]).astype(BF16)
    mixed = (jnp.dot(an, wo_ref[:half, :], preferred_element_type=F32)
             + jnp.dot(hn, wo_ref[half:, :], preferred_element_type=F32))
    x1 = _layer_norm(DN_ALPHA * x_ref[0] + g1_ref[0] * mixed, lg_ref[...], lb_ref[...])
    x1_ref[0] = x1
    u2 = x1 * (1.0 + sc2_ref[0]) + sh2_ref[0]
    u2_ref[0] = u2.astype(BF16)
    logits = _dot3(u2, wr_ref[...])
    lane = lax.broadcasted_iota(jnp.int32, logits.shape, 1)
    logits = jnp.where(lane < N_EXPERTS, logits, -jnp.inf)
    e = jnp.exp(logits - jnp.max(logits, axis=-1, keepdims=True))
    aff = e / jnp.sum(e, axis=-1, keepdims=True)
    aff_ref[0] = aff.T[:N_EXPERTS, :]


def _outproj(a, hy, x, g1, sc2, sh2, ag, hg, wo, lg, lb, wr_pad):
    B, S, D = x.shape
    tm = TOK_TILE
    half = a.shape[2]
    full = lambda shape: pl.BlockSpec(shape, lambda b, i: (0,) * len(shape))
    tok = lambda w: pl.BlockSpec((1, tm, w), lambda b, i: (b, i, 0))
    per_b = pl.BlockSpec((1, 1, D), lambda b, i: (b, 0, 0))
    return pl.pallas_call(
        _outproj_kernel,
        grid=(B, S // tm),
        in_specs=[tok(half), tok(half), tok(D), per_b, per_b, per_b,
                  full((1, half)), full((1, half)), full((D, D)), full((1, D)), full((1, D)),
                  full((D, LANES))],
        out_specs=[tok(D), tok(D), pl.BlockSpec((1, N_EXPERTS, tm), lambda b, i: (b, 0, i))],
        out_shape=[
            jax.ShapeDtypeStruct((B, S, D), F32),
            jax.ShapeDtypeStruct((B, S, D), BF16),
            jax.ShapeDtypeStruct((B, N_EXPERTS, S), F32),
        ],
        compiler_params=_params("parallel", "parallel"),
        name="out_proj_ln1_router",
    )(a, hy, x, g1, sc2, sh2, ag, hg, wo, lg, lb, wr_pad)


def _select_kernel(aff_ref, ones_ref, utri_ref, bsum_ref, ltri_ref, pos_ref, posall_ref, g_ref, *, cap):
    aff = aff_ref[0]
    ones = ones_ref[...]

    def rowsum(mf):
        return jnp.dot(mf, ones, preferred_element_type=F32)

    def expert_total(mf):
        return jnp.dot(bsum_ref[...], rowsum(mf).astype(BF16), preferred_element_type=F32)

    def prefix(mf):
        return (jnp.dot(mf, utri_ref[...], preferred_element_type=F32)
                + jnp.dot(ltri_ref[...], rowsum(mf).astype(BF16), preferred_element_type=F32))

    def ind(m):
        return jnp.where(m, 1.0, 0.0).astype(BF16)

    def search(i, thr_bits):
        cand = thr_bits | jnp.left_shift(jnp.int32(1), 30 - i)
        enough = expert_total(ind(aff >= lax.bitcast_convert_type(cand, F32))) >= cap
        return jnp.where(enough, cand, thr_bits)

    thr = lax.bitcast_convert_type(lax.fori_loop(0, 31, search, jnp.zeros(aff.shape, jnp.int32)), F32)
    gt = aff > thr
    eq = aff == thr
    room = cap - expert_total(ind(gt))
    take_eq = jnp.where(eq, jnp.where(prefix(ind(eq)) < room, 1.0, 0.0), 0.0)
    sel = jnp.where(gt, 1.0, take_eq)
    pos = prefix(sel.astype(BF16)).astype(jnp.int32)
    posall_ref[0] = pos
    pos_ref[0] = jnp.where(sel > 0.0, pos, -1)
    g_ref[0] = jnp.where(sel > 0.0, aff, 0.0)


def _select(aff2, cap):
    B, R, _ = aff2.shape
    rpe = R // N_EXPERTS
    li = jnp.arange(LANES, dtype=jnp.int32)
    ri = jnp.arange(R, dtype=jnp.int32)
    same = (ri[:, None] // rpe) == (ri[None, :] // rpe)
    ones = jnp.ones((LANES, LANES), BF16)
    utri = (li[:, None] < li[None, :]).astype(BF16)
    bsum = same.astype(BF16)
    ltri = (same & (ri[None, :] < ri[:, None])).astype(BF16)
    blk = pl.BlockSpec((1, R, LANES), lambda b: (b, 0, 0))
    full = lambda arr: pl.BlockSpec(arr.shape, lambda b: (0, 0))
    return pl.pallas_call(
        functools.partial(_select_kernel, cap=cap),
        grid=(B,),
        in_specs=[blk, full(ones), full(utri), full(bsum), full(ltri)],
        out_specs=[blk, blk, blk],
        out_shape=[jax.ShapeDtypeStruct((B, R, LANES), jnp.int32), jax.ShapeDtypeStruct((B, R, LANES), jnp.int32),
                   jax.ShapeDtypeStruct((B, R, LANES), F32)],
        compiler_params=_params("parallel"),
        name="expert_choice_select",
    )(aff2, ones, utri, bsum, ltri)


GATHER_TOK = 1024
GATHER_WIN = 256


def _gather_kernel(offs_ref, u_ref, pos_ref, xe_ref, acc, *, cap, nblk):
    b, e = pl.program_id(0), pl.program_id(1)
    base = (b * N_EXPERTS + e) * (nblk + 1)
    rows = GATHER_TOK // LANES
    acc[...] = jnp.zeros(acc.shape, F32)
    slot = lax.broadcasted_iota(jnp.int32, (GATHER_WIN, LANES), 0)

    def block(j, carry):
        off, end = offs_ref[base + j], offs_ref[base + j + 1]
        w = (off // SUBLANES) * SUBLANES
        nsub = jnp.where(end > off, (end - w + GATHER_WIN - 1) // GATHER_WIN, 0)
        prow = pos_ref[0, pl.ds(pl.multiple_of(j * rows, rows), rows), :]
        ublk = u_ref[0, pl.ds(pl.multiple_of(j * GATHER_TOK, GATHER_TOK), GATHER_TOK), :]

        def window(i, c):
            s0 = pl.multiple_of(w + i * GATHER_WIN, SUBLANES)
            onehot = jnp.concatenate(
                [jnp.where(slot == prow[r:r + 1, :] - s0, 1.0, 0.0).astype(BF16) for r in range(rows)], axis=1)
            acc[pl.ds(s0, GATHER_WIN), :] += jnp.dot(onehot, ublk, preferred_element_type=F32)
            return c

        lax.fori_loop(0, nsub, window, 0)
        return carry

    lax.fori_loop(0, nblk, block, 0)
    xe_ref[0] = acc[:cap, :].astype(BF16)


def _gather(offs, u2, pos_sel, cap):
    B, S, D = u2.shape
    rpe = S // LANES
    nblk = S // GATHER_TOK
    return pl.pallas_call(
        functools.partial(_gather_kernel, cap=cap, nblk=nblk),
        grid_spec=pltpu.PrefetchScalarGridSpec(
            num_scalar_prefetch=1,
            grid=(B, N_EXPERTS),
            in_specs=[pl.BlockSpec((1, S, D), lambda b, e, o: (b, 0, 0)),
                      pl.BlockSpec((1, rpe, LANES), lambda b, e, o: (b, e, 0))],
            out_specs=pl.BlockSpec((1, cap, D), lambda b, e, o: (e, b, 0)),
            scratch_shapes=[pltpu.VMEM((cap + GATHER_WIN + SUBLANES, D), F32)],
        ),
        out_shape=jax.ShapeDtypeStruct((N_EXPERTS, B * cap, D), BF16),
        compiler_params=_params("parallel", "parallel"),
        name="expert_gather",
    )(offs, u2, pos_sel)


def _moe_kernel(x_ref, wg_ref, wu_ref, wd_ref, o_ref, wgb, wub, wdb):
    @pl.when(pl.program_id(1) == 0)
    def _():
        wgb[...] = wg_ref[0].astype(BF16)
        wub[...] = wu_ref[0].astype(BF16)
        wdb[...] = wd_ref[0].astype(BF16)

    x = x_ref[0]
    hg = jnp.dot(x, wgb[...], preferred_element_type=F32)
    hu = jnp.dot(x, wub[...], preferred_element_type=F32)
    h = (hg * jax.nn.sigmoid(hg) * hu).astype(BF16)
    o_ref[0] = jnp.dot(h, wdb[...], preferred_element_type=F32).astype(BF16)


def _moe_ffn(xe, w_gate, w_up, w_down):
    E, R, D = xe.shape
    Fh = w_gate.shape[2]
    tr = MOE_ROWS
    return pl.pallas_call(
        _moe_kernel,
        grid=(E, R // tr),
        in_specs=[
            pl.BlockSpec((1, tr, D), lambda e, r: (e, r, 0)),
            pl.BlockSpec((1, D, Fh), lambda e, r: (e, 0, 0)),
            pl.BlockSpec((1, D, Fh), lambda e, r: (e, 0, 0)),
            pl.BlockSpec((1, Fh, D), lambda e, r: (e, 0, 0)),
        ],
        out_specs=pl.BlockSpec((1, tr, D), lambda e, r: (e, r, 0)),
        out_shape=jax.ShapeDtypeStruct((E, R, D), BF16),
        scratch_shapes=[pltpu.VMEM((D, Fh), BF16), pltpu.VMEM((D, Fh), BF16), pltpu.VMEM((Fh, D), BF16)],
        compiler_params=_params("parallel", "arbitrary"),
        name="moe_swiglu",
    )(xe, w_gate, w_up, w_down)


CMB_TOK = 512
CMB_WIN = 128
BF16_ROWS = 16


def _combine_kernel(offs_ref, x1_ref, pos_ref, g_ref, g2_ref, lg_ref, lb_ref, ye_hbm, o_ref,
                    rhs, lhs, acc, sem, *, cap, nblk, nbatch):
    b, i = pl.program_id(0), pl.program_id(1)
    W = CMB_WIN
    total = ye_hbm.shape[0]
    row_base, first, npass = [], [], jnp.int32(0)
    for e in range(N_EXPERTS):
        k = (b * N_EXPERTS + e) * (nblk + 1) + i
        off, end = offs_ref[k], offs_ref[k + 1]
        lo = (off // BF16_ROWS) * BF16_ROWS
        row_base.append((e * nbatch + b) * cap)
        first.append(lo)
        npass = jnp.maximum(npass, jnp.where(end > off, (end - lo + W - 1) // W, 0))
    acc[...] = jnp.zeros(acc.shape, F32)
    lane_e = lax.broadcasted_iota(jnp.int32, (1, N_EXPERTS), 1)
    lane_w = lax.broadcasted_iota(jnp.int32, (CMB_TOK, W), 1)

    def window_copy(e, start):
        return pltpu.make_async_copy(ye_hbm.at[pl.ds(pl.multiple_of(start, BF16_ROWS), W), :],
                                     rhs.at[pl.ds(e * W, W), :], sem.at[e])

    def one_pass(p, carry):
        lo_vec = jnp.zeros((1, N_EXPERTS), jnp.int32)
        shift_vec = jnp.zeros((1, N_EXPERTS), jnp.int32)
        starts = []
        for e in range(N_EXPERTS):
            lo = first[e] + p * W
            start = jnp.minimum(row_base[e] + lo, total - W)
            starts.append(start)
            lo_vec = jnp.where(lane_e == e, lo, lo_vec)
            shift_vec = jnp.where(lane_e == e, row_base[e] + lo - start, shift_vec)
        for e in range(N_EXPERTS):
            window_copy(e, starts[e]).start()
        d = pos_ref[0] - lo_vec
        rel = jnp.where((d >= 0) & (d < W), d + shift_vec, -1)
        g = g_ref[0]
        for e in range(N_EXPERTS):
            lhs[:, e * W:(e + 1) * W] = jnp.where(lane_w == rel[:, e:e + 1], g[:, e:e + 1], 0.0).astype(BF16)
        for e in range(N_EXPERTS):
            window_copy(e, starts[e]).wait()
        acc[...] += jnp.dot(lhs[...], rhs[...], preferred_element_type=F32)
        return carry

    lax.fori_loop(0, npass, one_pass, 0)
    o_ref[0] = _layer_norm(DN_ALPHA * x1_ref[0] + g2_ref[0] * acc[...], lg_ref[...], lb_ref[...])


def _combine_ln2(offs, x1, pos_tok, g_tok, g2, lg, lb, ye_rows, cap):
    B, S, D = x1.shape
    T = CMB_TOK
    nblk = S // T
    tok = lambda w: pl.BlockSpec((1, T, w), lambda b, i, o: (b, i, 0))
    vec = pl.BlockSpec((1, D), lambda b, i, o: (0, 0))
    return pl.pallas_call(
        functools.partial(_combine_kernel, cap=cap, nblk=nblk, nbatch=B),
        grid_spec=pltpu.PrefetchScalarGridSpec(
            num_scalar_prefetch=1,
            grid=(B, nblk),
            in_specs=[tok(D), tok(N_EXPERTS), tok(N_EXPERTS),
                      pl.BlockSpec((1, 1, D), lambda b, i, o: (b, 0, 0)), vec, vec,
                      pl.BlockSpec(memory_space=pl.ANY)],
            out_specs=tok(D),
            scratch_shapes=[pltpu.VMEM((N_EXPERTS * CMB_WIN, D), BF16),
                            pltpu.VMEM((T, N_EXPERTS * CMB_WIN), BF16),
                            pltpu.VMEM((T, D), F32),
                            pltpu.SemaphoreType.DMA((N_EXPERTS,))],
        ),
        out_shape=jax.ShapeDtypeStruct((B, S, D), F32),
        compiler_params=_params("parallel", "parallel"),
        name="combine_residual_ln2",
    )(offs, x1, pos_tok, g_tok, g2, lg, lb, ye_rows)


DFT_R = 128
K1_HALF = 40
K1_VALID = DFT_R // 2 + 1
K1_GROUPS = (K1_HALF // SUBLANES, -(-(K1_VALID - K1_HALF) // SUBLANES))
K1_SPEC = K1_HALF + K1_GROUPS[1] * SUBLANES
HY_CT = 128


@functools.lru_cache(maxsize=None)
def _dft_tables_np(S):
    N = 2 * S
    na = S // DFT_R
    r = np.arange(2 * K1_HALF)
    k1 = np.arange(2)[:, None] * K1_HALF + r[None, :] % K1_HALF
    valid = (k1 < K1_VALID).astype(np.float64)
    n = DFT_R * np.arange(na)[None, :] + np.arange(DFT_R)[:, None]
    theta = ((n[None, :, None, :] * k1[:, None, :, None]) % N) * (2.0 * math.pi / N)
    is_im = (r >= K1_HALF)[None, None, :, None]
    g = np.where(is_im, -np.sin(theta), np.cos(theta)) * valid[:, None, :, None]
    cw = np.where((k1 == 0) | (k1 == DFT_R // 2), 1.0, 2.0) * valid / N
    gi = np.transpose(g * cw[:, None, :, None], (0, 1, 3, 2))
    ph = ((np.arange(DFT_R)[:, None] * np.arange(DFT_R)[None, :]) % DFT_R) * (2.0 * math.pi / DFT_R)
    C, Sn = np.cos(ph), np.sin(ph)
    f3f = np.block([[C, Sn], [-Sn, C]])
    f3i = np.block([[C, -Sn], [Sn, C]])

    def pair_lanes(t):
        h, nb, nr, nc = t.shape
        return t.reshape(h, nb // 2, 2, nr, nc).transpose(0, 1, 3, 2, 4).reshape(h, nb // 2, nr, 2 * nc)

    return tuple(np.ascontiguousarray(t, dtype=np.float32) for t in (pair_lanes(g), pair_lanes(gi), f3f, f3i))


def _dft_tables(S):
    return tuple(jnp.asarray(t).astype(BF16) for t in _dft_tables_np(S))


def _swap01(x):
    return jnp.swapaxes(x, 0, 1)


def _block_diag_lanes(top, bot):
    return jnp.concatenate([jnp.concatenate([top, jnp.zeros((top.shape[0], bot.shape[1]), top.dtype)], axis=1),
                            jnp.concatenate([jnp.zeros((bot.shape[0], top.shape[1]), bot.dtype), bot], axis=1)], axis=0)


def _dft_stage1(z_ref, gf_ref, hf, a_scr):
    ct = a_scr.shape[2]

    def body(g, carry):
        b0 = pl.multiple_of(g * SUBLANES, SUBLANES)
        zt = _swap01(z_ref[:, pl.ds(b0, SUBLANES), :]).astype(BF16)
        for p in range(SUBLANES // 2):
            a2 = jnp.dot(gf_ref[hf, g * (SUBLANES // 2) + p], _block_diag_lanes(zt[2 * p], zt[2 * p + 1]),
                         preferred_element_type=F32)
            a_scr[b0 + 2 * p] = a2[:, :ct]
            a_scr[b0 + 2 * p + 1] = a2[:, ct:]
        return carry

    lax.fori_loop(0, DFT_R // SUBLANES, body, 0, unroll=4)


def _dft_stage3_operands(a_scr, r0):
    mr = _swap01(a_scr[:, pl.ds(r0, SUBLANES), :])
    mi = _swap01(a_scr[:, pl.ds(K1_HALF + r0, SUBLANES), :])
    return [jnp.concatenate([jnp.concatenate([mr[j], mi[j]], axis=0),
                             jnp.concatenate([mr[j + 1], mi[j + 1]], axis=0)], axis=1).astype(BF16)
            for j in range(0, SUBLANES, 2)]


def _filter_mlp_kernel(feat_ref, w1_ref, b1_ref, fr_ref, w2_ref, b2_ref, w3_ref, b3_ref, dec_ref, o_ref):
    feats = feat_ref[...]
    fr = fr_ref[...]
    h = jnp.sin(fr * (jnp.dot(feats, w1_ref[...], precision=HIGHEST, preferred_element_type=F32) + b1_ref[...]))
    h = jnp.sin(fr * (jnp.dot(h, w2_ref[...], precision=HIGHEST, preferred_element_type=F32) + b2_ref[...]))
    h = _dot3(h, w3_ref[...]) + b3_ref[...]
    h = h * jnp.exp(-feats[:, 0:1] * jnp.abs(dec_ref[...]))
    row = lax.broadcasted_iota(jnp.int32, h.shape, 0) + pl.program_id(0) * h.shape[0]
    col = lax.broadcasted_iota(jnp.int32, h.shape, 1)
    o_ref[...] = jnp.where((row == 0) & (col >= h.shape[1] // 2), 0.0, h)


@functools.lru_cache(maxsize=None)
def _filter_features_np(L, nf):
    pos = np.arange(L, dtype=np.float64)
    t = pos[:, None] / (L - 1)
    bands = (nf - 1) // 2
    freqs = np.linspace(1e-4, bands - 1, bands)
    phase = (2.0 * math.pi / L) * pos[:, None] * freqs[None, :]
    feats = np.concatenate([t, np.cos(phase), -np.sin(phase), np.zeros((L, LANES - nf))], axis=-1)
    return np.ascontiguousarray(feats, dtype=np.float32)


def _filter_taps(L, w1, b1, freq, w2, b2, w3, b3, decay):
    nf = w1.shape[0]
    feats = jnp.asarray(_filter_features_np(L, nf))
    w1p = jnp.concatenate([w1, jnp.zeros((LANES - nf, w1.shape[1]), F32)], axis=0)
    fo = w1.shape[1]
    ncol = w3.shape[1]
    tl = 512
    full = lambda shape: pl.BlockSpec(shape, lambda i: (0,) * len(shape))
    return pl.pallas_call(
        _filter_mlp_kernel,
        grid=(L // tl,),
        in_specs=[pl.BlockSpec((tl, LANES), lambda i: (i, 0)), full((LANES, fo)), full((1, fo)), full((1, fo)),
                  full((fo, fo)), full((1, fo)), full((fo, ncol)), full((1, ncol)), full((1, ncol))],
        out_specs=pl.BlockSpec((tl, ncol), lambda i: (i, 0)),
        out_shape=jax.ShapeDtypeStruct((L, ncol), F32),
        compiler_params=_params("parallel"),
        name="hyena_filter_mlp",
    )(feats, w1p, b1[None, :], freq[None, :], w2, b2[None, :], w3, b3[None, :], decay.reshape(1, ncol))


def _filter_spectrum_kernel(hf_ref, hb_ref, gf_ref, f3f_ref, o_ref, a_scr, x_scr):
    ct = a_scr.shape[2]
    for t, src in enumerate((hf_ref, hb_ref)):
        for hf in range(2):
            _dft_stage1(src, gf_ref, hf, a_scr)

            def body(g, carry):
                r0 = pl.multiple_of(g * SUBLANES, SUBLANES)
                for p, m in enumerate(_dft_stage3_operands(a_scr, r0)):
                    x2 = jnp.dot(f3f_ref[...], m, preferred_element_type=F32)
                    for q in range(2):
                        k1 = hf * K1_HALF + r0 + 2 * p + q
                        x = x2[:, q * ct:(q + 1) * ct]
                        if t == 0:
                            x_scr[k1] = x
                        else:
                            xf = x_scr[k1]
                            o_ref[0, k1] = jnp.concatenate([xf[:DFT_R] + x[:DFT_R], xf[DFT_R:] - x[DFT_R:]],
                                                           axis=0).astype(BF16)
                return carry

            lax.fori_loop(0, K1_GROUPS[hf], body, 0, unroll=True)


def _filter_spectrum(taps3, gf, f3f):
    na, nb, ncol = taps3.shape
    C = ncol // 4
    nct = C // HY_CT
    ct = HY_CT
    return pl.pallas_call(
        _filter_spectrum_kernel,
        grid=(2, nct),
        in_specs=[
            pl.BlockSpec((na, nb, ct), lambda o, j: (0, 0, o * nct + j)),
            pl.BlockSpec((na, nb, ct), lambda o, j: (0, 0, 2 * nct + o * nct + j)),
            pl.BlockSpec(gf.shape, lambda o, j: (0, 0, 0, 0), pipeline_mode=pl.Buffered(1)),
            pl.BlockSpec(f3f.shape, lambda o, j: (0, 0), pipeline_mode=pl.Buffered(1)),
        ],
        out_specs=pl.BlockSpec((1, K1_SPEC, 2 * DFT_R, ct), lambda o, j: (o, 0, 0, j)),
        out_shape=jax.ShapeDtypeStruct((2, K1_SPEC, 2 * DFT_R, C), BF16),
        scratch_shapes=[pltpu.VMEM((DFT_R, 2 * K1_HALF, ct), F32), pltpu.VMEM((K1_SPEC, 2 * DFT_R, ct), F32)],
        compiler_params=_params("parallel", "parallel"),
        name="hyena_filter_spectrum",
    )(taps3, taps3, gf, f3f)


def _long_conv_kernel(z_ref, xg_ref, h_ref, bias_ref, gf_ref, f3f_ref, f3i_ref, gi_ref, o_ref, a_scr, b_scr):
    ct = a_scr.shape[2]
    bias = bias_ref[0]
    for hf in range(2):
        _dft_stage1(z_ref.at[0], gf_ref, hf, a_scr)
        ngrp = K1_GROUPS[hf]
        if ngrp * SUBLANES < K1_HALF:
            b_scr[ngrp * SUBLANES:] = jnp.zeros((K1_HALF - ngrp * SUBLANES,) + b_scr.shape[1:], F32)

        def spectral(g, carry):
            r0 = pl.multiple_of(g * SUBLANES, SUBLANES)
            for p, m in enumerate(_dft_stage3_operands(a_scr, r0)):
                x = jnp.dot(f3f_ref[...], m, preferred_element_type=F32)
                k1 = hf * K1_HALF + r0 + 2 * p
                h = jnp.concatenate([h_ref[0, k1], h_ref[0, k1 + 1]], axis=1).astype(F32)
                xr, xi, hr, hi = x[:DFT_R], x[DFT_R:], h[:DFT_R], h[DFT_R:]
                y = jnp.concatenate([xr * hr - xi * hi, xr * hi + xi * hr], axis=0).astype(BF16)
                bb = jnp.dot(f3i_ref[...], y, preferred_element_type=F32)
                b_scr[r0 + 2 * p] = bb[:, :ct]
                b_scr[r0 + 2 * p + 1] = bb[:, ct:]
            return carry

        lax.fori_loop(0, ngrp, spectral, 0, unroll=True)

        def inverse(g, carry):
            b0 = pl.multiple_of(g * SUBLANES, SUBLANES)
            br = _swap01(b_scr[:, pl.ds(b0, SUBLANES), :])
            bi = _swap01(b_scr[:, pl.ds(DFT_R + b0, SUBLANES), :])
            ys = []
            for p in range(SUBLANES // 2):
                bm = [jnp.concatenate([br[2 * p + q], bi[2 * p + q]], axis=0).astype(BF16) for q in range(2)]
                y2 = jnp.dot(gi_ref[hf, g * (SUBLANES // 2) + p], _block_diag_lanes(bm[0], bm[1]),
                             preferred_element_type=F32)
                ys += [y2[:, :ct], y2[:, ct:]]
            y = _swap01(jnp.stack(ys, axis=0))
            sl = (0, slice(None), pl.ds(b0, SUBLANES), slice(None))
            if hf == 0:
                o_ref[sl] = y
            else:
                o_ref[sl] = xg_ref[sl] * (o_ref[sl] + y + z_ref[sl] * bias)
            return carry

        lax.fori_loop(0, DFT_R // SUBLANES, inverse, 0, unroll=4)


def _long_conv(z_arr, z_off, xg_arr, xg_off, spec, order, bias3, gf, f3f, f3i, gi):
    B, na, nb, _ = z_arr.shape
    C = spec.shape[3]
    ct = HY_CT
    const = lambda arr: pl.BlockSpec(arr.shape, lambda j, b: (0,) * arr.ndim, pipeline_mode=pl.Buffered(1))
    return pl.pallas_call(
        _long_conv_kernel,
        grid=(C // ct, B),
        in_specs=[
            pl.BlockSpec((1, na, nb, ct), lambda j, b: (b, 0, 0, z_off + j)),
            pl.BlockSpec((1, na, nb, ct), lambda j, b: (b, 0, 0, xg_off + j)),
            pl.BlockSpec((1, K1_SPEC, 2 * DFT_R, ct), lambda j, b: (order, 0, 0, j), pipeline_mode=pl.Buffered(1)),
            pl.BlockSpec((1, 1, ct), lambda j, b: (order, 0, j)),
            const(gf), const(f3f), const(f3i), const(gi),
        ],
        out_specs=pl.BlockSpec((1, na, nb, ct), lambda j, b: (b, 0, 0, j)),
        out_shape=jax.ShapeDtypeStruct((B, na, nb, C), F32),
        scratch_shapes=[pltpu.VMEM((DFT_R, 2 * K1_HALF, ct), F32), pltpu.VMEM((K1_HALF, 2 * DFT_R, ct), F32)],
        compiler_params=_params("parallel", "parallel"),
        name="hyena_long_conv",
    )(z_arr, xg_arr, spec, bias3, gf, f3f, f3i, gi)


def _hyena_group(hy_conv, fw1, fb1, ffreq, fw2, fb2, fw3, fb3, decay, hyena_bias):
    B, S, _ = hy_conv.shape
    na = S // DFT_R
    gf, gi, f3f, f3i = _dft_tables(S)
    taps = _filter_taps(S, fw1, fb1, ffreq, fw2, fb2, fw3, fb3, decay)
    spec = _filter_spectrum(taps.reshape(na, DFT_R, -1), gf, f3f)
    u = hy_conv.reshape(B, na, DFT_R, -1)
    nct = HY_W // HY_CT
    bias3 = hyena_bias[:, None, :]
    z1 = _long_conv(u, 0, u, nct, spec, 0, bias3, gf, f3f, f3i, gi)
    return _long_conv(z1, 0, u, 2 * nct, spec, 1, bias3, gf, f3f, f3i, gi).reshape(B, S, HY_W)


def _rope_rot_cols(w):
    half = ROPE // 2
    return jnp.concatenate([-w[..., half:], w[..., :half]], axis=-1)


def _prep_weights(w_in, w_qb, w_kvb):
    D = w_in.shape[0]
    o = Q_RANK + KV_RANK
    w_kr = w_in[:, o:o + ROPE]
    z64 = jnp.zeros((D, NOPE), F32)
    z32 = jnp.zeros((D, LANES - NOPE - ROPE), F32)
    win_aug = jnp.concatenate([
        w_in[:, :o],
        z64, w_kr, z32,
        z64, _rope_rot_cols(w_kr), z32,
        w_in[:, o + ROPE:],
    ], axis=1).astype(BF16)
    wq = w_qb.reshape(Q_RANK, MLA_HEADS, NOPE + ROPE)
    zq = jnp.zeros((Q_RANK, MLA_HEADS, LANES - NOPE - ROPE), F32)
    wq_main = jnp.concatenate([wq, zq], axis=-1).reshape(Q_RANK, MLA_HEADS * LANES).astype(BF16)
    wq_rot = jnp.concatenate([jnp.zeros((Q_RANK, MLA_HEADS, NOPE), F32), _rope_rot_cols(wq[..., NOPE:]), zq],
                             axis=-1).reshape(Q_RANK, MLA_HEADS * LANES).astype(BF16)
    wkv = w_kvb.reshape(KV_RANK, MLA_HEADS, NOPE + V_DIM)
    wk = jnp.concatenate([wkv[..., :NOPE], jnp.zeros((KV_RANK, MLA_HEADS, LANES - NOPE), F32)],
                         axis=-1).reshape(KV_RANK, MLA_HEADS * LANES).astype(BF16)
    wv = wkv[..., NOPE:].reshape(KV_RANK, MLA_HEADS * V_DIM).astype(BF16)
    return win_aug, wq_main, wq_rot, wk, wv


def kernel(x, c, positions, w_ada, b_ada, w_in, q_norm_g, w_qb, kv_norm_g, w_kvb, conv_w, conv_b, filt_w1, filt_b1, filt_freq, filt_w2, filt_b2, filt_w3, filt_b3, hyena_decay, hyena_bias, attn_out_g, hyena_out_g, w_o, ln1_g, ln1_b, w_router, w_gate, w_up, w_down, ln2_g, ln2_b):
    B, S, D = x.shape
    l = 0
    c8 = jnp.zeros((SUBLANES, D), F32).at[:B].set(c)
    mod = _ada_mod(c8, w_ada[l], b_ada[l][None, :])[:B]
    sh1, sc1, g1, sh2, sc2, g2 = [m[:, None, :] for m in jnp.split(mod, 6, axis=-1)]

    win_aug, wq_main, wq_rot, wk, wv = _prep_weights(w_in[l], w_qb[l], w_kvb[l])
    cos_h, sin_h = _rope_tables(positions)
    pad = lambda n, v: jnp.full((B, n, S), v, F32)
    cos_t = jnp.concatenate([pad(NOPE, 1.0), cos_h, cos_h, pad(LANES - NOPE - ROPE, 1.0)], axis=1)
    sin_t = jnp.concatenate([pad(NOPE, 0.0), sin_h, sin_h, pad(LANES - NOPE - ROPE, 0.0)], axis=1)
    q, kt, v, hy_conv = _inproj(x, sc1, sh1, cos_t, sin_t, win_aug, q_norm_g[l][None, :], wq_main, wq_rot,
                                kv_norm_g[l][None, :], wk, wv, conv_w[l], conv_b[l])
    a = _attention(q, kt, v)

    hy = _hyena_group(hy_conv, filt_w1[l], filt_b1[l], filt_freq[l], filt_w2[l],
                      filt_b2[l], filt_w3[l], filt_b3[l], hyena_decay[l], hyena_bias[l])

    wr_pad = jnp.concatenate([w_router[l], jnp.zeros((D, LANES - N_EXPERTS), F32)], axis=1)
    x1, u2, aff_t = _outproj(a, hy, x, g1, sc2, sh2, attn_out_g[l][None, :], hyena_out_g[l][None, :],
                             w_o[l].astype(BF16), ln1_g[l][None, :], ln1_b[l][None, :], wr_pad)

    cap = EC_FACTOR * S // N_EXPERTS
    rpe = S // LANES
    aff2 = aff_t.reshape(B, N_EXPERTS * rpe, LANES)
    pos_sel, pos_all, gsel = _select(aff2, cap)
    row_start = pos_all.reshape(B, N_EXPERTS, rpe, LANES)[:, :, :, 0]

    def block_offsets(tokens):
        ends = jnp.full((B, N_EXPERTS, 1), cap, jnp.int32)
        return jnp.concatenate([row_start[:, :, ::tokens // LANES], ends], axis=-1).reshape(-1)

    xe = _gather(block_offsets(GATHER_TOK), u2, pos_sel, cap)
    ye = _moe_ffn(xe, w_gate[l], w_up[l], w_down[l])
    pos_tok = pos_sel.reshape(B, N_EXPERTS, S).transpose(0, 2, 1)
    g_tok = gsel.reshape(B, N_EXPERTS, S).transpose(0, 2, 1)
    return _combine_ln2(block_offsets(CMB_TOK), x1, pos_tok, g_tok, g2, ln2_g[l][None, :], ln2_b[l][None, :],
                        ye.reshape(N_EXPERTS * B * cap, D), cap)
```

```python
import functools
import math

import jax
import jax.numpy as jnp
import numpy as np
from jax import lax
from jax.experimental import pallas as pl
from jax.experimental.pallas import tpu as pltpu

F32 = jnp.float32
BF16 = jnp.bfloat16
HIGHEST = lax.Precision.HIGHEST

LANES = 128
SUBLANES = 8
VMEM_LIMIT = 56 * 1024 * 1024

D_MODEL = 1024
MLA_HEADS = 8
NOPE = 64
ROPE = 32
V_DIM = 64
Q_RANK = 256
KV_RANK = 128
HY_W = 512
HY_COLS = 3 * HY_W
N_EXPERTS = 16
EC_FACTOR = 2
ROPE_THETA = 10000.0
EPS = 1e-5
DN_ALPHA = 2.0 ** 0.25
LOG2E = 1.4426950408889634

TOK_TILE = 512
ATT_TQ = 512
ATT_TK = 512
MOE_ROWS = 512


def _params(*sem):
    return pltpu.CompilerParams(dimension_semantics=sem, vmem_limit_bytes=VMEM_LIMIT)


def _ada_kernel(c_ref, w_ref, b_ref, o_ref):
    c = c_ref[...]
    sc = c * jax.nn.sigmoid(c)
    o_ref[...] = jnp.dot(sc, w_ref[...], precision=HIGHEST, preferred_element_type=F32) + b_ref[...]


def _ada_mod(c8, w_ada, b_ada):
    n = w_ada.shape[1]
    tn = 1024
    return pl.pallas_call(
        _ada_kernel,
        grid=(n // tn,),
        in_specs=[
            pl.BlockSpec((SUBLANES, D_MODEL), lambda j: (0, 0)),
            pl.BlockSpec((D_MODEL, tn), lambda j: (0, j)),
            pl.BlockSpec((1, tn), lambda j: (0, j)),
        ],
        out_specs=pl.BlockSpec((SUBLANES, tn), lambda j: (0, j)),
        out_shape=jax.ShapeDtypeStruct((SUBLANES, n), F32),
        compiler_params=_params("arbitrary"),
        name="ada_mod",
    )(c8, w_ada, b_ada)


def _dot3(a, b):
    ah, bh = a.astype(BF16), b.astype(BF16)
    al = (a - ah.astype(F32)).astype(BF16)
    bl = (b - bh.astype(F32)).astype(BF16)
    dot = functools.partial(jnp.dot, preferred_element_type=F32)
    return dot(ah, bh) + (dot(ah, bl) + dot(al, bh))


def _rms(x, g):
    return x * lax.rsqrt(jnp.mean(x * x, axis=-1, keepdims=True) + EPS) * g


def _rope_kernel(pos_ref, freq_ref, cos_ref, sin_ref):
    ang = freq_ref[...] * pos_ref[0].astype(F32)
    cos_ref[0] = jnp.cos(ang)
    sin_ref[0] = jnp.sin(ang)


def _rope_tables(positions):
    B, S = positions.shape
    half = ROPE // 2
    inv_freq = (ROPE_THETA ** (-jnp.arange(half, dtype=F32) / half))[:, None]
    out = jax.ShapeDtypeStruct((B, half, S), F32)
    blk = pl.BlockSpec((1, half, S), lambda b: (b, 0, 0))
    return pl.pallas_call(
        _rope_kernel,
        grid=(B,),
        in_specs=[pl.BlockSpec((1, 1, S), lambda b: (b, 0, 0)), pl.BlockSpec((half, 1), lambda b: (0, 0))],
        out_specs=[blk, blk],
        out_shape=[out, out],
        compiler_params=_params("parallel"),
        name="rope_tables",
    )(positions[:, None, :], inv_freq)


def _inproj_kernel(x_ref, xp_ref, xn_ref, sc_ref, sh_ref, cos_ref, sin_ref, win_ref, qg_ref, wq_ref, wqr_ref,
                   kvg_ref, wk_ref, wv_ref, cw_ref, cb_ref, q_ref, k_ref, v_ref, hy_ref):
    def modulate(xv):
        return (xv * (1.0 + sc_ref[0]) + sh_ref[0]).astype(BF16)

    u = modulate(x_ref[0])
    cos = cos_ref[0].T
    sin = sin_ref[0].T

    def proj(lo, hi):
        return jnp.dot(u, win_ref[:, lo:hi], preferred_element_type=F32)

    cq = _rms(proj(0, Q_RANK), qg_ref[...]).astype(BF16)
    q_main = jnp.dot(cq, wq_ref[...], preferred_element_type=F32)
    q_rot = jnp.dot(cq, wqr_ref[...], preferred_element_type=F32)
    qscale = LOG2E * (NOPE + ROPE) ** -0.5
    cq_s = cos * qscale
    sq_s = sin * qscale
    for h in range(MLA_HEADS):
        sl = slice(h * LANES, (h + 1) * LANES)
        q_ref[0, h] = (q_main[:, sl] * cq_s + q_rot[:, sl] * sq_s).T.astype(BF16)

    o = Q_RANK
    ckv = _rms(proj(o, o + KV_RANK), kvg_ref[...]).astype(BF16)
    kr = proj(o + KV_RANK, o + 2 * KV_RANK) * cos + proj(o + 2 * KV_RANK, o + 3 * KV_RANK) * sin
    kn = jnp.dot(ckv, wk_ref[...], preferred_element_type=F32)
    for h in range(MLA_HEADS):
        k_ref[0, h] = (kn[:, h * LANES:(h + 1) * LANES] + kr).astype(BF16)
    v_ref[0] = jnp.dot(ckv, wv_ref[...], preferred_element_type=F32).T.astype(BF16)

    o = Q_RANK + 3 * KV_RANK
    hy = proj(o, o + HY_COLS)
    tm = hy.shape[0]
    i, last = pl.program_id(1), pl.num_programs(1) - 1
    edge = jnp.dot(modulate(jnp.concatenate([xp_ref[0], xn_ref[0]], axis=0)), win_ref[:, o:o + HY_COLS],
                   preferred_element_type=F32)
    prev = jnp.where(i == 0, 0.0, edge[SUBLANES - 1:SUBLANES, :])
    nxt = jnp.where(i == last, 0.0, edge[SUBLANES:SUBLANES + 1, :])
    row = lax.broadcasted_iota(jnp.int32, hy.shape, 0)
    up = jnp.where(row == 0, prev, pltpu.roll(hy, 1, 0))
    dn = jnp.where(row == tm - 1, nxt, pltpu.roll(hy, tm - 1, 0))
    hy_ref[0] = cb_ref[...] + cw_ref[0:1, :] * up + cw_ref[1:2, :] * hy + cw_ref[2:3, :] * dn


def _inproj(x, sc1, sh1, cos_t, sin_t, win_aug, qg, wq, wqr, kvg, wk, wv, conv_w, conv_b):
    B, S, D = x.shape
    tm = TOK_TILE
    ncol = win_aug.shape[1]
    g8 = tm // SUBLANES
    n8 = S // SUBLANES
    full = lambda shape: pl.BlockSpec(shape, lambda b, i: (0,) * len(shape))
    return pl.pallas_call(
        _inproj_kernel,
        grid=(B, S // tm),
        in_specs=[
            pl.BlockSpec((1, tm, D), lambda b, i: (b, i, 0)),
            pl.BlockSpec((1, SUBLANES, D), lambda b, i: (b, jnp.maximum(i * g8 - 1, 0), 0)),
            pl.BlockSpec((1, SUBLANES, D), lambda b, i: (b, jnp.minimum((i + 1) * g8, n8 - 1), 0)),
            pl.BlockSpec((1, 1, D), lambda b, i: (b, 0, 0)),
            pl.BlockSpec((1, 1, D), lambda b, i: (b, 0, 0)),
            pl.BlockSpec((1, LANES, tm), lambda b, i: (b, 0, i)),
            pl.BlockSpec((1, LANES, tm), lambda b, i: (b, 0, i)),
            full((D, ncol)),
            full((1, Q_RANK)),
            full((Q_RANK, MLA_HEADS * LANES)),
            full((Q_RANK, MLA_HEADS * LANES)),
            full((1, KV_RANK)),
            full((KV_RANK, MLA_HEADS * LANES)),
            full((KV_RANK, MLA_HEADS * V_DIM)),
            full(conv_w.shape),
            full((1, HY_COLS)),
        ],
        out_specs=[
            pl.BlockSpec((1, MLA_HEADS, LANES, tm), lambda b, i: (b, 0, 0, i)),
            pl.BlockSpec((1, MLA_HEADS, tm, LANES), lambda b, i: (b, 0, i, 0)),
            pl.BlockSpec((1, MLA_HEADS * V_DIM, tm), lambda b, i: (b, 0, i)),
            pl.BlockSpec((1, tm, HY_COLS), lambda b, i: (b, i, 0)),
        ],
        out_shape=[
            jax.ShapeDtypeStruct((B, MLA_HEADS, LANES, S), BF16),
            jax.ShapeDtypeStruct((B, MLA_HEADS, S, LANES), BF16),
            jax.ShapeDtypeStruct((B, MLA_HEADS * V_DIM, S), BF16),
            jax.ShapeDtypeStruct((B, S, HY_COLS), F32),
        ],
        compiler_params=_params("parallel", "parallel"),
        name="in_proj",
    )(x, x, x, sc1, sh1, cos_t, sin_t, win_aug, qg, wq, wqr, kvg, wk, wv, conv_w, conv_b[None, :])


ATT_AUG = 16


def _attn_kernel(qt_ref, k_ref, vt_ref, o_ref, s_scr):
    S = k_ref.shape[2]
    tq = qt_ref.shape[3]
    nk = S // ATT_TK
    ones_rows = jnp.where(lax.broadcasted_iota(jnp.int32, (ATT_AUG, ATT_TK), 0) == 0, 1.0, 0.0).astype(BF16)

    def scores(c, slot):
        k0 = pl.multiple_of(c * ATT_TK, ATT_TK)
        for h in range(2):
            s_scr[slot, h] = jnp.dot(k_ref[0, h, pl.ds(k0, ATT_TK), :], qt_ref[0, h], preferred_element_type=F32)

    def absorb(c, slot, carry):
        k0 = pl.multiple_of(c * ATT_TK, ATT_TK)
        out = []
        for h in range(2):
            m_prev, acc = carry[h]
            s = s_scr[slot, h]
            m_new = jnp.maximum(m_prev, jnp.max(s, axis=0, keepdims=True))
            alpha = jnp.exp2(m_prev - m_new)
            p = jnp.exp2(s - m_new).astype(BF16)
            v_aug = jnp.concatenate([vt_ref[0, h * V_DIM:(h + 1) * V_DIM, pl.ds(k0, ATT_TK)], ones_rows], axis=0)
            out.append((m_new, alpha * acc + jnp.dot(v_aug, p, preferred_element_type=F32)))
        return tuple(out)

    def pair(i, carry):
        c0 = 2 * i
        scores(c0 + 1, 1)
        carry = absorb(c0, 0, carry)
        scores(jnp.minimum(c0 + 2, nk - 1), 0)
        return absorb(c0 + 1, 1, carry)

    scores(0, 0)
    init = tuple((jnp.full((1, tq), -jnp.inf, F32), jnp.zeros((V_DIM + ATT_AUG, tq), F32)) for _ in range(2))
    fin = lax.fori_loop(0, nk // 2, pair, init, unroll=4)
    o = jnp.concatenate([acc[:V_DIM] * (1.0 / acc[V_DIM:V_DIM + 1]) for (_, acc) in fin], axis=0)
    o_ref[0] = o.T


def _attention(qt, k, vt):
    B, H, _, S = qt.shape
    tq = ATT_TQ
    return pl.pallas_call(
        _attn_kernel,
        grid=(B, H // 2, S // tq),
        in_specs=[
            pl.BlockSpec((1, 2, LANES, tq), lambda b, hp, i: (b, hp, 0, i)),
            pl.BlockSpec((1, 2, S, LANES), lambda b, hp, i: (b, hp, 0, 0)),
            pl.BlockSpec((1, 2 * V_DIM, S), lambda b, hp, i: (b, hp, 0)),
        ],
        out_specs=pl.BlockSpec((1, tq, 2 * V_DIM), lambda b, hp, i: (b, i, hp)),
        out_shape=jax.ShapeDtypeStruct((B, S, H * V_DIM), F32),
        scratch_shapes=[pltpu.VMEM((2, 2, ATT_TK, tq), F32)],
        compiler_params=_params("parallel", "parallel", "parallel"),
        name="mla_attention",
    )(qt, k, vt)


def _layer_norm(x, g, b):
    mu = jnp.mean(x, axis=-1, keepdims=True)
    xc = x - mu
    var = jnp.mean(xc * xc, axis=-1, keepdims=True)
    return xc * lax.rsqrt(var + EPS) * g + b


def _outproj_kernel(a_ref, hy_ref, x_ref, g1_ref, sc2_ref, sh2_ref, ag_ref, hg_ref, wo_ref,
                    lg_ref, lb_ref, wr_ref, x1_ref, u2_ref, aff_ref):
    half = a_ref.shape[2]
    an = _rms(a_ref[0], ag_ref[...]).astype(BF16)
    hn = _rms(hy_ref[0], hg_ref[...]).astype(BF16)
    mixed = (jnp.dot(an, wo_ref[:half, :], preferred_element_type=F32)
             + jnp.dot(hn, wo_ref[half:, :], preferred_element_type=F32))
    x1 = _layer_norm(DN_ALPHA * x_ref[0] + g1_ref[0] * mixed, lg_ref[...], lb_ref[...])
    x1_ref[0] = x1
    u2 = x1 * (1.0 + sc2_ref[0]) + sh2_ref[0]
    u2_ref[0] = u2.astype(BF16)
    logits = _dot3(u2, wr_ref[...])
    lane = lax.broadcasted_iota(jnp.int32, logits.shape, 1)
    logits = jnp.where(lane < N_EXPERTS, logits, -jnp.inf)
    e = jnp.exp(logits - jnp.max(logits, axis=-1, keepdims=True))
    aff = e / jnp.sum(e, axis=-1, keepdims=True)
    aff_ref[0] = aff.T[:N_EXPERTS, :]


def _outproj(a, hy, x, g1, sc2, sh2, ag, hg, wo, lg, lb, wr_pad):
    B, S, D = x.shape
    tm = TOK_TILE
    half = a.shape[2]
    full = lambda shape: pl.BlockSpec(shape, lambda b, i: (0,) * len(shape))
    tok = lambda w: pl.BlockSpec((1, tm, w), lambda b, i: (b, i, 0))
    per_b = pl.BlockSpec((1, 1, D), lambda b, i: (b, 0, 0))
    return pl.pallas_call(
        _outproj_kernel,
        grid=(B, S // tm),
        in_specs=[tok(half), tok(half), tok(D), per_b, per_b, per_b,
                  full((1, half)), full((1, half)), full((D, D)), full((1, D)), full((1, D)),
                  full((D, LANES))],
        out_specs=[tok(D), tok(D), pl.BlockSpec((1, N_EXPERTS, tm), lambda b, i: (b, 0, i))],
        out_shape=[
            jax.ShapeDtypeStruct((B, S, D), F32),
            jax.ShapeDtypeStruct((B, S, D), BF16),
            jax.ShapeDtypeStruct((B, N_EXPERTS, S), F32),
        ],
        compiler_params=_params("parallel", "parallel"),
        name="out_proj_ln1_router",
    )(a, hy, x, g1, sc2, sh2, ag, hg, wo, lg, lb, wr_pad)


def _select_kernel(aff_ref, ones_ref, utri_ref, ltri_ref, pos_ref, posall_ref, g_ref, *, cap):
    aff = aff_ref[0]
    ones = ones_ref[...]

    def rowsum(mf):
        return jnp.dot(mf, ones, preferred_element_type=F32)

    def expert_total(mf):
        x = mf.astype(F32).reshape(N_EXPERTS, aff.shape[0] // N_EXPERTS, LANES)
        tot = jnp.sum(jnp.sum(x, axis=1, keepdims=True), axis=2, keepdims=True)
        return jnp.broadcast_to(tot, x.shape).reshape(aff.shape)

    def prefix(mf):
        return (jnp.dot(mf, utri_ref[...], preferred_element_type=F32)
                + jnp.dot(ltri_ref[...], rowsum(mf).astype(BF16), preferred_element_type=F32))

    def ind(m):
        return jnp.where(m, 1.0, 0.0).astype(BF16)

    def search(i, thr_bits):
        cand = thr_bits | jnp.left_shift(jnp.int32(1), 30 - i)
        enough = expert_total(ind(aff >= lax.bitcast_convert_type(cand, F32))) >= cap
        return jnp.where(enough, cand, thr_bits)

    thr = lax.bitcast_convert_type(lax.fori_loop(0, 31, search, jnp.zeros(aff.shape, jnp.int32)), F32)
    gt = aff > thr
    eq = aff == thr
    room = cap - expert_total(ind(gt))
    take_eq = jnp.where(eq, jnp.where(prefix(ind(eq)) < room, 1.0, 0.0), 0.0)
    sel = jnp.where(gt, 1.0, take_eq)
    pos = prefix(sel.astype(BF16)).astype(jnp.int32)
    posall_ref[0] = pos
    pos_ref[0] = jnp.where(sel > 0.0, pos, -1)
    g_ref[0] = jnp.where(sel > 0.0, aff, 0.0)


def _select(aff2, cap):
    B, R, _ = aff2.shape
    rpe = R // N_EXPERTS
    li = jnp.arange(LANES, dtype=jnp.int32)
    ri = jnp.arange(R, dtype=jnp.int32)
    same = (ri[:, None] // rpe) == (ri[None, :] // rpe)
    ones = jnp.ones((LANES, LANES), BF16)
    utri = (li[:, None] < li[None, :]).astype(BF16)
    ltri = (same & (ri[None, :] < ri[:, None])).astype(BF16)
    blk = pl.BlockSpec((1, R, LANES), lambda b: (b, 0, 0))
    full = lambda arr: pl.BlockSpec(arr.shape, lambda b: (0, 0))
    return pl.pallas_call(
        functools.partial(_select_kernel, cap=cap),
        grid=(B,),
        in_specs=[blk, full(ones), full(utri), full(ltri)],
        out_specs=[blk, blk, blk],
        out_shape=[jax.ShapeDtypeStruct((B, R, LANES), jnp.int32), jax.ShapeDtypeStruct((B, R, LANES), jnp.int32),
                   jax.ShapeDtypeStruct((B, R, LANES), F32)],
        compiler_params=_params("parallel"),
        name="expert_choice_select",
    )(aff2, ones, utri, ltri)


GATHER_TOK = 1024
GATHER_WIN = 160


def _gather_kernel(offs_ref, u_ref, pos_ref, xe_ref, acc, *, cap, nblk):
    b, e = pl.program_id(0), pl.program_id(1)
    base = (b * N_EXPERTS + e) * (nblk + 1)
    rows = GATHER_TOK // LANES
    acc[...] = jnp.zeros(acc.shape, F32)
    slot = lax.broadcasted_iota(jnp.int32, (GATHER_WIN, LANES), 0)

    def block(j, carry):
        off, end = offs_ref[base + j], offs_ref[base + j + 1]
        w = (off // SUBLANES) * SUBLANES
        nsub = jnp.where(end > off, (end - w + GATHER_WIN - 1) // GATHER_WIN, 0)
        prow = pos_ref[0, pl.ds(pl.multiple_of(j * rows, rows), rows), :]
        ublk = u_ref[0, pl.ds(pl.multiple_of(j * GATHER_TOK, GATHER_TOK), GATHER_TOK), :]

        def window(i, c):
            s0 = pl.multiple_of(w + i * GATHER_WIN, SUBLANES)
            onehot = jnp.concatenate(
                [jnp.where(slot == prow[r:r + 1, :] - s0, 1.0, 0.0).astype(BF16) for r in range(rows)], axis=1)
            acc[pl.ds(s0, GATHER_WIN), :] += jnp.dot(onehot, ublk, preferred_element_type=F32)
            return c

        lax.fori_loop(0, nsub, window, 0)
        return carry

    lax.fori_loop(0, nblk, block, 0)
    xe_ref[0] = acc[:cap, :].astype(BF16)


def _gather(offs, u2, pos_sel, cap):
    B, S, D = u2.shape
    rpe = S // LANES
    nblk = S // GATHER_TOK
    return pl.pallas_call(
        functools.partial(_gather_kernel, cap=cap, nblk=nblk),
        grid_spec=pltpu.PrefetchScalarGridSpec(
            num_scalar_prefetch=1,
            grid=(B, N_EXPERTS),
            in_specs=[pl.BlockSpec((1, S, D), lambda b, e, o: (b, 0, 0)),
                      pl.BlockSpec((1, rpe, LANES), lambda b, e, o: (b, e, 0))],
            out_specs=pl.BlockSpec((1, cap, D), lambda b, e, o: (e, b, 0)),
            scratch_shapes=[pltpu.VMEM((cap + GATHER_WIN + SUBLANES, D), F32)],
        ),
        out_shape=jax.ShapeDtypeStruct((N_EXPERTS, B * cap, D), BF16),
        compiler_params=_params("parallel", "parallel"),
        name="expert_gather",
    )(offs, u2, pos_sel)


def _moe_kernel(x_ref, wg_ref, wu_ref, wd_ref, o_ref, wgb, wub, wdb):
    @pl.when(pl.program_id(1) == 0)
    def _():
        wgb[...] = wg_ref[0].astype(BF16)
        wub[...] = wu_ref[0].astype(BF16)
        wdb[...] = wd_ref[0].astype(BF16)

    x = x_ref[0]
    hg = jnp.dot(x, wgb[...], preferred_element_type=F32)
    hu = jnp.dot(x, wub[...], preferred_element_type=F32)
    h = (hg * jax.nn.sigmoid(hg) * hu).astype(BF16)
    o_ref[0] = jnp.dot(h, wdb[...], preferred_element_type=F32).astype(BF16)


def _moe_ffn(xe, w_gate, w_up, w_down):
    E, R, D = xe.shape
    Fh = w_gate.shape[2]
    tr = MOE_ROWS
    return pl.pallas_call(
        _moe_kernel,
        grid=(E, R // tr),
        in_specs=[
            pl.BlockSpec((1, tr, D), lambda e, r: (e, r, 0)),
            pl.BlockSpec((1, D, Fh), lambda e, r: (e, 0, 0)),
            pl.BlockSpec((1, D, Fh), lambda e, r: (e, 0, 0)),
            pl.BlockSpec((1, Fh, D), lambda e, r: (e, 0, 0)),
        ],
        out_specs=pl.BlockSpec((1, tr, D), lambda e, r: (e, r, 0)),
        out_shape=jax.ShapeDtypeStruct((E, R, D), BF16),
        scratch_shapes=[pltpu.VMEM((D, Fh), BF16), pltpu.VMEM((D, Fh), BF16), pltpu.VMEM((Fh, D), BF16)],
        compiler_params=_params("parallel", "arbitrary"),
        name="moe_swiglu",
    )(xe, w_gate, w_up, w_down)


CMB_TOK = 512
CMB_WIN = 128
BF16_ROWS = 16


def _combine_kernel(offs_ref, x1_ref, pos_ref, g_ref, g2_ref, lg_ref, lb_ref, ye_hbm, o_ref,
                    rhs, lhs, acc, sem, *, cap, nblk, nbatch):
    b, i = pl.program_id(0), pl.program_id(1)
    W = CMB_WIN
    total = ye_hbm.shape[0]
    row_base, first, npass = [], [], jnp.int32(0)
    for e in range(N_EXPERTS):
        k = (b * N_EXPERTS + e) * (nblk + 1) + i
        off, end = offs_ref[k], offs_ref[k + 1]
        lo = (off // BF16_ROWS) * BF16_ROWS
        row_base.append((e * nbatch + b) * cap)
        first.append(lo)
        npass = jnp.maximum(npass, jnp.where(end > off, (end - lo + W - 1) // W, 0))
    acc[...] = jnp.zeros(acc.shape, F32)
    row_e = lax.broadcasted_iota(jnp.int32, (N_EXPERTS, 1), 0)
    slot_w = lax.broadcasted_iota(jnp.int32, (W, CMB_TOK), 0)

    def window_copy(e, start):
        return pltpu.make_async_copy(ye_hbm.at[pl.ds(pl.multiple_of(start, BF16_ROWS), W), :],
                                     rhs.at[pl.ds(e * W, W), :], sem.at[e])

    def one_pass(p, carry):
        lo_col = jnp.zeros((N_EXPERTS, 1), jnp.int32)
        shift_col = jnp.zeros((N_EXPERTS, 1), jnp.int32)
        starts = []
        for e in range(N_EXPERTS):
            lo = first[e] + p * W
            start = jnp.minimum(row_base[e] + lo, total - W)
            starts.append(start)
            lo_col = jnp.where(row_e == e, lo, lo_col)
            shift_col = jnp.where(row_e == e, row_base[e] + lo - start, shift_col)
        for e in range(N_EXPERTS):
            window_copy(e, starts[e]).start()
        d = pos_ref[0] - lo_col
        rel = jnp.where((d >= 0) & (d < W), d + shift_col, -1)
        g = g_ref[0]
        for e in range(N_EXPERTS):
            lhs[e * W:(e + 1) * W, :] = jnp.where(slot_w == rel[e:e + 1, :], g[e:e + 1, :], 0.0).astype(BF16)
        for e in range(N_EXPERTS):
            window_copy(e, starts[e]).wait()
        acc[...] += lax.dot_general(lhs[...], rhs[...], (((0,), (0,)), ((), ())), preferred_element_type=F32)
        return carry

    lax.fori_loop(0, npass, one_pass, 0)
    o_ref[0] = _layer_norm(DN_ALPHA * x1_ref[0] + g2_ref[0] * acc[...], lg_ref[...], lb_ref[...])


def _combine_ln2(offs, x1, pos_exp, g_exp, g2, lg, lb, ye_rows, cap):
    B, S, D = x1.shape
    T = CMB_TOK
    nblk = S // T
    tok = lambda w: pl.BlockSpec((1, T, w), lambda b, i, o: (b, i, 0))
    vec = pl.BlockSpec((1, D), lambda b, i, o: (0, 0))
    exp_major = pl.BlockSpec((1, N_EXPERTS, T), lambda b, i, o: (b, 0, i))
    return pl.pallas_call(
        functools.partial(_combine_kernel, cap=cap, nblk=nblk, nbatch=B),
        grid_spec=pltpu.PrefetchScalarGridSpec(
            num_scalar_prefetch=1,
            grid=(B, nblk),
            in_specs=[tok(D), exp_major, exp_major,
                      pl.BlockSpec((1, 1, D), lambda b, i, o: (b, 0, 0)), vec, vec,
                      pl.BlockSpec(memory_space=pl.ANY)],
            out_specs=tok(D),
            scratch_shapes=[pltpu.VMEM((N_EXPERTS * CMB_WIN, D), BF16),
                            pltpu.VMEM((N_EXPERTS * CMB_WIN, T), BF16),
                            pltpu.VMEM((T, D), F32),
                            pltpu.SemaphoreType.DMA((N_EXPERTS,))],
        ),
        out_shape=jax.ShapeDtypeStruct((B, S, D), F32),
        compiler_params=_params("parallel", "parallel"),
        name="combine_residual_ln2",
    )(offs, x1, pos_exp, g_exp, g2, lg, lb, ye_rows)


DFT_R = 128
K1_HALF = 40
K1_VALID = DFT_R // 2 + 1
K1_GROUPS = (K1_HALF // SUBLANES, -(-(K1_VALID - K1_HALF) // SUBLANES))
K1_SPEC = K1_HALF + K1_GROUPS[1] * SUBLANES
HY_CT = 128


@functools.lru_cache(maxsize=None)
def _dft_tables_np(S):
    N = 2 * S
    na = S // DFT_R
    r = np.arange(2 * K1_HALF)
    k1 = np.arange(2)[:, None] * K1_HALF + r[None, :] % K1_HALF
    valid = (k1 < K1_VALID).astype(np.float64)
    n = DFT_R * np.arange(na)[None, :] + np.arange(DFT_R)[:, None]
    theta = ((n[None, :, None, :] * k1[:, None, :, None]) % N) * (2.0 * math.pi / N)
    is_im = (r >= K1_HALF)[None, None, :, None]
    g = np.where(is_im, -np.sin(theta), np.cos(theta)) * valid[:, None, :, None]
    cw = np.where((k1 == 0) | (k1 == DFT_R // 2), 1.0, 2.0) * valid / N
    gi = np.transpose(g * cw[:, None, :, None], (0, 1, 3, 2))
    ph = ((np.arange(DFT_R)[:, None] * np.arange(DFT_R)[None, :]) % DFT_R) * (2.0 * math.pi / DFT_R)
    C, Sn = np.cos(ph), np.sin(ph)
    f3f = np.block([[C, Sn], [-Sn, C]])
    f3i = np.block([[C, -Sn], [Sn, C]])

    def pair_lanes(t):
        h, nb, nr, nc = t.shape
        return t.reshape(h, nb // 2, 2, nr, nc).transpose(0, 1, 3, 2, 4).reshape(h, nb // 2, nr, 2 * nc)

    return tuple(np.ascontiguousarray(t, dtype=np.float32) for t in (pair_lanes(g), pair_lanes(gi), f3f, f3i))


def _dft_tables(S):
    return tuple(jnp.asarray(t).astype(BF16) for t in _dft_tables_np(S))


def _swap01(x):
    return jnp.swapaxes(x, 0, 1)


def _block_diag_lanes(top, bot):
    return jnp.concatenate([jnp.concatenate([top, jnp.zeros((top.shape[0], bot.shape[1]), top.dtype)], axis=1),
                            jnp.concatenate([jnp.zeros((bot.shape[0], top.shape[1]), bot.dtype), bot], axis=1)], axis=0)


def _dft_stage1(z_ref, gf_ref, hf, a_scr):
    ct = a_scr.shape[2]

    def body(g, carry):
        b0 = pl.multiple_of(g * SUBLANES, SUBLANES)
        zt = _swap01(z_ref[:, pl.ds(b0, SUBLANES), :]).astype(BF16)
        for p in range(SUBLANES // 2):
            a2 = jnp.dot(gf_ref[hf, g * (SUBLANES // 2) + p], _block_diag_lanes(zt[2 * p], zt[2 * p + 1]),
                         preferred_element_type=F32)
            a_scr[b0 + 2 * p] = a2[:, :ct]
            a_scr[b0 + 2 * p + 1] = a2[:, ct:]
        return carry

    lax.fori_loop(0, DFT_R // SUBLANES, body, 0, unroll=4)


def _dft_stage3_operands(a_scr, r0):
    mr = _swap01(a_scr[:, pl.ds(r0, SUBLANES), :])
    mi = _swap01(a_scr[:, pl.ds(K1_HALF + r0, SUBLANES), :])
    return [jnp.concatenate([jnp.concatenate([mr[j], mi[j]], axis=0),
                             jnp.concatenate([mr[j + 1], mi[j + 1]], axis=0)], axis=1).astype(BF16)
            for j in range(0, SUBLANES, 2)]


def _filter_mlp_kernel(feat_ref, w1_ref, b1_ref, fr_ref, w2_ref, b2_ref, w3_ref, b3_ref, dec_ref, o_ref):
    feats = feat_ref[...]
    fr = fr_ref[...]
    h = jnp.sin(fr * (jnp.dot(feats, w1_ref[...], precision=HIGHEST, preferred_element_type=F32) + b1_ref[...]))
    h = jnp.sin(fr * (jnp.dot(h, w2_ref[...], precision=HIGHEST, preferred_element_type=F32) + b2_ref[...]))
    h = _dot3(h, w3_ref[...]) + b3_ref[...]
    h = h * jnp.exp(-feats[:, 0:1] * jnp.abs(dec_ref[...]))
    row = lax.broadcasted_iota(jnp.int32, h.shape, 0) + pl.program_id(0) * h.shape[0]
    col = lax.broadcasted_iota(jnp.int32, h.shape, 1)
    o_ref[...] = jnp.where((row == 0) & (col >= h.shape[1] // 2), 0.0, h)


@functools.lru_cache(maxsize=None)
def _filter_features_np(L, nf):
    pos = np.arange(L, dtype=np.float64)
    t = pos[:, None] / (L - 1)
    bands = (nf - 1) // 2
    freqs = np.linspace(1e-4, bands - 1, bands)
    phase = (2.0 * math.pi / L) * pos[:, None] * freqs[None, :]
    feats = np.concatenate([t, np.cos(phase), -np.sin(phase), np.zeros((L, LANES - nf))], axis=-1)
    return np.ascontiguousarray(feats, dtype=np.float32)


def _filter_taps(L, w1, b1, freq, w2, b2, w3, b3, decay):
    nf = w1.shape[0]
    feats = jnp.asarray(_filter_features_np(L, nf))
    w1p = jnp.concatenate([w1, jnp.zeros((LANES - nf, w1.shape[1]), F32)], axis=0)
    fo = w1.shape[1]
    ncol = w3.shape[1]
    tl = 512
    full = lambda shape: pl.BlockSpec(shape, lambda i: (0,) * len(shape))
    return pl.pallas_call(
        _filter_mlp_kernel,
        grid=(L // tl,),
        in_specs=[pl.BlockSpec((tl, LANES), lambda i: (i, 0)), full((LANES, fo)), full((1, fo)), full((1, fo)),
                  full((fo, fo)), full((1, fo)), full((fo, ncol)), full((1, ncol)), full((1, ncol))],
        out_specs=pl.BlockSpec((tl, ncol), lambda i: (i, 0)),
        out_shape=jax.ShapeDtypeStruct((L, ncol), F32),
        compiler_params=_params("parallel"),
        name="hyena_filter_mlp",
    )(feats, w1p, b1[None, :], freq[None, :], w2, b2[None, :], w3, b3[None, :], decay.reshape(1, ncol))


def _filter_spectrum_kernel(hf_ref, hb_ref, gf_ref, f3f_ref, o_ref, a_scr, x_scr):
    ct = a_scr.shape[2]
    for t, src in enumerate((hf_ref, hb_ref)):
        for hf in range(2):
            _dft_stage1(src, gf_ref, hf, a_scr)

            def body(g, carry):
                r0 = pl.multiple_of(g * SUBLANES, SUBLANES)
                for p, m in enumerate(_dft_stage3_operands(a_scr, r0)):
                    x2 = jnp.dot(f3f_ref[...], m, preferred_element_type=F32)
                    for q in range(2):
                        k1 = hf * K1_HALF + r0 + 2 * p + q
                        x = x2[:, q * ct:(q + 1) * ct]
                        if t == 0:
                            x_scr[k1] = x
                        else:
                            xf = x_scr[k1]
                            o_ref[0, k1] = jnp.concatenate([xf[:DFT_R] + x[:DFT_R], xf[DFT_R:] - x[DFT_R:]],
                                                           axis=0).astype(BF16)
                return carry

            lax.fori_loop(0, K1_GROUPS[hf], body, 0, unroll=True)


def _filter_spectrum(taps3, gf, f3f):
    na, nb, ncol = taps3.shape
    C = ncol // 4
    nct = C // HY_CT
    ct = HY_CT
    return pl.pallas_call(
        _filter_spectrum_kernel,
        grid=(2, nct),
        in_specs=[
            pl.BlockSpec((na, nb, ct), lambda o, j: (0, 0, o * nct + j)),
            pl.BlockSpec((na, nb, ct), lambda o, j: (0, 0, 2 * nct + o * nct + j)),
            pl.BlockSpec(gf.shape, lambda o, j: (0, 0, 0, 0), pipeline_mode=pl.Buffered(1)),
            pl.BlockSpec(f3f.shape, lambda o, j: (0, 0), pipeline_mode=pl.Buffered(1)),
        ],
        out_specs=pl.BlockSpec((1, K1_SPEC, 2 * DFT_R, ct), lambda o, j: (o, 0, 0, j)),
        out_shape=jax.ShapeDtypeStruct((2, K1_SPEC, 2 * DFT_R, C), BF16),
        scratch_shapes=[pltpu.VMEM((DFT_R, 2 * K1_HALF, ct), F32), pltpu.VMEM((K1_SPEC, 2 * DFT_R, ct), F32)],
        compiler_params=_params("parallel", "parallel"),
        name="hyena_filter_spectrum",
    )(taps3, taps3, gf, f3f)


def _long_conv_kernel(z_ref, xg_ref, h_ref, bias_ref, gf_ref, f3f_ref, f3i_ref, gi_ref, o_ref, a_scr, b_scr):
    ct = a_scr.shape[2]
    bias = bias_ref[0]
    for hf in range(2):
        _dft_stage1(z_ref.at[0], gf_ref, hf, a_scr)
        ngrp = K1_GROUPS[hf]
        if ngrp * SUBLANES < K1_HALF:
            b_scr[ngrp * SUBLANES:] = jnp.zeros((K1_HALF - ngrp * SUBLANES,) + b_scr.shape[1:], F32)

        def spectral(g, carry):
            r0 = pl.multiple_of(g * SUBLANES, SUBLANES)
            for p, m in enumerate(_dft_stage3_operands(a_scr, r0)):
                x = jnp.dot(f3f_ref[...], m, preferred_element_type=F32)
                k1 = hf * K1_HALF + r0 + 2 * p
                h = jnp.concatenate([h_ref[0, k1], h_ref[0, k1 + 1]], axis=1).astype(F32)
                xr, xi, hr, hi = x[:DFT_R], x[DFT_R:], h[:DFT_R], h[DFT_R:]
                y = jnp.concatenate([xr * hr - xi * hi, xr * hi + xi * hr], axis=0).astype(BF16)
                bb = jnp.dot(f3i_ref[...], y, preferred_element_type=F32)
                b_scr[r0 + 2 * p] = bb[:, :ct]
                b_scr[r0 + 2 * p + 1] = bb[:, ct:]
            return carry

        lax.fori_loop(0, ngrp, spectral, 0, unroll=True)

        def inverse(g, carry):
            b0 = pl.multiple_of(g * SUBLANES, SUBLANES)
            br = _swap01(b_scr[:, pl.ds(b0, SUBLANES), :])
            bi = _swap01(b_scr[:, pl.ds(DFT_R + b0, SUBLANES), :])
            ys = []
            for p in range(SUBLANES // 2):
                bm = [jnp.concatenate([br[2 * p + q], bi[2 * p + q]], axis=0).astype(BF16) for q in range(2)]
                y2 = jnp.dot(gi_ref[hf, g * (SUBLANES // 2) + p], _block_diag_lanes(bm[0], bm[1]),
                             preferred_element_type=F32)
                ys += [y2[:, :ct], y2[:, ct:]]
            y = _swap01(jnp.stack(ys, axis=0))
            sl = (0, slice(None), pl.ds(b0, SUBLANES), slice(None))
            if hf == 0:
                o_ref[sl] = y
            else:
                o_ref[sl] = xg_ref[sl] * (o_ref[sl] + y + z_ref[sl] * bias)
            return carry

        lax.fori_loop(0, DFT_R // SUBLANES, inverse, 0, unroll=4)


def _long_conv(z_arr, z_off, xg_arr, xg_off, spec, order, bias3, gf, f3f, f3i, gi):
    B, na, nb, _ = z_arr.shape
    C = spec.shape[3]
    ct = HY_CT
    const = lambda arr: pl.BlockSpec(arr.shape, lambda j, b: (0,) * arr.ndim, pipeline_mode=pl.Buffered(1))
    return pl.pallas_call(
        _long_conv_kernel,
        grid=(C // ct, B),
        in_specs=[
            pl.BlockSpec((1, na, nb, ct), lambda j, b: (b, 0, 0, z_off + j)),
            pl.BlockSpec((1, na, nb, ct), lambda j, b: (b, 0, 0, xg_off + j)),
            pl.BlockSpec((1, K1_SPEC, 2 * DFT_R, ct), lambda j, b: (order, 0, 0, j), pipeline_mode=pl.Buffered(1)),
            pl.BlockSpec((1, 1, ct), lambda j, b: (order, 0, j)),
            const(gf), const(f3f), const(f3i), const(gi),
        ],
        out_specs=pl.BlockSpec((1, na, nb, ct), lambda j, b: (b, 0, 0, j)),
        out_shape=jax.ShapeDtypeStruct((B, na, nb, C), F32),
        scratch_shapes=[pltpu.VMEM((DFT_R, 2 * K1_HALF, ct), F32), pltpu.VMEM((K1_HALF, 2 * DFT_R, ct), F32)],
        compiler_params=_params("parallel", "parallel"),
        name="hyena_long_conv",
    )(z_arr, xg_arr, spec, bias3, gf, f3f, f3i, gi)


def _hyena_group(hy_conv, fw1, fb1, ffreq, fw2, fb2, fw3, fb3, decay, hyena_bias):
    B, S, _ = hy_conv.shape
    na = S // DFT_R
    gf, gi, f3f, f3i = _dft_tables(S)
    taps = _filter_taps(S, fw1, fb1, ffreq, fw2, fb2, fw3, fb3, decay)
    spec = _filter_spectrum(taps.reshape(na, DFT_R, -1), gf, f3f)
    u = hy_conv.reshape(B, na, DFT_R, -1)
    nct = HY_W // HY_CT
    bias3 = hyena_bias[:, None, :]
    z1 = _long_conv(u, 0, u, nct, spec, 0, bias3, gf, f3f, f3i, gi)
    return _long_conv(z1, 0, u, 2 * nct, spec, 1, bias3, gf, f3f, f3i, gi).reshape(B, S, HY_W)


def _rope_rot_cols(w):
    half = ROPE // 2
    return jnp.concatenate([-w[..., half:], w[..., :half]], axis=-1)


def _prep_weights(w_in, w_qb, w_kvb):
    D = w_in.shape[0]
    o = Q_RANK + KV_RANK
    w_kr = w_in[:, o:o + ROPE]
    z64 = jnp.zeros((D, NOPE), F32)
    z32 = jnp.zeros((D, LANES - NOPE - ROPE), F32)
    win_aug = jnp.concatenate([
        w_in[:, :o],
        z64, w_kr, z32,
        z64, _rope_rot_cols(w_kr), z32,
        w_in[:, o + ROPE:],
    ], axis=1).astype(BF16)
    wq = w_qb.reshape(Q_RANK, MLA_HEADS, NOPE + ROPE)
    zq = jnp.zeros((Q_RANK, MLA_HEADS, LANES - NOPE - ROPE), F32)
    wq_main = jnp.concatenate([wq, zq], axis=-1).reshape(Q_RANK, MLA_HEADS * LANES).astype(BF16)
    wq_rot = jnp.concatenate([jnp.zeros((Q_RANK, MLA_HEADS, NOPE), F32), _rope_rot_cols(wq[..., NOPE:]), zq],
                             axis=-1).reshape(Q_RANK, MLA_HEADS * LANES).astype(BF16)
    wkv = w_kvb.reshape(KV_RANK, MLA_HEADS, NOPE + V_DIM)
    wk = jnp.concatenate([wkv[..., :NOPE], jnp.zeros((KV_RANK, MLA_HEADS, LANES - NOPE), F32)],
                         axis=-1).reshape(KV_RANK, MLA_HEADS * LANES).astype(BF16)
    wv = wkv[..., NOPE:].reshape(KV_RANK, MLA_HEADS * V_DIM).astype(BF16)
    return win_aug, wq_main, wq_rot, wk, wv


def kernel(x, c, positions, w_ada, b_ada, w_in, q_norm_g, w_qb, kv_norm_g, w_kvb, conv_w, conv_b, filt_w1, filt_b1, filt_freq, filt_w2, filt_b2, filt_w3, filt_b3, hyena_decay, hyena_bias, attn_out_g, hyena_out_g, w_o, ln1_g, ln1_b, w_router, w_gate, w_up, w_down, ln2_g, ln2_b):
    B, S, D = x.shape
    l = 0
    c8 = jnp.zeros((SUBLANES, D), F32).at[:B].set(c)
    mod = _ada_mod(c8, w_ada[l], b_ada[l][None, :])[:B]
    sh1, sc1, g1, sh2, sc2, g2 = [m[:, None, :] for m in jnp.split(mod, 6, axis=-1)]

    win_aug, wq_main, wq_rot, wk, wv = _prep_weights(w_in[l], w_qb[l], w_kvb[l])
    cos_h, sin_h = _rope_tables(positions)
    pad = lambda n, v: jnp.full((B, n, S), v, F32)
    cos_t = jnp.concatenate([pad(NOPE, 1.0), cos_h, cos_h, pad(LANES - NOPE - ROPE, 1.0)], axis=1)
    sin_t = jnp.concatenate([pad(NOPE, 0.0), sin_h, sin_h, pad(LANES - NOPE - ROPE, 0.0)], axis=1)
    q, kt, v, hy_conv = _inproj(x, sc1, sh1, cos_t, sin_t, win_aug, q_norm_g[l][None, :], wq_main, wq_rot,
                                kv_norm_g[l][None, :], wk, wv, conv_w[l], conv_b[l])
    a = _attention(q, kt, v)

    hy = _hyena_group(hy_conv, filt_w1[l], filt_b1[l], filt_freq[l], filt_w2[l],
                      filt_b2[l], filt_w3[l], filt_b3[l], hyena_decay[l], hyena_bias[l])

    wr_pad = jnp.concatenate([w_router[l], jnp.zeros((D, LANES - N_EXPERTS), F32)], axis=1)
    x1, u2, aff_t = _outproj(a, hy, x, g1, sc2, sh2, attn_out_g[l][None, :], hyena_out_g[l][None, :],
                             w_o[l].astype(BF16), ln1_g[l][None, :], ln1_b[l][None, :], wr_pad)

    cap = EC_FACTOR * S // N_EXPERTS
    rpe = S // LANES
    aff2 = aff_t.reshape(B, N_EXPERTS * rpe, LANES)
    pos_sel, pos_all, gsel = _select(aff2, cap)
    row_start = pos_all.reshape(B, N_EXPERTS, rpe, LANES)[:, :, :, 0]

    def block_offsets(tokens):
        ends = jnp.full((B, N_EXPERTS, 1), cap, jnp.int32)
        return jnp.concatenate([row_start[:, :, ::tokens // LANES], ends], axis=-1).reshape(-1)

    xe = _gather(block_offsets(GATHER_TOK), u2, pos_sel, cap)
    ye = _moe_ffn(xe, w_gate[l], w_up[l], w_down[l])
    return _combine_ln2(block_offsets(CMB_TOK), x1, pos_sel.reshape(B, N_EXPERTS, S), gsel.reshape(B, N_EXPERTS, S), g2, ln2_g[l][None, :], ln2_b[l][None, :],
                        ye.reshape(N_EXPERTS * B * cap, D), cap)
```

```python
import functools
import math

import jax
import jax.numpy as jnp
import numpy as np
from jax import lax
from jax.experimental import pallas as pl
from jax.experimental.pallas import tpu as pltpu

F32 = jnp.float32
BF16 = jnp.bfloat16
HIGHEST = lax.Precision.HIGHEST

LANES = 128
SUBLANES = 8
VMEM_LIMIT = 56 * 1024 * 1024

D_MODEL = 1024
MLA_HEADS = 8
NOPE = 64
ROPE = 32
V_DIM = 64
Q_RANK = 256
KV_RANK = 128
HY_W = 512
HY_COLS = 3 * HY_W
N_EXPERTS = 16
EC_FACTOR = 2
ROPE_THETA = 10000.0
EPS = 1e-5
DN_ALPHA = 2.0 ** 0.25
LOG2E = 1.4426950408889634

TOK_TILE = 512
ATT_TQ = 512
ATT_TK = 512
MOE_ROWS = 512


def _params(*sem):
    return pltpu.CompilerParams(dimension_semantics=sem, vmem_limit_bytes=VMEM_LIMIT)


def _ada_kernel(c_ref, w_ref, b_ref, o_ref):
    c = c_ref[...]
    sc = c * jax.nn.sigmoid(c)
    o_ref[...] = jnp.dot(sc, w_ref[...], precision=HIGHEST, preferred_element_type=F32) + b_ref[...]


def _ada_mod(c8, w_ada, b_ada):
    n = w_ada.shape[1]
    tn = 1024
    return pl.pallas_call(
        _ada_kernel,
        grid=(n // tn,),
        in_specs=[
            pl.BlockSpec((SUBLANES, D_MODEL), lambda j: (0, 0)),
            pl.BlockSpec((D_MODEL, tn), lambda j: (0, j)),
            pl.BlockSpec((1, tn), lambda j: (0, j)),
        ],
        out_specs=pl.BlockSpec((SUBLANES, tn), lambda j: (0, j)),
        out_shape=jax.ShapeDtypeStruct((SUBLANES, n), F32),
        compiler_params=_params("arbitrary"),
        name="ada_mod",
    )(c8, w_ada, b_ada)


def _dot3(a, b):
    ah, bh = a.astype(BF16), b.astype(BF16)
    al = (a - ah.astype(F32)).astype(BF16)
    bl = (b - bh.astype(F32)).astype(BF16)
    dot = functools.partial(jnp.dot, preferred_element_type=F32)
    return dot(ah, bh) + (dot(ah, bl) + dot(al, bh))


def _rms(x, g):
    return x * lax.rsqrt(jnp.mean(x * x, axis=-1, keepdims=True) + EPS) * g


def _rope_kernel(pos_ref, freq_ref, cos_ref, sin_ref):
    ang = freq_ref[...] * pos_ref[0].astype(F32)
    cos_ref[0] = jnp.cos(ang)
    sin_ref[0] = jnp.sin(ang)


def _rope_tables(positions):
    B, S = positions.shape
    half = ROPE // 2
    inv_freq = (ROPE_THETA ** (-jnp.arange(half, dtype=F32) / half))[:, None]
    out = jax.ShapeDtypeStruct((B, half, S), F32)
    blk = pl.BlockSpec((1, half, S), lambda b: (b, 0, 0))
    return pl.pallas_call(
        _rope_kernel,
        grid=(B,),
        in_specs=[pl.BlockSpec((1, 1, S), lambda b: (b, 0, 0)), pl.BlockSpec((half, 1), lambda b: (0, 0))],
        out_specs=[blk, blk],
        out_shape=[out, out],
        compiler_params=_params("parallel"),
        name="rope_tables",
    )(positions[:, None, :], inv_freq)


def _inproj_kernel(x_ref, xp_ref, xn_ref, sc_ref, sh_ref, cos_ref, sin_ref, win_ref, qg_ref, wq_ref, wqr_ref,
                   kvg_ref, wk_ref, wv_ref, cw_ref, cb_ref, q_ref, k_ref, v_ref, hy_ref):
    def modulate(xv):
        return (xv * (1.0 + sc_ref[0]) + sh_ref[0]).astype(BF16)

    u = modulate(x_ref[0])
    cos = cos_ref[0].T
    sin = sin_ref[0].T

    def proj(lo, hi):
        return jnp.dot(u, win_ref[:, lo:hi], preferred_element_type=F32)

    cq = _rms(proj(0, Q_RANK), qg_ref[...]).astype(BF16)
    q_main = jnp.dot(cq, wq_ref[...], preferred_element_type=F32)
    q_rot = jnp.dot(cq, wqr_ref[...], preferred_element_type=F32)
    qscale = LOG2E * (NOPE + ROPE) ** -0.5
    cq_s = cos * qscale
    sq_s = sin * qscale
    for h in range(MLA_HEADS):
        sl = slice(h * LANES, (h + 1) * LANES)
        q_ref[0, h] = (q_main[:, sl] * cq_s + q_rot[:, sl] * sq_s).T.astype(BF16)

    o = Q_RANK
    ckv = _rms(proj(o, o + KV_RANK), kvg_ref[...]).astype(BF16)
    kr = proj(o + KV_RANK, o + 2 * KV_RANK) * cos + proj(o + 2 * KV_RANK, o + 3 * KV_RANK) * sin
    kn = jnp.dot(ckv, wk_ref[...], preferred_element_type=F32)
    for h in range(MLA_HEADS):
        k_ref[0, h] = (kn[:, h * LANES:(h + 1) * LANES] + kr).astype(BF16)
    v_ref[0] = jnp.dot(ckv, wv_ref[...], preferred_element_type=F32).T.astype(BF16)

    o = Q_RANK + 3 * KV_RANK
    hy = proj(o, o + HY_COLS)
    tm = hy.shape[0]
    i, last = pl.program_id(1), pl.num_programs(1) - 1
    edge = jnp.dot(modulate(jnp.concatenate([xp_ref[0], xn_ref[0]], axis=0)), win_ref[:, o:o + HY_COLS],
                   preferred_element_type=F32)
    prev = jnp.where(i == 0, 0.0, edge[SUBLANES - 1:SUBLANES, :])
    nxt = jnp.where(i == last, 0.0, edge[SUBLANES:SUBLANES + 1, :])
    row = lax.broadcasted_iota(jnp.int32, hy.shape, 0)
    up = jnp.where(row == 0, prev, pltpu.roll(hy, 1, 0))
    dn = jnp.where(row == tm - 1, nxt, pltpu.roll(hy, tm - 1, 0))
    hy_ref[0] = cb_ref[...] + cw_ref[0:1, :] * up + cw_ref[1:2, :] * hy + cw_ref[2:3, :] * dn


def _inproj(x, sc1, sh1, cos_t, sin_t, win_aug, qg, wq, wqr, kvg, wk, wv, conv_w, conv_b):
    B, S, D = x.shape
    tm = TOK_TILE
    ncol = win_aug.shape[1]
    g8 = tm // SUBLANES
    n8 = S // SUBLANES
    full = lambda shape: pl.BlockSpec(shape, lambda b, i: (0,) * len(shape))
    return pl.pallas_call(
        _inproj_kernel,
        grid=(B, S // tm),
        in_specs=[
            pl.BlockSpec((1, tm, D), lambda b, i: (b, i, 0)),
            pl.BlockSpec((1, SUBLANES, D), lambda b, i: (b, jnp.maximum(i * g8 - 1, 0), 0)),
            pl.BlockSpec((1, SUBLANES, D), lambda b, i: (b, jnp.minimum((i + 1) * g8, n8 - 1), 0)),
            pl.BlockSpec((1, 1, D), lambda b, i: (b, 0, 0)),
            pl.BlockSpec((1, 1, D), lambda b, i: (b, 0, 0)),
            pl.BlockSpec((1, LANES, tm), lambda b, i: (b, 0, i)),
            pl.BlockSpec((1, LANES, tm), lambda b, i: (b, 0, i)),
            full((D, ncol)),
            full((1, Q_RANK)),
            full((Q_RANK, MLA_HEADS * LANES)),
            full((Q_RANK, MLA_HEADS * LANES)),
            full((1, KV_RANK)),
            full((KV_RANK, MLA_HEADS * LANES)),
            full((KV_RANK, MLA_HEADS * V_DIM)),
            full(conv_w.shape),
            full((1, HY_COLS)),
        ],
        out_specs=[
            pl.BlockSpec((1, MLA_HEADS, LANES, tm), lambda b, i: (b, 0, 0, i)),
            pl.BlockSpec((1, MLA_HEADS, tm, LANES), lambda b, i: (b, 0, i, 0)),
            pl.BlockSpec((1, MLA_HEADS * V_DIM, tm), lambda b, i: (b, 0, i)),
            pl.BlockSpec((1, tm, HY_COLS), lambda b, i: (b, i, 0)),
        ],
        out_shape=[
            jax.ShapeDtypeStruct((B, MLA_HEADS, LANES, S), BF16),
            jax.ShapeDtypeStruct((B, MLA_HEADS, S, LANES), BF16),
            jax.ShapeDtypeStruct((B, MLA_HEADS * V_DIM, S), BF16),
            jax.ShapeDtypeStruct((B, S, HY_COLS), F32),
        ],
        compiler_params=_params("parallel", "parallel"),
        name="in_proj",
    )(x, x, x, sc1, sh1, cos_t, sin_t, win_aug, qg, wq, wqr, kvg, wk, wv, conv_w, conv_b[None, :])


ATT_AUG = 16


def _attn_kernel(qt_ref, k_ref, vt_ref, o_ref, s_scr):
    S = k_ref.shape[2]
    tq = qt_ref.shape[3]
    nk = S // ATT_TK
    ones_rows = jnp.where(lax.broadcasted_iota(jnp.int32, (ATT_AUG, ATT_TK), 0) == 0, 1.0, 0.0).astype(BF16)

    def scores(c, slot):
        k0 = pl.multiple_of(c * ATT_TK, ATT_TK)
        for h in range(2):
            s_scr[slot, h] = jnp.dot(k_ref[0, h, pl.ds(k0, ATT_TK), :], qt_ref[0, h], preferred_element_type=F32)

    def absorb(c, slot, carry):
        k0 = pl.multiple_of(c * ATT_TK, ATT_TK)
        out = []
        for h in range(2):
            m_prev, acc = carry[h]
            s = s_scr[slot, h]
            m_new = jnp.maximum(m_prev, jnp.max(s, axis=0, keepdims=True))
            alpha = jnp.exp2(m_prev - m_new)
            p = jnp.exp2(s - m_new).astype(BF16)
            v_aug = jnp.concatenate([vt_ref[0, h * V_DIM:(h + 1) * V_DIM, pl.ds(k0, ATT_TK)], ones_rows], axis=0)
            out.append((m_new, alpha * acc + jnp.dot(v_aug, p, preferred_element_type=F32)))
        return tuple(out)

    def pair(i, carry):
        c0 = 2 * i
        scores(c0 + 1, 1)
        carry = absorb(c0, 0, carry)
        scores(jnp.minimum(c0 + 2, nk - 1), 0)
        return absorb(c0 + 1, 1, carry)

    scores(0, 0)
    init = tuple((jnp.full((1, tq), -jnp.inf, F32), jnp.zeros((V_DIM + ATT_AUG, tq), F32)) for _ in range(2))
    fin = lax.fori_loop(0, nk // 2, pair, init, unroll=4)
    o = jnp.concatenate([acc[:V_DIM] * (1.0 / acc[V_DIM:V_DIM + 1]) for (_, acc) in fin], axis=0)
    o_ref[0] = o.T


def _attention(qt, k, vt):
    B, H, _, S = qt.shape
    tq = ATT_TQ
    return pl.pallas_call(
        _attn_kernel,
        grid=(B, H // 2, S // tq),
        in_specs=[
            pl.BlockSpec((1, 2, LANES, tq), lambda b, hp, i: (b, hp, 0, i)),
            pl.BlockSpec((1, 2, S, LANES), lambda b, hp, i: (b, hp, 0, 0)),
            pl.BlockSpec((1, 2 * V_DIM, S), lambda b, hp, i: (b, hp, 0)),
        ],
        out_specs=pl.BlockSpec((1, tq, 2 * V_DIM), lambda b, hp, i: (b, i, hp)),
        out_shape=jax.ShapeDtypeStruct((B, S, H * V_DIM), F32),
        scratch_shapes=[pltpu.VMEM((2, 2, ATT_TK, tq), F32)],
        compiler_params=_params("parallel", "parallel", "parallel"),
        name="mla_attention",
    )(qt, k, vt)


def _layer_norm(x, g, b):
    mu = jnp.mean(x, axis=-1, keepdims=True)
    xc = x - mu
    var = jnp.mean(xc * xc, axis=-1, keepdims=True)
    return xc * lax.rsqrt(var + EPS) * g + b


def _outproj_kernel(a_ref, hy_ref, x_ref, g1_ref, sc2_ref, sh2_ref, ag_ref, hg_ref, wo_ref,
                    lg_ref, lb_ref, wr_ref, x1_ref, u2_ref, aff_ref):
    half = a_ref.shape[2]
    an = _rms(a_ref[0], ag_ref[...]).astype(BF16)
    hn = _rms(hy_ref[0], hg_ref[...]).astype(BF16)
    mixed = (jnp.dot(an, wo_ref[:half, :], preferred_element_type=F32)
             + jnp.dot(hn, wo_ref[half:, :], preferred_element_type=F32))
    x1 = _layer_norm(DN_ALPHA * x_ref[0] + g1_ref[0] * mixed, lg_ref[...], lb_ref[...])
    x1_ref[0] = x1
    u2 = x1 * (1.0 + sc2_ref[0]) + sh2_ref[0]
    u2_ref[0] = u2.astype(BF16)
    logits = _dot3(u2, wr_ref[...])
    lane = lax.broadcasted_iota(jnp.int32, logits.shape, 1)
    logits = jnp.where(lane < N_EXPERTS, logits, -jnp.inf)
    e = jnp.exp(logits - jnp.max(logits, axis=-1, keepdims=True))
    aff = e / jnp.sum(e, axis=-1, keepdims=True)
    aff_ref[0] = aff.T[:N_EXPERTS, :]


def _outproj(a, hy, x, g1, sc2, sh2, ag, hg, wo, lg, lb, wr_pad):
    B, S, D = x.shape
    tm = TOK_TILE
    half = a.shape[2]
    full = lambda shape: pl.BlockSpec(shape, lambda b, i: (0,) * len(shape))
    tok = lambda w: pl.BlockSpec((1, tm, w), lambda b, i: (b, i, 0))
    per_b = pl.BlockSpec((1, 1, D), lambda b, i: (b, 0, 0))
    return pl.pallas_call(
        _outproj_kernel,
        grid=(B, S // tm),
        in_specs=[tok(half), tok(half), tok(D), per_b, per_b, per_b,
                  full((1, half)), full((1, half)), full((D, D)), full((1, D)), full((1, D)),
                  full((D, LANES))],
        out_specs=[tok(D), tok(D), pl.BlockSpec((1, N_EXPERTS, tm), lambda b, i: (b, 0, i))],
        out_shape=[
            jax.ShapeDtypeStruct((B, S, D), F32),
            jax.ShapeDtypeStruct((B, S, D), BF16),
            jax.ShapeDtypeStruct((B, N_EXPERTS, S), F32),
        ],
        compiler_params=_params("parallel", "parallel"),
        name="out_proj_ln1_router",
    )(a, hy, x, g1, sc2, sh2, ag, hg, wo, lg, lb, wr_pad)


def _select_kernel(aff_ref, ones_ref, utri_ref, ltri_ref, pos_ref, posall_ref, g_ref, *, cap):
    aff = aff_ref[0]
    ones = ones_ref[...]

    def rowsum(mf):
        return jnp.dot(mf, ones, preferred_element_type=F32)

    def expert_total(mf):
        x = mf.astype(F32).reshape(N_EXPERTS, aff.shape[0] // N_EXPERTS, LANES)
        tot = jnp.sum(jnp.sum(x, axis=1, keepdims=True), axis=2, keepdims=True)
        return jnp.broadcast_to(tot, x.shape).reshape(aff.shape)

    def prefix(mf):
        return (jnp.dot(mf, utri_ref[...], preferred_element_type=F32)
                + jnp.dot(ltri_ref[...], rowsum(mf).astype(BF16), preferred_element_type=F32))

    def ind(m):
        return jnp.where(m, 1.0, 0.0).astype(BF16)

    def search(i, thr_bits):
        cand = thr_bits | jnp.left_shift(jnp.int32(1), 30 - i)
        enough = expert_total(ind(aff >= lax.bitcast_convert_type(cand, F32))) >= cap
        return jnp.where(enough, cand, thr_bits)

    thr = lax.bitcast_convert_type(lax.fori_loop(0, 31, search, jnp.zeros(aff.shape, jnp.int32)), F32)
    gt = aff > thr
    eq = aff == thr
    room = cap - expert_total(ind(gt))
    take_eq = jnp.where(eq, jnp.where(prefix(ind(eq)) < room, 1.0, 0.0), 0.0)
    sel = jnp.where(gt, 1.0, take_eq)
    pos = prefix(sel.astype(BF16)).astype(jnp.int32)
    posall_ref[0] = pos
    pos_ref[0] = jnp.where(sel > 0.0, pos, -1)
    g_ref[0] = jnp.where(sel > 0.0, aff, 0.0)


def _select(aff2, cap):
    B, R, _ = aff2.shape
    rpe = R // N_EXPERTS
    li = jnp.arange(LANES, dtype=jnp.int32)
    ri = jnp.arange(R, dtype=jnp.int32)
    same = (ri[:, None] // rpe) == (ri[None, :] // rpe)
    ones = jnp.ones((LANES, LANES), BF16)
    utri = (li[:, None] < li[None, :]).astype(BF16)
    ltri = (same & (ri[None, :] < ri[:, None])).astype(BF16)
    blk = pl.BlockSpec((1, R, LANES), lambda b: (b, 0, 0))
    full = lambda arr: pl.BlockSpec(arr.shape, lambda b: (0, 0))
    return pl.pallas_call(
        functools.partial(_select_kernel, cap=cap),
        grid=(B,),
        in_specs=[blk, full(ones), full(utri), full(ltri)],
        out_specs=[blk, blk, blk],
        out_shape=[jax.ShapeDtypeStruct((B, R, LANES), jnp.int32), jax.ShapeDtypeStruct((B, R, LANES), jnp.int32),
                   jax.ShapeDtypeStruct((B, R, LANES), F32)],
        compiler_params=_params("parallel"),
        name="expert_choice_select",
    )(aff2, ones, utri, ltri)


GATHER_TOK = 1024
GATHER_WIN = 160


def _gather_kernel(offs_ref, u_ref, pos_ref, xe_ref, acc, *, cap, nblk):
    b, e = pl.program_id(0), pl.program_id(1)
    base = (b * N_EXPERTS + e) * (nblk + 1)
    rows = GATHER_TOK // LANES
    acc[...] = jnp.zeros(acc.shape, F32)
    slot = lax.broadcasted_iota(jnp.int32, (GATHER_WIN, LANES), 0)

    def block(j, carry):
        off, end = offs_ref[base + j], offs_ref[base + j + 1]
        w = (off // SUBLANES) * SUBLANES
        nsub = jnp.where(end > off, (end - w + GATHER_WIN - 1) // GATHER_WIN, 0)
        prow = pos_ref[0, pl.ds(pl.multiple_of(j * rows, rows), rows), :]
        ublk = u_ref[0, pl.ds(pl.multiple_of(j * GATHER_TOK, GATHER_TOK), GATHER_TOK), :]

        def window(i, c):
            s0 = pl.multiple_of(w + i * GATHER_WIN, SUBLANES)
            onehot = jnp.concatenate(
                [jnp.where(slot == prow[r:r + 1, :] - s0, 1.0, 0.0).astype(BF16) for r in range(rows)], axis=1)
            acc[pl.ds(s0, GATHER_WIN), :] += jnp.dot(onehot, ublk, preferred_element_type=F32)
            return c

        lax.fori_loop(0, nsub, window, 0)
        return carry

    lax.fori_loop(0, nblk, block, 0)
    xe_ref[0] = acc[:cap, :].astype(BF16)


def _gather(offs, u2, pos_sel, cap):
    B, S, D = u2.shape
    rpe = S // LANES
    nblk = S // GATHER_TOK
    return pl.pallas_call(
        functools.partial(_gather_kernel, cap=cap, nblk=nblk),
        grid_spec=pltpu.PrefetchScalarGridSpec(
            num_scalar_prefetch=1,
            grid=(B, N_EXPERTS),
            in_specs=[pl.BlockSpec((1, S, D), lambda b, e, o: (b, 0, 0)),
                      pl.BlockSpec((1, rpe, LANES), lambda b, e, o: (b, e, 0))],
            out_specs=pl.BlockSpec((1, cap, D), lambda b, e, o: (e, b, 0)),
            scratch_shapes=[pltpu.VMEM((cap + GATHER_WIN + SUBLANES, D), F32)],
        ),
        out_shape=jax.ShapeDtypeStruct((N_EXPERTS, B * cap, D), BF16),
        compiler_params=_params("parallel", "parallel"),
        name="expert_gather",
    )(offs, u2, pos_sel)


def _moe_kernel(x_ref, wg_ref, wu_ref, wd_ref, o_ref, wgb, wub, wdb):
    @pl.when(pl.program_id(1) == 0)
    def _():
        wgb[...] = wg_ref[0].astype(BF16)
        wub[...] = wu_ref[0].astype(BF16)
        wdb[...] = wd_ref[0].astype(BF16)

    x = x_ref[0]
    hg = jnp.dot(x, wgb[...], preferred_element_type=F32)
    hu = jnp.dot(x, wub[...], preferred_element_type=F32)
    h = (hg * jax.nn.sigmoid(hg) * hu).astype(BF16)
    o_ref[0] = jnp.dot(h, wdb[...], preferred_element_type=F32).astype(BF16)


def _moe_ffn(xe, w_gate, w_up, w_down):
    E, R, D = xe.shape
    Fh = w_gate.shape[2]
    tr = MOE_ROWS
    return pl.pallas_call(
        _moe_kernel,
        grid=(E, R // tr),
        in_specs=[
            pl.BlockSpec((1, tr, D), lambda e, r: (e, r, 0)),
            pl.BlockSpec((1, D, Fh), lambda e, r: (e, 0, 0)),
            pl.BlockSpec((1, D, Fh), lambda e, r: (e, 0, 0)),
            pl.BlockSpec((1, Fh, D), lambda e, r: (e, 0, 0)),
        ],
        out_specs=pl.BlockSpec((1, tr, D), lambda e, r: (e, r, 0)),
        out_shape=jax.ShapeDtypeStruct((E, R, D), BF16),
        scratch_shapes=[pltpu.VMEM((D, Fh), BF16), pltpu.VMEM((D, Fh), BF16), pltpu.VMEM((Fh, D), BF16)],
        compiler_params=_params("parallel", "arbitrary"),
        name="moe_swiglu",
    )(xe, w_gate, w_up, w_down)


CMB_TOK = 512
CMB_WIN = 128
BF16_ROWS = 16


def _combine_kernel(offs_ref, x1_ref, pos_ref, g_ref, g2_ref, lg_ref, lb_ref, ye_hbm, o_ref,
                    rhs, lhs, acc, sem, *, cap, nblk, nbatch):
    b, i = pl.program_id(0), pl.program_id(1)
    W = CMB_WIN
    total = ye_hbm.shape[0]
    step = b * nblk + i
    cur_slot = step % 2
    row_e = lax.broadcasted_iota(jnp.int32, (N_EXPERTS, 1), 0)
    slot_w = lax.broadcasted_iota(jnp.int32, (W, CMB_TOK), 0)

    def windows(bb, ii):
        out = []
        for e in range(N_EXPERTS):
            k = (bb * N_EXPERTS + e) * (nblk + 1) + ii
            off, end = offs_ref[k], offs_ref[k + 1]
            lo = (off // BF16_ROWS) * BF16_ROWS
            out.append(((e * nbatch + bb) * cap, lo, jnp.where(end > off, (end - lo + W - 1) // W, 0)))
        return out

    def fetch_starts(win, p):
        return [jnp.minimum(rb + lo + p * W, total - W) for rb, lo, _ in win]

    def window_copy(e, start, s):
        return pltpu.make_async_copy(ye_hbm.at[pl.ds(pl.multiple_of(start, BF16_ROWS), W), :],
                                     rhs.at[s, pl.ds(e * W, W), :], sem.at[s, e])

    def start_fetch(win, p, s):
        for e, start in enumerate(fetch_starts(win, p)):
            window_copy(e, start, s).start()

    def absorb(win, p, s, first):
        starts = fetch_starts(win, p)
        lo_col = jnp.zeros((N_EXPERTS, 1), jnp.int32)
        shift_col = jnp.zeros((N_EXPERTS, 1), jnp.int32)
        for e, (rb, lo0, _) in enumerate(win):
            lo = lo0 + p * W
            lo_col = jnp.where(row_e == e, lo, lo_col)
            shift_col = jnp.where(row_e == e, rb + lo - starts[e], shift_col)
        d = pos_ref[0] - lo_col
        rel = jnp.where((d >= 0) & (d < W), d + shift_col, -1)
        g = g_ref[0]
        for e in range(N_EXPERTS):
            lhs[e * W:(e + 1) * W, :] = jnp.where(slot_w == rel[e:e + 1, :], g[e:e + 1, :], 0.0).astype(BF16)
        for e in range(N_EXPERTS):
            window_copy(e, starts[e], s).wait()
        part = lax.dot_general(lhs[...], rhs[s], (((0,), (0,)), ((), ())), preferred_element_type=F32)
        acc[...] = part if first else acc[...] + part

    cur = windows(b, i)

    @pl.when(step == 0)
    def _():
        start_fetch(cur, 0, 0)

    @pl.when(step + 1 < nbatch * nblk)
    def _():
        wrap = i + 1 == nblk
        start_fetch(windows(jnp.where(wrap, b + 1, b), jnp.where(wrap, 0, i + 1)), 0, 1 - cur_slot)

    absorb(cur, 0, cur_slot, True)

    def extra_pass(p, carry):
        start_fetch(cur, p, cur_slot)
        absorb(cur, p, cur_slot, False)
        return carry

    npass = functools.reduce(jnp.maximum, [n for _, _, n in cur])
    lax.fori_loop(1, npass, extra_pass, 0)
    o_ref[0] = _layer_norm(DN_ALPHA * x1_ref[0] + g2_ref[0] * acc[...], lg_ref[...], lb_ref[...])


def _combine_ln2(offs, x1, pos_exp, g_exp, g2, lg, lb, ye_rows, cap):
    B, S, D = x1.shape
    T = CMB_TOK
    nblk = S // T
    tok = lambda w: pl.BlockSpec((1, T, w), lambda b, i, o: (b, i, 0))
    vec = pl.BlockSpec((1, D), lambda b, i, o: (0, 0))
    exp_major = pl.BlockSpec((1, N_EXPERTS, T), lambda b, i, o: (b, 0, i))
    return pl.pallas_call(
        functools.partial(_combine_kernel, cap=cap, nblk=nblk, nbatch=B),
        grid_spec=pltpu.PrefetchScalarGridSpec(
            num_scalar_prefetch=1,
            grid=(B, nblk),
            in_specs=[tok(D), exp_major, exp_major,
                      pl.BlockSpec((1, 1, D), lambda b, i, o: (b, 0, 0)), vec, vec,
                      pl.BlockSpec(memory_space=pl.ANY)],
            out_specs=tok(D),
            scratch_shapes=[pltpu.VMEM((2, N_EXPERTS * CMB_WIN, D), BF16),
                            pltpu.VMEM((N_EXPERTS * CMB_WIN, T), BF16),
                            pltpu.VMEM((T, D), F32),
                            pltpu.SemaphoreType.DMA((2, N_EXPERTS))],
        ),
        out_shape=jax.ShapeDtypeStruct((B, S, D), F32),
        compiler_params=_params("arbitrary", "arbitrary"),
        name="combine_residual_ln2",
    )(offs, x1, pos_exp, g_exp, g2, lg, lb, ye_rows)


DFT_R = 128
K1_HALF = 40
K1_VALID = DFT_R // 2 + 1
K1_GROUPS = (K1_HALF // SUBLANES, -(-(K1_VALID - K1_HALF) // SUBLANES))
K1_SPEC = K1_HALF + K1_GROUPS[1] * SUBLANES
HY_CT = 128


@functools.lru_cache(maxsize=None)
def _dft_tables_np(S):
    N = 2 * S
    na = S // DFT_R
    r = np.arange(2 * K1_HALF)
    k1 = np.arange(2)[:, None] * K1_HALF + r[None, :] % K1_HALF
    valid = (k1 < K1_VALID).astype(np.float64)
    n = DFT_R * np.arange(na)[None, :] + np.arange(DFT_R)[:, None]
    theta = ((n[None, :, None, :] * k1[:, None, :, None]) % N) * (2.0 * math.pi / N)
    is_im = (r >= K1_HALF)[None, None, :, None]
    g = np.where(is_im, -np.sin(theta), np.cos(theta)) * valid[:, None, :, None]
    cw = np.where((k1 == 0) | (k1 == DFT_R // 2), 1.0, 2.0) * valid / N
    gi = np.transpose(g * cw[:, None, :, None], (0, 1, 3, 2))
    ph = ((np.arange(DFT_R)[:, None] * np.arange(DFT_R)[None, :]) % DFT_R) * (2.0 * math.pi / DFT_R)
    C, Sn = np.cos(ph), np.sin(ph)
    f3f = np.block([[C, Sn], [-Sn, C]])
    f3i = np.block([[C, -Sn], [Sn, C]])

    def pair_lanes(t):
        h, nb, nr, nc = t.shape
        return t.reshape(h, nb // 2, 2, nr, nc).transpose(0, 1, 3, 2, 4).reshape(h, nb // 2, nr, 2 * nc)

    return tuple(np.ascontiguousarray(t, dtype=np.float32) for t in (pair_lanes(g), pair_lanes(gi), f3f, f3i))


def _dft_tables(S):
    return tuple(jnp.asarray(t).astype(BF16) for t in _dft_tables_np(S))


def _swap01(x):
    return jnp.swapaxes(x, 0, 1)


def _block_diag_lanes(top, bot):
    return jnp.concatenate([jnp.concatenate([top, jnp.zeros((top.shape[0], bot.shape[1]), top.dtype)], axis=1),
                            jnp.concatenate([jnp.zeros((bot.shape[0], top.shape[1]), bot.dtype), bot], axis=1)], axis=0)


def _dft_stage1(z_ref, gf_ref, hf, a_scr):
    ct = a_scr.shape[2]

    def body(g, carry):
        b0 = pl.multiple_of(g * SUBLANES, SUBLANES)
        zt = _swap01(z_ref[:, pl.ds(b0, SUBLANES), :]).astype(BF16)
        for p in range(SUBLANES // 2):
            a2 = jnp.dot(gf_ref[hf, g * (SUBLANES // 2) + p], _block_diag_lanes(zt[2 * p], zt[2 * p + 1]),
                         preferred_element_type=F32)
            a_scr[b0 + 2 * p] = a2[:, :ct]
            a_scr[b0 + 2 * p + 1] = a2[:, ct:]
        return carry

    lax.fori_loop(0, DFT_R // SUBLANES, body, 0, unroll=4)


def _dft_stage3_operands(a_scr, r0):
    mr = _swap01(a_scr[:, pl.ds(r0, SUBLANES), :])
    mi = _swap01(a_scr[:, pl.ds(K1_HALF + r0, SUBLANES), :])
    return [jnp.concatenate([jnp.concatenate([mr[j], mi[j]], axis=0),
                             jnp.concatenate([mr[j + 1], mi[j + 1]], axis=0)], axis=1).astype(BF16)
            for j in range(0, SUBLANES, 2)]


def _filter_mlp_kernel(feat_ref, w1_ref, b1_ref, fr_ref, w2_ref, b2_ref, w3_ref, b3_ref, dec_ref, o_ref):
    feats = feat_ref[...]
    fr = fr_ref[...]
    h = jnp.sin(fr * (jnp.dot(feats, w1_ref[...], precision=HIGHEST, preferred_element_type=F32) + b1_ref[...]))
    h = jnp.sin(fr * (jnp.dot(h, w2_ref[...], precision=HIGHEST, preferred_element_type=F32) + b2_ref[...]))
    h = _dot3(h, w3_ref[...]) + b3_ref[...]
    h = h * jnp.exp(-feats[:, 0:1] * jnp.abs(dec_ref[...]))
    row = lax.broadcasted_iota(jnp.int32, h.shape, 0) + pl.program_id(0) * h.shape[0]
    col = lax.broadcasted_iota(jnp.int32, h.shape, 1)
    o_ref[...] = jnp.where((row == 0) & (col >= h.shape[1] // 2), 0.0, h)


@functools.lru_cache(maxsize=None)
def _filter_features_np(L, nf):
    pos = np.arange(L, dtype=np.float64)
    t = pos[:, None] / (L - 1)
    bands = (nf - 1) // 2
    freqs = np.linspace(1e-4, bands - 1, bands)
    phase = (2.0 * math.pi / L) * pos[:, None] * freqs[None, :]
    feats = np.concatenate([t, np.cos(phase), -np.sin(phase), np.zeros((L, LANES - nf))], axis=-1)
    return np.ascontiguousarray(feats, dtype=np.float32)


def _filter_taps(L, w1, b1, freq, w2, b2, w3, b3, decay):
    nf = w1.shape[0]
    feats = jnp.asarray(_filter_features_np(L, nf))
    w1p = jnp.concatenate([w1, jnp.zeros((LANES - nf, w1.shape[1]), F32)], axis=0)
    fo = w1.shape[1]
    ncol = w3.shape[1]
    tl = 512
    full = lambda shape: pl.BlockSpec(shape, lambda i: (0,) * len(shape))
    return pl.pallas_call(
        _filter_mlp_kernel,
        grid=(L // tl,),
        in_specs=[pl.BlockSpec((tl, LANES), lambda i: (i, 0)), full((LANES, fo)), full((1, fo)), full((1, fo)),
                  full((fo, fo)), full((1, fo)), full((fo, ncol)), full((1, ncol)), full((1, ncol))],
        out_specs=pl.BlockSpec((tl, ncol), lambda i: (i, 0)),
        out_shape=jax.ShapeDtypeStruct((L, ncol), F32),
        compiler_params=_params("parallel"),
        name="hyena_filter_mlp",
    )(feats, w1p, b1[None, :], freq[None, :], w2, b2[None, :], w3, b3[None, :], decay.reshape(1, ncol))


def _filter_spectrum_kernel(hf_ref, hb_ref, gf_ref, f3f_ref, o_ref, a_scr, x_scr):
    ct = a_scr.shape[2]
    for t, src in enumerate((hf_ref, hb_ref)):
        for hf in range(2):
            _dft_stage1(src, gf_ref, hf, a_scr)

            def body(g, carry):
                r0 = pl.multiple_of(g * SUBLANES, SUBLANES)
                for p, m in enumerate(_dft_stage3_operands(a_scr, r0)):
                    x2 = jnp.dot(f3f_ref[...], m, preferred_element_type=F32)
                    for q in range(2):
                        k1 = hf * K1_HALF + r0 + 2 * p + q
                        x = x2[:, q * ct:(q + 1) * ct]
                        if t == 0:
                            x_scr[k1] = x
                        else:
                            xf = x_scr[k1]
                            o_ref[0, k1] = jnp.concatenate([xf[:DFT_R] + x[:DFT_R], xf[DFT_R:] - x[DFT_R:]],
                                                           axis=0).astype(BF16)
                return carry

            lax.fori_loop(0, K1_GROUPS[hf], body, 0, unroll=True)


def _filter_spectrum(taps3, gf, f3f):
    na, nb, ncol = taps3.shape
    C = ncol // 4
    nct = C // HY_CT
    ct = HY_CT
    return pl.pallas_call(
        _filter_spectrum_kernel,
        grid=(2, nct),
        in_specs=[
            pl.BlockSpec((na, nb, ct), lambda o, j: (0, 0, o * nct + j)),
            pl.BlockSpec((na, nb, ct), lambda o, j: (0, 0, 2 * nct + o * nct + j)),
            pl.BlockSpec(gf.shape, lambda o, j: (0, 0, 0, 0), pipeline_mode=pl.Buffered(1)),
            pl.BlockSpec(f3f.shape, lambda o, j: (0, 0), pipeline_mode=pl.Buffered(1)),
        ],
        out_specs=pl.BlockSpec((1, K1_SPEC, 2 * DFT_R, ct), lambda o, j: (o, 0, 0, j)),
        out_shape=jax.ShapeDtypeStruct((2, K1_SPEC, 2 * DFT_R, C), BF16),
        scratch_shapes=[pltpu.VMEM((DFT_R, 2 * K1_HALF, ct), F32), pltpu.VMEM((K1_SPEC, 2 * DFT_R, ct), F32)],
        compiler_params=_params("parallel", "parallel"),
        name="hyena_filter_spectrum",
    )(taps3, taps3, gf, f3f)


def _long_conv_kernel(z_ref, xg_ref, h_ref, bias_ref, gf_ref, f3f_ref, f3i_ref, gi_ref, o_ref, a_scr, b_scr):
    ct = a_scr.shape[2]
    bias = bias_ref[0]
    for hf in range(2):
        _dft_stage1(z_ref.at[0], gf_ref, hf, a_scr)
        ngrp = K1_GROUPS[hf]
        if ngrp * SUBLANES < K1_HALF:
            b_scr[ngrp * SUBLANES:] = jnp.zeros((K1_HALF - ngrp * SUBLANES,) + b_scr.shape[1:], F32)

        def spectral(g, carry):
            r0 = pl.multiple_of(g * SUBLANES, SUBLANES)
            for p, m in enumerate(_dft_stage3_operands(a_scr, r0)):
                x = jnp.dot(f3f_ref[...], m, preferred_element_type=F32)
                k1 = hf * K1_HALF + r0 + 2 * p
                h = jnp.concatenate([h_ref[0, k1], h_ref[0, k1 + 1]], axis=1).astype(F32)
                xr, xi, hr, hi = x[:DFT_R], x[DFT_R:], h[:DFT_R], h[DFT_R:]
                y = jnp.concatenate([xr * hr - xi * hi, xr * hi + xi * hr], axis=0).astype(BF16)
                bb = jnp.dot(f3i_ref[...], y, preferred_element_type=F32)
                b_scr[r0 + 2 * p] = bb[:, :ct]
                b_scr[r0 + 2 * p + 1] = bb[:, ct:]
            return carry

        lax.fori_loop(0, ngrp, spectral, 0, unroll=True)

        def inverse(g, carry):
            b0 = pl.multiple_of(g * SUBLANES, SUBLANES)
            br = _swap01(b_scr[:, pl.ds(b0, SUBLANES), :])
            bi = _swap01(b_scr[:, pl.ds(DFT_R + b0, SUBLANES), :])
            ys = []
            for p in range(SUBLANES // 2):
                bm = [jnp.concatenate([br[2 * p + q], bi[2 * p + q]], axis=0).astype(BF16) for q in range(2)]
                y2 = jnp.dot(gi_ref[hf, g * (SUBLANES // 2) + p], _block_diag_lanes(bm[0], bm[1]),
                             preferred_element_type=F32)
                ys += [y2[:, :ct], y2[:, ct:]]
            y = _swap01(jnp.stack(ys, axis=0))
            sl = (0, slice(None), pl.ds(b0, SUBLANES), slice(None))
            if hf == 0:
                o_ref[sl] = y
            else:
                o_ref[sl] = xg_ref[sl] * (o_ref[sl] + y + z_ref[sl] * bias)
            return carry

        lax.fori_loop(0, DFT_R // SUBLANES, inverse, 0, unroll=4)


def _long_conv(z_arr, z_off, xg_arr, xg_off, spec, order, bias3, gf, f3f, f3i, gi):
    B, na, nb, _ = z_arr.shape
    C = spec.shape[3]
    ct = HY_CT
    const = lambda arr: pl.BlockSpec(arr.shape, lambda j, b: (0,) * arr.ndim, pipeline_mode=pl.Buffered(1))
    return pl.pallas_call(
        _long_conv_kernel,
        grid=(C // ct, B),
        in_specs=[
            pl.BlockSpec((1, na, nb, ct), lambda j, b: (b, 0, 0, z_off + j)),
            pl.BlockSpec((1, na, nb, ct), lambda j, b: (b, 0, 0, xg_off + j)),
            pl.BlockSpec((1, K1_SPEC, 2 * DFT_R, ct), lambda j, b: (order, 0, 0, j), pipeline_mode=pl.Buffered(1)),
            pl.BlockSpec((1, 1, ct), lambda j, b: (order, 0, j)),
            const(gf), const(f3f), const(f3i), const(gi),
        ],
        out_specs=pl.BlockSpec((1, na, nb, ct), lambda j, b: (b, 0, 0, j)),
        out_shape=jax.ShapeDtypeStruct((B, na, nb, C), F32),
        scratch_shapes=[pltpu.VMEM((DFT_R, 2 * K1_HALF, ct), F32), pltpu.VMEM((K1_HALF, 2 * DFT_R, ct), F32)],
        compiler_params=_params("parallel", "parallel"),
        name="hyena_long_conv",
    )(z_arr, xg_arr, spec, bias3, gf, f3f, f3i, gi)


def _hyena_group(hy_conv, fw1, fb1, ffreq, fw2, fb2, fw3, fb3, decay, hyena_bias):
    B, S, _ = hy_conv.shape
    na = S // DFT_R
    gf, gi, f3f, f3i = _dft_tables(S)
    taps = _filter_taps(S, fw1, fb1, ffreq, fw2, fb2, fw3, fb3, decay)
    spec = _filter_spectrum(taps.reshape(na, DFT_R, -1), gf, f3f)
    u = hy_conv.reshape(B, na, DFT_R, -1)
    nct = HY_W // HY_CT
    bias3 = hyena_bias[:, None, :]
    z1 = _long_conv(u, 0, u, nct, spec, 0, bias3, gf, f3f, f3i, gi)
    return _long_conv(z1, 0, u, 2 * nct, spec, 1, bias3, gf, f3f, f3i, gi).reshape(B, S, HY_W)


def _rope_rot_cols(w):
    half = ROPE // 2
    return jnp.concatenate([-w[..., half:], w[..., :half]], axis=-1)


def _prep_weights(w_in, w_qb, w_kvb):
    D = w_in.shape[0]
    o = Q_RANK + KV_RANK
    w_kr = w_in[:, o:o + ROPE]
    z64 = jnp.zeros((D, NOPE), F32)
    z32 = jnp.zeros((D, LANES - NOPE - ROPE), F32)
    win_aug = jnp.concatenate([
        w_in[:, :o],
        z64, w_kr, z32,
        z64, _rope_rot_cols(w_kr), z32,
        w_in[:, o + ROPE:],
    ], axis=1).astype(BF16)
    wq = w_qb.reshape(Q_RANK, MLA_HEADS, NOPE + ROPE)
    zq = jnp.zeros((Q_RANK, MLA_HEADS, LANES - NOPE - ROPE), F32)
    wq_main = jnp.concatenate([wq, zq], axis=-1).reshape(Q_RANK, MLA_HEADS * LANES).astype(BF16)
    wq_rot = jnp.concatenate([jnp.zeros((Q_RANK, MLA_HEADS, NOPE), F32), _rope_rot_cols(wq[..., NOPE:]), zq],
                             axis=-1).reshape(Q_RANK, MLA_HEADS * LANES).astype(BF16)
    wkv = w_kvb.reshape(KV_RANK, MLA_HEADS, NOPE + V_DIM)
    wk = jnp.concatenate([wkv[..., :NOPE], jnp.zeros((KV_RANK, MLA_HEADS, LANES - NOPE), F32)],
                         axis=-1).reshape(KV_RANK, MLA_HEADS * LANES).astype(BF16)
    wv = wkv[..., NOPE:].reshape(KV_RANK, MLA_HEADS * V_DIM).astype(BF16)
    return win_aug, wq_main, wq_rot, wk, wv


def kernel(x, c, positions, w_ada, b_ada, w_in, q_norm_g, w_qb, kv_norm_g, w_kvb, conv_w, conv_b, filt_w1, filt_b1, filt_freq, filt_w2, filt_b2, filt_w3, filt_b3, hyena_decay, hyena_bias, attn_out_g, hyena_out_g, w_o, ln1_g, ln1_b, w_router, w_gate, w_up, w_down, ln2_g, ln2_b):
    B, S, D = x.shape
    l = 0
    c8 = jnp.zeros((SUBLANES, D), F32).at[:B].set(c)
    mod = _ada_mod(c8, w_ada[l], b_ada[l][None, :])[:B]
    sh1, sc1, g1, sh2, sc2, g2 = [m[:, None, :] for m in jnp.split(mod, 6, axis=-1)]

    win_aug, wq_main, wq_rot, wk, wv = _prep_weights(w_in[l], w_qb[l], w_kvb[l])
    cos_h, sin_h = _rope_tables(positions)
    pad = lambda n, v: jnp.full((B, n, S), v, F32)
    cos_t = jnp.concatenate([pad(NOPE, 1.0), cos_h, cos_h, pad(LANES - NOPE - ROPE, 1.0)], axis=1)
    sin_t = jnp.concatenate([pad(NOPE, 0.0), sin_h, sin_h, pad(LANES - NOPE - ROPE, 0.0)], axis=1)
    q, kt, v, hy_conv = _inproj(x, sc1, sh1, cos_t, sin_t, win_aug, q_norm_g[l][None, :], wq_main, wq_rot,
                                kv_norm_g[l][None, :], wk, wv, conv_w[l], conv_b[l])
    a = _attention(q, kt, v)

    hy = _hyena_group(hy_conv, filt_w1[l], filt_b1[l], filt_freq[l], filt_w2[l],
                      filt_b2[l], filt_w3[l], filt_b3[l], hyena_decay[l], hyena_bias[l])

    wr_pad = jnp.concatenate([w_router[l], jnp.zeros((D, LANES - N_EXPERTS), F32)], axis=1)
    x1, u2, aff_t = _outproj(a, hy, x, g1, sc2, sh2, attn_out_g[l][None, :], hyena_out_g[l][None, :],
                             w_o[l].astype(BF16), ln1_g[l][None, :], ln1_b[l][None, :], wr_pad)

    cap = EC_FACTOR * S // N_EXPERTS
    rpe = S // LANES
    aff2 = aff_t.reshape(B, N_EXPERTS * rpe, LANES)
    pos_sel, pos_all, gsel = _select(aff2, cap)
    row_start = pos_all.reshape(B, N_EXPERTS, rpe, LANES)[:, :, :, 0]

    def block_offsets(tokens):
        ends = jnp.full((B, N_EXPERTS, 1), cap, jnp.int32)
        return jnp.concatenate([row_start[:, :, ::tokens // LANES], ends], axis=-1).reshape(-1)

    xe = _gather(block_offsets(GATHER_TOK), u2, pos_sel, cap)
    ye = _moe_ffn(xe, w_gate[l], w_up[l], w_down[l])
    return _combine_ln2(block_offsets(CMB_TOK), x1, pos_sel.reshape(B, N_EXPERTS, S), gsel.reshape(B, N_EXPERTS, S), g2, ln2_g[l][None, :], ln2_b[l][None, :],
                        ye.reshape(N_EXPERTS * B * cap, D), cap)
```

```python
import functools
import math

import jax
import jax.numpy as jnp
import numpy as np
from jax import lax
from jax.experimental import pallas as pl
from jax.experimental.pallas import tpu as pltpu

F32 = jnp.float32
BF16 = jnp.bfloat16
HIGHEST = lax.Precision.HIGHEST

LANES = 128
SUBLANES = 8
VMEM_LIMIT = 56 * 1024 * 1024

D_MODEL = 1024
MLA_HEADS = 8
NOPE = 64
ROPE = 32
V_DIM = 64
Q_RANK = 256
KV_RANK = 128
HY_W = 512
HY_COLS = 3 * HY_W
N_EXPERTS = 16
EC_FACTOR = 2
ROPE_THETA = 10000.0
EPS = 1e-5
DN_ALPHA = 2.0 ** 0.25
LOG2E = 1.4426950408889634

TOK_TILE = 512
ATT_TQ = 512
ATT_TK = 512
MOE_ROWS = 512


def _params(*sem):
    return pltpu.CompilerParams(dimension_semantics=sem, vmem_limit_bytes=VMEM_LIMIT)


def _ada_kernel(c_ref, w_ref, b_ref, o_ref):
    c = c_ref[...]
    sc = c * jax.nn.sigmoid(c)
    o_ref[...] = jnp.dot(sc, w_ref[...], precision=HIGHEST, preferred_element_type=F32) + b_ref[...]


def _ada_mod(c8, w_ada, b_ada):
    n = w_ada.shape[1]
    tn = 1024
    return pl.pallas_call(
        _ada_kernel,
        grid=(n // tn,),
        in_specs=[
            pl.BlockSpec((SUBLANES, D_MODEL), lambda j: (0, 0)),
            pl.BlockSpec((D_MODEL, tn), lambda j: (0, j)),
            pl.BlockSpec((1, tn), lambda j: (0, j)),
        ],
        out_specs=pl.BlockSpec((SUBLANES, tn), lambda j: (0, j)),
        out_shape=jax.ShapeDtypeStruct((SUBLANES, n), F32),
        compiler_params=_params("arbitrary"),
        name="ada_mod",
    )(c8, w_ada, b_ada)


def _dot3(a, b):
    ah, bh = a.astype(BF16), b.astype(BF16)
    al = (a - ah.astype(F32)).astype(BF16)
    bl = (b - bh.astype(F32)).astype(BF16)
    dot = functools.partial(jnp.dot, preferred_element_type=F32)
    return dot(ah, bh) + (dot(ah, bl) + dot(al, bh))


def _rms(x, g):
    return x * lax.rsqrt(jnp.mean(x * x, axis=-1, keepdims=True) + EPS) * g


def _rope_kernel(pos_ref, freq_ref, cos_ref, sin_ref):
    ang = freq_ref[...] * pos_ref[0].astype(F32)
    cos_ref[0] = jnp.cos(ang)
    sin_ref[0] = jnp.sin(ang)


def _rope_tables(positions):
    B, S = positions.shape
    half = ROPE // 2
    inv_freq = (ROPE_THETA ** (-jnp.arange(half, dtype=F32) / half))[:, None]
    out = jax.ShapeDtypeStruct((B, half, S), F32)
    blk = pl.BlockSpec((1, half, S), lambda b: (b, 0, 0))
    return pl.pallas_call(
        _rope_kernel,
        grid=(B,),
        in_specs=[pl.BlockSpec((1, 1, S), lambda b: (b, 0, 0)), pl.BlockSpec((half, 1), lambda b: (0, 0))],
        out_specs=[blk, blk],
        out_shape=[out, out],
        compiler_params=_params("parallel"),
        name="rope_tables",
    )(positions[:, None, :], inv_freq)


def _inproj_kernel(x_ref, xp_ref, xn_ref, sc_ref, sh_ref, cos_ref, sin_ref, win_ref, qg_ref, wq_ref, wqr_ref,
                   kvg_ref, wk_ref, wv_ref, cw_ref, cb_ref, q_ref, k_ref, v_ref, hy_ref):
    def modulate(xv):
        return (xv * (1.0 + sc_ref[0]) + sh_ref[0]).astype(BF16)

    u = modulate(x_ref[0])
    cos = cos_ref[0].T
    sin = sin_ref[0].T

    def proj(lo, hi):
        return jnp.dot(u, win_ref[:, lo:hi], preferred_element_type=F32)

    cq = _rms(proj(0, Q_RANK), qg_ref[...]).astype(BF16)
    q_main = jnp.dot(cq, wq_ref[...], preferred_element_type=F32)
    q_rot = jnp.dot(cq, wqr_ref[...], preferred_element_type=F32)
    qscale = LOG2E * (NOPE + ROPE) ** -0.5
    cq_s = cos * qscale
    sq_s = sin * qscale
    for h in range(MLA_HEADS):
        sl = slice(h * LANES, (h + 1) * LANES)
        q_ref[0, h] = (q_main[:, sl] * cq_s + q_rot[:, sl] * sq_s).T.astype(BF16)

    o = Q_RANK
    ckv = _rms(proj(o, o + KV_RANK), kvg_ref[...]).astype(BF16)
    kr = proj(o + KV_RANK, o + 2 * KV_RANK) * cos + proj(o + 2 * KV_RANK, o + 3 * KV_RANK) * sin
    kn = jnp.dot(ckv, wk_ref[...], preferred_element_type=F32)
    for h in range(MLA_HEADS):
        k_ref[0, h] = (kn[:, h * LANES:(h + 1) * LANES] + kr).astype(BF16)
    v_ref[0] = jnp.dot(ckv, wv_ref[...], preferred_element_type=F32).T.astype(BF16)

    o = Q_RANK + 3 * KV_RANK
    hy = proj(o, o + HY_COLS)
    tm = hy.shape[0]
    i, last = pl.program_id(1), pl.num_programs(1) - 1
    edge = jnp.dot(modulate(jnp.concatenate([xp_ref[0], xn_ref[0]], axis=0)), win_ref[:, o:o + HY_COLS],
                   preferred_element_type=F32)
    prev = jnp.where(i == 0, 0.0, edge[SUBLANES - 1:SUBLANES, :])
    nxt = jnp.where(i == last, 0.0, edge[SUBLANES:SUBLANES + 1, :])
    row = lax.broadcasted_iota(jnp.int32, hy.shape, 0)
    up = jnp.where(row == 0, prev, pltpu.roll(hy, 1, 0))
    dn = jnp.where(row == tm - 1, nxt, pltpu.roll(hy, tm - 1, 0))
    hy_ref[0] = cb_ref[...] + cw_ref[0:1, :] * up + cw_ref[1:2, :] * hy + cw_ref[2:3, :] * dn


def _inproj(x, sc1, sh1, cos_t, sin_t, win_aug, qg, wq, wqr, kvg, wk, wv, conv_w, conv_b):
    B, S, D = x.shape
    tm = TOK_TILE
    ncol = win_aug.shape[1]
    g8 = tm // SUBLANES
    n8 = S // SUBLANES
    full = lambda shape: pl.BlockSpec(shape, lambda b, i: (0,) * len(shape))
    return pl.pallas_call(
        _inproj_kernel,
        grid=(B, S // tm),
        in_specs=[
            pl.BlockSpec((1, tm, D), lambda b, i: (b, i, 0)),
            pl.BlockSpec((1, SUBLANES, D), lambda b, i: (b, jnp.maximum(i * g8 - 1, 0), 0)),
            pl.BlockSpec((1, SUBLANES, D), lambda b, i: (b, jnp.minimum((i + 1) * g8, n8 - 1), 0)),
            pl.BlockSpec((1, 1, D), lambda b, i: (b, 0, 0)),
            pl.BlockSpec((1, 1, D), lambda b, i: (b, 0, 0)),
            pl.BlockSpec((1, LANES, tm), lambda b, i: (b, 0, i)),
            pl.BlockSpec((1, LANES, tm), lambda b, i: (b, 0, i)),
            full((D, ncol)),
            full((1, Q_RANK)),
            full((Q_RANK, MLA_HEADS * LANES)),
            full((Q_RANK, MLA_HEADS * LANES)),
            full((1, KV_RANK)),
            full((KV_RANK, MLA_HEADS * LANES)),
            full((KV_RANK, MLA_HEADS * V_DIM)),
            full(conv_w.shape),
            full((1, HY_COLS)),
        ],
        out_specs=[
            pl.BlockSpec((1, MLA_HEADS, LANES, tm), lambda b, i: (b, 0, 0, i)),
            pl.BlockSpec((1, MLA_HEADS, tm, LANES), lambda b, i: (b, 0, i, 0)),
            pl.BlockSpec((1, MLA_HEADS * V_DIM, tm), lambda b, i: (b, 0, i)),
            pl.BlockSpec((1, tm, HY_COLS), lambda b, i: (b, i, 0)),
        ],
        out_shape=[
            jax.ShapeDtypeStruct((B, MLA_HEADS, LANES, S), BF16),
            jax.ShapeDtypeStruct((B, MLA_HEADS, S, LANES), BF16),
            jax.ShapeDtypeStruct((B, MLA_HEADS * V_DIM, S), BF16),
            jax.ShapeDtypeStruct((B, S, HY_COLS), F32),
        ],
        compiler_params=_params("parallel", "parallel"),
        name="in_proj",
    )(x, x, x, sc1, sh1, cos_t, sin_t, win_aug, qg, wq, wqr, kvg, wk, wv, conv_w, conv_b[None, :])


ATT_AUG = 16


def _attn_kernel(qt_ref, k_ref, vt_ref, o_ref, s_scr):
    S = k_ref.shape[2]
    tq = qt_ref.shape[3]
    nk = S // ATT_TK
    ones_rows = jnp.where(lax.broadcasted_iota(jnp.int32, (ATT_AUG, ATT_TK), 0) == 0, 1.0, 0.0).astype(BF16)

    def scores(c, slot):
        k0 = pl.multiple_of(c * ATT_TK, ATT_TK)
        for h in range(2):
            s_scr[slot, h] = jnp.dot(k_ref[0, h, pl.ds(k0, ATT_TK), :], qt_ref[0, h], preferred_element_type=F32)

    def absorb(c, slot, carry):
        k0 = pl.multiple_of(c * ATT_TK, ATT_TK)
        out = []
        for h in range(2):
            m_prev, acc = carry[h]
            s = s_scr[slot, h]
            m_new = jnp.maximum(m_prev, jnp.max(s, axis=0, keepdims=True))
            alpha = jnp.exp2(m_prev - m_new)
            p = jnp.exp2(s - m_new).astype(BF16)
            v_aug = jnp.concatenate([vt_ref[0, h * V_DIM:(h + 1) * V_DIM, pl.ds(k0, ATT_TK)], ones_rows], axis=0)
            out.append((m_new, alpha * acc + jnp.dot(v_aug, p, preferred_element_type=F32)))
        return tuple(out)

    def pair(i, carry):
        c0 = 2 * i
        scores(c0 + 1, 1)
        carry = absorb(c0, 0, carry)
        scores(jnp.minimum(c0 + 2, nk - 1), 0)
        return absorb(c0 + 1, 1, carry)

    scores(0, 0)
    init = tuple((jnp.full((1, tq), -jnp.inf, F32), jnp.zeros((V_DIM + ATT_AUG, tq), F32)) for _ in range(2))
    fin = lax.fori_loop(0, nk // 2, pair, init, unroll=4)
    o = jnp.concatenate([acc[:V_DIM] * (1.0 / acc[V_DIM:V_DIM + 1]) for (_, acc) in fin], axis=0)
    o_ref[0] = o.T


def _attention(qt, k, vt):
    B, H, _, S = qt.shape
    tq = ATT_TQ
    return pl.pallas_call(
        _attn_kernel,
        grid=(B, H // 2, S // tq),
        in_specs=[
            pl.BlockSpec((1, 2, LANES, tq), lambda b, hp, i: (b, hp, 0, i)),
            pl.BlockSpec((1, 2, S, LANES), lambda b, hp, i: (b, hp, 0, 0)),
            pl.BlockSpec((1, 2 * V_DIM, S), lambda b, hp, i: (b, hp, 0)),
        ],
        out_specs=pl.BlockSpec((1, tq, 2 * V_DIM), lambda b, hp, i: (b, i, hp)),
        out_shape=jax.ShapeDtypeStruct((B, S, H * V_DIM), F32),
        scratch_shapes=[pltpu.VMEM((2, 2, ATT_TK, tq), F32)],
        compiler_params=_params("parallel", "parallel", "parallel"),
        name="mla_attention",
    )(qt, k, vt)


def _layer_norm(x, g, b):
    mu = jnp.mean(x, axis=-1, keepdims=True)
    xc = x - mu
    var = jnp.mean(xc * xc, axis=-1, keepdims=True)
    return xc * lax.rsqrt(var + EPS) * g + b


def _outproj_kernel(a_ref, hy_ref, x_ref, g1_ref, sc2_ref, sh2_ref, ag_ref, hg_ref, wo_ref,
                    lg_ref, lb_ref, wr_ref, x1_ref, u2_ref, aff_ref):
    half = a_ref.shape[2]
    an = _rms(a_ref[0], ag_ref[...]).astype(BF16)
    hn = _rms(hy_ref[0], hg_ref[...]).astype(BF16)
    mixed = (jnp.dot(an, wo_ref[:half, :], preferred_element_type=F32)
             + jnp.dot(hn, wo_ref[half:, :], preferred_element_type=F32))
    x1 = _layer_norm(DN_ALPHA * x_ref[0] + g1_ref[0] * mixed, lg_ref[...], lb_ref[...])
    x1_ref[0] = x1
    u2 = x1 * (1.0 + sc2_ref[0]) + sh2_ref[0]
    u2_ref[0] = u2.astype(BF16)
    logits = _dot3(u2, wr_ref[...])
    lane = lax.broadcasted_iota(jnp.int32, logits.shape, 1)
    logits = jnp.where(lane < N_EXPERTS, logits, -jnp.inf)
    e = jnp.exp(logits - jnp.max(logits, axis=-1, keepdims=True))
    aff = e / jnp.sum(e, axis=-1, keepdims=True)
    aff_ref[0] = aff.T[:N_EXPERTS, :]


def _outproj(a, hy, x, g1, sc2, sh2, ag, hg, wo, lg, lb, wr_pad):
    B, S, D = x.shape
    tm = TOK_TILE
    half = a.shape[2]
    full = lambda shape: pl.BlockSpec(shape, lambda b, i: (0,) * len(shape))
    tok = lambda w: pl.BlockSpec((1, tm, w), lambda b, i: (b, i, 0))
    per_b = pl.BlockSpec((1, 1, D), lambda b, i: (b, 0, 0))
    return pl.pallas_call(
        _outproj_kernel,
        grid=(B, S // tm),
        in_specs=[tok(half), tok(half), tok(D), per_b, per_b, per_b,
                  full((1, half)), full((1, half)), full((D, D)), full((1, D)), full((1, D)),
                  full((D, LANES))],
        out_specs=[tok(D), tok(D), pl.BlockSpec((1, N_EXPERTS, tm), lambda b, i: (b, 0, i))],
        out_shape=[
            jax.ShapeDtypeStruct((B, S, D), F32),
            jax.ShapeDtypeStruct((B, S, D), BF16),
            jax.ShapeDtypeStruct((B, N_EXPERTS, S), F32),
        ],
        compiler_params=_params("parallel", "parallel"),
        name="out_proj_ln1_router",
    )(a, hy, x, g1, sc2, sh2, ag, hg, wo, lg, lb, wr_pad)


def _select_kernel(aff_ref, ones_ref, utri_ref, ltri_ref, pos_ref, posall_ref, g_ref, *, cap):
    aff = aff_ref[0]
    ones = ones_ref[...]

    def rowsum(mf):
        return jnp.dot(mf, ones, preferred_element_type=F32)

    def expert_total(mf):
        x = mf.astype(F32).reshape(N_EXPERTS, aff.shape[0] // N_EXPERTS, LANES)
        tot = jnp.sum(jnp.sum(x, axis=1, keepdims=True), axis=2, keepdims=True)
        return jnp.broadcast_to(tot, x.shape).reshape(aff.shape)

    def prefix(mf):
        return (jnp.dot(mf, utri_ref[...], preferred_element_type=F32)
                + jnp.dot(ltri_ref[...], rowsum(mf).astype(BF16), preferred_element_type=F32))

    def ind(m):
        return jnp.where(m, 1.0, 0.0).astype(BF16)

    def search(i, thr_bits):
        cand = thr_bits | jnp.left_shift(jnp.int32(1), 30 - i)
        enough = expert_total(ind(aff >= lax.bitcast_convert_type(cand, F32))) >= cap
        return jnp.where(enough, cand, thr_bits)

    thr = lax.bitcast_convert_type(lax.fori_loop(0, 31, search, jnp.zeros(aff.shape, jnp.int32)), F32)
    gt = aff > thr
    eq = aff == thr
    room = cap - expert_total(ind(gt))
    take_eq = jnp.where(eq, jnp.where(prefix(ind(eq)) < room, 1.0, 0.0), 0.0)
    sel = jnp.where(gt, 1.0, take_eq)
    pos = prefix(sel.astype(BF16)).astype(jnp.int32)
    posall_ref[0] = pos
    pos_ref[0] = jnp.where(sel > 0.0, pos, -1)
    g_ref[0] = jnp.where(sel > 0.0, aff, 0.0)


def _select(aff2, cap):
    B, R, _ = aff2.shape
    rpe = R // N_EXPERTS
    li = jnp.arange(LANES, dtype=jnp.int32)
    ri = jnp.arange(R, dtype=jnp.int32)
    same = (ri[:, None] // rpe) == (ri[None, :] // rpe)
    ones = jnp.ones((LANES, LANES), BF16)
    utri = (li[:, None] < li[None, :]).astype(BF16)
    ltri = (same & (ri[None, :] < ri[:, None])).astype(BF16)
    blk = pl.BlockSpec((1, R, LANES), lambda b: (b, 0, 0))
    full = lambda arr: pl.BlockSpec(arr.shape, lambda b: (0, 0))
    return pl.pallas_call(
        functools.partial(_select_kernel, cap=cap),
        grid=(B,),
        in_specs=[blk, full(ones), full(utri), full(ltri)],
        out_specs=[blk, blk, blk],
        out_shape=[jax.ShapeDtypeStruct((B, R, LANES), jnp.int32), jax.ShapeDtypeStruct((B, R, LANES), jnp.int32),
                   jax.ShapeDtypeStruct((B, R, LANES), F32)],
        compiler_params=_params("parallel"),
        name="expert_choice_select",
    )(aff2, ones, utri, ltri)


GATHER_TOK = 1024
GATHER_WIN = 160


def _gather_kernel(offs_ref, u_ref, pos_ref, xe_ref, acc, *, cap, nblk):
    b, e = pl.program_id(0), pl.program_id(1)
    base = (b * N_EXPERTS + e) * (nblk + 1)
    rows = GATHER_TOK // LANES
    acc[...] = jnp.zeros(acc.shape, F32)
    slot = lax.broadcasted_iota(jnp.int32, (GATHER_WIN, LANES), 0)

    def block(j, carry):
        off, end = offs_ref[base + j], offs_ref[base + j + 1]
        w = (off // SUBLANES) * SUBLANES
        prow = pos_ref[0, pl.ds(pl.multiple_of(j * rows, rows), rows), :]
        ublk = u_ref[0, pl.ds(pl.multiple_of(j * GATHER_TOK, GATHER_TOK), GATHER_TOK), :]

        def window(i, c):
            s0 = pl.multiple_of(w + i * GATHER_WIN, SUBLANES)
            onehot = jnp.concatenate(
                [jnp.where(slot == prow[r:r + 1, :] - s0, 1.0, 0.0).astype(BF16) for r in range(rows)], axis=1)
            acc[pl.ds(s0, GATHER_WIN), :] += jnp.dot(onehot, ublk, preferred_element_type=F32)
            return c

        window(0, 0)
        lax.fori_loop(1, (end - w + GATHER_WIN - 1) // GATHER_WIN, window, 0)
        return carry

    lax.fori_loop(0, nblk, block, 0, unroll=2)
    xe_ref[0] = acc[:cap, :].astype(BF16)


def _gather(offs, u2, pos_sel, cap):
    B, S, D = u2.shape
    rpe = S // LANES
    nblk = S // GATHER_TOK
    return pl.pallas_call(
        functools.partial(_gather_kernel, cap=cap, nblk=nblk),
        grid_spec=pltpu.PrefetchScalarGridSpec(
            num_scalar_prefetch=1,
            grid=(B, N_EXPERTS),
            in_specs=[pl.BlockSpec((1, S, D), lambda b, e, o: (b, 0, 0)),
                      pl.BlockSpec((1, rpe, LANES), lambda b, e, o: (b, e, 0))],
            out_specs=pl.BlockSpec((1, cap, D), lambda b, e, o: (e, b, 0)),
            scratch_shapes=[pltpu.VMEM((cap + GATHER_WIN + SUBLANES, D), F32)],
        ),
        out_shape=jax.ShapeDtypeStruct((N_EXPERTS, B * cap, D), BF16),
        compiler_params=_params("parallel", "parallel"),
        name="expert_gather",
    )(offs, u2, pos_sel)


def _moe_kernel(x_ref, wg_ref, wu_ref, wd_ref, o_ref, wgb, wub, wdb):
    @pl.when(pl.program_id(1) == 0)
    def _():
        wgb[...] = wg_ref[0].astype(BF16)
        wub[...] = wu_ref[0].astype(BF16)
        wdb[...] = wd_ref[0].astype(BF16)

    x = x_ref[0]
    hg = jnp.dot(x, wgb[...], preferred_element_type=F32)
    hu = jnp.dot(x, wub[...], preferred_element_type=F32)
    h = (hg * jax.nn.sigmoid(hg) * hu).astype(BF16)
    o_ref[0] = jnp.dot(h, wdb[...], preferred_element_type=F32).astype(BF16)


def _moe_ffn(xe, w_gate, w_up, w_down):
    E, R, D = xe.shape
    Fh = w_gate.shape[2]
    tr = MOE_ROWS
    return pl.pallas_call(
        _moe_kernel,
        grid=(E, R // tr),
        in_specs=[
            pl.BlockSpec((1, tr, D), lambda e, r: (e, r, 0)),
            pl.BlockSpec((1, D, Fh), lambda e, r: (e, 0, 0)),
            pl.BlockSpec((1, D, Fh), lambda e, r: (e, 0, 0)),
            pl.BlockSpec((1, Fh, D), lambda e, r: (e, 0, 0)),
        ],
        out_specs=pl.BlockSpec((1, tr, D), lambda e, r: (e, r, 0)),
        out_shape=jax.ShapeDtypeStruct((E, R, D), BF16),
        scratch_shapes=[pltpu.VMEM((D, Fh), BF16), pltpu.VMEM((D, Fh), BF16), pltpu.VMEM((Fh, D), BF16)],
        compiler_params=_params("parallel", "arbitrary"),
        name="moe_swiglu",
    )(xe, w_gate, w_up, w_down)


CMB_TOK = 512
CMB_WIN = 128
BF16_ROWS = 16


def _combine_kernel(offs_ref, x1_ref, pos_ref, g_ref, g2_ref, lg_ref, lb_ref, ye_hbm, o_ref,
                    rhs, lhs, acc, sem, *, cap, nblk, nbatch):
    b, i = pl.program_id(0), pl.program_id(1)
    W = CMB_WIN
    total = ye_hbm.shape[0]
    step = b * nblk + i
    cur_slot = step % 2
    row_e = lax.broadcasted_iota(jnp.int32, (N_EXPERTS, 1), 0)
    slot_w = lax.broadcasted_iota(jnp.int32, (W, CMB_TOK), 0)

    def windows(bb, ii):
        out = []
        for e in range(N_EXPERTS):
            k = (bb * N_EXPERTS + e) * (nblk + 1) + ii
            off, end = offs_ref[k], offs_ref[k + 1]
            lo = (off // BF16_ROWS) * BF16_ROWS
            out.append(((e * nbatch + bb) * cap, lo, jnp.where(end > off, (end - lo + W - 1) // W, 0)))
        return out

    def fetch_starts(win, p):
        return [jnp.minimum(rb + lo + p * W, total - W) for rb, lo, _ in win]

    def window_copy(e, start, s):
        return pltpu.make_async_copy(ye_hbm.at[pl.ds(pl.multiple_of(start, BF16_ROWS), W), :],
                                     rhs.at[s, pl.ds(e * W, W), :], sem.at[s, e])

    def start_fetch(win, p, s):
        for e, start in enumerate(fetch_starts(win, p)):
            window_copy(e, start, s).start()

    def absorb(win, p, s, first):
        starts = fetch_starts(win, p)
        lo_col = jnp.zeros((N_EXPERTS, 1), jnp.int32)
        shift_col = jnp.zeros((N_EXPERTS, 1), jnp.int32)
        for e, (rb, lo0, _) in enumerate(win):
            lo = lo0 + p * W
            lo_col = jnp.where(row_e == e, lo, lo_col)
            shift_col = jnp.where(row_e == e, rb + lo - starts[e], shift_col)
        d = pos_ref[0] - lo_col
        rel = jnp.where((d >= 0) & (d < W), d + shift_col, -1)
        g = g_ref[0]
        for e in range(N_EXPERTS):
            lhs[e * W:(e + 1) * W, :] = jnp.where(slot_w == rel[e:e + 1, :], g[e:e + 1, :], 0.0).astype(BF16)
        for e in range(N_EXPERTS):
            window_copy(e, starts[e], s).wait()
        part = lax.dot_general(lhs[...], rhs[s], (((0,), (0,)), ((), ())), preferred_element_type=F32)
        acc[...] = part if first else acc[...] + part

    cur = windows(b, i)

    @pl.when(step == 0)
    def _():
        start_fetch(cur, 0, 0)

    @pl.when(step + 1 < nbatch * nblk)
    def _():
        wrap = i + 1 == nblk
        start_fetch(windows(jnp.where(wrap, b + 1, b), jnp.where(wrap, 0, i + 1)), 0, 1 - cur_slot)

    absorb(cur, 0, cur_slot, True)

    def extra_pass(p, carry):
        start_fetch(cur, p, cur_slot)
        absorb(cur, p, cur_slot, False)
        return carry

    npass = functools.reduce(jnp.maximum, [n for _, _, n in cur])
    lax.fori_loop(1, npass, extra_pass, 0)
    o_ref[0] = _layer_norm(DN_ALPHA * x1_ref[0] + g2_ref[0] * acc[...], lg_ref[...], lb_ref[...])


def _combine_ln2(offs, x1, pos_exp, g_exp, g2, lg, lb, ye_rows, cap):
    B, S, D = x1.shape
    T = CMB_TOK
    nblk = S // T
    tok = lambda w: pl.BlockSpec((1, T, w), lambda b, i, o: (b, i, 0))
    vec = pl.BlockSpec((1, D), lambda b, i, o: (0, 0))
    exp_major = pl.BlockSpec((1, N_EXPERTS, T), lambda b, i, o: (b, 0, i))
    return pl.pallas_call(
        functools.partial(_combine_kernel, cap=cap, nblk=nblk, nbatch=B),
        grid_spec=pltpu.PrefetchScalarGridSpec(
            num_scalar_prefetch=1,
            grid=(B, nblk),
            in_specs=[tok(D), exp_major, exp_major,
                      pl.BlockSpec((1, 1, D), lambda b, i, o: (b, 0, 0)), vec, vec,
                      pl.BlockSpec(memory_space=pl.ANY)],
            out_specs=tok(D),
            scratch_shapes=[pltpu.VMEM((2, N_EXPERTS * CMB_WIN, D), BF16),
                            pltpu.VMEM((N_EXPERTS * CMB_WIN, T), BF16),
                            pltpu.VMEM((T, D), F32),
                            pltpu.SemaphoreType.DMA((2, N_EXPERTS))],
        ),
        out_shape=jax.ShapeDtypeStruct((B, S, D), F32),
        compiler_params=_params("arbitrary", "arbitrary"),
        name="combine_residual_ln2",
    )(offs, x1, pos_exp, g_exp, g2, lg, lb, ye_rows)


DFT_R = 128
K1_HALF = 40
K1_VALID = DFT_R // 2 + 1
K1_GROUPS = (K1_HALF // SUBLANES, -(-(K1_VALID - K1_HALF) // SUBLANES))
K1_SPEC = K1_HALF + K1_GROUPS[1] * SUBLANES
HY_CT = 128


@functools.lru_cache(maxsize=None)
def _dft_tables_np(S):
    N = 2 * S
    na = S // DFT_R
    r = np.arange(2 * K1_HALF)
    k1 = np.arange(2)[:, None] * K1_HALF + r[None, :] % K1_HALF
    valid = (k1 < K1_VALID).astype(np.float64)
    n = DFT_R * np.arange(na)[None, :] + np.arange(DFT_R)[:, None]
    theta = ((n[None, :, None, :] * k1[:, None, :, None]) % N) * (2.0 * math.pi / N)
    is_im = (r >= K1_HALF)[None, None, :, None]
    g = np.where(is_im, -np.sin(theta), np.cos(theta)) * valid[:, None, :, None]
    cw = np.where((k1 == 0) | (k1 == DFT_R // 2), 1.0, 2.0) * valid / N
    gi = np.transpose(g * cw[:, None, :, None], (0, 1, 3, 2))
    ph = ((np.arange(DFT_R)[:, None] * np.arange(DFT_R)[None, :]) % DFT_R) * (2.0 * math.pi / DFT_R)
    C, Sn = np.cos(ph), np.sin(ph)
    f3f = np.block([[C, Sn], [-Sn, C]])
    f3i = np.block([[C, -Sn], [Sn, C]])

    def pair_lanes(t):
        h, nb, nr, nc = t.shape
        return t.reshape(h, nb // 2, 2, nr, nc).transpose(0, 1, 3, 2, 4).reshape(h, nb // 2, nr, 2 * nc)

    return tuple(np.ascontiguousarray(t, dtype=np.float32) for t in (pair_lanes(g), pair_lanes(gi), f3f, f3i))


def _dft_tables(S):
    return tuple(jnp.asarray(t).astype(BF16) for t in _dft_tables_np(S))


def _swap01(x):
    return jnp.swapaxes(x, 0, 1)


def _block_diag_lanes(top, bot):
    return jnp.concatenate([jnp.concatenate([top, jnp.zeros((top.shape[0], bot.shape[1]), top.dtype)], axis=1),
                            jnp.concatenate([jnp.zeros((bot.shape[0], top.shape[1]), bot.dtype), bot], axis=1)], axis=0)


def _dft_stage1(z_ref, gf_ref, hf, a_scr):
    ct = a_scr.shape[2]

    def body(g, carry):
        b0 = pl.multiple_of(g * SUBLANES, SUBLANES)
        zt = _swap01(z_ref[:, pl.ds(b0, SUBLANES), :]).astype(BF16)
        for p in range(SUBLANES // 2):
            a2 = jnp.dot(gf_ref[hf, g * (SUBLANES // 2) + p], _block_diag_lanes(zt[2 * p], zt[2 * p + 1]),
                         preferred_element_type=F32)
            a_scr[b0 + 2 * p] = a2[:, :ct]
            a_scr[b0 + 2 * p + 1] = a2[:, ct:]
        return carry

    lax.fori_loop(0, DFT_R // SUBLANES, body, 0, unroll=4)


def _dft_stage3_operands(a_scr, r0):
    mr = _swap01(a_scr[:, pl.ds(r0, SUBLANES), :])
    mi = _swap01(a_scr[:, pl.ds(K1_HALF + r0, SUBLANES), :])
    return [jnp.concatenate([jnp.concatenate([mr[j], mi[j]], axis=0),
                             jnp.concatenate([mr[j + 1], mi[j + 1]], axis=0)], axis=1).astype(BF16)
            for j in range(0, SUBLANES, 2)]


def _filter_mlp_kernel(feat_ref, w1_ref, b1_ref, fr_ref, w2_ref, b2_ref, w3_ref, b3_ref, dec_ref, o_ref):
    feats = feat_ref[...]
    fr = fr_ref[...]
    h = jnp.sin(fr * (jnp.dot(feats, w1_ref[...], precision=HIGHEST, preferred_element_type=F32) + b1_ref[...]))
    h = jnp.sin(fr * (jnp.dot(h, w2_ref[...], precision=HIGHEST, preferred_element_type=F32) + b2_ref[...]))
    h = _dot3(h, w3_ref[...]) + b3_ref[...]
    h = h * jnp.exp(-feats[:, 0:1] * jnp.abs(dec_ref[...]))
    row = lax.broadcasted_iota(jnp.int32, h.shape, 0) + pl.program_id(0) * h.shape[0]
    col = lax.broadcasted_iota(jnp.int32, h.shape, 1)
    o_ref[...] = jnp.where((row == 0) & (col >= h.shape[1] // 2), 0.0, h)


@functools.lru_cache(maxsize=None)
def _filter_features_np(L, nf):
    pos = np.arange(L, dtype=np.float64)
    t = pos[:, None] / (L - 1)
    bands = (nf - 1) // 2
    freqs = np.linspace(1e-4, bands - 1, bands)
    phase = (2.0 * math.pi / L) * pos[:, None] * freqs[None, :]
    feats = np.concatenate([t, np.cos(phase), -np.sin(phase), np.zeros((L, LANES - nf))], axis=-1)
    return np.ascontiguousarray(feats, dtype=np.float32)


def _filter_taps(L, w1, b1, freq, w2, b2, w3, b3, decay):
    nf = w1.shape[0]
    feats = jnp.asarray(_filter_features_np(L, nf))
    w1p = jnp.concatenate([w1, jnp.zeros((LANES - nf, w1.shape[1]), F32)], axis=0)
    fo = w1.shape[1]
    ncol = w3.shape[1]
    tl = 512
    full = lambda shape: pl.BlockSpec(shape, lambda i: (0,) * len(shape))
    return pl.pallas_call(
        _filter_mlp_kernel,
        grid=(L // tl,),
        in_specs=[pl.BlockSpec((tl, LANES), lambda i: (i, 0)), full((LANES, fo)), full((1, fo)), full((1, fo)),
                  full((fo, fo)), full((1, fo)), full((fo, ncol)), full((1, ncol)), full((1, ncol))],
        out_specs=pl.BlockSpec((tl, ncol), lambda i: (i, 0)),
        out_shape=jax.ShapeDtypeStruct((L, ncol), F32),
        compiler_params=_params("parallel"),
        name="hyena_filter_mlp",
    )(feats, w1p, b1[None, :], freq[None, :], w2, b2[None, :], w3, b3[None, :], decay.reshape(1, ncol))


def _filter_spectrum_kernel(hf_ref, hb_ref, gf_ref, f3f_ref, o_ref, a_scr, x_scr):
    ct = a_scr.shape[2]
    for t, src in enumerate((hf_ref, hb_ref)):
        for hf in range(2):
            _dft_stage1(src, gf_ref, hf, a_scr)

            def body(g, carry):
                r0 = pl.multiple_of(g * SUBLANES, SUBLANES)
                for p, m in enumerate(_dft_stage3_operands(a_scr, r0)):
                    x2 = jnp.dot(f3f_ref[...], m, preferred_element_type=F32)
                    for q in range(2):
                        k1 = hf * K1_HALF + r0 + 2 * p + q
                        x = x2[:, q * ct:(q + 1) * ct]
                        if t == 0:
                            x_scr[k1] = x
                        else:
                            xf = x_scr[k1]
                            o_ref[0, k1] = jnp.concatenate([xf[:DFT_R] + x[:DFT_R], xf[DFT_R:] - x[DFT_R:]],
                                                           axis=0).astype(BF16)
                return carry

            lax.fori_loop(0, K1_GROUPS[hf], body, 0, unroll=True)


def _filter_spectrum(taps3, gf, f3f):
    na, nb, ncol = taps3.shape
    C = ncol // 4
    nct = C // HY_CT
    ct = HY_CT
    return pl.pallas_call(
        _filter_spectrum_kernel,
        grid=(2, nct),
        in_specs=[
            pl.BlockSpec((na, nb, ct), lambda o, j: (0, 0, o * nct + j)),
            pl.BlockSpec((na, nb, ct), lambda o, j: (0, 0, 2 * nct + o * nct + j)),
            pl.BlockSpec(gf.shape, lambda o, j: (0, 0, 0, 0), pipeline_mode=pl.Buffered(1)),
            pl.BlockSpec(f3f.shape, lambda o, j: (0, 0), pipeline_mode=pl.Buffered(1)),
        ],
        out_specs=pl.BlockSpec((1, K1_SPEC, 2 * DFT_R, ct), lambda o, j: (o, 0, 0, j)),
        out_shape=jax.ShapeDtypeStruct((2, K1_SPEC, 2 * DFT_R, C), BF16),
        scratch_shapes=[pltpu.VMEM((DFT_R, 2 * K1_HALF, ct), F32), pltpu.VMEM((K1_SPEC, 2 * DFT_R, ct), F32)],
        compiler_params=_params("parallel", "parallel"),
        name="hyena_filter_spectrum",
    )(taps3, taps3, gf, f3f)


def _long_conv_kernel(z_ref, xg_ref, h_ref, bias_ref, gf_ref, f3f_ref, f3i_ref, gi_ref, o_ref, a_scr, b_scr):
    ct = a_scr.shape[2]
    bias = bias_ref[0]
    for hf in range(2):
        _dft_stage1(z_ref.at[0], gf_ref, hf, a_scr)
        ngrp = K1_GROUPS[hf]
        if ngrp * SUBLANES < K1_HALF:
            b_scr[ngrp * SUBLANES:] = jnp.zeros((K1_HALF - ngrp * SUBLANES,) + b_scr.shape[1:], F32)

        def spectral(g, carry):
            r0 = pl.multiple_of(g * SUBLANES, SUBLANES)
            for p, m in enumerate(_dft_stage3_operands(a_scr, r0)):
                x = jnp.dot(f3f_ref[...], m, preferred_element_type=F32)
                k1 = hf * K1_HALF + r0 + 2 * p
                h = jnp.concatenate([h_ref[0, k1], h_ref[0, k1 + 1]], axis=1).astype(F32)
                xr, xi, hr, hi = x[:DFT_R], x[DFT_R:], h[:DFT_R], h[DFT_R:]
                y = jnp.concatenate([xr * hr - xi * hi, xr * hi + xi * hr], axis=0).astype(BF16)
                bb = jnp.dot(f3i_ref[...], y, preferred_element_type=F32)
                b_scr[r0 + 2 * p] = bb[:, :ct]
                b_scr[r0 + 2 * p + 1] = bb[:, ct:]
            return carry

        lax.fori_loop(0, ngrp, spectral, 0, unroll=True)

        def inverse(g, carry):
            b0 = pl.multiple_of(g * SUBLANES, SUBLANES)
            br = _swap01(b_scr[:, pl.ds(b0, SUBLANES), :])
            bi = _swap01(b_scr[:, pl.ds(DFT_R + b0, SUBLANES), :])
            ys = []
            for p in range(SUBLANES // 2):
                bm = [jnp.concatenate([br[2 * p + q], bi[2 * p + q]], axis=0).astype(BF16) for q in range(2)]
                y2 = jnp.dot(gi_ref[hf, g * (SUBLANES // 2) + p], _block_diag_lanes(bm[0], bm[1]),
                             preferred_element_type=F32)
                ys += [y2[:, :ct], y2[:, ct:]]
            y = _swap01(jnp.stack(ys, axis=0))
            sl = (0, slice(None), pl.ds(b0, SUBLANES), slice(None))
            if hf == 0:
                o_ref[sl] = y
            else:
                o_ref[sl] = xg_ref[sl] * (o_ref[sl] + y + z_ref[sl] * bias)
            return carry

        lax.fori_loop(0, DFT_R // SUBLANES, inverse, 0, unroll=4)


def _long_conv(z_arr, z_off, xg_arr, xg_off, spec, order, bias3, gf, f3f, f3i, gi):
    B, na, nb, _ = z_arr.shape
    C = spec.shape[3]
    ct = HY_CT
    const = lambda arr: pl.BlockSpec(arr.shape, lambda j, b: (0,) * arr.ndim, pipeline_mode=pl.Buffered(1))
    return pl.pallas_call(
        _long_conv_kernel,
        grid=(C // ct, B),
        in_specs=[
            pl.BlockSpec((1, na, nb, ct), lambda j, b: (b, 0, 0, z_off + j)),
            pl.BlockSpec((1, na, nb, ct), lambda j, b: (b, 0, 0, xg_off + j)),
            pl.BlockSpec((1, K1_SPEC, 2 * DFT_R, ct), lambda j, b: (order, 0, 0, j), pipeline_mode=pl.Buffered(1)),
            pl.BlockSpec((1, 1, ct), lambda j, b: (order, 0, j)),
            const(gf), const(f3f), const(f3i), const(gi),
        ],
        out_specs=pl.BlockSpec((1, na, nb, ct), lambda j, b: (b, 0, 0, j)),
        out_shape=jax.ShapeDtypeStruct((B, na, nb, C), F32),
        scratch_shapes=[pltpu.VMEM((DFT_R, 2 * K1_HALF, ct), F32), pltpu.VMEM((K1_HALF, 2 * DFT_R, ct), F32)],
        compiler_params=_params("parallel", "parallel"),
        name="hyena_long_conv",
    )(z_arr, xg_arr, spec, bias3, gf, f3f, f3i, gi)


def _hyena_group(hy_conv, fw1, fb1, ffreq, fw2, fb2, fw3, fb3, decay, hyena_bias):
    B, S, _ = hy_conv.shape
    na = S // DFT_R
    gf, gi, f3f, f3i = _dft_tables(S)
    taps = _filter_taps(S, fw1, fb1, ffreq, fw2, fb2, fw3, fb3, decay)
    spec = _filter_spectrum(taps.reshape(na, DFT_R, -1), gf, f3f)
    u = hy_conv.reshape(B, na, DFT_R, -1)
    nct = HY_W // HY_CT
    bias3 = hyena_bias[:, None, :]
    z1 = _long_conv(u, 0, u, nct, spec, 0, bias3, gf, f3f, f3i, gi)
    return _long_conv(z1, 0, u, 2 * nct, spec, 1, bias3, gf, f3f, f3i, gi).reshape(B, S, HY_W)


def _rope_rot_cols(w):
    half = ROPE // 2
    return jnp.concatenate([-w[..., half:], w[..., :half]], axis=-1)


def _prep_weights(w_in, w_qb, w_kvb):
    D = w_in.shape[0]
    o = Q_RANK + KV_RANK
    w_kr = w_in[:, o:o + ROPE]
    z64 = jnp.zeros((D, NOPE), F32)
    z32 = jnp.zeros((D, LANES - NOPE - ROPE), F32)
    win_aug = jnp.concatenate([
        w_in[:, :o],
        z64, w_kr, z32,
        z64, _rope_rot_cols(w_kr), z32,
        w_in[:, o + ROPE:],
    ], axis=1).astype(BF16)
    wq = w_qb.reshape(Q_RANK, MLA_HEADS, NOPE + ROPE)
    zq = jnp.zeros((Q_RANK, MLA_HEADS, LANES - NOPE - ROPE), F32)
    wq_main = jnp.concatenate([wq, zq], axis=-1).reshape(Q_RANK, MLA_HEADS * LANES).astype(BF16)
    wq_rot = jnp.concatenate([jnp.zeros((Q_RANK, MLA_HEADS, NOPE), F32), _rope_rot_cols(wq[..., NOPE:]), zq],
                             axis=-1).reshape(Q_RANK, MLA_HEADS * LANES).astype(BF16)
    wkv = w_kvb.reshape(KV_RANK, MLA_HEADS, NOPE + V_DIM)
    wk = jnp.concatenate([wkv[..., :NOPE], jnp.zeros((KV_RANK, MLA_HEADS, LANES - NOPE), F32)],
                         axis=-1).reshape(KV_RANK, MLA_HEADS * LANES).astype(BF16)
    wv = wkv[..., NOPE:].reshape(KV_RANK, MLA_HEADS * V_DIM).astype(BF16)
    return win_aug, wq_main, wq_rot, wk, wv


def kernel(x, c, positions, w_ada, b_ada, w_in, q_norm_g, w_qb, kv_norm_g, w_kvb, conv_w, conv_b, filt_w1, filt_b1, filt_freq, filt_w2, filt_b2, filt_w3, filt_b3, hyena_decay, hyena_bias, attn_out_g, hyena_out_g, w_o, ln1_g, ln1_b, w_router, w_gate, w_up, w_down, ln2_g, ln2_b):
    B, S, D = x.shape
    l = 0
    c8 = jnp.zeros((SUBLANES, D), F32).at[:B].set(c)
    mod = _ada_mod(c8, w_ada[l], b_ada[l][None, :])[:B]
    sh1, sc1, g1, sh2, sc2, g2 = [m[:, None, :] for m in jnp.split(mod, 6, axis=-1)]

    win_aug, wq_main, wq_rot, wk, wv = _prep_weights(w_in[l], w_qb[l], w_kvb[l])
    cos_h, sin_h = _rope_tables(positions)
    pad = lambda n, v: jnp.full((B, n, S), v, F32)
    cos_t = jnp.concatenate([pad(NOPE, 1.0), cos_h, cos_h, pad(LANES - NOPE - ROPE, 1.0)], axis=1)
    sin_t = jnp.concatenate([pad(NOPE, 0.0), sin_h, sin_h, pad(LANES - NOPE - ROPE, 0.0)], axis=1)
    q, kt, v, hy_conv = _inproj(x, sc1, sh1, cos_t, sin_t, win_aug, q_norm_g[l][None, :], wq_main, wq_rot,
                                kv_norm_g[l][None, :], wk, wv, conv_w[l], conv_b[l])
    a = _attention(q, kt, v)

    hy = _hyena_group(hy_conv, filt_w1[l], filt_b1[l], filt_freq[l], filt_w2[l],
                      filt_b2[l], filt_w3[l], filt_b3[l], hyena_decay[l], hyena_bias[l])

    wr_pad = jnp.concatenate([w_router[l], jnp.zeros((D, LANES - N_EXPERTS), F32)], axis=1)
    x1, u2, aff_t = _outproj(a, hy, x, g1, sc2, sh2, attn_out_g[l][None, :], hyena_out_g[l][None, :],
                             w_o[l].astype(BF16), ln1_g[l][None, :], ln1_b[l][None, :], wr_pad)

    cap = EC_FACTOR * S // N_EXPERTS
    rpe = S // LANES
    aff2 = aff_t.reshape(B, N_EXPERTS * rpe, LANES)
    pos_sel, pos_all, gsel = _select(aff2, cap)
    row_start = pos_all.reshape(B, N_EXPERTS, rpe, LANES)[:, :, :, 0]

    def block_offsets(tokens):
        ends = jnp.full((B, N_EXPERTS, 1), cap, jnp.int32)
        return jnp.concatenate([row_start[:, :, ::tokens // LANES], ends], axis=-1).reshape(-1)

    xe = _gather(block_offsets(GATHER_TOK), u2, pos_sel, cap)
    ye = _moe_ffn(xe, w_gate[l], w_up[l], w_down[l])
    return _combine_ln2(block_offsets(CMB_TOK), x1, pos_sel.reshape(B, N_EXPERTS, S), gsel.reshape(B, N_EXPERTS, S), g2, ln2_g[l][None, :], ln2_b[l][None, :],
                        ye.reshape(N_EXPERTS * B * cap, D), cap)
```

```python
import functools
import math

import jax
import jax.numpy as jnp
import numpy as np
from jax import lax
from jax.experimental import pallas as pl
from jax.experimental.pallas import tpu as pltpu

F32 = jnp.float32
BF16 = jnp.bfloat16
HIGHEST = lax.Precision.HIGHEST

LANES = 128
SUBLANES = 8
VMEM_LIMIT = 56 * 1024 * 1024

D_MODEL = 1024
MLA_HEADS = 8
NOPE = 64
ROPE = 32
V_DIM = 64
Q_RANK = 256
KV_RANK = 128
HY_W = 512
HY_COLS = 3 * HY_W
N_EXPERTS = 16
EC_FACTOR = 2
ROPE_THETA = 10000.0
EPS = 1e-5
DN_ALPHA = 2.0 ** 0.25
LOG2E = 1.4426950408889634

TOK_TILE = 512
ATT_TQ = 512
ATT_TK = 512
MOE_ROWS = 512


def _params(*sem):
    return pltpu.CompilerParams(dimension_semantics=sem, vmem_limit_bytes=VMEM_LIMIT)


def _ada_kernel(c_ref, w_ref, b_ref, o_ref):
    c = c_ref[...]
    sc = c * jax.nn.sigmoid(c)
    o_ref[...] = jnp.dot(sc, w_ref[...], precision=HIGHEST, preferred_element_type=F32) + b_ref[...]


def _ada_mod(c8, w_ada, b_ada):
    n = w_ada.shape[1]
    tn = 1024
    return pl.pallas_call(
        _ada_kernel,
        grid=(n // tn,),
        in_specs=[
            pl.BlockSpec((SUBLANES, D_MODEL), lambda j: (0, 0)),
            pl.BlockSpec((D_MODEL, tn), lambda j: (0, j)),
            pl.BlockSpec((1, tn), lambda j: (0, j)),
        ],
        out_specs=pl.BlockSpec((SUBLANES, tn), lambda j: (0, j)),
        out_shape=jax.ShapeDtypeStruct((SUBLANES, n), F32),
        compiler_params=_params("arbitrary"),
        name="ada_mod",
    )(c8, w_ada, b_ada)


def _dot3(a, b):
    ah, bh = a.astype(BF16), b.astype(BF16)
    al = (a - ah.astype(F32)).astype(BF16)
    bl = (b - bh.astype(F32)).astype(BF16)
    dot = functools.partial(jnp.dot, preferred_element_type=F32)
    return dot(ah, bh) + (dot(ah, bl) + dot(al, bh))


def _rms(x, g):
    return x * lax.rsqrt(jnp.mean(x * x, axis=-1, keepdims=True) + EPS) * g


def _rope_kernel(pos_ref, freq_ref, cos_ref, sin_ref):
    ang = freq_ref[...] * pos_ref[0].astype(F32)
    cos_ref[0] = jnp.cos(ang)
    sin_ref[0] = jnp.sin(ang)


def _rope_tables(positions):
    B, S = positions.shape
    half = ROPE // 2
    inv_freq = (ROPE_THETA ** (-jnp.arange(half, dtype=F32) / half))[:, None]
    out = jax.ShapeDtypeStruct((B, half, S), F32)
    blk = pl.BlockSpec((1, half, S), lambda b: (b, 0, 0))
    return pl.pallas_call(
        _rope_kernel,
        grid=(B,),
        in_specs=[pl.BlockSpec((1, 1, S), lambda b: (b, 0, 0)), pl.BlockSpec((half, 1), lambda b: (0, 0))],
        out_specs=[blk, blk],
        out_shape=[out, out],
        compiler_params=_params("parallel"),
        name="rope_tables",
    )(positions[:, None, :], inv_freq)


def _inproj_kernel(x_ref, xp_ref, xn_ref, sc_ref, sh_ref, cos_ref, sin_ref, win_ref, qg_ref, wq_ref, wqr_ref,
                   kvg_ref, wk_ref, wv_ref, cw_ref, cb_ref, q_ref, k_ref, v_ref, hy_ref):
    def modulate(xv):
        return (xv * (1.0 + sc_ref[0]) + sh_ref[0]).astype(BF16)

    u = modulate(x_ref[0])
    cos = cos_ref[0].T
    sin = sin_ref[0].T

    def proj(lo, hi):
        return jnp.dot(u, win_ref[:, lo:hi], preferred_element_type=F32)

    cq = _rms(proj(0, Q_RANK), qg_ref[...]).astype(BF16)
    q_main = jnp.dot(cq, wq_ref[...], preferred_element_type=F32)
    q_rot = jnp.dot(cq, wqr_ref[...], preferred_element_type=F32)
    qscale = LOG2E * (NOPE + ROPE) ** -0.5
    cq_s = cos * qscale
    sq_s = sin * qscale
    for h in range(MLA_HEADS):
        sl = slice(h * LANES, (h + 1) * LANES)
        q_ref[0, h] = (q_main[:, sl] * cq_s + q_rot[:, sl] * sq_s).T.astype(BF16)

    o = Q_RANK
    ckv = _rms(proj(o, o + KV_RANK), kvg_ref[...]).astype(BF16)
    kr = proj(o + KV_RANK, o + 2 * KV_RANK) * cos + proj(o + 2 * KV_RANK, o + 3 * KV_RANK) * sin
    kn = jnp.dot(ckv, wk_ref[...], preferred_element_type=F32)
    for h in range(MLA_HEADS):
        k_ref[0, h] = (kn[:, h * LANES:(h + 1) * LANES] + kr).astype(BF16)
    v_ref[0] = jnp.dot(ckv, wv_ref[...], preferred_element_type=F32).T.astype(BF16)

    o = Q_RANK + 3 * KV_RANK
    hy = proj(o, o + HY_COLS)
    tm = hy.shape[0]
    i, last = pl.program_id(1), pl.num_programs(1) - 1
    edge = jnp.dot(modulate(jnp.concatenate([xp_ref[0], xn_ref[0]], axis=0)), win_ref[:, o:o + HY_COLS],
                   preferred_element_type=F32)
    prev = jnp.where(i == 0, 0.0, edge[SUBLANES - 1:SUBLANES, :])
    nxt = jnp.where(i == last, 0.0, edge[SUBLANES:SUBLANES + 1, :])
    row = lax.broadcasted_iota(jnp.int32, hy.shape, 0)
    up = jnp.where(row == 0, prev, pltpu.roll(hy, 1, 0))
    dn = jnp.where(row == tm - 1, nxt, pltpu.roll(hy, tm - 1, 0))
    hy_ref[0] = cb_ref[...] + cw_ref[0:1, :] * up + cw_ref[1:2, :] * hy + cw_ref[2:3, :] * dn


def _inproj(x, sc1, sh1, cos_t, sin_t, win_aug, qg, wq, wqr, kvg, wk, wv, conv_w, conv_b):
    B, S, D = x.shape
    tm = TOK_TILE
    ncol = win_aug.shape[1]
    g8 = tm // SUBLANES
    n8 = S // SUBLANES
    full = lambda shape: pl.BlockSpec(shape, lambda b, i: (0,) * len(shape))
    return pl.pallas_call(
        _inproj_kernel,
        grid=(B, S // tm),
        in_specs=[
            pl.BlockSpec((1, tm, D), lambda b, i: (b, i, 0)),
            pl.BlockSpec((1, SUBLANES, D), lambda b, i: (b, jnp.maximum(i * g8 - 1, 0), 0)),
            pl.BlockSpec((1, SUBLANES, D), lambda b, i: (b, jnp.minimum((i + 1) * g8, n8 - 1), 0)),
            pl.BlockSpec((1, 1, D), lambda b, i: (b, 0, 0)),
            pl.BlockSpec((1, 1, D), lambda b, i: (b, 0, 0)),
            pl.BlockSpec((1, LANES, tm), lambda b, i: (b, 0, i)),
            pl.BlockSpec((1, LANES, tm), lambda b, i: (b, 0, i)),
            full((D, ncol)),
            full((1, Q_RANK)),
            full((Q_RANK, MLA_HEADS * LANES)),
            full((Q_RANK, MLA_HEADS * LANES)),
            full((1, KV_RANK)),
            full((KV_RANK, MLA_HEADS * LANES)),
            full((KV_RANK, MLA_HEADS * V_DIM)),
            full(conv_w.shape),
            full((1, HY_COLS)),
        ],
        out_specs=[
            pl.BlockSpec((1, MLA_HEADS, LANES, tm), lambda b, i: (b, 0, 0, i)),
            pl.BlockSpec((1, MLA_HEADS, tm, LANES), lambda b, i: (b, 0, i, 0)),
            pl.BlockSpec((1, MLA_HEADS * V_DIM, tm), lambda b, i: (b, 0, i)),
            pl.BlockSpec((1, tm, HY_COLS), lambda b, i: (b, i, 0)),
        ],
        out_shape=[
            jax.ShapeDtypeStruct((B, MLA_HEADS, LANES, S), BF16),
            jax.ShapeDtypeStruct((B, MLA_HEADS, S, LANES), BF16),
            jax.ShapeDtypeStruct((B, MLA_HEADS * V_DIM, S), BF16),
            jax.ShapeDtypeStruct((B, S, HY_COLS), F32),
        ],
        compiler_params=_params("parallel", "parallel"),
        name="in_proj",
    )(x, x, x, sc1, sh1, cos_t, sin_t, win_aug, qg, wq, wqr, kvg, wk, wv, conv_w, conv_b[None, :])


ATT_AUG = 16


def _attn_kernel(qt_ref, k_ref, vt_ref, o_ref, s_scr):
    S = k_ref.shape[2]
    tq = ATT_TQ
    nk = S // ATT_TK
    nq = S // tq
    ones_rows = jnp.where(lax.broadcasted_iota(jnp.int32, (ATT_AUG, ATT_TK), 0) == 0, 1.0, 0.0).astype(BF16)

    def scores(qi, c, slot):
        k0 = pl.multiple_of(c * ATT_TK, ATT_TK)
        q0 = pl.multiple_of(qi * tq, tq)
        for h in range(2):
            s_scr[slot, h] = jnp.dot(k_ref[0, h, pl.ds(k0, ATT_TK), :], qt_ref[0, h, :, pl.ds(q0, tq)],
                                     preferred_element_type=F32)

    def absorb(c, slot, carry):
        k0 = pl.multiple_of(c * ATT_TK, ATT_TK)
        out = []
        for h in range(2):
            m_prev, acc = carry[h]
            s = s_scr[slot, h]
            m_new = jnp.maximum(m_prev, jnp.max(s, axis=0, keepdims=True))
            alpha = jnp.exp2(m_prev - m_new)
            p = jnp.exp2(s - m_new).astype(BF16)
            v_aug = jnp.concatenate([vt_ref[0, h * V_DIM:(h + 1) * V_DIM, pl.ds(k0, ATT_TK)], ones_rows], axis=0)
            out.append((m_new, alpha * acc + jnp.dot(v_aug, p, preferred_element_type=F32)))
        return tuple(out)

    def query_tile(qi, _):
        def pair(i, carry):
            c0 = 2 * i
            scores(qi, c0 + 1, 1)
            carry = absorb(c0, 0, carry)
            wrap = c0 + 2 >= nk
            scores(jnp.where(wrap, jnp.minimum(qi + 1, nq - 1), qi), jnp.where(wrap, 0, c0 + 2), 0)
            return absorb(c0 + 1, 1, carry)

        init = tuple((jnp.full((1, tq), -jnp.inf, F32), jnp.zeros((V_DIM + ATT_AUG, tq), F32)) for _ in range(2))
        fin = lax.fori_loop(0, nk // 2, pair, init, unroll=4)
        o = jnp.concatenate([acc[:V_DIM] * (1.0 / acc[V_DIM:V_DIM + 1]) for (_, acc) in fin], axis=0)
        o_ref[0, pl.ds(pl.multiple_of(qi * tq, tq), tq), :] = o.T
        return 0

    scores(0, 0, 0)
    lax.fori_loop(0, nq, query_tile, 0)


def _attention(qt, k, vt):
    B, H, _, S = qt.shape
    return pl.pallas_call(
        _attn_kernel,
        grid=(B, H // 2),
        in_specs=[
            pl.BlockSpec((1, 2, LANES, S), lambda b, hp: (b, hp, 0, 0)),
            pl.BlockSpec((1, 2, S, LANES), lambda b, hp: (b, hp, 0, 0)),
            pl.BlockSpec((1, 2 * V_DIM, S), lambda b, hp: (b, hp, 0)),
        ],
        out_specs=pl.BlockSpec((1, S, 2 * V_DIM), lambda b, hp: (b, 0, hp)),
        out_shape=jax.ShapeDtypeStruct((B, S, H * V_DIM), F32),
        scratch_shapes=[pltpu.VMEM((2, 2, ATT_TK, ATT_TQ), F32)],
        compiler_params=_params("parallel", "parallel"),
        name="mla_attention",
    )(qt, k, vt)


def _layer_norm(x, g, b):
    mu = jnp.mean(x, axis=-1, keepdims=True)
    xc = x - mu
    var = jnp.mean(xc * xc, axis=-1, keepdims=True)
    return xc * lax.rsqrt(var + EPS) * g + b


def _outproj_kernel(a_ref, hy_ref, x_ref, g1_ref, sc2_ref, sh2_ref, ag_ref, hg_ref, wo_ref,
                    lg_ref, lb_ref, wr_ref, x1_ref, u2_ref, aff_ref):
    half = a_ref.shape[2]
    an = _rms(a_ref[0], ag_ref[...]).astype(BF16)
    hn = _rms(hy_ref[0], hg_ref[...]).astype(BF16)
    mixed = (jnp.dot(an, wo_ref[:half, :], preferred_element_type=F32)
             + jnp.dot(hn, wo_ref[half:, :], preferred_element_type=F32))
    x1 = _layer_norm(DN_ALPHA * x_ref[0] + g1_ref[0] * mixed, lg_ref[...], lb_ref[...])
    x1_ref[0] = x1
    u2 = x1 * (1.0 + sc2_ref[0]) + sh2_ref[0]
    u2_ref[0] = u2.astype(BF16)
    logits = _dot3(u2, wr_ref[...])
    lane = lax.broadcasted_iota(jnp.int32, logits.shape, 1)
    logits = jnp.where(lane < N_EXPERTS, logits, -jnp.inf)
    e = jnp.exp(logits - jnp.max(logits, axis=-1, keepdims=True))
    aff = e / jnp.sum(e, axis=-1, keepdims=True)
    aff_ref[0] = aff.T[:N_EXPERTS, :]


def _outproj(a, hy, x, g1, sc2, sh2, ag, hg, wo, lg, lb, wr_pad):
    B, S, D = x.shape
    tm = TOK_TILE
    half = a.shape[2]
    full = lambda shape: pl.BlockSpec(shape, lambda b, i: (0,) * len(shape))
    tok = lambda w: pl.BlockSpec((1, tm, w), lambda b, i: (b, i, 0))
    per_b = pl.BlockSpec((1, 1, D), lambda b, i: (b, 0, 0))
    return pl.pallas_call(
        _outproj_kernel,
        grid=(B, S // tm),
        in_specs=[tok(half), tok(half), tok(D), per_b, per_b, per_b,
                  full((1, half)), full((1, half)), full((D, D)), full((1, D)), full((1, D)),
                  full((D, LANES))],
        out_specs=[tok(D), tok(D), pl.BlockSpec((1, N_EXPERTS, tm), lambda b, i: (b, 0, i))],
        out_shape=[
            jax.ShapeDtypeStruct((B, S, D), F32),
            jax.ShapeDtypeStruct((B, S, D), BF16),
            jax.ShapeDtypeStruct((B, N_EXPERTS, S), F32),
        ],
        compiler_params=_params("parallel", "parallel"),
        name="out_proj_ln1_router",
    )(a, hy, x, g1, sc2, sh2, ag, hg, wo, lg, lb, wr_pad)


def _select_kernel(aff_ref, ones_ref, utri_ref, ltri_ref, pos_ref, posall_ref, g_ref, *, cap):
    aff = aff_ref[0]
    ones = ones_ref[...]

    def rowsum(mf):
        return jnp.dot(mf, ones, preferred_element_type=F32)

    def expert_total(mf):
        x = mf.astype(F32).reshape(N_EXPERTS, aff.shape[0] // N_EXPERTS, LANES)
        tot = jnp.sum(jnp.sum(x, axis=1, keepdims=True), axis=2, keepdims=True)
        return jnp.broadcast_to(tot, x.shape).reshape(aff.shape)

    def prefix(mf):
        return (jnp.dot(mf, utri_ref[...], preferred_element_type=F32)
                + jnp.dot(ltri_ref[...], rowsum(mf).astype(BF16), preferred_element_type=F32))

    def ind(m):
        return jnp.where(m, 1.0, 0.0).astype(BF16)

    def search(i, thr_bits):
        cand = thr_bits | jnp.left_shift(jnp.int32(1), 30 - i)
        enough = expert_total(ind(aff >= lax.bitcast_convert_type(cand, F32))) >= cap
        return jnp.where(enough, cand, thr_bits)

    thr = lax.bitcast_convert_type(lax.fori_loop(0, 31, search, jnp.zeros(aff.shape, jnp.int32)), F32)
    gt = aff > thr
    eq = aff == thr
    room = cap - expert_total(ind(gt))
    take_eq = jnp.where(eq, jnp.where(prefix(ind(eq)) < room, 1.0, 0.0), 0.0)
    sel = jnp.where(gt, 1.0, take_eq)
    pos = prefix(sel.astype(BF16)).astype(jnp.int32)
    posall_ref[0] = pos
    pos_ref[0] = jnp.where(sel > 0.0, pos, -1)
    g_ref[0] = jnp.where(sel > 0.0, aff, 0.0)


def _select(aff2, cap):
    B, R, _ = aff2.shape
    rpe = R // N_EXPERTS
    li = jnp.arange(LANES, dtype=jnp.int32)
    ri = jnp.arange(R, dtype=jnp.int32)
    same = (ri[:, None] // rpe) == (ri[None, :] // rpe)
    ones = jnp.ones((LANES, LANES), BF16)
    utri = (li[:, None] < li[None, :]).astype(BF16)
    ltri = (same & (ri[None, :] < ri[:, None])).astype(BF16)
    blk = pl.BlockSpec((1, R, LANES), lambda b: (b, 0, 0))
    full = lambda arr: pl.BlockSpec(arr.shape, lambda b: (0, 0))
    return pl.pallas_call(
        functools.partial(_select_kernel, cap=cap),
        grid=(B,),
        in_specs=[blk, full(ones), full(utri), full(ltri)],
        out_specs=[blk, blk, blk],
        out_shape=[jax.ShapeDtypeStruct((B, R, LANES), jnp.int32), jax.ShapeDtypeStruct((B, R, LANES), jnp.int32),
                   jax.ShapeDtypeStruct((B, R, LANES), F32)],
        compiler_params=_params("parallel"),
        name="expert_choice_select",
    )(aff2, ones, utri, ltri)


GATHER_TOK = 1024
GATHER_WIN = 160


def _gather_kernel(offs_ref, u_ref, pos_ref, xe_ref, acc, *, cap, nblk):
    b, e = pl.program_id(0), pl.program_id(1)
    base = (b * N_EXPERTS + e) * (nblk + 1)
    rows = GATHER_TOK // LANES
    acc[...] = jnp.zeros(acc.shape, F32)
    slot = lax.broadcasted_iota(jnp.int32, (GATHER_WIN, LANES), 0)

    def block(j, carry):
        off, end = offs_ref[base + j], offs_ref[base + j + 1]
        w = (off // SUBLANES) * SUBLANES
        prow = pos_ref[0, pl.ds(pl.multiple_of(j * rows, rows), rows), :]
        ublk = u_ref[0, pl.ds(pl.multiple_of(j * GATHER_TOK, GATHER_TOK), GATHER_TOK), :]

        def window(i, c):
            s0 = pl.multiple_of(w + i * GATHER_WIN, SUBLANES)
            onehot = jnp.concatenate(
                [jnp.where(slot == prow[r:r + 1, :] - s0, 1.0, 0.0).astype(BF16) for r in range(rows)], axis=1)
            acc[pl.ds(s0, GATHER_WIN), :] += jnp.dot(onehot, ublk, preferred_element_type=F32)
            return c

        window(0, 0)
        lax.fori_loop(1, (end - w + GATHER_WIN - 1) // GATHER_WIN, window, 0)
        return carry

    lax.fori_loop(0, nblk, block, 0, unroll=2)
    xe_ref[0] = acc[:cap, :].astype(BF16)


def _gather(offs, u2, pos_sel, cap):
    B, S, D = u2.shape
    rpe = S // LANES
    nblk = S // GATHER_TOK
    return pl.pallas_call(
        functools.partial(_gather_kernel, cap=cap, nblk=nblk),
        grid_spec=pltpu.PrefetchScalarGridSpec(
            num_scalar_prefetch=1,
            grid=(B, N_EXPERTS),
            in_specs=[pl.BlockSpec((1, S, D), lambda b, e, o: (b, 0, 0)),
                      pl.BlockSpec((1, rpe, LANES), lambda b, e, o: (b, e, 0))],
            out_specs=pl.BlockSpec((1, cap, D), lambda b, e, o: (e, b, 0)),
            scratch_shapes=[pltpu.VMEM((cap + GATHER_WIN + SUBLANES, D), F32)],
        ),
        out_shape=jax.ShapeDtypeStruct((N_EXPERTS, B * cap, D), BF16),
        compiler_params=_params("parallel", "parallel"),
        name="expert_gather",
    )(offs, u2, pos_sel)


def _moe_kernel(x_ref, wg_ref, wu_ref, wd_ref, o_ref, wgb, wub, wdb):
    @pl.when(pl.program_id(1) == 0)
    def _():
        wgb[...] = wg_ref[0].astype(BF16)
        wub[...] = wu_ref[0].astype(BF16)
        wdb[...] = wd_ref[0].astype(BF16)

    x = x_ref[0]
    hg = jnp.dot(x, wgb[...], preferred_element_type=F32)
    hu = jnp.dot(x, wub[...], preferred_element_type=F32)
    h = (hg * jax.nn.sigmoid(hg) * hu).astype(BF16)
    o_ref[0] = jnp.dot(h, wdb[...], preferred_element_type=F32).astype(BF16)


def _moe_ffn(xe, w_gate, w_up, w_down):
    E, R, D = xe.shape
    Fh = w_gate.shape[2]
    tr = MOE_ROWS
    return pl.pallas_call(
        _moe_kernel,
        grid=(E, R // tr),
        in_specs=[
            pl.BlockSpec((1, tr, D), lambda e, r: (e, r, 0)),
            pl.BlockSpec((1, D, Fh), lambda e, r: (e, 0, 0)),
            pl.BlockSpec((1, D, Fh), lambda e, r: (e, 0, 0)),
            pl.BlockSpec((1, Fh, D), lambda e, r: (e, 0, 0)),
        ],
        out_specs=pl.BlockSpec((1, tr, D), lambda e, r: (e, r, 0)),
        out_shape=jax.ShapeDtypeStruct((E, R, D), BF16),
        scratch_shapes=[pltpu.VMEM((D, Fh), BF16), pltpu.VMEM((D, Fh), BF16), pltpu.VMEM((Fh, D), BF16)],
        compiler_params=_params("parallel", "arbitrary"),
        name="moe_swiglu",
    )(xe, w_gate, w_up, w_down)


CMB_TOK = 512
CMB_WIN = 128
BF16_ROWS = 16


def _combine_kernel(offs_ref, x1_ref, pos_ref, g_ref, g2_ref, lg_ref, lb_ref, ye_hbm, o_ref,
                    rhs, lhs, acc, sem, *, cap, nblk, nbatch):
    b, i = pl.program_id(0), pl.program_id(1)
    W = CMB_WIN
    total = ye_hbm.shape[0]
    step = b * nblk + i
    cur_slot = step % 2
    row_e = lax.broadcasted_iota(jnp.int32, (N_EXPERTS, 1), 0)
    slot_w = lax.broadcasted_iota(jnp.int32, (W, CMB_TOK), 0)

    def windows(bb, ii):
        out = []
        for e in range(N_EXPERTS):
            k = (bb * N_EXPERTS + e) * (nblk + 1) + ii
            off, end = offs_ref[k], offs_ref[k + 1]
            lo = (off // BF16_ROWS) * BF16_ROWS
            out.append(((e * nbatch + bb) * cap, lo, jnp.where(end > off, (end - lo + W - 1) // W, 0)))
        return out

    def fetch_starts(win, p):
        return [jnp.minimum(rb + lo + p * W, total - W) for rb, lo, _ in win]

    def window_copy(e, start, s):
        return pltpu.make_async_copy(ye_hbm.at[pl.ds(pl.multiple_of(start, BF16_ROWS), W), :],
                                     rhs.at[s, pl.ds(e * W, W), :], sem.at[s, e])

    def start_fetch(win, p, s):
        for e, start in enumerate(fetch_starts(win, p)):
            window_copy(e, start, s).start()

    def absorb(win, p, s, first):
        starts = fetch_starts(win, p)
        lo_col = jnp.zeros((N_EXPERTS, 1), jnp.int32)
        shift_col = jnp.zeros((N_EXPERTS, 1), jnp.int32)
        for e, (rb, lo0, _) in enumerate(win):
            lo = lo0 + p * W
            lo_col = jnp.where(row_e == e, lo, lo_col)
            shift_col = jnp.where(row_e == e, rb + lo - starts[e], shift_col)
        d = pos_ref[0] - lo_col
        rel = jnp.where((d >= 0) & (d < W), d + shift_col, -1)
        g = g_ref[0]
        for e in range(N_EXPERTS):
            lhs[e * W:(e + 1) * W, :] = jnp.where(slot_w == rel[e:e + 1, :], g[e:e + 1, :], 0.0).astype(BF16)
        for e in range(N_EXPERTS):
            window_copy(e, starts[e], s).wait()
        part = lax.dot_general(lhs[...], rhs[s], (((0,), (0,)), ((), ())), preferred_element_type=F32)
        acc[...] = part if first else acc[...] + part

    cur = windows(b, i)

    @pl.when(step == 0)
    def _():
        start_fetch(cur, 0, 0)

    @pl.when(step + 1 < nbatch * nblk)
    def _():
        wrap = i + 1 == nblk
        start_fetch(windows(jnp.where(wrap, b + 1, b), jnp.where(wrap, 0, i + 1)), 0, 1 - cur_slot)

    absorb(cur, 0, cur_slot, True)

    def extra_pass(p, carry):
        start_fetch(cur, p, cur_slot)
        absorb(cur, p, cur_slot, False)
        return carry

    npass = functools.reduce(jnp.maximum, [n for _, _, n in cur])
    lax.fori_loop(1, npass, extra_pass, 0)
    o_ref[0] = _layer_norm(DN_ALPHA * x1_ref[0] + g2_ref[0] * acc[...], lg_ref[...], lb_ref[...])


def _combine_ln2(offs, x1, pos_exp, g_exp, g2, lg, lb, ye_rows, cap):
    B, S, D = x1.shape
    T = CMB_TOK
    nblk = S // T
    tok = lambda w: pl.BlockSpec((1, T, w), lambda b, i, o: (b, i, 0))
    vec = pl.BlockSpec((1, D), lambda b, i, o: (0, 0))
    exp_major = pl.BlockSpec((1, N_EXPERTS, T), lambda b, i, o: (b, 0, i))
    return pl.pallas_call(
        functools.partial(_combine_kernel, cap=cap, nblk=nblk, nbatch=B),
        grid_spec=pltpu.PrefetchScalarGridSpec(
            num_scalar_prefetch=1,
            grid=(B, nblk),
            in_specs=[tok(D), exp_major, exp_major,
                      pl.BlockSpec((1, 1, D), lambda b, i, o: (b, 0, 0)), vec, vec,
                      pl.BlockSpec(memory_space=pl.ANY)],
            out_specs=tok(D),
            scratch_shapes=[pltpu.VMEM((2, N_EXPERTS * CMB_WIN, D), BF16),
                            pltpu.VMEM((N_EXPERTS * CMB_WIN, T), BF16),
                            pltpu.VMEM((T, D), F32),
                            pltpu.SemaphoreType.DMA((2, N_EXPERTS))],
        ),
        out_shape=jax.ShapeDtypeStruct((B, S, D), F32),
        compiler_params=_params("arbitrary", "arbitrary"),
        name="combine_residual_ln2",
    )(offs, x1, pos_exp, g_exp, g2, lg, lb, ye_rows)


DFT_R = 128
K1_HALF = 40
K1_VALID = DFT_R // 2 + 1
K1_GROUPS = (K1_HALF // SUBLANES, -(-(K1_VALID - K1_HALF) // SUBLANES))
K1_SPEC = K1_HALF + K1_GROUPS[1] * SUBLANES
HY_CT = 128


@functools.lru_cache(maxsize=None)
def _dft_tables_np(S):
    N = 2 * S
    na = S // DFT_R
    r = np.arange(2 * K1_HALF)
    k1 = np.arange(2)[:, None] * K1_HALF + r[None, :] % K1_HALF
    valid = (k1 < K1_VALID).astype(np.float64)
    n = DFT_R * np.arange(na)[None, :] + np.arange(DFT_R)[:, None]
    theta = ((n[None, :, None, :] * k1[:, None, :, None]) % N) * (2.0 * math.pi / N)
    is_im = (r >= K1_HALF)[None, None, :, None]
    g = np.where(is_im, -np.sin(theta), np.cos(theta)) * valid[:, None, :, None]
    cw = np.where((k1 == 0) | (k1 == DFT_R // 2), 1.0, 2.0) * valid / N
    gi = np.transpose(g * cw[:, None, :, None], (0, 1, 3, 2))
    ph = ((np.arange(DFT_R)[:, None] * np.arange(DFT_R)[None, :]) % DFT_R) * (2.0 * math.pi / DFT_R)
    C, Sn = np.cos(ph), np.sin(ph)
    f3f = np.block([[C, Sn], [-Sn, C]])
    f3i = np.block([[C, -Sn], [Sn, C]])

    def pair_lanes(t):
        h, nb, nr, nc = t.shape
        return t.reshape(h, nb // 2, 2, nr, nc).transpose(0, 1, 3, 2, 4).reshape(h, nb // 2, nr, 2 * nc)

    return tuple(np.ascontiguousarray(t, dtype=np.float32) for t in (pair_lanes(g), pair_lanes(gi), f3f, f3i))


def _dft_tables(S):
    return tuple(jnp.asarray(t).astype(BF16) for t in _dft_tables_np(S))


def _swap01(x):
    return jnp.swapaxes(x, 0, 1)


def _block_diag_lanes(top, bot):
    return jnp.concatenate([jnp.concatenate([top, jnp.zeros((top.shape[0], bot.shape[1]), top.dtype)], axis=1),
                            jnp.concatenate([jnp.zeros((bot.shape[0], top.shape[1]), bot.dtype), bot], axis=1)], axis=0)


def _dft_stage1(z_ref, gf_ref, hf, a_scr):
    ct = a_scr.shape[2]

    def body(g, carry):
        b0 = pl.multiple_of(g * SUBLANES, SUBLANES)
        zt = _swap01(z_ref[:, pl.ds(b0, SUBLANES), :]).astype(BF16)
        for p in range(SUBLANES // 2):
            a2 = jnp.dot(gf_ref[hf, g * (SUBLANES // 2) + p], _block_diag_lanes(zt[2 * p], zt[2 * p + 1]),
                         preferred_element_type=F32)
            a_scr[b0 + 2 * p] = a2[:, :ct]
            a_scr[b0 + 2 * p + 1] = a2[:, ct:]
        return carry

    lax.fori_loop(0, DFT_R // SUBLANES, body, 0, unroll=4)


def _dft_stage3_operands(a_scr, r0):
    mr = _swap01(a_scr[:, pl.ds(r0, SUBLANES), :])
    mi = _swap01(a_scr[:, pl.ds(K1_HALF + r0, SUBLANES), :])
    return [jnp.concatenate([jnp.concatenate([mr[j], mi[j]], axis=0),
                             jnp.concatenate([mr[j + 1], mi[j + 1]], axis=0)], axis=1).astype(BF16)
            for j in range(0, SUBLANES, 2)]


def _filter_mlp_kernel(feat_ref, w1_ref, b1_ref, fr_ref, w2_ref, b2_ref, w3_ref, b3_ref, dec_ref, o_ref):
    feats = feat_ref[...]
    fr = fr_ref[...]
    h = jnp.sin(fr * (jnp.dot(feats, w1_ref[...], precision=HIGHEST, preferred_element_type=F32) + b1_ref[...]))
    h = jnp.sin(fr * (jnp.dot(h, w2_ref[...], precision=HIGHEST, preferred_element_type=F32) + b2_ref[...]))
    h = _dot3(h, w3_ref[...]) + b3_ref[...]
    h = h * jnp.exp(-feats[:, 0:1] * jnp.abs(dec_ref[...]))
    row = lax.broadcasted_iota(jnp.int32, h.shape, 0) + pl.program_id(0) * h.shape[0]
    col = lax.broadcasted_iota(jnp.int32, h.shape, 1)
    o_ref[...] = jnp.where((row == 0) & (col >= h.shape[1] // 2), 0.0, h)


@functools.lru_cache(maxsize=None)
def _filter_features_np(L, nf):
    pos = np.arange(L, dtype=np.float64)
    t = pos[:, None] / (L - 1)
    bands = (nf - 1) // 2
    freqs = np.linspace(1e-4, bands - 1, bands)
    phase = (2.0 * math.pi / L) * pos[:, None] * freqs[None, :]
    feats = np.concatenate([t, np.cos(phase), -np.sin(phase), np.zeros((L, LANES - nf))], axis=-1)
    return np.ascontiguousarray(feats, dtype=np.float32)


def _filter_taps(L, w1, b1, freq, w2, b2, w3, b3, decay):
    nf = w1.shape[0]
    feats = jnp.asarray(_filter_features_np(L, nf))
    w1p = jnp.concatenate([w1, jnp.zeros((LANES - nf, w1.shape[1]), F32)], axis=0)
    fo = w1.shape[1]
    ncol = w3.shape[1]
    tl = 512
    full = lambda shape: pl.BlockSpec(shape, lambda i: (0,) * len(shape))
    return pl.pallas_call(
        _filter_mlp_kernel,
        grid=(L // tl,),
        in_specs=[pl.BlockSpec((tl, LANES), lambda i: (i, 0)), full((LANES, fo)), full((1, fo)), full((1, fo)),
                  full((fo, fo)), full((1, fo)), full((fo, ncol)), full((1, ncol)), full((1, ncol))],
        out_specs=pl.BlockSpec((tl, ncol), lambda i: (i, 0)),
        out_shape=jax.ShapeDtypeStruct((L, ncol), F32),
        compiler_params=_params("parallel"),
        name="hyena_filter_mlp",
    )(feats, w1p, b1[None, :], freq[None, :], w2, b2[None, :], w3, b3[None, :], decay.reshape(1, ncol))


def _filter_spectrum_kernel(hf_ref, hb_ref, gf_ref, f3f_ref, o_ref, a_scr, x_scr):
    ct = a_scr.shape[2]
    for t, src in enumerate((hf_ref, hb_ref)):
        for hf in range(2):
            _dft_stage1(src, gf_ref, hf, a_scr)

            def body(g, carry):
                r0 = pl.multiple_of(g * SUBLANES, SUBLANES)
                for p, m in enumerate(_dft_stage3_operands(a_scr, r0)):
                    x2 = jnp.dot(f3f_ref[...], m, preferred_element_type=F32)
                    for q in range(2):
                        k1 = hf * K1_HALF + r0 + 2 * p + q
                        x = x2[:, q * ct:(q + 1) * ct]
                        if t == 0:
                            x_scr[k1] = x
                        else:
                            xf = x_scr[k1]
                            o_ref[0, k1] = jnp.concatenate([xf[:DFT_R] + x[:DFT_R], xf[DFT_R:] - x[DFT_R:]],
                                                           axis=0).astype(BF16)
                return carry

            lax.fori_loop(0, K1_GROUPS[hf], body, 0, unroll=True)


def _filter_spectrum(taps3, gf, f3f):
    na, nb, ncol = taps3.shape
    C = ncol // 4
    nct = C // HY_CT
    ct = HY_CT
    return pl.pallas_call(
        _filter_spectrum_kernel,
        grid=(2, nct),
        in_specs=[
            pl.BlockSpec((na, nb, ct), lambda o, j: (0, 0, o * nct + j)),
            pl.BlockSpec((na, nb, ct), lambda o, j: (0, 0, 2 * nct + o * nct + j)),
            pl.BlockSpec(gf.shape, lambda o, j: (0, 0, 0, 0), pipeline_mode=pl.Buffered(1)),
            pl.BlockSpec(f3f.shape, lambda o, j: (0, 0), pipeline_mode=pl.Buffered(1)),
        ],
        out_specs=pl.BlockSpec((1, K1_SPEC, 2 * DFT_R, ct), lambda o, j: (o, 0, 0, j)),
        out_shape=jax.ShapeDtypeStruct((2, K1_SPEC, 2 * DFT_R, C), BF16),
        scratch_shapes=[pltpu.VMEM((DFT_R, 2 * K1_HALF, ct), F32), pltpu.VMEM((K1_SPEC, 2 * DFT_R, ct), F32)],
        compiler_params=_params("parallel", "parallel"),
        name="hyena_filter_spectrum",
    )(taps3, taps3, gf, f3f)


def _long_conv_kernel(z_ref, xg_ref, h_ref, bias_ref, gf_ref, f3f_ref, f3i_ref, gi_ref, o_ref, a_scr, b_scr):
    ct = a_scr.shape[2]
    bias = bias_ref[0]
    for hf in range(2):
        _dft_stage1(z_ref.at[0], gf_ref, hf, a_scr)
        ngrp = K1_GROUPS[hf]
        if ngrp * SUBLANES < K1_HALF:
            b_scr[ngrp * SUBLANES:] = jnp.zeros((K1_HALF - ngrp * SUBLANES,) + b_scr.shape[1:], F32)

        def spectral(g, carry):
            r0 = pl.multiple_of(g * SUBLANES, SUBLANES)
            for p, m in enumerate(_dft_stage3_operands(a_scr, r0)):
                x = jnp.dot(f3f_ref[...], m, preferred_element_type=F32)
                k1 = hf * K1_HALF + r0 + 2 * p
                h = jnp.concatenate([h_ref[0, k1], h_ref[0, k1 + 1]], axis=1).astype(F32)
                xr, xi, hr, hi = x[:DFT_R], x[DFT_R:], h[:DFT_R], h[DFT_R:]
                y = jnp.concatenate([xr * hr - xi * hi, xr * hi + xi * hr], axis=0).astype(BF16)
                bb = jnp.dot(f3i_ref[...], y, preferred_element_type=F32)
                b_scr[r0 + 2 * p] = bb[:, :ct]
                b_scr[r0 + 2 * p + 1] = bb[:, ct:]
            return carry

        lax.fori_loop(0, ngrp, spectral, 0, unroll=True)

        def inverse(g, carry):
            b0 = pl.multiple_of(g * SUBLANES, SUBLANES)
            br = _swap01(b_scr[:, pl.ds(b0, SUBLANES), :])
            bi = _swap01(b_scr[:, pl.ds(DFT_R + b0, SUBLANES), :])
            ys = []
            for p in range(SUBLANES // 2):
                bm = [jnp.concatenate([br[2 * p + q], bi[2 * p + q]], axis=0).astype(BF16) for q in range(2)]
                y2 = jnp.dot(gi_ref[hf, g * (SUBLANES // 2) + p], _block_diag_lanes(bm[0], bm[1]),
                             preferred_element_type=F32)
                ys += [y2[:, :ct], y2[:, ct:]]
            y = _swap01(jnp.stack(ys, axis=0))
            sl = (0, slice(None), pl.ds(b0, SUBLANES), slice(None))
            if hf == 0:
                o_ref[sl] = y
            else:
                o_ref[sl] = xg_ref[sl] * (o_ref[sl] + y + z_ref[sl] * bias)
            return carry

        lax.fori_loop(0, DFT_R // SUBLANES, inverse, 0, unroll=4)


def _long_conv(z_arr, z_off, xg_arr, xg_off, spec, order, bias3, gf, f3f, f3i, gi):
    B, na, nb, _ = z_arr.shape
    C = spec.shape[3]
    ct = HY_CT
    const = lambda arr: pl.BlockSpec(arr.shape, lambda j, b: (0,) * arr.ndim, pipeline_mode=pl.Buffered(1))
    return pl.pallas_call(
        _long_conv_kernel,
        grid=(C // ct, B),
        in_specs=[
            pl.BlockSpec((1, na, nb, ct), lambda j, b: (b, 0, 0, z_off + j)),
            pl.BlockSpec((1, na, nb, ct), lambda j, b: (b, 0, 0, xg_off + j)),
            pl.BlockSpec((1, K1_SPEC, 2 * DFT_R, ct), lambda j, b: (order, 0, 0, j), pipeline_mode=pl.Buffered(1)),
            pl.BlockSpec((1, 1, ct), lambda j, b: (order, 0, j)),
            const(gf), const(f3f), const(f3i), const(gi),
        ],
        out_specs=pl.BlockSpec((1, na, nb, ct), lambda j, b: (b, 0, 0, j)),
        out_shape=jax.ShapeDtypeStruct((B, na, nb, C), F32),
        scratch_shapes=[pltpu.VMEM((DFT_R, 2 * K1_HALF, ct), F32), pltpu.VMEM((K1_HALF, 2 * DFT_R, ct), F32)],
        compiler_params=_params("parallel", "parallel"),
        name="hyena_long_conv",
    )(z_arr, xg_arr, spec, bias3, gf, f3f, f3i, gi)


def _hyena_group(hy_conv, fw1, fb1, ffreq, fw2, fb2, fw3, fb3, decay, hyena_bias):
    B, S, _ = hy_conv.shape
    na = S // DFT_R
    gf, gi, f3f, f3i = _dft_tables(S)
    taps = _filter_taps(S, fw1, fb1, ffreq, fw2, fb2, fw3, fb3, decay)
    spec = _filter_spectrum(taps.reshape(na, DFT_R, -1), gf, f3f)
    u = hy_conv.reshape(B, na, DFT_R, -1)
    nct = HY_W // HY_CT
    bias3 = hyena_bias[:, None, :]
    z1 = _long_conv(u, 0, u, nct, spec, 0, bias3, gf, f3f, f3i, gi)
    return _long_conv(z1, 0, u, 2 * nct, spec, 1, bias3, gf, f3f, f3i, gi).reshape(B, S, HY_W)


def _rope_rot_cols(w):
    half = ROPE // 2
    return jnp.concatenate([-w[..., half:], w[..., :half]], axis=-1)


def _prep_weights(w_in, w_qb, w_kvb):
    D = w_in.shape[0]
    o = Q_RANK + KV_RANK
    w_kr = w_in[:, o:o + ROPE]
    z64 = jnp.zeros((D, NOPE), F32)
    z32 = jnp.zeros((D, LANES - NOPE - ROPE), F32)
    win_aug = jnp.concatenate([
        w_in[:, :o],
        z64, w_kr, z32,
        z64, _rope_rot_cols(w_kr), z32,
        w_in[:, o + ROPE:],
    ], axis=1).astype(BF16)
    wq = w_qb.reshape(Q_RANK, MLA_HEADS, NOPE + ROPE)
    zq = jnp.zeros((Q_RANK, MLA_HEADS, LANES - NOPE - ROPE), F32)
    wq_main = jnp.concatenate([wq, zq], axis=-1).reshape(Q_RANK, MLA_HEADS * LANES).astype(BF16)
    wq_rot = jnp.concatenate([jnp.zeros((Q_RANK, MLA_HEADS, NOPE), F32), _rope_rot_cols(wq[..., NOPE:]), zq],
                             axis=-1).reshape(Q_RANK, MLA_HEADS * LANES).astype(BF16)
    wkv = w_kvb.reshape(KV_RANK, MLA_HEADS, NOPE + V_DIM)
    wk = jnp.concatenate([wkv[..., :NOPE], jnp.zeros((KV_RANK, MLA_HEADS, LANES - NOPE), F32)],
                         axis=-1).reshape(KV_RANK, MLA_HEADS * LANES).astype(BF16)
    wv = wkv[..., NOPE:].reshape(KV_RANK, MLA_HEADS * V_DIM).astype(BF16)
    return win_aug, wq_main, wq_rot, wk, wv


def kernel(x, c, positions, w_ada, b_ada, w_in, q_norm_g, w_qb, kv_norm_g, w_kvb, conv_w, conv_b, filt_w1, filt_b1, filt_freq, filt_w2, filt_b2, filt_w3, filt_b3, hyena_decay, hyena_bias, attn_out_g, hyena_out_g, w_o, ln1_g, ln1_b, w_router, w_gate, w_up, w_down, ln2_g, ln2_b):
    B, S, D = x.shape
    l = 0
    c8 = jnp.zeros((SUBLANES, D), F32).at[:B].set(c)
    mod = _ada_mod(c8, w_ada[l], b_ada[l][None, :])[:B]
    sh1, sc1, g1, sh2, sc2, g2 = [m[:, None, :] for m in jnp.split(mod, 6, axis=-1)]

    win_aug, wq_main, wq_rot, wk, wv = _prep_weights(w_in[l], w_qb[l], w_kvb[l])
    cos_h, sin_h = _rope_tables(positions)
    pad = lambda n, v: jnp.full((B, n, S), v, F32)
    cos_t = jnp.concatenate([pad(NOPE, 1.0), cos_h, cos_h, pad(LANES - NOPE - ROPE, 1.0)], axis=1)
    sin_t = jnp.concatenate([pad(NOPE, 0.0), sin_h, sin_h, pad(LANES - NOPE - ROPE, 0.0)], axis=1)
    q, kt, v, hy_conv = _inproj(x, sc1, sh1, cos_t, sin_t, win_aug, q_norm_g[l][None, :], wq_main, wq_rot,
                                kv_norm_g[l][None, :], wk, wv, conv_w[l], conv_b[l])
    a = _attention(q, kt, v)

    hy = _hyena_group(hy_conv, filt_w1[l], filt_b1[l], filt_freq[l], filt_w2[l],
                      filt_b2[l], filt_w3[l], filt_b3[l], hyena_decay[l], hyena_bias[l])

    wr_pad = jnp.concatenate([w_router[l], jnp.zeros((D, LANES - N_EXPERTS), F32)], axis=1)
    x1, u2, aff_t = _outproj(a, hy, x, g1, sc2, sh2, attn_out_g[l][None, :], hyena_out_g[l][None, :],
                             w_o[l].astype(BF16), ln1_g[l][None, :], ln1_b[l][None, :], wr_pad)

    cap = EC_FACTOR * S // N_EXPERTS
    rpe = S // LANES
    aff2 = aff_t.reshape(B, N_EXPERTS * rpe, LANES)
    pos_sel, pos_all, gsel = _select(aff2, cap)
    row_start = pos_all.reshape(B, N_EXPERTS, rpe, LANES)[:, :, :, 0]

    def block_offsets(tokens):
        ends = jnp.full((B, N_EXPERTS, 1), cap, jnp.int32)
        return jnp.concatenate([row_start[:, :, ::tokens // LANES], ends], axis=-1).reshape(-1)

    xe = _gather(block_offsets(GATHER_TOK), u2, pos_sel, cap)
    ye = _moe_ffn(xe, w_gate[l], w_up[l], w_down[l])
    return _combine_ln2(block_offsets(CMB_TOK), x1, pos_sel.reshape(B, N_EXPERTS, S), gsel.reshape(B, N_EXPERTS, S), g2, ln2_g[l][None, :], ln2_b[l][None, :],
                        ye.reshape(N_EXPERTS * B * cap, D), cap)
```

```python
import functools
import math

import jax
import jax.numpy as jnp
import numpy as np
from jax import lax
from jax.experimental import pallas as pl
from jax.experimental.pallas import tpu as pltpu

F32 = jnp.float32
BF16 = jnp.bfloat16
HIGHEST = lax.Precision.HIGHEST

LANES = 128
SUBLANES = 8
VMEM_LIMIT = 56 * 1024 * 1024

D_MODEL = 1024
MLA_HEADS = 8
NOPE = 64
ROPE = 32
V_DIM = 64
Q_RANK = 256
KV_RANK = 128
HY_W = 512
HY_COLS = 3 * HY_W
N_EXPERTS = 16
EC_FACTOR = 2
ROPE_THETA = 10000.0
EPS = 1e-5
DN_ALPHA = 2.0 ** 0.25
LOG2E = 1.4426950408889634

TOK_TILE = 512
ATT_TQ = 512
ATT_TK = 512
MOE_ROWS = 512


def _params(*sem):
    return pltpu.CompilerParams(dimension_semantics=sem, vmem_limit_bytes=VMEM_LIMIT)


def _ada_kernel(c_ref, w_ref, b_ref, o_ref):
    c = c_ref[...]
    sc = c * jax.nn.sigmoid(c)
    o_ref[...] = jnp.dot(sc, w_ref[...], precision=HIGHEST, preferred_element_type=F32) + b_ref[...]


def _ada_mod(c8, w_ada, b_ada):
    n = w_ada.shape[1]
    tn = 1024
    return pl.pallas_call(
        _ada_kernel,
        grid=(n // tn,),
        in_specs=[
            pl.BlockSpec((SUBLANES, D_MODEL), lambda j: (0, 0)),
            pl.BlockSpec((D_MODEL, tn), lambda j: (0, j)),
            pl.BlockSpec((1, tn), lambda j: (0, j)),
        ],
        out_specs=pl.BlockSpec((SUBLANES, tn), lambda j: (0, j)),
        out_shape=jax.ShapeDtypeStruct((SUBLANES, n), F32),
        compiler_params=_params("arbitrary"),
        name="ada_mod",
    )(c8, w_ada, b_ada)


def _dot3(a, b):
    ah, bh = a.astype(BF16), b.astype(BF16)
    al = (a - ah.astype(F32)).astype(BF16)
    bl = (b - bh.astype(F32)).astype(BF16)
    dot = functools.partial(jnp.dot, preferred_element_type=F32)
    return dot(ah, bh) + (dot(ah, bl) + dot(al, bh))


def _rms(x, g):
    return x * lax.rsqrt(jnp.mean(x * x, axis=-1, keepdims=True) + EPS) * g


def _rope_kernel(pos_ref, freq_ref, cos_ref, sin_ref):
    ang = freq_ref[...] * pos_ref[0].astype(F32)
    cos_ref[0] = jnp.cos(ang)
    sin_ref[0] = jnp.sin(ang)


def _rope_tables(positions):
    B, S = positions.shape
    half = ROPE // 2
    inv_freq = (ROPE_THETA ** (-jnp.arange(half, dtype=F32) / half))[:, None]
    out = jax.ShapeDtypeStruct((B, half, S), F32)
    blk = pl.BlockSpec((1, half, S), lambda b: (b, 0, 0))
    return pl.pallas_call(
        _rope_kernel,
        grid=(B,),
        in_specs=[pl.BlockSpec((1, 1, S), lambda b: (b, 0, 0)), pl.BlockSpec((half, 1), lambda b: (0, 0))],
        out_specs=[blk, blk],
        out_shape=[out, out],
        compiler_params=_params("parallel"),
        name="rope_tables",
    )(positions[:, None, :], inv_freq)


def _inproj_kernel(x_ref, xp_ref, xn_ref, sc_ref, sh_ref, cos_ref, sin_ref, win_ref, qg_ref, wq_ref, wqr_ref,
                   kvg_ref, wk_ref, wv_ref, cw_ref, cb_ref, q_ref, k_ref, v_ref, hy_ref):
    def modulate(xv):
        return (xv * (1.0 + sc_ref[0]) + sh_ref[0]).astype(BF16)

    u = modulate(x_ref[0])
    cos = cos_ref[0].T
    sin = sin_ref[0].T

    def proj(lo, hi):
        return jnp.dot(u, win_ref[:, lo:hi], preferred_element_type=F32)

    o = Q_RANK + 3 * KV_RANK
    hy = proj(o, o + HY_COLS)
    tm = hy.shape[0]
    i, last = pl.program_id(1), pl.num_programs(1) - 1
    edge = jnp.dot(modulate(jnp.concatenate([xp_ref[0], xn_ref[0]], axis=0)), win_ref[:, o:o + HY_COLS],
                   preferred_element_type=F32)
    prev = jnp.where(i == 0, 0.0, edge[SUBLANES - 1:SUBLANES, :])
    nxt = jnp.where(i == last, 0.0, edge[SUBLANES:SUBLANES + 1, :])
    row = lax.broadcasted_iota(jnp.int32, hy.shape, 0)
    up = jnp.where(row == 0, prev, pltpu.roll(hy, 1, 0))
    dn = jnp.where(row == tm - 1, nxt, pltpu.roll(hy, tm - 1, 0))
    hy_ref[0] = cb_ref[...] + cw_ref[0:1, :] * up + cw_ref[1:2, :] * hy + cw_ref[2:3, :] * dn

    cq = _rms(proj(0, Q_RANK), qg_ref[...]).astype(BF16)
    q_main = jnp.dot(cq, wq_ref[...], preferred_element_type=F32)
    q_rot = jnp.dot(cq, wqr_ref[...], preferred_element_type=F32)
    qscale = LOG2E * (NOPE + ROPE) ** -0.5
    cq_s = cos * qscale
    sq_s = sin * qscale
    for h in range(MLA_HEADS):
        sl = slice(h * LANES, (h + 1) * LANES)
        q_ref[0, h] = (q_main[:, sl] * cq_s + q_rot[:, sl] * sq_s).T.astype(BF16)

    o = Q_RANK
    ckv = _rms(proj(o, o + KV_RANK), kvg_ref[...]).astype(BF16)
    kr = proj(o + KV_RANK, o + 2 * KV_RANK) * cos + proj(o + 2 * KV_RANK, o + 3 * KV_RANK) * sin
    kn = jnp.dot(ckv, wk_ref[...], preferred_element_type=F32)
    for h in range(MLA_HEADS):
        k_ref[0, h] = (kn[:, h * LANES:(h + 1) * LANES] + kr).astype(BF16)
    v_ref[0] = jnp.dot(ckv, wv_ref[...], preferred_element_type=F32).T.astype(BF16)


def _inproj(x, sc1, sh1, cos_t, sin_t, win_aug, qg, wq, wqr, kvg, wk, wv, conv_w, conv_b):
    B, S, D = x.shape
    tm = TOK_TILE
    ncol = win_aug.shape[1]
    g8 = tm // SUBLANES
    n8 = S // SUBLANES
    full = lambda shape: pl.BlockSpec(shape, lambda b, i: (0,) * len(shape))
    return pl.pallas_call(
        _inproj_kernel,
        grid=(B, S // tm),
        in_specs=[
            pl.BlockSpec((1, tm, D), lambda b, i: (b, i, 0)),
            pl.BlockSpec((1, SUBLANES, D), lambda b, i: (b, jnp.maximum(i * g8 - 1, 0), 0)),
            pl.BlockSpec((1, SUBLANES, D), lambda b, i: (b, jnp.minimum((i + 1) * g8, n8 - 1), 0)),
            pl.BlockSpec((1, 1, D), lambda b, i: (b, 0, 0)),
            pl.BlockSpec((1, 1, D), lambda b, i: (b, 0, 0)),
            pl.BlockSpec((1, LANES, tm), lambda b, i: (b, 0, i)),
            pl.BlockSpec((1, LANES, tm), lambda b, i: (b, 0, i)),
            full((D, ncol)),
            full((1, Q_RANK)),
            full((Q_RANK, MLA_HEADS * LANES)),
            full((Q_RANK, MLA_HEADS * LANES)),
            full((1, KV_RANK)),
            full((KV_RANK, MLA_HEADS * LANES)),
            full((KV_RANK, MLA_HEADS * V_DIM)),
            full(conv_w.shape),
            full((1, HY_COLS)),
        ],
        out_specs=[
            pl.BlockSpec((1, MLA_HEADS, LANES, tm), lambda b, i: (b, 0, 0, i)),
            pl.BlockSpec((1, MLA_HEADS, tm, LANES), lambda b, i: (b, 0, i, 0)),
            pl.BlockSpec((1, MLA_HEADS * V_DIM, tm), lambda b, i: (b, 0, i)),
            pl.BlockSpec((1, tm, HY_COLS), lambda b, i: (b, i, 0)),
        ],
        out_shape=[
            jax.ShapeDtypeStruct((B, MLA_HEADS, LANES, S), BF16),
            jax.ShapeDtypeStruct((B, MLA_HEADS, S, LANES), BF16),
            jax.ShapeDtypeStruct((B, MLA_HEADS * V_DIM, S), BF16),
            jax.ShapeDtypeStruct((B, S, HY_COLS), F32),
        ],
        compiler_params=_params("parallel", "parallel"),
        name="in_proj",
    )(x, x, x, sc1, sh1, cos_t, sin_t, win_aug, qg, wq, wqr, kvg, wk, wv, conv_w, conv_b[None, :])


ATT_AUG = 16


def _attn_kernel(qt_ref, k_ref, vt_ref, o_ref, s_scr):
    S = k_ref.shape[2]
    tq = ATT_TQ
    nk = S // ATT_TK
    nq = S // tq
    ones_rows = jnp.where(lax.broadcasted_iota(jnp.int32, (ATT_AUG, ATT_TK), 0) == 0, 1.0, 0.0).astype(BF16)

    def scores(qi, c, slot):
        k0 = pl.multiple_of(c * ATT_TK, ATT_TK)
        q0 = pl.multiple_of(qi * tq, tq)
        for h in range(2):
            s_scr[slot, h] = jnp.dot(k_ref[0, h, pl.ds(k0, ATT_TK), :], qt_ref[0, h, :, pl.ds(q0, tq)],
                                     preferred_element_type=F32)

    def absorb(c, slot, carry):
        k0 = pl.multiple_of(c * ATT_TK, ATT_TK)
        out = []
        for h in range(2):
            m_prev, acc = carry[h]
            s = s_scr[slot, h]
            m_new = jnp.maximum(m_prev, jnp.max(s, axis=0, keepdims=True))
            alpha = jnp.exp2(m_prev - m_new)
            p = jnp.exp2(s - m_new).astype(BF16)
            v_aug = jnp.concatenate([vt_ref[0, h * V_DIM:(h + 1) * V_DIM, pl.ds(k0, ATT_TK)], ones_rows], axis=0)
            out.append((m_new, alpha * acc + jnp.dot(v_aug, p, preferred_element_type=F32)))
        return tuple(out)

    def query_tile(qi, _):
        def pair(i, carry):
            c0 = 2 * i
            scores(qi, c0 + 1, 1)
            carry = absorb(c0, 0, carry)
            wrap = c0 + 2 >= nk
            scores(jnp.where(wrap, jnp.minimum(qi + 1, nq - 1), qi), jnp.where(wrap, 0, c0 + 2), 0)
            return absorb(c0 + 1, 1, carry)

        init = tuple((jnp.full((1, tq), -jnp.inf, F32), jnp.zeros((V_DIM + ATT_AUG, tq), F32)) for _ in range(2))
        fin = lax.fori_loop(0, nk // 2, pair, init, unroll=4)
        o = jnp.concatenate([acc[:V_DIM] * (1.0 / acc[V_DIM:V_DIM + 1]) for (_, acc) in fin], axis=0)
        o_ref[0, pl.ds(pl.multiple_of(qi * tq, tq), tq), :] = o.T
        return 0

    scores(0, 0, 0)
    lax.fori_loop(0, nq, query_tile, 0)


def _attention(qt, k, vt):
    B, H, _, S = qt.shape
    return pl.pallas_call(
        _attn_kernel,
        grid=(B, H // 2),
        in_specs=[
            pl.BlockSpec((1, 2, LANES, S), lambda b, hp: (b, hp, 0, 0)),
            pl.BlockSpec((1, 2, S, LANES), lambda b, hp: (b, hp, 0, 0)),
            pl.BlockSpec((1, 2 * V_DIM, S), lambda b, hp: (b, hp, 0)),
        ],
        out_specs=pl.BlockSpec((1, S, 2 * V_DIM), lambda b, hp: (b, 0, hp)),
        out_shape=jax.ShapeDtypeStruct((B, S, H * V_DIM), F32),
        scratch_shapes=[pltpu.VMEM((2, 2, ATT_TK, ATT_TQ), F32)],
        compiler_params=_params("parallel", "parallel"),
        name="mla_attention",
    )(qt, k, vt)


def _layer_norm(x, g, b):
    mu = jnp.mean(x, axis=-1, keepdims=True)
    xc = x - mu
    var = jnp.mean(xc * xc, axis=-1, keepdims=True)
    return xc * lax.rsqrt(var + EPS) * g + b


def _outproj_kernel(a_ref, hy_ref, x_ref, g1_ref, sc2_ref, sh2_ref, ag_ref, hg_ref, wo_ref,
                    lg_ref, lb_ref, wr_ref, x1_ref, u2_ref, aff_ref):
    half = a_ref.shape[2]
    an = _rms(a_ref[0], ag_ref[...]).astype(BF16)
    hn = _rms(hy_ref[0], hg_ref[...]).astype(BF16)
    mixed = (jnp.dot(an, wo_ref[:half, :], preferred_element_type=F32)
             + jnp.dot(hn, wo_ref[half:, :], preferred_element_type=F32))
    x1 = _layer_norm(DN_ALPHA * x_ref[0] + g1_ref[0] * mixed, lg_ref[...], lb_ref[...])
    x1_ref[0] = x1
    u2 = x1 * (1.0 + sc2_ref[0]) + sh2_ref[0]
    u2_ref[0] = u2.astype(BF16)
    logits = _dot3(u2, wr_ref[...])
    lane = lax.broadcasted_iota(jnp.int32, logits.shape, 1)
    logits = jnp.where(lane < N_EXPERTS, logits, -jnp.inf)
    e = jnp.exp(logits - jnp.max(logits, axis=-1, keepdims=True))
    aff = e / jnp.sum(e, axis=-1, keepdims=True)
    aff_ref[0] = aff.T[:N_EXPERTS, :]


def _outproj(a, hy, x, g1, sc2, sh2, ag, hg, wo, lg, lb, wr_pad):
    B, S, D = x.shape
    tm = TOK_TILE
    half = a.shape[2]
    full = lambda shape: pl.BlockSpec(shape, lambda b, i: (0,) * len(shape))
    tok = lambda w: pl.BlockSpec((1, tm, w), lambda b, i: (b, i, 0))
    per_b = pl.BlockSpec((1, 1, D), lambda b, i: (b, 0, 0))
    return pl.pallas_call(
        _outproj_kernel,
        grid=(B, S // tm),
        in_specs=[tok(half), tok(half), tok(D), per_b, per_b, per_b,
                  full((1, half)), full((1, half)), full((D, D)), full((1, D)), full((1, D)),
                  full((D, LANES))],
        out_specs=[tok(D), tok(D), pl.BlockSpec((1, N_EXPERTS, tm), lambda b, i: (b, 0, i))],
        out_shape=[
            jax.ShapeDtypeStruct((B, S, D), F32),
            jax.ShapeDtypeStruct((B, S, D), BF16),
            jax.ShapeDtypeStruct((B, N_EXPERTS, S), F32),
        ],
        compiler_params=_params("parallel", "parallel"),
        name="out_proj_ln1_router",
    )(a, hy, x, g1, sc2, sh2, ag, hg, wo, lg, lb, wr_pad)


def _select_kernel(aff_ref, ones_ref, utri_ref, ltri_ref, pos_ref, posall_ref, g_ref, *, cap):
    aff = aff_ref[0]
    ones = ones_ref[...]

    def rowsum(mf):
        return jnp.dot(mf, ones, preferred_element_type=F32)

    def expert_total(mf):
        x = mf.astype(F32).reshape(N_EXPERTS, aff.shape[0] // N_EXPERTS, LANES)
        tot = jnp.sum(jnp.sum(x, axis=1, keepdims=True), axis=2, keepdims=True)
        return jnp.broadcast_to(tot, x.shape).reshape(aff.shape)

    def prefix(mf):
        return (jnp.dot(mf, utri_ref[...], preferred_element_type=F32)
                + jnp.dot(ltri_ref[...], rowsum(mf).astype(BF16), preferred_element_type=F32))

    def ind(m):
        return jnp.where(m, 1.0, 0.0).astype(BF16)

    def search(i, thr_bits):
        cand = thr_bits | jnp.left_shift(jnp.int32(1), 30 - i)
        enough = expert_total(ind(aff >= lax.bitcast_convert_type(cand, F32))) >= cap
        return jnp.where(enough, cand, thr_bits)

    thr = lax.bitcast_convert_type(lax.fori_loop(0, 31, search, jnp.zeros(aff.shape, jnp.int32)), F32)
    gt = aff > thr
    eq = aff == thr
    room = cap - expert_total(ind(gt))
    take_eq = jnp.where(eq, jnp.where(prefix(ind(eq)) < room, 1.0, 0.0), 0.0)
    sel = jnp.where(gt, 1.0, take_eq)
    pos = prefix(sel.astype(BF16)).astype(jnp.int32)
    posall_ref[0] = pos
    pos_ref[0] = jnp.where(sel > 0.0, pos, -1)
    g_ref[0] = jnp.where(sel > 0.0, aff, 0.0)


def _select(aff2, cap):
    B, R, _ = aff2.shape
    rpe = R // N_EXPERTS
    li = jnp.arange(LANES, dtype=jnp.int32)
    ri = jnp.arange(R, dtype=jnp.int32)
    same = (ri[:, None] // rpe) == (ri[None, :] // rpe)
    ones = jnp.ones((LANES, LANES), BF16)
    utri = (li[:, None] < li[None, :]).astype(BF16)
    ltri = (same & (ri[None, :] < ri[:, None])).astype(BF16)
    blk = pl.BlockSpec((1, R, LANES), lambda b: (b, 0, 0))
    full = lambda arr: pl.BlockSpec(arr.shape, lambda b: (0, 0))
    return pl.pallas_call(
        functools.partial(_select_kernel, cap=cap),
        grid=(B,),
        in_specs=[blk, full(ones), full(utri), full(ltri)],
        out_specs=[blk, blk, blk],
        out_shape=[jax.ShapeDtypeStruct((B, R, LANES), jnp.int32), jax.ShapeDtypeStruct((B, R, LANES), jnp.int32),
                   jax.ShapeDtypeStruct((B, R, LANES), F32)],
        compiler_params=_params("parallel"),
        name="expert_choice_select",
    )(aff2, ones, utri, ltri)


GATHER_TOK = 1024
GATHER_WIN = 160


def _gather_kernel(offs_ref, u_ref, pos_ref, xe_ref, acc, *, cap, nblk):
    b, e = pl.program_id(0), pl.program_id(1)
    base = (b * N_EXPERTS + e) * (nblk + 1)
    rows = GATHER_TOK // LANES
    acc[...] = jnp.zeros(acc.shape, F32)
    slot = lax.broadcasted_iota(jnp.int32, (GATHER_WIN, LANES), 0)

    def block(j, carry):
        off, end = offs_ref[base + j], offs_ref[base + j + 1]
        w = (off // SUBLANES) * SUBLANES
        prow = pos_ref[0, pl.ds(pl.multiple_of(j * rows, rows), rows), :]
        ublk = u_ref[0, pl.ds(pl.multiple_of(j * GATHER_TOK, GATHER_TOK), GATHER_TOK), :]

        def window(i, c):
            s0 = pl.multiple_of(w + i * GATHER_WIN, SUBLANES)
            onehot = jnp.concatenate(
                [jnp.where(slot == prow[r:r + 1, :] - s0, 1.0, 0.0).astype(BF16) for r in range(rows)], axis=1)
            acc[pl.ds(s0, GATHER_WIN), :] += jnp.dot(onehot, ublk, preferred_element_type=F32)
            return c

        window(0, 0)
        lax.fori_loop(1, (end - w + GATHER_WIN - 1) // GATHER_WIN, window, 0)
        return carry

    lax.fori_loop(0, nblk, block, 0, unroll=2)
    xe_ref[0] = acc[:cap, :].astype(BF16)


def _gather(offs, u2, pos_sel, cap):
    B, S, D = u2.shape
    rpe = S // LANES
    nblk = S // GATHER_TOK
    return pl.pallas_call(
        functools.partial(_gather_kernel, cap=cap, nblk=nblk),
        grid_spec=pltpu.PrefetchScalarGridSpec(
            num_scalar_prefetch=1,
            grid=(B, N_EXPERTS),
            in_specs=[pl.BlockSpec((1, S, D), lambda b, e, o: (b, 0, 0)),
                      pl.BlockSpec((1, rpe, LANES), lambda b, e, o: (b, e, 0))],
            out_specs=pl.BlockSpec((1, cap, D), lambda b, e, o: (e, b, 0)),
            scratch_shapes=[pltpu.VMEM((cap + GATHER_WIN + SUBLANES, D), F32)],
        ),
        out_shape=jax.ShapeDtypeStruct((N_EXPERTS, B * cap, D), BF16),
        compiler_params=_params("parallel", "parallel"),
        name="expert_gather",
    )(offs, u2, pos_sel)


def _moe_kernel(x_ref, wg_ref, wu_ref, wd_ref, o_ref, wgb, wub, wdb):
    @pl.when(pl.program_id(1) == 0)
    def _():
        wgb[...] = wg_ref[0].astype(BF16)
        wub[...] = wu_ref[0].astype(BF16)
        wdb[...] = wd_ref[0].astype(BF16)

    x = x_ref[0]
    hg = jnp.dot(x, wgb[...], preferred_element_type=F32)
    hu = jnp.dot(x, wub[...], preferred_element_type=F32)
    h = (hg * jax.nn.sigmoid(hg) * hu).astype(BF16)
    o_ref[0] = jnp.dot(h, wdb[...], preferred_element_type=F32).astype(BF16)


def _moe_ffn(xe, w_gate, w_up, w_down):
    E, R, D = xe.shape
    Fh = w_gate.shape[2]
    tr = MOE_ROWS
    return pl.pallas_call(
        _moe_kernel,
        grid=(E, R // tr),
        in_specs=[
            pl.BlockSpec((1, tr, D), lambda e, r: (e, r, 0)),
            pl.BlockSpec((1, D, Fh), lambda e, r: (e, 0, 0)),
            pl.BlockSpec((1, D, Fh), lambda e, r: (e, 0, 0)),
            pl.BlockSpec((1, Fh, D), lambda e, r: (e, 0, 0)),
        ],
        out_specs=pl.BlockSpec((1, tr, D), lambda e, r: (e, r, 0)),
        out_shape=jax.ShapeDtypeStruct((E, R, D), BF16),
        scratch_shapes=[pltpu.VMEM((D, Fh), BF16), pltpu.VMEM((D, Fh), BF16), pltpu.VMEM((Fh, D), BF16)],
        compiler_params=_params("parallel", "arbitrary"),
        name="moe_swiglu",
    )(xe, w_gate, w_up, w_down)


CMB_TOK = 512
CMB_WIN = 128
BF16_ROWS = 16


def _combine_kernel(offs_ref, x1_ref, pos_ref, g_ref, g2_ref, lg_ref, lb_ref, ye_hbm, o_ref,
                    rhs, lhs, acc, sem, *, cap, nblk, nbatch):
    b, i = pl.program_id(0), pl.program_id(1)
    W = CMB_WIN
    total = ye_hbm.shape[0]
    step = b * nblk + i
    cur_slot = step % 2
    row_e = lax.broadcasted_iota(jnp.int32, (N_EXPERTS, 1), 0)
    slot_w = lax.broadcasted_iota(jnp.int32, (W, CMB_TOK), 0)

    def windows(bb, ii):
        out = []
        for e in range(N_EXPERTS):
            k = (bb * N_EXPERTS + e) * (nblk + 1) + ii
            off, end = offs_ref[k], offs_ref[k + 1]
            lo = (off // BF16_ROWS) * BF16_ROWS
            out.append(((e * nbatch + bb) * cap, lo, jnp.where(end > off, (end - lo + W - 1) // W, 0)))
        return out

    def fetch_starts(win, p):
        return [jnp.minimum(rb + lo + p * W, total - W) for rb, lo, _ in win]

    def window_copy(e, start, s):
        return pltpu.make_async_copy(ye_hbm.at[pl.ds(pl.multiple_of(start, BF16_ROWS), W), :],
                                     rhs.at[s, pl.ds(e * W, W), :], sem.at[s, e])

    def start_fetch(win, p, s):
        for e, start in enumerate(fetch_starts(win, p)):
            window_copy(e, start, s).start()

    def absorb(win, p, s, first):
        starts = fetch_starts(win, p)
        lo_col = jnp.zeros((N_EXPERTS, 1), jnp.int32)
        shift_col = jnp.zeros((N_EXPERTS, 1), jnp.int32)
        for e, (rb, lo0, _) in enumerate(win):
            lo = lo0 + p * W
            lo_col = jnp.where(row_e == e, lo, lo_col)
            shift_col = jnp.where(row_e == e, rb + lo - starts[e], shift_col)
        d = pos_ref[0] - lo_col
        rel = jnp.where((d >= 0) & (d < W), d + shift_col, -1)
        g = g_ref[0]
        for e in range(N_EXPERTS):
            lhs[e * W:(e + 1) * W, :] = jnp.where(slot_w == rel[e:e + 1, :], g[e:e + 1, :], 0.0).astype(BF16)
        for e in range(N_EXPERTS):
            window_copy(e, starts[e], s).wait()
        part = lax.dot_general(lhs[...], rhs[s], (((0,), (0,)), ((), ())), preferred_element_type=F32)
        acc[...] = part if first else acc[...] + part

    cur = windows(b, i)

    @pl.when(step == 0)
    def _():
        start_fetch(cur, 0, 0)

    @pl.when(step + 1 < nbatch * nblk)
    def _():
        wrap = i + 1 == nblk
        start_fetch(windows(jnp.where(wrap, b + 1, b), jnp.where(wrap, 0, i + 1)), 0, 1 - cur_slot)

    absorb(cur, 0, cur_slot, True)

    def extra_pass(p, carry):
        start_fetch(cur, p, cur_slot)
        absorb(cur, p, cur_slot, False)
        return carry

    npass = functools.reduce(jnp.maximum, [n for _, _, n in cur])
    lax.fori_loop(1, npass, extra_pass, 0)
    o_ref[0] = _layer_norm(DN_ALPHA * x1_ref[0] + g2_ref[0] * acc[...], lg_ref[...], lb_ref[...])


def _combine_ln2(offs, x1, pos_exp, g_exp, g2, lg, lb, ye_rows, cap):
    B, S, D = x1.shape
    T = CMB_TOK
    nblk = S // T
    tok = lambda w: pl.BlockSpec((1, T, w), lambda b, i, o: (b, i, 0))
    vec = pl.BlockSpec((1, D), lambda b, i, o: (0, 0))
    exp_major = pl.BlockSpec((1, N_EXPERTS, T), lambda b, i, o: (b, 0, i))
    return pl.pallas_call(
        functools.partial(_combine_kernel, cap=cap, nblk=nblk, nbatch=B),
        grid_spec=pltpu.PrefetchScalarGridSpec(
            num_scalar_prefetch=1,
            grid=(B, nblk),
            in_specs=[tok(D), exp_major, exp_major,
                      pl.BlockSpec((1, 1, D), lambda b, i, o: (b, 0, 0)), vec, vec,
                      pl.BlockSpec(memory_space=pl.ANY)],
            out_specs=tok(D),
            scratch_shapes=[pltpu.VMEM((2, N_EXPERTS * CMB_WIN, D), BF16),
                            pltpu.VMEM((N_EXPERTS * CMB_WIN, T), BF16),
                            pltpu.VMEM((T, D), F32),
                            pltpu.SemaphoreType.DMA((2, N_EXPERTS))],
        ),
        out_shape=jax.ShapeDtypeStruct((B, S, D), F32),
        compiler_params=_params("arbitrary", "arbitrary"),
        name="combine_residual_ln2",
    )(offs, x1, pos_exp, g_exp, g2, lg, lb, ye_rows)


DFT_R = 128
K1_HALF = 40
K1_VALID = DFT_R // 2 + 1
K1_GROUPS = (K1_HALF // SUBLANES, -(-(K1_VALID - K1_HALF) // SUBLANES))
K1_SPEC = K1_HALF + K1_GROUPS[1] * SUBLANES
HY_CT = 128


@functools.lru_cache(maxsize=None)
def _dft_tables_np(S, two_sided):
    N = 2 * S
    half = np.arange(S // DFT_R)
    a_idx = np.concatenate([half, N // DFT_R - 1 - half]) if two_sided else half
    r = np.arange(2 * K1_HALF)
    k1 = np.arange(2)[:, None] * K1_HALF + r[None, :] % K1_HALF
    valid = (k1 < K1_VALID).astype(np.float64)
    n = DFT_R * a_idx[None, :] + np.arange(DFT_R)[:, None]
    theta = ((n[None, :, None, :] * k1[:, None, :, None]) % N) * (2.0 * math.pi / N)
    is_im = (r >= K1_HALF)[None, None, :, None]
    g = np.where(is_im, -np.sin(theta), np.cos(theta)) * valid[:, None, :, None]
    cw = np.where((k1 == 0) | (k1 == DFT_R // 2), 1.0, 2.0) * valid / N
    gi = np.transpose(g * cw[:, None, :, None], (0, 1, 3, 2))
    ph = ((np.arange(DFT_R)[:, None] * np.arange(DFT_R)[None, :]) % DFT_R) * (2.0 * math.pi / DFT_R)
    C, Sn = np.cos(ph), np.sin(ph)
    f3f = np.block([[C, Sn], [-Sn, C]])
    f3i = np.block([[C, -Sn], [Sn, C]])

    def pair_lanes(t):
        h, nb, nr, nc = t.shape
        return t.reshape(h, nb // 2, 2, nr, nc).transpose(0, 1, 3, 2, 4).reshape(h, nb // 2, nr, 2 * nc)

    return tuple(np.ascontiguousarray(t, dtype=np.float32) for t in (pair_lanes(g), pair_lanes(gi), f3f, f3i))


def _dft_tables(S, two_sided=False):
    return tuple(jnp.asarray(t).astype(BF16) for t in _dft_tables_np(S, two_sided))


def _swap01(x):
    return jnp.swapaxes(x, 0, 1)


def _block_diag_lanes(top, bot):
    return jnp.concatenate([jnp.concatenate([top, jnp.zeros((top.shape[0], bot.shape[1]), top.dtype)], axis=1),
                            jnp.concatenate([jnp.zeros((bot.shape[0], top.shape[1]), bot.dtype), bot], axis=1)], axis=0)


def _dft_stage1(load, gf_ref, hf, a_scr):
    ct = a_scr.shape[2]

    def body(g, carry):
        b0 = pl.multiple_of(g * SUBLANES, SUBLANES)
        zt = load(b0)
        for p in range(SUBLANES // 2):
            a2 = jnp.dot(gf_ref[hf, g * (SUBLANES // 2) + p], _block_diag_lanes(zt[2 * p], zt[2 * p + 1]),
                         preferred_element_type=F32)
            a_scr[b0 + 2 * p] = a2[:, :ct]
            a_scr[b0 + 2 * p + 1] = a2[:, ct:]
        return carry

    lax.fori_loop(0, DFT_R // SUBLANES, body, 0, unroll=4)


def _dft_stage3_operands(a_scr, r0):
    mr = _swap01(a_scr[:, pl.ds(r0, SUBLANES), :])
    mi = _swap01(a_scr[:, pl.ds(K1_HALF + r0, SUBLANES), :])
    return [jnp.concatenate([jnp.concatenate([mr[j], mi[j]], axis=0),
                             jnp.concatenate([mr[j + 1], mi[j + 1]], axis=0)], axis=1).astype(BF16)
            for j in range(0, SUBLANES, 2)]


def _filter_mlp_kernel(feat_ref, featx_ref, w1_ref, b1_ref, fr_ref, w2_ref, b2_ref, w3_ref, b3_ref, dec_ref,
                       of_ref, ob_ref):
    tl = feat_ref.shape[0]
    ncol = of_ref.shape[1]
    fr = fr_ref[...]

    def hidden(f):
        h = jnp.sin(fr * (jnp.dot(f, w1_ref[...], precision=HIGHEST, preferred_element_type=F32) + b1_ref[...]))
        return jnp.sin(fr * (jnp.dot(h, w2_ref[...], precision=HIGHEST, preferred_element_type=F32) + b2_ref[...]))

    def taps(h2, t, lo):
        out = _dot3(h2, w3_ref[:, lo:lo + ncol]) + b3_ref[:, lo:lo + ncol]
        return out * jnp.exp(-t * jnp.abs(dec_ref[:, lo:lo + ncol]))

    feats, featx = feat_ref[...], featx_ref[...]
    h2, h2x = hidden(feats), hidden(featx)
    of_ref[...] = taps(h2, feats[:, 0:1], 0)
    row = lax.broadcasted_iota(jnp.int32, (tl, 1), 0)
    last = row == tl - 1
    h2n = jnp.where(last, h2x[0:1, :], pltpu.roll(h2, tl - 1, 0))
    tn = jnp.where(last, featx[0:1, :], pltpu.roll(feats, tl - 1, 0))[:, 0:1]
    ob_ref[...] = jnp.where(last & (pl.program_id(0) == pl.num_programs(0) - 1), 0.0, taps(h2n, tn, ncol))


@functools.lru_cache(maxsize=None)
def _filter_features_np(L, nf):
    pos = np.arange(L + SUBLANES, dtype=np.float64)
    t = pos[:, None] / (L - 1)
    bands = (nf - 1) // 2
    freqs = np.linspace(1e-4, bands - 1, bands)
    phase = (2.0 * math.pi / L) * pos[:, None] * freqs[None, :]
    feats = np.concatenate([t, np.cos(phase), -np.sin(phase), np.zeros((L + SUBLANES, LANES - nf))], axis=-1)
    return np.ascontiguousarray(feats, dtype=np.float32)


def _filter_taps(L, w1, b1, freq, w2, b2, w3, b3, decay):
    nf = w1.shape[0]
    feats = jnp.asarray(_filter_features_np(L, nf))
    w1p = jnp.concatenate([w1, jnp.zeros((LANES - nf, w1.shape[1]), F32)], axis=0)
    fo = w1.shape[1]
    ncol = w3.shape[1] // 2
    tl = 512
    full = lambda shape: pl.BlockSpec(shape, lambda i: (0,) * len(shape))
    out = jax.ShapeDtypeStruct((L, ncol), F32)
    blk = pl.BlockSpec((tl, ncol), lambda i: (i, 0))
    return pl.pallas_call(
        _filter_mlp_kernel,
        grid=(L // tl,),
        in_specs=[pl.BlockSpec((tl, LANES), lambda i: (i, 0)),
                  pl.BlockSpec((SUBLANES, LANES), lambda i: ((i + 1) * (tl // SUBLANES), 0)),
                  full((LANES, fo)), full((1, fo)), full((1, fo)), full((fo, fo)), full((1, fo)),
                  full((fo, 2 * ncol)), full((1, 2 * ncol)), full((1, 2 * ncol))],
        out_specs=[blk, blk],
        out_shape=[out, out],
        compiler_params=_params("parallel"),
        name="hyena_filter_mlp",
    )(feats, feats, w1p, b1[None, :], freq[None, :], w2, b2[None, :], w3, b3[None, :], decay.reshape(1, 2 * ncol))


def _filter_spectrum_kernel(hf_ref, hb_ref, gf_ref, f3f_ref, o_ref, a_scr):
    ct = a_scr.shape[2]

    def load(b0):
        zf = _swap01(hf_ref[:, pl.ds(b0, SUBLANES), :])
        zb = _swap01(hb_ref[:, pl.ds(pl.multiple_of(DFT_R - SUBLANES - b0, SUBLANES), SUBLANES), :])
        return [jnp.concatenate([zf[j], zb[SUBLANES - 1 - j]], axis=0).astype(BF16) for j in range(SUBLANES)]

    for hf in range(2):
        _dft_stage1(load, gf_ref, hf, a_scr)

        def body(g, carry):
            r0 = pl.multiple_of(g * SUBLANES, SUBLANES)
            for p, m in enumerate(_dft_stage3_operands(a_scr, r0)):
                x2 = jnp.dot(f3f_ref[...], m, preferred_element_type=F32)
                for q in range(2):
                    o_ref[0, hf * K1_HALF + r0 + 2 * p + q] = x2[:, q * ct:(q + 1) * ct].astype(BF16)
            return carry

        lax.fori_loop(0, K1_GROUPS[hf], body, 0, unroll=True)


def _filter_spectrum(fwd3, bwd3, gf, f3f):
    na, nb, ncol = fwd3.shape
    C = ncol // 2
    nct = C // HY_CT
    ct = HY_CT
    taps = pl.BlockSpec((na, nb, ct), lambda o, j: (0, 0, o * nct + j))
    return pl.pallas_call(
        _filter_spectrum_kernel,
        grid=(2, nct),
        in_specs=[
            taps, taps,
            pl.BlockSpec(gf.shape, lambda o, j: (0, 0, 0, 0), pipeline_mode=pl.Buffered(1)),
            pl.BlockSpec(f3f.shape, lambda o, j: (0, 0), pipeline_mode=pl.Buffered(1)),
        ],
        out_specs=pl.BlockSpec((1, K1_SPEC, 2 * DFT_R, ct), lambda o, j: (o, 0, 0, j)),
        out_shape=jax.ShapeDtypeStruct((2, K1_SPEC, 2 * DFT_R, C), BF16),
        scratch_shapes=[pltpu.VMEM((DFT_R, 2 * K1_HALF, ct), F32)],
        compiler_params=_params("parallel", "parallel"),
        name="hyena_filter_spectrum",
    )(fwd3, bwd3, gf, f3f)


def _long_conv_kernel(z_ref, xg_ref, h_ref, bias_ref, gf_ref, f3f_ref, f3i_ref, gi_ref, o_ref, a_scr, b_scr):
    ct = a_scr.shape[2]
    bias = bias_ref[0]
    for hf in range(2):
        _dft_stage1(lambda b0: _swap01(z_ref[0, :, pl.ds(b0, SUBLANES), :]).astype(BF16), gf_ref, hf, a_scr)
        ngrp = K1_GROUPS[hf]
        if ngrp * SUBLANES < K1_HALF:
            b_scr[ngrp * SUBLANES:] = jnp.zeros((K1_HALF - ngrp * SUBLANES,) + b_scr.shape[1:], F32)

        def spectral(g, carry):
            r0 = pl.multiple_of(g * SUBLANES, SUBLANES)
            for p, m in enumerate(_dft_stage3_operands(a_scr, r0)):
                x = jnp.dot(f3f_ref[...], m, preferred_element_type=F32)
                k1 = hf * K1_HALF + r0 + 2 * p
                h = jnp.concatenate([h_ref[0, k1], h_ref[0, k1 + 1]], axis=1).astype(F32)
                xr, xi, hr, hi = x[:DFT_R], x[DFT_R:], h[:DFT_R], h[DFT_R:]
                y = jnp.concatenate([xr * hr - xi * hi, xr * hi + xi * hr], axis=0).astype(BF16)
                bb = jnp.dot(f3i_ref[...], y, preferred_element_type=F32)
                b_scr[r0 + 2 * p] = bb[:, :ct]
                b_scr[r0 + 2 * p + 1] = bb[:, ct:]
            return carry

        lax.fori_loop(0, ngrp, spectral, 0, unroll=True)

        def inverse(g, carry):
            b0 = pl.multiple_of(g * SUBLANES, SUBLANES)
            br = _swap01(b_scr[:, pl.ds(b0, SUBLANES), :])
            bi = _swap01(b_scr[:, pl.ds(DFT_R + b0, SUBLANES), :])
            ys = []
            for p in range(SUBLANES // 2):
                bm = [jnp.concatenate([br[2 * p + q], bi[2 * p + q]], axis=0).astype(BF16) for q in range(2)]
                y2 = jnp.dot(gi_ref[hf, g * (SUBLANES // 2) + p], _block_diag_lanes(bm[0], bm[1]),
                             preferred_element_type=F32)
                ys += [y2[:, :ct], y2[:, ct:]]
            y = _swap01(jnp.stack(ys, axis=0))
            sl = (0, slice(None), pl.ds(b0, SUBLANES), slice(None))
            if hf == 0:
                o_ref[sl] = y
            else:
                o_ref[sl] = xg_ref[sl] * (o_ref[sl] + y + z_ref[sl] * bias)
            return carry

        lax.fori_loop(0, DFT_R // SUBLANES, inverse, 0, unroll=4)


def _long_conv(z_arr, z_off, xg_arr, xg_off, spec, order, bias3, gf, f3f, f3i, gi):
    B, na, nb, _ = z_arr.shape
    C = spec.shape[3]
    ct = HY_CT
    const = lambda arr: pl.BlockSpec(arr.shape, lambda j, b: (0,) * arr.ndim, pipeline_mode=pl.Buffered(1))
    return pl.pallas_call(
        _long_conv_kernel,
        grid=(C // ct, B),
        in_specs=[
            pl.BlockSpec((1, na, nb, ct), lambda j, b: (b, 0, 0, z_off + j)),
            pl.BlockSpec((1, na, nb, ct), lambda j, b: (b, 0, 0, xg_off + j)),
            pl.BlockSpec((1, K1_SPEC, 2 * DFT_R, ct), lambda j, b: (order, 0, 0, j), pipeline_mode=pl.Buffered(1)),
            pl.BlockSpec((1, 1, ct), lambda j, b: (order, 0, j)),
            const(gf), const(f3f), const(f3i), const(gi),
        ],
        out_specs=pl.BlockSpec((1, na, nb, ct), lambda j, b: (b, 0, 0, j)),
        out_shape=jax.ShapeDtypeStruct((B, na, nb, C), F32),
        scratch_shapes=[pltpu.VMEM((DFT_R, 2 * K1_HALF, ct), F32), pltpu.VMEM((K1_HALF, 2 * DFT_R, ct), F32)],
        compiler_params=_params("parallel", "parallel"),
        name="hyena_long_conv",
    )(z_arr, xg_arr, spec, bias3, gf, f3f, f3i, gi)


def _hyena_group(hy_conv, fw1, fb1, ffreq, fw2, fb2, fw3, fb3, decay, hyena_bias):
    B, S, _ = hy_conv.shape
    na = S // DFT_R
    gf, gi, f3f, f3i = _dft_tables(S)
    fwd, bwd = _filter_taps(S, fw1, fb1, ffreq, fw2, fb2, fw3, fb3, decay)
    spec = _filter_spectrum(fwd.reshape(na, DFT_R, -1), bwd.reshape(na, DFT_R, -1),
                            _dft_tables(S, two_sided=True)[0], f3f)
    u = hy_conv.reshape(B, na, DFT_R, -1)
    nct = HY_W // HY_CT
    bias3 = hyena_bias[:, None, :]
    z1 = _long_conv(u, 0, u, nct, spec, 0, bias3, gf, f3f, f3i, gi)
    return _long_conv(z1, 0, u, 2 * nct, spec, 1, bias3, gf, f3f, f3i, gi).reshape(B, S, HY_W)


def _rope_rot_cols(w):
    half = ROPE // 2
    return jnp.concatenate([-w[..., half:], w[..., :half]], axis=-1)


def _prep_weights(w_in, w_qb, w_kvb):
    D = w_in.shape[0]
    o = Q_RANK + KV_RANK
    w_kr = w_in[:, o:o + ROPE]
    z64 = jnp.zeros((D, NOPE), F32)
    z32 = jnp.zeros((D, LANES - NOPE - ROPE), F32)
    win_aug = jnp.concatenate([
        w_in[:, :o],
        z64, w_kr, z32,
        z64, _rope_rot_cols(w_kr), z32,
        w_in[:, o + ROPE:],
    ], axis=1).astype(BF16)
    wq = w_qb.reshape(Q_RANK, MLA_HEADS, NOPE + ROPE)
    zq = jnp.zeros((Q_RANK, MLA_HEADS, LANES - NOPE - ROPE), F32)
    wq_main = jnp.concatenate([wq, zq], axis=-1).reshape(Q_RANK, MLA_HEADS * LANES).astype(BF16)
    wq_rot = jnp.concatenate([jnp.zeros((Q_RANK, MLA_HEADS, NOPE), F32), _rope_rot_cols(wq[..., NOPE:]), zq],
                             axis=-1).reshape(Q_RANK, MLA_HEADS * LANES).astype(BF16)
    wkv = w_kvb.reshape(KV_RANK, MLA_HEADS, NOPE + V_DIM)
    wk = jnp.concatenate([wkv[..., :NOPE], jnp.zeros((KV_RANK, MLA_HEADS, LANES - NOPE), F32)],
                         axis=-1).reshape(KV_RANK, MLA_HEADS * LANES).astype(BF16)
    wv = wkv[..., NOPE:].reshape(KV_RANK, MLA_HEADS * V_DIM).astype(BF16)
    return win_aug, wq_main, wq_rot, wk, wv


def kernel(x, c, positions, w_ada, b_ada, w_in, q_norm_g, w_qb, kv_norm_g, w_kvb, conv_w, conv_b, filt_w1, filt_b1, filt_freq, filt_w2, filt_b2, filt_w3, filt_b3, hyena_decay, hyena_bias, attn_out_g, hyena_out_g, w_o, ln1_g, ln1_b, w_router, w_gate, w_up, w_down, ln2_g, ln2_b):
    B, S, D = x.shape
    l = 0
    c8 = jnp.zeros((SUBLANES, D), F32).at[:B].set(c)
    mod = _ada_mod(c8, w_ada[l], b_ada[l][None, :])[:B]
    sh1, sc1, g1, sh2, sc2, g2 = [m[:, None, :] for m in jnp.split(mod, 6, axis=-1)]

    win_aug, wq_main, wq_rot, wk, wv = _prep_weights(w_in[l], w_qb[l], w_kvb[l])
    cos_h, sin_h = _rope_tables(positions)
    pad = lambda n, v: jnp.full((B, n, S), v, F32)
    cos_t = jnp.concatenate([pad(NOPE, 1.0), cos_h, cos_h, pad(LANES - NOPE - ROPE, 1.0)], axis=1)
    sin_t = jnp.concatenate([pad(NOPE, 0.0), sin_h, sin_h, pad(LANES - NOPE - ROPE, 0.0)], axis=1)
    q, kt, v, hy_conv = _inproj(x, sc1, sh1, cos_t, sin_t, win_aug, q_norm_g[l][None, :], wq_main, wq_rot,
                                kv_norm_g[l][None, :], wk, wv, conv_w[l], conv_b[l])
    a = _attention(q, kt, v)

    hy = _hyena_group(hy_conv, filt_w1[l], filt_b1[l], filt_freq[l], filt_w2[l],
                      filt_b2[l], filt_w3[l], filt_b3[l], hyena_decay[l], hyena_bias[l])

    wr_pad = jnp.concatenate([w_router[l], jnp.zeros((D, LANES - N_EXPERTS), F32)], axis=1)
    x1, u2, aff_t = _outproj(a, hy, x, g1, sc2, sh2, attn_out_g[l][None, :], hyena_out_g[l][None, :],
                             w_o[l].astype(BF16), ln1_g[l][None, :], ln1_b[l][None, :], wr_pad)

    cap = EC_FACTOR * S // N_EXPERTS
    rpe = S // LANES
    aff2 = aff_t.reshape(B, N_EXPERTS * rpe, LANES)
    pos_sel, pos_all, gsel = _select(aff2, cap)
    row_start = pos_all.reshape(B, N_EXPERTS, rpe, LANES)[:, :, :, 0]

    def block_offsets(tokens):
        ends = jnp.full((B, N_EXPERTS, 1), cap, jnp.int32)
        return jnp.concatenate([row_start[:, :, ::tokens // LANES], ends], axis=-1).reshape(-1)

    xe = _gather(block_offsets(GATHER_TOK), u2, pos_sel, cap)
    ye = _moe_ffn(xe, w_gate[l], w_up[l], w_down[l])
    return _combine_ln2(block_offsets(CMB_TOK), x1, pos_sel.reshape(B, N_EXPERTS, S), gsel.reshape(B, N_EXPERTS, S), g2, ln2_g[l][None, :], ln2_b[l][None, :],
                        ye.reshape(N_EXPERTS * B * cap, D), cap)
```

```python
import functools
import math

import jax
import jax.numpy as jnp
import numpy as np
from jax import lax
from jax.experimental import pallas as pl
from jax.experimental.pallas import tpu as pltpu

F32 = jnp.float32
BF16 = jnp.bfloat16
HIGHEST = lax.Precision.HIGHEST

LANES = 128
SUBLANES = 8
VMEM_LIMIT = 56 * 1024 * 1024

D_MODEL = 1024
MLA_HEADS = 8
NOPE = 64
ROPE = 32
V_DIM = 64
Q_RANK = 256
KV_RANK = 128
HY_W = 512
HY_COLS = 3 * HY_W
N_EXPERTS = 16
EC_FACTOR = 2
ROPE_THETA = 10000.0
EPS = 1e-5
DN_ALPHA = 2.0 ** 0.25
LOG2E = 1.4426950408889634

TOK_TILE = 512
ATT_TQ = 512
ATT_TK = 512
MOE_ROWS = 512


def _params(*sem):
    return pltpu.CompilerParams(dimension_semantics=sem, vmem_limit_bytes=VMEM_LIMIT)


def _ada_kernel(ct_ref, w_ref, b_ref, o_ref):
    ct = ct_ref[...]
    sc = ct * jax.nn.sigmoid(ct)
    w = w_ref[...]
    rows = [jnp.sum(w * sc[:, b:b + 1], axis=0, keepdims=True) for b in range(ct.shape[1])]
    o_ref[...] = jnp.concatenate(rows, axis=0) + b_ref[...]


def _ada_mod(c, w_ada, b_ada):
    B, D = c.shape
    n = w_ada.shape[1]
    tn = 1024
    return pl.pallas_call(
        _ada_kernel,
        grid=(n // tn,),
        in_specs=[
            pl.BlockSpec((D, B), lambda j: (0, 0)),
            pl.BlockSpec((D, tn), lambda j: (0, j)),
            pl.BlockSpec((1, tn), lambda j: (0, j)),
        ],
        out_specs=pl.BlockSpec((B, tn), lambda j: (0, j)),
        out_shape=jax.ShapeDtypeStruct((B, n), F32),
        compiler_params=_params("parallel"),
        name="ada_mod",
    )(c.T, w_ada, b_ada)


def _dot3(a, b):
    ah, bh = a.astype(BF16), b.astype(BF16)
    al = (a - ah.astype(F32)).astype(BF16)
    bl = (b - bh.astype(F32)).astype(BF16)
    dot = functools.partial(jnp.dot, preferred_element_type=F32)
    return dot(ah, bh) + (dot(ah, bl) + dot(al, bh))


def _rms(x, g):
    return x * lax.rsqrt(jnp.mean(x * x, axis=-1, keepdims=True) + EPS) * g


def _rope_kernel(pos_ref, freq_ref, cos_ref, sin_ref):
    ang = freq_ref[...] * pos_ref[0].astype(F32)
    cos_ref[0] = jnp.cos(ang)
    sin_ref[0] = jnp.sin(ang)


def _rope_tables(positions):
    B, S = positions.shape
    half = ROPE // 2
    inv_freq = (ROPE_THETA ** (-jnp.arange(half, dtype=F32) / half))[:, None]
    out = jax.ShapeDtypeStruct((B, half, S), F32)
    blk = pl.BlockSpec((1, half, S), lambda b: (b, 0, 0))
    return pl.pallas_call(
        _rope_kernel,
        grid=(B,),
        in_specs=[pl.BlockSpec((1, 1, S), lambda b: (b, 0, 0)), pl.BlockSpec((half, 1), lambda b: (0, 0))],
        out_specs=[blk, blk],
        out_shape=[out, out],
        compiler_params=_params("parallel"),
        name="rope_tables",
    )(positions[:, None, :], inv_freq)


def _inproj_kernel(x_ref, xp_ref, xn_ref, sc_ref, sh_ref, cos_ref, sin_ref, win_ref, qg_ref, wq_ref, wqr_ref,
                   kvg_ref, wk_ref, wv_ref, cw_ref, cb_ref, q_ref, k_ref, v_ref, hy_ref):
    def modulate(xv):
        return (xv * (1.0 + sc_ref[0]) + sh_ref[0]).astype(BF16)

    u = modulate(x_ref[0])
    cos = cos_ref[0].T
    sin = sin_ref[0].T

    def proj(lo, hi):
        return jnp.dot(u, win_ref[:, lo:hi], preferred_element_type=F32)

    o = Q_RANK + 3 * KV_RANK
    hy = proj(o, o + HY_COLS)
    tm = hy.shape[0]
    i, last = pl.program_id(1), pl.num_programs(1) - 1
    edge = jnp.dot(modulate(jnp.concatenate([xp_ref[0], xn_ref[0]], axis=0)), win_ref[:, o:o + HY_COLS],
                   preferred_element_type=F32)
    prev = jnp.where(i == 0, 0.0, edge[SUBLANES - 1:SUBLANES, :])
    nxt = jnp.where(i == last, 0.0, edge[SUBLANES:SUBLANES + 1, :])
    row = lax.broadcasted_iota(jnp.int32, hy.shape, 0)
    up = jnp.where(row == 0, prev, pltpu.roll(hy, 1, 0))
    dn = jnp.where(row == tm - 1, nxt, pltpu.roll(hy, tm - 1, 0))
    hy_ref[0] = cb_ref[...] + cw_ref[0:1, :] * up + cw_ref[1:2, :] * hy + cw_ref[2:3, :] * dn

    cq = _rms(proj(0, Q_RANK), qg_ref[...]).astype(BF16)
    q_main = jnp.dot(cq, wq_ref[...], preferred_element_type=F32)
    q_rot = jnp.dot(cq, wqr_ref[...], preferred_element_type=F32)
    qscale = LOG2E * (NOPE + ROPE) ** -0.5
    cq_s = cos * qscale
    sq_s = sin * qscale
    for h in range(MLA_HEADS):
        sl = slice(h * LANES, (h + 1) * LANES)
        q_ref[0, h] = (q_main[:, sl] * cq_s + q_rot[:, sl] * sq_s).T.astype(BF16)

    o = Q_RANK
    ckv = _rms(proj(o, o + KV_RANK), kvg_ref[...]).astype(BF16)
    kr = proj(o + KV_RANK, o + 2 * KV_RANK) * cos + proj(o + 2 * KV_RANK, o + 3 * KV_RANK) * sin
    kn = jnp.dot(ckv, wk_ref[...], preferred_element_type=F32)
    for h in range(MLA_HEADS):
        k_ref[0, h] = (kn[:, h * LANES:(h + 1) * LANES] + kr).astype(BF16)
    v_ref[0] = jnp.dot(ckv, wv_ref[...], preferred_element_type=F32).T.astype(BF16)


def _inproj(x, sc1, sh1, cos_t, sin_t, win_aug, qg, wq, wqr, kvg, wk, wv, conv_w, conv_b):
    B, S, D = x.shape
    tm = TOK_TILE
    ncol = win_aug.shape[1]
    g8 = tm // SUBLANES
    n8 = S // SUBLANES
    full = lambda shape: pl.BlockSpec(shape, lambda b, i: (0,) * len(shape))
    return pl.pallas_call(
        _inproj_kernel,
        grid=(B, S // tm),
        in_specs=[
            pl.BlockSpec((1, tm, D), lambda b, i: (b, i, 0)),
            pl.BlockSpec((1, SUBLANES, D), lambda b, i: (b, jnp.maximum(i * g8 - 1, 0), 0)),
            pl.BlockSpec((1, SUBLANES, D), lambda b, i: (b, jnp.minimum((i + 1) * g8, n8 - 1), 0)),
            pl.BlockSpec((1, 1, D), lambda b, i: (b, 0, 0)),
            pl.BlockSpec((1, 1, D), lambda b, i: (b, 0, 0)),
            pl.BlockSpec((1, LANES, tm), lambda b, i: (b, 0, i)),
            pl.BlockSpec((1, LANES, tm), lambda b, i: (b, 0, i)),
            full((D, ncol)),
            full((1, Q_RANK)),
            full((Q_RANK, MLA_HEADS * LANES)),
            full((Q_RANK, MLA_HEADS * LANES)),
            full((1, KV_RANK)),
            full((KV_RANK, MLA_HEADS * LANES)),
            full((KV_RANK, MLA_HEADS * V_DIM)),
            full(conv_w.shape),
            full((1, HY_COLS)),
        ],
        out_specs=[
            pl.BlockSpec((1, MLA_HEADS, LANES, tm), lambda b, i: (b, 0, 0, i)),
            pl.BlockSpec((1, MLA_HEADS, tm, LANES), lambda b, i: (b, 0, i, 0)),
            pl.BlockSpec((1, MLA_HEADS * V_DIM, tm), lambda b, i: (b, 0, i)),
            pl.BlockSpec((1, tm, HY_COLS), lambda b, i: (b, i, 0)),
        ],
        out_shape=[
            jax.ShapeDtypeStruct((B, MLA_HEADS, LANES, S), BF16),
            jax.ShapeDtypeStruct((B, MLA_HEADS, S, LANES), BF16),
            jax.ShapeDtypeStruct((B, MLA_HEADS * V_DIM, S), BF16),
            jax.ShapeDtypeStruct((B, S, HY_COLS), F32),
        ],
        compiler_params=_params("parallel", "parallel"),
        name="in_proj",
    )(x, x, x, sc1, sh1, cos_t, sin_t, win_aug, qg, wq, wqr, kvg, wk, wv, conv_w, conv_b[None, :])


ATT_AUG = 16


def _attn_kernel(qt_ref, k_ref, vt_ref, o_ref, s_scr):
    S = k_ref.shape[2]
    tq = ATT_TQ
    nk = S // ATT_TK
    nq = S // tq
    ones_rows = jnp.where(lax.broadcasted_iota(jnp.int32, (ATT_AUG, ATT_TK), 0) == 0, 1.0, 0.0).astype(BF16)

    def scores(qi, c, slot):
        k0 = pl.multiple_of(c * ATT_TK, ATT_TK)
        q0 = pl.multiple_of(qi * tq, tq)
        for h in range(2):
            s_scr[slot, h] = jnp.dot(k_ref[0, h, pl.ds(k0, ATT_TK), :], qt_ref[0, h, :, pl.ds(q0, tq)],
                                     preferred_element_type=F32)

    def absorb(c, slot, carry):
        k0 = pl.multiple_of(c * ATT_TK, ATT_TK)
        out = []
        for h in range(2):
            m_prev, acc = carry[h]
            s = s_scr[slot, h]
            m_new = jnp.maximum(m_prev, jnp.max(s, axis=0, keepdims=True))
            alpha = jnp.exp2(m_prev - m_new)
            p = jnp.exp2(s - m_new).astype(BF16)
            v_aug = jnp.concatenate([vt_ref[0, h * V_DIM:(h + 1) * V_DIM, pl.ds(k0, ATT_TK)], ones_rows], axis=0)
            out.append((m_new, alpha * acc + jnp.dot(v_aug, p, preferred_element_type=F32)))
        return tuple(out)

    def query_tile(qi, _):
        def pair(i, carry):
            c0 = 2 * i
            scores(qi, c0 + 1, 1)
            carry = absorb(c0, 0, carry)
            wrap = c0 + 2 >= nk
            scores(jnp.where(wrap, jnp.minimum(qi + 1, nq - 1), qi), jnp.where(wrap, 0, c0 + 2), 0)
            return absorb(c0 + 1, 1, carry)

        init = tuple((jnp.full((1, tq), -jnp.inf, F32), jnp.zeros((V_DIM + ATT_AUG, tq), F32)) for _ in range(2))
        fin = lax.fori_loop(0, nk // 2, pair, init, unroll=4)
        o = jnp.concatenate([acc[:V_DIM] * (1.0 / acc[V_DIM:V_DIM + 1]) for (_, acc) in fin], axis=0)
        o_ref[0, pl.ds(pl.multiple_of(qi * tq, tq), tq), :] = o.T
        return 0

    scores(0, 0, 0)
    lax.fori_loop(0, nq, query_tile, 0)


def _attention(qt, k, vt):
    B, H, _, S = qt.shape
    return pl.pallas_call(
        _attn_kernel,
        grid=(B, H // 2),
        in_specs=[
            pl.BlockSpec((1, 2, LANES, S), lambda b, hp: (b, hp, 0, 0)),
            pl.BlockSpec((1, 2, S, LANES), lambda b, hp: (b, hp, 0, 0)),
            pl.BlockSpec((1, 2 * V_DIM, S), lambda b, hp: (b, hp, 0)),
        ],
        out_specs=pl.BlockSpec((1, S, 2 * V_DIM), lambda b, hp: (b, 0, hp)),
        out_shape=jax.ShapeDtypeStruct((B, S, H * V_DIM), F32),
        scratch_shapes=[pltpu.VMEM((2, 2, ATT_TK, ATT_TQ), F32)],
        compiler_params=_params("parallel", "parallel"),
        name="mla_attention",
    )(qt, k, vt)


def _layer_norm(x, g, b):
    mu = jnp.mean(x, axis=-1, keepdims=True)
    xc = x - mu
    var = jnp.mean(xc * xc, axis=-1, keepdims=True)
    return xc * lax.rsqrt(var + EPS) * g + b


def _outproj_kernel(a_ref, hy_ref, x_ref, g1_ref, sc2_ref, sh2_ref, ag_ref, hg_ref, wo_ref,
                    lg_ref, lb_ref, wr_ref, x1_ref, u2_ref, aff_ref):
    half = a_ref.shape[2]
    tm = a_ref.shape[1]
    nsplit = 2
    for r0 in range(0, tm, tm // nsplit):
        rs = slice(r0, r0 + tm // nsplit)
        an = _rms(a_ref[0, rs, :], ag_ref[...]).astype(BF16)
        hn = _rms(hy_ref[0, rs, :], hg_ref[...]).astype(BF16)
        mixed = (jnp.dot(an, wo_ref[:half, :], preferred_element_type=F32)
                 + jnp.dot(hn, wo_ref[half:, :], preferred_element_type=F32))
        x1 = _layer_norm(DN_ALPHA * x_ref[0, rs, :] + g1_ref[0] * mixed, lg_ref[...], lb_ref[...])
        x1_ref[0, rs, :] = x1
        u2 = x1 * (1.0 + sc2_ref[0]) + sh2_ref[0]
        u2_ref[0, rs, :] = u2.astype(BF16)
        logits = _dot3(u2, wr_ref[...])
        lane = lax.broadcasted_iota(jnp.int32, logits.shape, 1)
        logits = jnp.where(lane < N_EXPERTS, logits, -jnp.inf)
        e = jnp.exp(logits - jnp.max(logits, axis=-1, keepdims=True))
        aff = e / jnp.sum(e, axis=-1, keepdims=True)
        aff_ref[0, :, rs] = aff.T[:N_EXPERTS, :]


def _outproj(a, hy, x, g1, sc2, sh2, ag, hg, wo, lg, lb, wr_pad):
    B, S, D = x.shape
    tm = TOK_TILE
    half = a.shape[2]
    full = lambda shape: pl.BlockSpec(shape, lambda b, i: (0,) * len(shape))
    tok = lambda w: pl.BlockSpec((1, tm, w), lambda b, i: (b, i, 0))
    per_b = pl.BlockSpec((1, 1, D), lambda b, i: (b, 0, 0))
    return pl.pallas_call(
        _outproj_kernel,
        grid=(B, S // tm),
        in_specs=[tok(half), tok(half), tok(D), per_b, per_b, per_b,
                  full((1, half)), full((1, half)), full((D, D)), full((1, D)), full((1, D)),
                  full((D, LANES))],
        out_specs=[tok(D), tok(D), pl.BlockSpec((1, N_EXPERTS, tm), lambda b, i: (b, 0, i))],
        out_shape=[
            jax.ShapeDtypeStruct((B, S, D), F32),
            jax.ShapeDtypeStruct((B, S, D), BF16),
            jax.ShapeDtypeStruct((B, N_EXPERTS, S), F32),
        ],
        compiler_params=_params("parallel", "parallel"),
        name="out_proj_ln1_router",
    )(a, hy, x, g1, sc2, sh2, ag, hg, wo, lg, lb, wr_pad)


def _select_kernel(aff_ref, ones_ref, utri_ref, ltri_ref, pos_ref, posall_ref, g_ref, *, cap):
    aff = aff_ref[0]
    ones = ones_ref[...]

    def rowsum(mf):
        return jnp.dot(mf, ones, preferred_element_type=F32)

    def expert_total(mf):
        x = mf.astype(F32).reshape(N_EXPERTS, aff.shape[0] // N_EXPERTS, LANES)
        tot = jnp.sum(jnp.sum(x, axis=1, keepdims=True), axis=2, keepdims=True)
        return jnp.broadcast_to(tot, x.shape).reshape(aff.shape)

    def prefix(mf):
        return (jnp.dot(mf, utri_ref[...], preferred_element_type=F32)
                + jnp.dot(ltri_ref[...], rowsum(mf).astype(BF16), preferred_element_type=F32))

    def ind(m):
        return jnp.where(m, 1.0, 0.0).astype(BF16)

    def search(i, thr_bits):
        cand = thr_bits | jnp.left_shift(jnp.int32(1), 30 - i)
        enough = expert_total(ind(aff >= lax.bitcast_convert_type(cand, F32))) >= cap
        return jnp.where(enough, cand, thr_bits)

    thr = lax.bitcast_convert_type(lax.fori_loop(0, 31, search, jnp.zeros(aff.shape, jnp.int32)), F32)
    gt = aff > thr
    eq = aff == thr
    room = cap - expert_total(ind(gt))
    take_eq = jnp.where(eq, jnp.where(prefix(ind(eq)) < room, 1.0, 0.0), 0.0)
    sel = jnp.where(gt, 1.0, take_eq)
    pos = prefix(sel.astype(BF16)).astype(jnp.int32)
    posall_ref[0] = pos
    pos_ref[0] = jnp.where(sel > 0.0, pos, -1)
    g_ref[0] = jnp.where(sel > 0.0, aff, 0.0)


def _select(aff2, cap):
    B, R, _ = aff2.shape
    rpe = R // N_EXPERTS
    li = jnp.arange(LANES, dtype=jnp.int32)
    ri = jnp.arange(R, dtype=jnp.int32)
    same = (ri[:, None] // rpe) == (ri[None, :] // rpe)
    ones = jnp.ones((LANES, LANES), BF16)
    utri = (li[:, None] < li[None, :]).astype(BF16)
    ltri = (same & (ri[None, :] < ri[:, None])).astype(BF16)
    blk = pl.BlockSpec((1, R, LANES), lambda b: (b, 0, 0))
    full = lambda arr: pl.BlockSpec(arr.shape, lambda b: (0, 0))
    return pl.pallas_call(
        functools.partial(_select_kernel, cap=cap),
        grid=(B,),
        in_specs=[blk, full(ones), full(utri), full(ltri)],
        out_specs=[blk, blk, blk],
        out_shape=[jax.ShapeDtypeStruct((B, R, LANES), jnp.int32), jax.ShapeDtypeStruct((B, R, LANES), jnp.int32),
                   jax.ShapeDtypeStruct((B, R, LANES), F32)],
        compiler_params=_params("parallel"),
        name="expert_choice_select",
    )(aff2, ones, utri, ltri)


GATHER_TOK = 1024
GATHER_WIN = 160


def _gather_kernel(offs_ref, u_ref, pos_ref, xe_ref, acc, *, cap, nblk):
    b, e = pl.program_id(0), pl.program_id(1)
    base = (b * N_EXPERTS + e) * (nblk + 1)
    rows = GATHER_TOK // LANES
    acc[...] = jnp.zeros(acc.shape, F32)
    slot = lax.broadcasted_iota(jnp.int32, (GATHER_WIN, LANES), 0)

    def block(j, carry):
        off, end = offs_ref[base + j], offs_ref[base + j + 1]
        w = (off // SUBLANES) * SUBLANES
        prow = pos_ref[0, pl.ds(pl.multiple_of(j * rows, rows), rows), :]
        ublk = u_ref[0, pl.ds(pl.multiple_of(j * GATHER_TOK, GATHER_TOK), GATHER_TOK), :]

        def window(i, c):
            s0 = pl.multiple_of(w + i * GATHER_WIN, SUBLANES)
            onehot = jnp.concatenate(
                [jnp.where(slot == prow[r:r + 1, :] - s0, 1.0, 0.0).astype(BF16) for r in range(rows)], axis=1)
            acc[pl.ds(s0, GATHER_WIN), :] += jnp.dot(onehot, ublk, preferred_element_type=F32)
            return c

        window(0, 0)
        lax.fori_loop(1, (end - w + GATHER_WIN - 1) // GATHER_WIN, window, 0)
        return carry

    lax.fori_loop(0, nblk, block, 0, unroll=2)
    xe_ref[0] = acc[:cap, :].astype(BF16)


def _gather(offs, u2, pos_sel, cap):
    B, S, D = u2.shape
    rpe = S // LANES
    nblk = S // GATHER_TOK
    return pl.pallas_call(
        functools.partial(_gather_kernel, cap=cap, nblk=nblk),
        grid_spec=pltpu.PrefetchScalarGridSpec(
            num_scalar_prefetch=1,
            grid=(B, N_EXPERTS),
            in_specs=[pl.BlockSpec((1, S, D), lambda b, e, o: (b, 0, 0)),
                      pl.BlockSpec((1, rpe, LANES), lambda b, e, o: (b, e, 0))],
            out_specs=pl.BlockSpec((1, cap, D), lambda b, e, o: (e, b, 0)),
            scratch_shapes=[pltpu.VMEM((cap + GATHER_WIN + SUBLANES, D), F32)],
        ),
        out_shape=jax.ShapeDtypeStruct((N_EXPERTS, B * cap, D), BF16),
        compiler_params=_params("parallel", "parallel"),
        name="expert_gather",
    )(offs, u2, pos_sel)


def _moe_kernel(x_ref, wg_ref, wu_ref, wd_ref, o_ref, wgb, wub, wdb):
    @pl.when(pl.program_id(1) == 0)
    def _():
        wgb[...] = wg_ref[0].astype(BF16)
        wub[...] = wu_ref[0].astype(BF16)
        wdb[...] = wd_ref[0].astype(BF16)

    x = x_ref[0]
    hg = jnp.dot(x, wgb[...], preferred_element_type=F32)
    hu = jnp.dot(x, wub[...], preferred_element_type=F32)
    h = (hg * jax.nn.sigmoid(hg) * hu).astype(BF16)
    o_ref[0] = jnp.dot(h, wdb[...], preferred_element_type=F32).astype(BF16)


def _moe_ffn(xe, w_gate, w_up, w_down):
    E, R, D = xe.shape
    Fh = w_gate.shape[2]
    tr = MOE_ROWS
    return pl.pallas_call(
        _moe_kernel,
        grid=(E, R // tr),
        in_specs=[
            pl.BlockSpec((1, tr, D), lambda e, r: (e, r, 0)),
            pl.BlockSpec((1, D, Fh), lambda e, r: (e, 0, 0)),
            pl.BlockSpec((1, D, Fh), lambda e, r: (e, 0, 0)),
            pl.BlockSpec((1, Fh, D), lambda e, r: (e, 0, 0)),
        ],
        out_specs=pl.BlockSpec((1, tr, D), lambda e, r: (e, r, 0)),
        out_shape=jax.ShapeDtypeStruct((E, R, D), BF16),
        scratch_shapes=[pltpu.VMEM((D, Fh), BF16), pltpu.VMEM((D, Fh), BF16), pltpu.VMEM((Fh, D), BF16)],
        compiler_params=_params("parallel", "arbitrary"),
        name="moe_swiglu",
    )(xe, w_gate, w_up, w_down)


CMB_TOK = 512
CMB_WIN = 128
BF16_ROWS = 16


def _combine_kernel(offs_ref, x1_ref, pos_ref, g_ref, g2_ref, lg_ref, lb_ref, ye_hbm, o_ref,
                    rhs, lhs, acc, sem, *, cap, nblk, nbatch):
    b, i = pl.program_id(0), pl.program_id(1)
    W = CMB_WIN
    total = ye_hbm.shape[0]
    step = b * nblk + i
    cur_slot = step % 2
    row_e = lax.broadcasted_iota(jnp.int32, (N_EXPERTS, 1), 0)
    slot_w = lax.broadcasted_iota(jnp.int32, (W, CMB_TOK), 0)

    def windows(bb, ii):
        out = []
        for e in range(N_EXPERTS):
            k = (bb * N_EXPERTS + e) * (nblk + 1) + ii
            off, end = offs_ref[k], offs_ref[k + 1]
            lo = (off // BF16_ROWS) * BF16_ROWS
            out.append(((e * nbatch + bb) * cap, lo, jnp.where(end > off, (end - lo + W - 1) // W, 0)))
        return out

    def fetch_starts(win, p):
        return [jnp.minimum(rb + lo + p * W, total - W) for rb, lo, _ in win]

    def window_copy(e, start, s):
        return pltpu.make_async_copy(ye_hbm.at[pl.ds(pl.multiple_of(start, BF16_ROWS), W), :],
                                     rhs.at[s, pl.ds(e * W, W), :], sem.at[s, e])

    def start_fetch(win, p, s):
        for e, start in enumerate(fetch_starts(win, p)):
            window_copy(e, start, s).start()

    def absorb(win, p, s, first):
        starts = fetch_starts(win, p)
        lo_col = jnp.zeros((N_EXPERTS, 1), jnp.int32)
        shift_col = jnp.zeros((N_EXPERTS, 1), jnp.int32)
        for e, (rb, lo0, _) in enumerate(win):
            lo = lo0 + p * W
            lo_col = jnp.where(row_e == e, lo, lo_col)
            shift_col = jnp.where(row_e == e, rb + lo - starts[e], shift_col)
        d = pos_ref[0] - lo_col
        rel = jnp.where((d >= 0) & (d < W), d + shift_col, -1)
        g = g_ref[0]
        for e in range(N_EXPERTS):
            lhs[e * W:(e + 1) * W, :] = jnp.where(slot_w == rel[e:e + 1, :], g[e:e + 1, :], 0.0).astype(BF16)
        for e in range(N_EXPERTS):
            window_copy(e, starts[e], s).wait()
        part = lax.dot_general(lhs[...], rhs[s], (((0,), (0,)), ((), ())), preferred_element_type=F32)
        acc[...] = part if first else acc[...] + part

    cur = windows(b, i)

    @pl.when(step == 0)
    def _():
        start_fetch(cur, 0, 0)

    @pl.when(step + 1 < nbatch * nblk)
    def _():
        wrap = i + 1 == nblk
        start_fetch(windows(jnp.where(wrap, b + 1, b), jnp.where(wrap, 0, i + 1)), 0, 1 - cur_slot)

    absorb(cur, 0, cur_slot, True)

    def extra_pass(p, carry):
        start_fetch(cur, p, cur_slot)
        absorb(cur, p, cur_slot, False)
        return carry

    npass = functools.reduce(jnp.maximum, [n for _, _, n in cur])
    lax.fori_loop(1, npass, extra_pass, 0)
    o_ref[0] = _layer_norm(DN_ALPHA * x1_ref[0] + g2_ref[0] * acc[...], lg_ref[...], lb_ref[...])


def _combine_ln2(offs, x1, pos_exp, g_exp, g2, lg, lb, ye_rows, cap):
    B, S, D = x1.shape
    T = CMB_TOK
    nblk = S // T
    tok = lambda w: pl.BlockSpec((1, T, w), lambda b, i, o: (b, i, 0))
    vec = pl.BlockSpec((1, D), lambda b, i, o: (0, 0))
    exp_major = pl.BlockSpec((1, N_EXPERTS, T), lambda b, i, o: (b, 0, i))
    return pl.pallas_call(
        functools.partial(_combine_kernel, cap=cap, nblk=nblk, nbatch=B),
        grid_spec=pltpu.PrefetchScalarGridSpec(
            num_scalar_prefetch=1,
            grid=(B, nblk),
            in_specs=[tok(D), exp_major, exp_major,
                      pl.BlockSpec((1, 1, D), lambda b, i, o: (b, 0, 0)), vec, vec,
                      pl.BlockSpec(memory_space=pl.ANY)],
            out_specs=tok(D),
            scratch_shapes=[pltpu.VMEM((2, N_EXPERTS * CMB_WIN, D), BF16),
                            pltpu.VMEM((N_EXPERTS * CMB_WIN, T), BF16),
                            pltpu.VMEM((T, D), F32),
                            pltpu.SemaphoreType.DMA((2, N_EXPERTS))],
        ),
        out_shape=jax.ShapeDtypeStruct((B, S, D), F32),
        compiler_params=_params("arbitrary", "arbitrary"),
        name="combine_residual_ln2",
    )(offs, x1, pos_exp, g_exp, g2, lg, lb, ye_rows)


DFT_R = 128
K1_HALF = 40
K1_VALID = DFT_R // 2 + 1
K1_GROUPS = (K1_HALF // SUBLANES, -(-(K1_VALID - K1_HALF) // SUBLANES))
K1_SPEC = K1_HALF + K1_GROUPS[1] * SUBLANES
HY_CT = 128


@functools.lru_cache(maxsize=None)
def _dft_tables_np(S, two_sided):
    N = 2 * S
    half = np.arange(S // DFT_R)
    a_idx = np.concatenate([half, N // DFT_R - 1 - half]) if two_sided else half
    r = np.arange(2 * K1_HALF)
    k1 = np.arange(2)[:, None] * K1_HALF + r[None, :] % K1_HALF
    valid = (k1 < K1_VALID).astype(np.float64)
    n = DFT_R * a_idx[None, :] + np.arange(DFT_R)[:, None]
    theta = ((n[None, :, None, :] * k1[:, None, :, None]) % N) * (2.0 * math.pi / N)
    is_im = (r >= K1_HALF)[None, None, :, None]
    g = np.where(is_im, -np.sin(theta), np.cos(theta)) * valid[:, None, :, None]
    cw = np.where((k1 == 0) | (k1 == DFT_R // 2), 1.0, 2.0) * valid / N
    gi = np.transpose(g * cw[:, None, :, None], (0, 1, 3, 2))
    ph = ((np.arange(DFT_R)[:, None] * np.arange(DFT_R)[None, :]) % DFT_R) * (2.0 * math.pi / DFT_R)
    C, Sn = np.cos(ph), np.sin(ph)
    f3f = np.block([[C, Sn], [-Sn, C]])
    f3i = np.block([[C, -Sn], [Sn, C]])

    def pair_lanes(t):
        h, nb, nr, nc = t.shape
        return t.reshape(h, nb // 2, 2, nr, nc).transpose(0, 1, 3, 2, 4).reshape(h, nb // 2, nr, 2 * nc)

    return tuple(np.ascontiguousarray(t, dtype=np.float32) for t in (pair_lanes(g), pair_lanes(gi), f3f, f3i))


def _dft_tables(S, two_sided=False):
    return tuple(jnp.asarray(t).astype(BF16) for t in _dft_tables_np(S, two_sided))


def _swap01(x):
    return jnp.swapaxes(x, 0, 1)


def _block_diag_lanes(top, bot):
    return jnp.concatenate([jnp.concatenate([top, jnp.zeros((top.shape[0], bot.shape[1]), top.dtype)], axis=1),
                            jnp.concatenate([jnp.zeros((bot.shape[0], top.shape[1]), bot.dtype), bot], axis=1)], axis=0)


def _dft_stage1(load, gf_ref, hf, a_scr):
    ct = a_scr.shape[2]

    def body(g, carry):
        b0 = pl.multiple_of(g * SUBLANES, SUBLANES)
        zt = load(b0)
        for p in range(SUBLANES // 2):
            a2 = jnp.dot(gf_ref[hf, g * (SUBLANES // 2) + p], _block_diag_lanes(zt[2 * p], zt[2 * p + 1]),
                         preferred_element_type=F32)
            a_scr[b0 + 2 * p] = a2[:, :ct]
            a_scr[b0 + 2 * p + 1] = a2[:, ct:]
        return carry

    lax.fori_loop(0, DFT_R // SUBLANES, body, 0, unroll=4)


def _dft_stage3_operands(a_scr, r0):
    mr = _swap01(a_scr[:, pl.ds(r0, SUBLANES), :])
    mi = _swap01(a_scr[:, pl.ds(K1_HALF + r0, SUBLANES), :])
    return [jnp.concatenate([jnp.concatenate([mr[j], mi[j]], axis=0),
                             jnp.concatenate([mr[j + 1], mi[j + 1]], axis=0)], axis=1).astype(BF16)
            for j in range(0, SUBLANES, 2)]


def _filter_mlp_kernel(feat_ref, featx_ref, w1_ref, b1_ref, fr_ref, w2_ref, b2_ref, w3_ref, b3_ref, dec_ref,
                       of_ref, ob_ref):
    tl = feat_ref.shape[0]
    ncol = of_ref.shape[1]
    fr = fr_ref[...]

    def hidden(f):
        h = jnp.sin(fr * (jnp.dot(f, w1_ref[...], precision=HIGHEST, preferred_element_type=F32) + b1_ref[...]))
        return jnp.sin(fr * (jnp.dot(h, w2_ref[...], precision=HIGHEST, preferred_element_type=F32) + b2_ref[...]))

    def taps(h2, t, lo):
        out = _dot3(h2, w3_ref[:, lo:lo + ncol]) + b3_ref[:, lo:lo + ncol]
        return out * jnp.exp(-t * jnp.abs(dec_ref[:, lo:lo + ncol]))

    feats, featx = feat_ref[...], featx_ref[...]
    h2, h2x = hidden(feats), hidden(featx)
    of_ref[...] = taps(h2, feats[:, 0:1], 0)
    row = lax.broadcasted_iota(jnp.int32, (tl, 1), 0)
    last = row == tl - 1
    h2n = jnp.where(last, h2x[0:1, :], pltpu.roll(h2, tl - 1, 0))
    tn = jnp.where(last, featx[0:1, :], pltpu.roll(feats, tl - 1, 0))[:, 0:1]
    ob_ref[...] = jnp.where(last & (pl.program_id(0) == pl.num_programs(0) - 1), 0.0, taps(h2n, tn, ncol))


@functools.lru_cache(maxsize=None)
def _filter_features_np(L, nf):
    pos = np.arange(L + SUBLANES, dtype=np.float64)
    t = pos[:, None] / (L - 1)
    bands = (nf - 1) // 2
    freqs = np.linspace(1e-4, bands - 1, bands)
    phase = (2.0 * math.pi / L) * pos[:, None] * freqs[None, :]
    feats = np.concatenate([t, np.cos(phase), -np.sin(phase), np.zeros((L + SUBLANES, LANES - nf))], axis=-1)
    return np.ascontiguousarray(feats, dtype=np.float32)


def _filter_taps(L, w1, b1, freq, w2, b2, w3, b3, decay):
    nf = w1.shape[0]
    feats = jnp.asarray(_filter_features_np(L, nf))
    w1p = jnp.concatenate([w1, jnp.zeros((LANES - nf, w1.shape[1]), F32)], axis=0)
    fo = w1.shape[1]
    ncol = w3.shape[1] // 2
    tl = 512
    full = lambda shape: pl.BlockSpec(shape, lambda i: (0,) * len(shape))
    out = jax.ShapeDtypeStruct((L, ncol), F32)
    blk = pl.BlockSpec((tl, ncol), lambda i: (i, 0))
    return pl.pallas_call(
        _filter_mlp_kernel,
        grid=(L // tl,),
        in_specs=[pl.BlockSpec((tl, LANES), lambda i: (i, 0)),
                  pl.BlockSpec((SUBLANES, LANES), lambda i: ((i + 1) * (tl // SUBLANES), 0)),
                  full((LANES, fo)), full((1, fo)), full((1, fo)), full((fo, fo)), full((1, fo)),
                  full((fo, 2 * ncol)), full((1, 2 * ncol)), full((1, 2 * ncol))],
        out_specs=[blk, blk],
        out_shape=[out, out],
        compiler_params=_params("parallel"),
        name="hyena_filter_mlp",
    )(feats, feats, w1p, b1[None, :], freq[None, :], w2, b2[None, :], w3, b3[None, :], decay.reshape(1, 2 * ncol))


def _filter_spectrum_kernel(hf_ref, hb_ref, gf_ref, f3f_ref, o_ref, a_scr):
    ct = a_scr.shape[2]

    def load(b0):
        zf = _swap01(hf_ref[:, pl.ds(b0, SUBLANES), :])
        zb = _swap01(hb_ref[:, pl.ds(pl.multiple_of(DFT_R - SUBLANES - b0, SUBLANES), SUBLANES), :])
        return [jnp.concatenate([zf[j], zb[SUBLANES - 1 - j]], axis=0).astype(BF16) for j in range(SUBLANES)]

    for hf in range(2):
        _dft_stage1(load, gf_ref, hf, a_scr)

        def body(g, carry):
            r0 = pl.multiple_of(g * SUBLANES, SUBLANES)
            for p, m in enumerate(_dft_stage3_operands(a_scr, r0)):
                x2 = jnp.dot(f3f_ref[...], m, preferred_element_type=F32)
                for q in range(2):
                    o_ref[0, hf * K1_HALF + r0 + 2 * p + q] = x2[:, q * ct:(q + 1) * ct].astype(BF16)
            return carry

        lax.fori_loop(0, K1_GROUPS[hf], body, 0, unroll=True)


def _filter_spectrum(fwd3, bwd3, gf, f3f):
    na, nb, ncol = fwd3.shape
    C = ncol // 2
    nct = C // HY_CT
    ct = HY_CT
    taps = pl.BlockSpec((na, nb, ct), lambda o, j: (0, 0, o * nct + j))
    return pl.pallas_call(
        _filter_spectrum_kernel,
        grid=(2, nct),
        in_specs=[
            taps, taps,
            pl.BlockSpec(gf.shape, lambda o, j: (0, 0, 0, 0), pipeline_mode=pl.Buffered(1)),
            pl.BlockSpec(f3f.shape, lambda o, j: (0, 0), pipeline_mode=pl.Buffered(1)),
        ],
        out_specs=pl.BlockSpec((1, K1_SPEC, 2 * DFT_R, ct), lambda o, j: (o, 0, 0, j)),
        out_shape=jax.ShapeDtypeStruct((2, K1_SPEC, 2 * DFT_R, C), BF16),
        scratch_shapes=[pltpu.VMEM((DFT_R, 2 * K1_HALF, ct), F32)],
        compiler_params=_params("parallel", "parallel"),
        name="hyena_filter_spectrum",
    )(fwd3, bwd3, gf, f3f)


def _long_conv_kernel(z_ref, xg_ref, h_ref, bias_ref, gf_ref, f3f_ref, f3i_ref, gi_ref, o_ref, a_scr, b_scr):
    ct = a_scr.shape[2]
    bias = bias_ref[0]
    for hf in range(2):
        _dft_stage1(lambda b0: _swap01(z_ref[0, :, pl.ds(b0, SUBLANES), :]).astype(BF16), gf_ref, hf, a_scr)
        ngrp = K1_GROUPS[hf]
        if ngrp * SUBLANES < K1_HALF:
            b_scr[ngrp * SUBLANES:] = jnp.zeros((K1_HALF - ngrp * SUBLANES,) + b_scr.shape[1:], F32)

        def spectral(g, carry):
            r0 = pl.multiple_of(g * SUBLANES, SUBLANES)
            for p, m in enumerate(_dft_stage3_operands(a_scr, r0)):
                x = jnp.dot(f3f_ref[...], m, preferred_element_type=F32)
                k1 = hf * K1_HALF + r0 + 2 * p
                h = jnp.concatenate([h_ref[0, k1], h_ref[0, k1 + 1]], axis=1).astype(F32)
                xr, xi, hr, hi = x[:DFT_R], x[DFT_R:], h[:DFT_R], h[DFT_R:]
                y = jnp.concatenate([xr * hr - xi * hi, xr * hi + xi * hr], axis=0).astype(BF16)
                bb = jnp.dot(f3i_ref[...], y, preferred_element_type=F32)
                b_scr[r0 + 2 * p] = bb[:, :ct]
                b_scr[r0 + 2 * p + 1] = bb[:, ct:]
            return carry

        lax.fori_loop(0, ngrp, spectral, 0, unroll=True)

        def inverse(g, carry):
            b0 = pl.multiple_of(g * SUBLANES, SUBLANES)
            br = _swap01(b_scr[:, pl.ds(b0, SUBLANES), :])
            bi = _swap01(b_scr[:, pl.ds(DFT_R + b0, SUBLANES), :])
            ys = []
            for p in range(SUBLANES // 2):
                bm = [jnp.concatenate([br[2 * p + q], bi[2 * p + q]], axis=0).astype(BF16) for q in range(2)]
                y2 = jnp.dot(gi_ref[hf, g * (SUBLANES // 2) + p], _block_diag_lanes(bm[0], bm[1]),
                             preferred_element_type=F32)
                ys += [y2[:, :ct], y2[:, ct:]]
            y = _swap01(jnp.stack(ys, axis=0))
            sl = (0, slice(None), pl.ds(b0, SUBLANES), slice(None))
            if hf == 0:
                o_ref[sl] = y
            else:
                o_ref[sl] = xg_ref[sl] * (o_ref[sl] + y + z_ref[sl] * bias)
            return carry

        lax.fori_loop(0, DFT_R // SUBLANES, inverse, 0, unroll=4)


def _long_conv(z_arr, z_off, xg_arr, xg_off, spec, order, bias3, gf, f3f, f3i, gi):
    B, na, nb, _ = z_arr.shape
    C = spec.shape[3]
    ct = HY_CT
    const = lambda arr: pl.BlockSpec(arr.shape, lambda j, b: (0,) * arr.ndim, pipeline_mode=pl.Buffered(1))
    return pl.pallas_call(
        _long_conv_kernel,
        grid=(C // ct, B),
        in_specs=[
            pl.BlockSpec((1, na, nb, ct), lambda j, b: (b, 0, 0, z_off + j)),
            pl.BlockSpec((1, na, nb, ct), lambda j, b: (b, 0, 0, xg_off + j)),
            pl.BlockSpec((1, K1_SPEC, 2 * DFT_R, ct), lambda j, b: (order, 0, 0, j), pipeline_mode=pl.Buffered(1)),
            pl.BlockSpec((1, 1, ct), lambda j, b: (order, 0, j)),
            const(gf), const(f3f), const(f3i), const(gi),
        ],
        out_specs=pl.BlockSpec((1, na, nb, ct), lambda j, b: (b, 0, 0, j)),
        out_shape=jax.ShapeDtypeStruct((B, na, nb, C), F32),
        scratch_shapes=[pltpu.VMEM((DFT_R, 2 * K1_HALF, ct), F32), pltpu.VMEM((K1_HALF, 2 * DFT_R, ct), F32)],
        compiler_params=_params("parallel", "parallel"),
        name="hyena_long_conv",
    )(z_arr, xg_arr, spec, bias3, gf, f3f, f3i, gi)


def _hyena_group(hy_conv, fw1, fb1, ffreq, fw2, fb2, fw3, fb3, decay, hyena_bias):
    B, S, _ = hy_conv.shape
    na = S // DFT_R
    gf, gi, f3f, f3i = _dft_tables(S)
    fwd, bwd = _filter_taps(S, fw1, fb1, ffreq, fw2, fb2, fw3, fb3, decay)
    spec = _filter_spectrum(fwd.reshape(na, DFT_R, -1), bwd.reshape(na, DFT_R, -1),
                            _dft_tables(S, two_sided=True)[0], f3f)
    u = hy_conv.reshape(B, na, DFT_R, -1)
    nct = HY_W // HY_CT
    bias3 = hyena_bias[:, None, :]
    z1 = _long_conv(u, 0, u, nct, spec, 0, bias3, gf, f3f, f3i, gi)
    return _long_conv(z1, 0, u, 2 * nct, spec, 1, bias3, gf, f3f, f3i, gi).reshape(B, S, HY_W)


def _rope_rot_cols(w):
    half = ROPE // 2
    return jnp.concatenate([-w[..., half:], w[..., :half]], axis=-1)


def _prep_weights(w_in, w_qb, w_kvb):
    D = w_in.shape[0]
    o = Q_RANK + KV_RANK
    w_kr = w_in[:, o:o + ROPE]
    z64 = jnp.zeros((D, NOPE), F32)
    z32 = jnp.zeros((D, LANES - NOPE - ROPE), F32)
    win_aug = jnp.concatenate([
        w_in[:, :o],
        z64, w_kr, z32,
        z64, _rope_rot_cols(w_kr), z32,
        w_in[:, o + ROPE:],
    ], axis=1).astype(BF16)
    wq = w_qb.reshape(Q_RANK, MLA_HEADS, NOPE + ROPE)
    zq = jnp.zeros((Q_RANK, MLA_HEADS, LANES - NOPE - ROPE), F32)
    wq_main = jnp.concatenate([wq, zq], axis=-1).reshape(Q_RANK, MLA_HEADS * LANES).astype(BF16)
    wq_rot = jnp.concatenate([jnp.zeros((Q_RANK, MLA_HEADS, NOPE), F32), _rope_rot_cols(wq[..., NOPE:]), zq],
                             axis=-1).reshape(Q_RANK, MLA_HEADS * LANES).astype(BF16)
    wkv = w_kvb.reshape(KV_RANK, MLA_HEADS, NOPE + V_DIM)
    wk = jnp.concatenate([wkv[..., :NOPE], jnp.zeros((KV_RANK, MLA_HEADS, LANES - NOPE), F32)],
                         axis=-1).reshape(KV_RANK, MLA_HEADS * LANES).astype(BF16)
    wv = wkv[..., NOPE:].reshape(KV_RANK, MLA_HEADS * V_DIM).astype(BF16)
    return win_aug, wq_main, wq_rot, wk, wv


def kernel(x, c, positions, w_ada, b_ada, w_in, q_norm_g, w_qb, kv_norm_g, w_kvb, conv_w, conv_b, filt_w1, filt_b1, filt_freq, filt_w2, filt_b2, filt_w3, filt_b3, hyena_decay, hyena_bias, attn_out_g, hyena_out_g, w_o, ln1_g, ln1_b, w_router, w_gate, w_up, w_down, ln2_g, ln2_b):
    B, S, D = x.shape
    l = 0
    mod = _ada_mod(c, w_ada[l], b_ada[l][None, :])
    sh1, sc1, g1, sh2, sc2, g2 = [m[:, None, :] for m in jnp.split(mod, 6, axis=-1)]

    win_aug, wq_main, wq_rot, wk, wv = _prep_weights(w_in[l], w_qb[l], w_kvb[l])
    cos_h, sin_h = _rope_tables(positions)
    pad = lambda n, v: jnp.full((B, n, S), v, F32)
    cos_t = jnp.concatenate([pad(NOPE, 1.0), cos_h, cos_h, pad(LANES - NOPE - ROPE, 1.0)], axis=1)
    sin_t = jnp.concatenate([pad(NOPE, 0.0), sin_h, sin_h, pad(LANES - NOPE - ROPE, 0.0)], axis=1)
    q, kt, v, hy_conv = _inproj(x, sc1, sh1, cos_t, sin_t, win_aug, q_norm_g[l][None, :], wq_main, wq_rot,
                                kv_norm_g[l][None, :], wk, wv, conv_w[l], conv_b[l])
    a = _attention(q, kt, v)

    hy = _hyena_group(hy_conv, filt_w1[l], filt_b1[l], filt_freq[l], filt_w2[l],
                      filt_b2[l], filt_w3[l], filt_b3[l], hyena_decay[l], hyena_bias[l])

    wr_pad = jnp.concatenate([w_router[l], jnp.zeros((D, LANES - N_EXPERTS), F32)], axis=1)
    x1, u2, aff_t = _outproj(a, hy, x, g1, sc2, sh2, attn_out_g[l][None, :], hyena_out_g[l][None, :],
                             w_o[l].astype(BF16), ln1_g[l][None, :], ln1_b[l][None, :], wr_pad)

    cap = EC_FACTOR * S // N_EXPERTS
    rpe = S // LANES
    aff2 = aff_t.reshape(B, N_EXPERTS * rpe, LANES)
    pos_sel, pos_all, gsel = _select(aff2, cap)
    row_start = pos_all.reshape(B, N_EXPERTS, rpe, LANES)[:, :, :, 0]

    def block_offsets(tokens):
        ends = jnp.full((B, N_EXPERTS, 1), cap, jnp.int32)
        return jnp.concatenate([row_start[:, :, ::tokens // LANES], ends], axis=-1).reshape(-1)

    xe = _gather(block_offsets(GATHER_TOK), u2, pos_sel, cap)
    ye = _moe_ffn(xe, w_gate[l], w_up[l], w_down[l])
    return _combine_ln2(block_offsets(CMB_TOK), x1, pos_sel.reshape(B, N_EXPERTS, S), gsel.reshape(B, N_EXPERTS, S), g2, ln2_g[l][None, :], ln2_b[l][None, :],
                        ye.reshape(N_EXPERTS * B * cap, D), cap)
```

```python
import functools
import math

import jax
import jax.numpy as jnp
import numpy as np
from jax import lax
from jax.experimental import pallas as pl
from jax.experimental.pallas import tpu as pltpu

F32 = jnp.float32
BF16 = jnp.bfloat16
HIGHEST = lax.Precision.HIGHEST

LANES = 128
SUBLANES = 8
VMEM_LIMIT = 56 * 1024 * 1024

D_MODEL = 1024
MLA_HEADS = 8
NOPE = 64
ROPE = 32
V_DIM = 64
Q_RANK = 256
KV_RANK = 128
HY_W = 512
HY_COLS = 3 * HY_W
N_EXPERTS = 16
EC_FACTOR = 2
ROPE_THETA = 10000.0
EPS = 1e-5
DN_ALPHA = 2.0 ** 0.25
LOG2E = 1.4426950408889634

TOK_TILE = 512
ATT_TQ = 512
ATT_TK = 512
MOE_ROWS = 512


def _params(*sem):
    return pltpu.CompilerParams(dimension_semantics=sem, vmem_limit_bytes=VMEM_LIMIT)


def _ada_kernel(ct_ref, w_ref, b_ref, o_ref):
    ct = ct_ref[...]
    sc = ct * jax.nn.sigmoid(ct)
    w = w_ref[...]
    rows = [jnp.sum(w * sc[:, b:b + 1], axis=0, keepdims=True) for b in range(ct.shape[1])]
    o_ref[...] = jnp.concatenate(rows, axis=0) + b_ref[...]


def _ada_mod(c, w_ada, b_ada):
    B, D = c.shape
    n = w_ada.shape[1]
    tn = 1024
    return pl.pallas_call(
        _ada_kernel,
        grid=(n // tn,),
        in_specs=[
            pl.BlockSpec((D, B), lambda j: (0, 0)),
            pl.BlockSpec((D, tn), lambda j: (0, j)),
            pl.BlockSpec((1, tn), lambda j: (0, j)),
        ],
        out_specs=pl.BlockSpec((B, tn), lambda j: (0, j)),
        out_shape=jax.ShapeDtypeStruct((B, n), F32),
        compiler_params=_params("parallel"),
        name="ada_mod",
    )(c.T, w_ada, b_ada)


def _dot3(a, b):
    ah, bh = a.astype(BF16), b.astype(BF16)
    al = (a - ah.astype(F32)).astype(BF16)
    bl = (b - bh.astype(F32)).astype(BF16)
    dot = functools.partial(jnp.dot, preferred_element_type=F32)
    return dot(ah, bh) + (dot(ah, bl) + dot(al, bh))


def _rms(x, g):
    return x * lax.rsqrt(jnp.mean(x * x, axis=-1, keepdims=True) + EPS) * g


def _rope_kernel(pos_ref, freq_ref, cos_ref, sin_ref):
    ang = freq_ref[...] * pos_ref[0].astype(F32)
    cos_ref[0] = jnp.cos(ang)
    sin_ref[0] = jnp.sin(ang)


def _rope_tables(positions):
    B, S = positions.shape
    half = ROPE // 2
    inv_freq = (ROPE_THETA ** (-jnp.arange(half, dtype=F32) / half))[:, None]
    out = jax.ShapeDtypeStruct((B, half, S), F32)
    blk = pl.BlockSpec((1, half, S), lambda b: (b, 0, 0))
    return pl.pallas_call(
        _rope_kernel,
        grid=(B,),
        in_specs=[pl.BlockSpec((1, 1, S), lambda b: (b, 0, 0)), pl.BlockSpec((half, 1), lambda b: (0, 0))],
        out_specs=[blk, blk],
        out_shape=[out, out],
        compiler_params=_params("parallel"),
        name="rope_tables",
    )(positions[:, None, :], inv_freq)


def _inproj_kernel(x_ref, xp_ref, xn_ref, sc_ref, sh_ref, cos_ref, sin_ref, win_ref, qg_ref, wq_ref, wqr_ref,
                   kvg_ref, wk_ref, wv_ref, cw_ref, cb_ref, q_ref, k_ref, v_ref, hy_ref):
    def modulate(xv):
        return (xv * (1.0 + sc_ref[0]) + sh_ref[0]).astype(BF16)

    u = modulate(x_ref[0])
    cos = cos_ref[0].T
    sin = sin_ref[0].T

    def proj(lo, hi):
        return jnp.dot(u, win_ref[:, lo:hi], preferred_element_type=F32)

    o = Q_RANK + 3 * KV_RANK
    hy = proj(o, o + HY_COLS)
    tm = hy.shape[0]
    i, last = pl.program_id(1), pl.num_programs(1) - 1
    edge = jnp.dot(modulate(jnp.concatenate([xp_ref[0], xn_ref[0]], axis=0)), win_ref[:, o:o + HY_COLS],
                   preferred_element_type=F32)
    prev = jnp.where(i == 0, 0.0, edge[SUBLANES - 1:SUBLANES, :])
    nxt = jnp.where(i == last, 0.0, edge[SUBLANES:SUBLANES + 1, :])
    row = lax.broadcasted_iota(jnp.int32, hy.shape, 0)
    up = jnp.where(row == 0, prev, pltpu.roll(hy, 1, 0))
    dn = jnp.where(row == tm - 1, nxt, pltpu.roll(hy, tm - 1, 0))
    hy_ref[0] = cb_ref[...] + cw_ref[0:1, :] * up + cw_ref[1:2, :] * hy + cw_ref[2:3, :] * dn

    cq = _rms(proj(0, Q_RANK), qg_ref[...]).astype(BF16)
    q_main = jnp.dot(cq, wq_ref[...], preferred_element_type=F32)
    q_rot = jnp.dot(cq, wqr_ref[...], preferred_element_type=F32)
    qscale = LOG2E * (NOPE + ROPE) ** -0.5
    cq_s = cos * qscale
    sq_s = sin * qscale
    for h in range(MLA_HEADS):
        sl = slice(h * LANES, (h + 1) * LANES)
        q_ref[0, h] = (q_main[:, sl] * cq_s + q_rot[:, sl] * sq_s).T.astype(BF16)

    o = Q_RANK
    ckv = _rms(proj(o, o + KV_RANK), kvg_ref[...]).astype(BF16)
    kr = proj(o + KV_RANK, o + 2 * KV_RANK) * cos + proj(o + 2 * KV_RANK, o + 3 * KV_RANK) * sin
    kn = jnp.dot(ckv, wk_ref[...], preferred_element_type=F32)
    for h in range(MLA_HEADS):
        k_ref[0, h] = (kn[:, h * LANES:(h + 1) * LANES] + kr).astype(BF16)
    v_ref[0] = jnp.dot(ckv, wv_ref[...], preferred_element_type=F32).T.astype(BF16)


def _inproj(x, sc1, sh1, cos_t, sin_t, win_aug, qg, wq, wqr, kvg, wk, wv, conv_w, conv_b):
    B, S, D = x.shape
    tm = TOK_TILE
    ncol = win_aug.shape[1]
    g8 = tm // SUBLANES
    n8 = S // SUBLANES
    full = lambda shape: pl.BlockSpec(shape, lambda b, i: (0,) * len(shape))
    return pl.pallas_call(
        _inproj_kernel,
        grid=(B, S // tm),
        in_specs=[
            pl.BlockSpec((1, tm, D), lambda b, i: (b, i, 0)),
            pl.BlockSpec((1, SUBLANES, D), lambda b, i: (b, jnp.maximum(i * g8 - 1, 0), 0)),
            pl.BlockSpec((1, SUBLANES, D), lambda b, i: (b, jnp.minimum((i + 1) * g8, n8 - 1), 0)),
            pl.BlockSpec((1, 1, D), lambda b, i: (b, 0, 0)),
            pl.BlockSpec((1, 1, D), lambda b, i: (b, 0, 0)),
            pl.BlockSpec((1, LANES, tm), lambda b, i: (b, 0, i)),
            pl.BlockSpec((1, LANES, tm), lambda b, i: (b, 0, i)),
            full((D, ncol)),
            full((1, Q_RANK)),
            full((Q_RANK, MLA_HEADS * LANES)),
            full((Q_RANK, MLA_HEADS * LANES)),
            full((1, KV_RANK)),
            full((KV_RANK, MLA_HEADS * LANES)),
            full((KV_RANK, MLA_HEADS * V_DIM)),
            full(conv_w.shape),
            full((1, HY_COLS)),
        ],
        out_specs=[
            pl.BlockSpec((1, MLA_HEADS, LANES, tm), lambda b, i: (b, 0, 0, i)),
            pl.BlockSpec((1, MLA_HEADS, tm, LANES), lambda b, i: (b, 0, i, 0)),
            pl.BlockSpec((1, MLA_HEADS * V_DIM, tm), lambda b, i: (b, 0, i)),
            pl.BlockSpec((1, tm, HY_COLS), lambda b, i: (b, i, 0)),
        ],
        out_shape=[
            jax.ShapeDtypeStruct((B, MLA_HEADS, LANES, S), BF16),
            jax.ShapeDtypeStruct((B, MLA_HEADS, S, LANES), BF16),
            jax.ShapeDtypeStruct((B, MLA_HEADS * V_DIM, S), BF16),
            jax.ShapeDtypeStruct((B, S, HY_COLS), F32),
        ],
        compiler_params=_params("parallel", "parallel"),
        name="in_proj",
    )(x, x, x, sc1, sh1, cos_t, sin_t, win_aug, qg, wq, wqr, kvg, wk, wv, conv_w, conv_b[None, :])


ATT_AUG = 16


def _attn_kernel(qt_ref, k_ref, vt_ref, o_ref, s_scr):
    S = k_ref.shape[2]
    tq = ATT_TQ
    nk = S // ATT_TK
    nq = S // tq
    ones_rows = jnp.where(lax.broadcasted_iota(jnp.int32, (ATT_AUG, ATT_TK), 0) == 0, 1.0, 0.0).astype(BF16)

    def scores(qi, c, slot):
        k0 = pl.multiple_of(c * ATT_TK, ATT_TK)
        q0 = pl.multiple_of(qi * tq, tq)
        for h in range(2):
            s_scr[slot, h] = jnp.dot(k_ref[0, h, pl.ds(k0, ATT_TK), :], qt_ref[0, h, :, pl.ds(q0, tq)],
                                     preferred_element_type=F32)

    def absorb(c, slot, carry):
        k0 = pl.multiple_of(c * ATT_TK, ATT_TK)
        out = []
        for h in range(2):
            m_prev, acc = carry[h]
            s = s_scr[slot, h]
            m_new = jnp.maximum(m_prev, jnp.max(s, axis=0, keepdims=True))
            alpha = jnp.exp2(m_prev - m_new)
            p = jnp.exp2(s - m_new).astype(BF16)
            v_aug = jnp.concatenate([vt_ref[0, h * V_DIM:(h + 1) * V_DIM, pl.ds(k0, ATT_TK)], ones_rows], axis=0)
            out.append((m_new, alpha * acc + jnp.dot(v_aug, p, preferred_element_type=F32)))
        return tuple(out)

    def query_tile(qi, _):
        def pair(i, carry):
            c0 = 2 * i
            scores(qi, c0 + 1, 1)
            carry = absorb(c0, 0, carry)
            wrap = c0 + 2 >= nk
            scores(jnp.where(wrap, jnp.minimum(qi + 1, nq - 1), qi), jnp.where(wrap, 0, c0 + 2), 0)
            return absorb(c0 + 1, 1, carry)

        init = tuple((jnp.full((1, tq), -jnp.inf, F32), jnp.zeros((V_DIM + ATT_AUG, tq), F32)) for _ in range(2))
        fin = lax.fori_loop(0, nk // 2, pair, init, unroll=4)
        o = jnp.concatenate([acc[:V_DIM] * (1.0 / acc[V_DIM:V_DIM + 1]) for (_, acc) in fin], axis=0)
        o_ref[0, pl.ds(pl.multiple_of(qi * tq, tq), tq), :] = o.T
        return 0

    scores(0, 0, 0)
    lax.fori_loop(0, nq, query_tile, 0)


def _attention(qt, k, vt):
    B, H, _, S = qt.shape
    return pl.pallas_call(
        _attn_kernel,
        grid=(B, H // 2),
        in_specs=[
            pl.BlockSpec((1, 2, LANES, S), lambda b, hp: (b, hp, 0, 0)),
            pl.BlockSpec((1, 2, S, LANES), lambda b, hp: (b, hp, 0, 0)),
            pl.BlockSpec((1, 2 * V_DIM, S), lambda b, hp: (b, hp, 0)),
        ],
        out_specs=pl.BlockSpec((1, S, 2 * V_DIM), lambda b, hp: (b, 0, hp)),
        out_shape=jax.ShapeDtypeStruct((B, S, H * V_DIM), F32),
        scratch_shapes=[pltpu.VMEM((2, 2, ATT_TK, ATT_TQ), F32)],
        compiler_params=_params("parallel", "parallel"),
        name="mla_attention",
    )(qt, k, vt)


def _layer_norm(x, g, b):
    mu = jnp.mean(x, axis=-1, keepdims=True)
    xc = x - mu
    var = jnp.mean(xc * xc, axis=-1, keepdims=True)
    return xc * lax.rsqrt(var + EPS) * g + b


def _outproj_kernel(a_ref, hy_ref, x_ref, g1_ref, sc2_ref, sh2_ref, ag_ref, hg_ref, wo_ref,
                    lg_ref, lb_ref, wr_ref, x1_ref, u2_ref, aff_ref):
    half = a_ref.shape[2]
    tm = a_ref.shape[1]
    nsplit = 2
    for r0 in range(0, tm, tm // nsplit):
        rs = slice(r0, r0 + tm // nsplit)
        an = _rms(a_ref[0, rs, :], ag_ref[...]).astype(BF16)
        hn = _rms(hy_ref[0, rs, :], hg_ref[...]).astype(BF16)
        mixed = (jnp.dot(an, wo_ref[:half, :], preferred_element_type=F32)
                 + jnp.dot(hn, wo_ref[half:, :], preferred_element_type=F32))
        x1 = _layer_norm(DN_ALPHA * x_ref[0, rs, :] + g1_ref[0] * mixed, lg_ref[...], lb_ref[...])
        x1_ref[0, rs, :] = x1
        u2 = x1 * (1.0 + sc2_ref[0]) + sh2_ref[0]
        u2_ref[0, rs, :] = u2.astype(BF16)
        logits = _dot3(u2, wr_ref[...])
        lane = lax.broadcasted_iota(jnp.int32, logits.shape, 1)
        logits = jnp.where(lane < N_EXPERTS, logits, -jnp.inf)
        e = jnp.exp(logits - jnp.max(logits, axis=-1, keepdims=True))
        aff = e / jnp.sum(e, axis=-1, keepdims=True)
        aff_ref[0, :, rs] = aff.T[:N_EXPERTS, :]


def _outproj(a, hy, x, g1, sc2, sh2, ag, hg, wo, lg, lb, wr_pad):
    B, S, D = x.shape
    tm = TOK_TILE
    half = a.shape[2]
    full = lambda shape: pl.BlockSpec(shape, lambda b, i: (0,) * len(shape))
    tok = lambda w: pl.BlockSpec((1, tm, w), lambda b, i: (b, i, 0))
    per_b = pl.BlockSpec((1, 1, D), lambda b, i: (b, 0, 0))
    return pl.pallas_call(
        _outproj_kernel,
        grid=(B, S // tm),
        in_specs=[tok(half), tok(half), tok(D), per_b, per_b, per_b,
                  full((1, half)), full((1, half)), full((D, D)), full((1, D)), full((1, D)),
                  full((D, LANES))],
        out_specs=[tok(D), tok(D), pl.BlockSpec((1, N_EXPERTS, tm), lambda b, i: (b, 0, i))],
        out_shape=[
            jax.ShapeDtypeStruct((B, S, D), F32),
            jax.ShapeDtypeStruct((B, S, D), BF16),
            jax.ShapeDtypeStruct((B, N_EXPERTS, S), F32),
        ],
        compiler_params=_params("parallel", "parallel"),
        name="out_proj_ln1_router",
    )(a, hy, x, g1, sc2, sh2, ag, hg, wo, lg, lb, wr_pad)


def _select_kernel(aff_ref, ones_ref, utri_ref, ltri_ref, pos_ref, posall_ref, g_ref, *, cap):
    aff = aff_ref[0]
    ones = ones_ref[...]

    def rowsum(mf):
        return jnp.dot(mf, ones, preferred_element_type=F32)

    def expert_total(mf):
        x = mf.astype(F32).reshape(N_EXPERTS, aff.shape[0] // N_EXPERTS, LANES)
        tot = jnp.sum(jnp.sum(x, axis=1, keepdims=True), axis=2, keepdims=True)
        return jnp.broadcast_to(tot, x.shape).reshape(aff.shape)

    def prefix(mf):
        return (jnp.dot(mf, utri_ref[...], preferred_element_type=F32)
                + jnp.dot(ltri_ref[...], rowsum(mf).astype(BF16), preferred_element_type=F32))

    def ind(m):
        return jnp.where(m, 1.0, 0.0).astype(BF16)

    def search(i, thr_bits):
        cand = thr_bits | jnp.left_shift(jnp.int32(1), 30 - i)
        enough = expert_total(ind(aff >= lax.bitcast_convert_type(cand, F32))) >= cap
        return jnp.where(enough, cand, thr_bits)

    thr = lax.bitcast_convert_type(lax.fori_loop(0, 31, search, jnp.zeros(aff.shape, jnp.int32)), F32)
    gt = aff > thr
    eq = aff == thr
    room = cap - expert_total(ind(gt))
    take_eq = jnp.where(eq, jnp.where(prefix(ind(eq)) < room, 1.0, 0.0), 0.0)
    sel = jnp.where(gt, 1.0, take_eq)
    pos = prefix(sel.astype(BF16)).astype(jnp.int32)
    posall_ref[0] = pos
    pos_ref[0] = jnp.where(sel > 0.0, pos, -1)
    g_ref[0] = jnp.where(sel > 0.0, aff, 0.0)


def _select(aff2, cap):
    B, R, _ = aff2.shape
    rpe = R // N_EXPERTS
    li = jnp.arange(LANES, dtype=jnp.int32)
    ri = jnp.arange(R, dtype=jnp.int32)
    same = (ri[:, None] // rpe) == (ri[None, :] // rpe)
    ones = jnp.ones((LANES, LANES), BF16)
    utri = (li[:, None] < li[None, :]).astype(BF16)
    ltri = (same & (ri[None, :] < ri[:, None])).astype(BF16)
    blk = pl.BlockSpec((1, R, LANES), lambda b: (b, 0, 0))
    full = lambda arr: pl.BlockSpec(arr.shape, lambda b: (0, 0))
    return pl.pallas_call(
        functools.partial(_select_kernel, cap=cap),
        grid=(B,),
        in_specs=[blk, full(ones), full(utri), full(ltri)],
        out_specs=[blk, blk, blk],
        out_shape=[jax.ShapeDtypeStruct((B, R, LANES), jnp.int32), jax.ShapeDtypeStruct((B, R, LANES), jnp.int32),
                   jax.ShapeDtypeStruct((B, R, LANES), F32)],
        compiler_params=_params("parallel"),
        name="expert_choice_select",
    )(aff2, ones, utri, ltri)


GATHER_TOK = 1024
GATHER_WIN = 160


def _gather_kernel(offs_ref, u_ref, pos_ref, xe_ref, acc, *, cap, nblk):
    b, e = pl.program_id(0), pl.program_id(1)
    base = (b * N_EXPERTS + e) * (nblk + 1)
    rows = GATHER_TOK // LANES
    acc[...] = jnp.zeros(acc.shape, F32)
    slot = lax.broadcasted_iota(jnp.int32, (GATHER_WIN, LANES), 0)

    def block(j, carry):
        off, end = offs_ref[base + j], offs_ref[base + j + 1]
        w = (off // SUBLANES) * SUBLANES
        prow = pos_ref[0, pl.ds(pl.multiple_of(j * rows, rows), rows), :]
        ublk = u_ref[0, pl.ds(pl.multiple_of(j * GATHER_TOK, GATHER_TOK), GATHER_TOK), :]

        def window(i, c):
            s0 = pl.multiple_of(w + i * GATHER_WIN, SUBLANES)
            onehot = jnp.concatenate(
                [jnp.where(slot == prow[r:r + 1, :] - s0, 1.0, 0.0).astype(BF16) for r in range(rows)], axis=1)
            acc[pl.ds(s0, GATHER_WIN), :] += jnp.dot(onehot, ublk, preferred_element_type=F32)
            return c

        window(0, 0)
        lax.fori_loop(1, (end - w + GATHER_WIN - 1) // GATHER_WIN, window, 0)
        return carry

    lax.fori_loop(0, nblk, block, 0, unroll=2)
    xe_ref[0] = acc[:cap, :].astype(BF16)


def _gather(offs, u2, pos_sel, cap):
    B, S, D = u2.shape
    rpe = S // LANES
    nblk = S // GATHER_TOK
    return pl.pallas_call(
        functools.partial(_gather_kernel, cap=cap, nblk=nblk),
        grid_spec=pltpu.PrefetchScalarGridSpec(
            num_scalar_prefetch=1,
            grid=(B, N_EXPERTS),
            in_specs=[pl.BlockSpec((1, S, D), lambda b, e, o: (b, 0, 0)),
                      pl.BlockSpec((1, rpe, LANES), lambda b, e, o: (b, e, 0))],
            out_specs=pl.BlockSpec((1, cap, D), lambda b, e, o: (e, b, 0)),
            scratch_shapes=[pltpu.VMEM((cap + GATHER_WIN + SUBLANES, D), F32)],
        ),
        out_shape=jax.ShapeDtypeStruct((N_EXPERTS, B * cap, D), BF16),
        compiler_params=_params("parallel", "parallel"),
        name="expert_gather",
    )(offs, u2, pos_sel)


def _moe_kernel(x_ref, wg_ref, wu_ref, wd_ref, o_ref, wgb, wub, wdb):
    @pl.when(pl.program_id(1) == 0)
    def _():
        wgb[...] = wg_ref[0].astype(BF16)
        wub[...] = wu_ref[0].astype(BF16)
        wdb[...] = wd_ref[0].astype(BF16)

    x = x_ref[0]
    hg = jnp.dot(x, wgb[...], preferred_element_type=F32)
    hu = jnp.dot(x, wub[...], preferred_element_type=F32)
    h = (hg * jax.nn.sigmoid(hg) * hu).astype(BF16)
    o_ref[0] = jnp.dot(h, wdb[...], preferred_element_type=F32).astype(BF16)


def _moe_ffn(xe, w_gate, w_up, w_down):
    E, R, D = xe.shape
    Fh = w_gate.shape[2]
    tr = MOE_ROWS
    return pl.pallas_call(
        _moe_kernel,
        grid=(E, R // tr),
        in_specs=[
            pl.BlockSpec((1, tr, D), lambda e, r: (e, r, 0)),
            pl.BlockSpec((1, D, Fh), lambda e, r: (e, 0, 0)),
            pl.BlockSpec((1, D, Fh), lambda e, r: (e, 0, 0)),
            pl.BlockSpec((1, Fh, D), lambda e, r: (e, 0, 0)),
        ],
        out_specs=pl.BlockSpec((1, tr, D), lambda e, r: (e, r, 0)),
        out_shape=jax.ShapeDtypeStruct((E, R, D), BF16),
        scratch_shapes=[pltpu.VMEM((D, Fh), BF16), pltpu.VMEM((D, Fh), BF16), pltpu.VMEM((Fh, D), BF16)],
        compiler_params=_params("parallel", "arbitrary"),
        name="moe_swiglu",
    )(xe, w_gate, w_up, w_down)


CMB_TOK = 512
CMB_WIN = 128
BF16_ROWS = 16


def _combine_kernel(offs_ref, x1_ref, pos_ref, g_ref, g2_ref, lg_ref, lb_ref, ye_hbm, o_ref,
                    rhs, lhs, acc, sem, *, cap, nblk, nbatch):
    b, i = pl.program_id(0), pl.program_id(1)
    W = CMB_WIN
    total = ye_hbm.shape[0]
    step = b * nblk + i
    cur_slot = step % 2
    row_e = lax.broadcasted_iota(jnp.int32, (N_EXPERTS, 1), 0)
    slot_w = lax.broadcasted_iota(jnp.int32, (W, CMB_TOK), 0)

    def windows(bb, ii):
        out = []
        for e in range(N_EXPERTS):
            k = (bb * N_EXPERTS + e) * (nblk + 1) + ii
            off, end = offs_ref[k], offs_ref[k + 1]
            lo = (off // BF16_ROWS) * BF16_ROWS
            out.append(((e * nbatch + bb) * cap, lo, jnp.where(end > off, (end - lo + W - 1) // W, 0)))
        return out

    def fetch_starts(win, p):
        return [jnp.minimum(rb + lo + p * W, total - W) for rb, lo, _ in win]

    def window_copy(e, start, s):
        return pltpu.make_async_copy(ye_hbm.at[pl.ds(pl.multiple_of(start, BF16_ROWS), W), :],
                                     rhs.at[s, pl.ds(e * W, W), :], sem.at[s, e])

    def start_fetch(win, p, s):
        for e, start in enumerate(fetch_starts(win, p)):
            window_copy(e, start, s).start()

    def absorb(win, p, s, first):
        starts = fetch_starts(win, p)
        lo_col = jnp.zeros((N_EXPERTS, 1), jnp.int32)
        shift_col = jnp.zeros((N_EXPERTS, 1), jnp.int32)
        for e, (rb, lo0, _) in enumerate(win):
            lo = lo0 + p * W
            lo_col = jnp.where(row_e == e, lo, lo_col)
            shift_col = jnp.where(row_e == e, rb + lo - starts[e], shift_col)
        d = pos_ref[0] - lo_col
        rel = jnp.where((d >= 0) & (d < W), d + shift_col, -1)
        g = g_ref[0]
        for e in range(N_EXPERTS):
            lhs[e * W:(e + 1) * W, :] = jnp.where(slot_w == rel[e:e + 1, :], g[e:e + 1, :], 0.0).astype(BF16)
        for e in range(N_EXPERTS):
            window_copy(e, starts[e], s).wait()
        part = lax.dot_general(lhs[...], rhs[s], (((0,), (0,)), ((), ())), preferred_element_type=F32)
        acc[...] = part if first else acc[...] + part

    cur = windows(b, i)

    @pl.when(step == 0)
    def _():
        start_fetch(cur, 0, 0)

    @pl.when(step + 1 < nbatch * nblk)
    def _():
        wrap = i + 1 == nblk
        start_fetch(windows(jnp.where(wrap, b + 1, b), jnp.where(wrap, 0, i + 1)), 0, 1 - cur_slot)

    absorb(cur, 0, cur_slot, True)

    def extra_pass(p, carry):
        start_fetch(cur, p, cur_slot)
        absorb(cur, p, cur_slot, False)
        return carry

    npass = functools.reduce(jnp.maximum, [n for _, _, n in cur])
    lax.fori_loop(1, npass, extra_pass, 0)
    o_ref[0] = _layer_norm(DN_ALPHA * x1_ref[0] + g2_ref[0] * acc[...], lg_ref[...], lb_ref[...])


def _combine_ln2(offs, x1, pos_exp, g_exp, g2, lg, lb, ye_rows, cap):
    B, S, D = x1.shape
    T = CMB_TOK
    nblk = S // T
    tok = lambda w: pl.BlockSpec((1, T, w), lambda b, i, o: (b, i, 0))
    vec = pl.BlockSpec((1, D), lambda b, i, o: (0, 0))
    exp_major = pl.BlockSpec((1, N_EXPERTS, T), lambda b, i, o: (b, 0, i))
    return pl.pallas_call(
        functools.partial(_combine_kernel, cap=cap, nblk=nblk, nbatch=B),
        grid_spec=pltpu.PrefetchScalarGridSpec(
            num_scalar_prefetch=1,
            grid=(B, nblk),
            in_specs=[tok(D), exp_major, exp_major,
                      pl.BlockSpec((1, 1, D), lambda b, i, o: (b, 0, 0)), vec, vec,
                      pl.BlockSpec(memory_space=pl.ANY)],
            out_specs=tok(D),
            scratch_shapes=[pltpu.VMEM((2, N_EXPERTS * CMB_WIN, D), BF16),
                            pltpu.VMEM((N_EXPERTS * CMB_WIN, T), BF16),
                            pltpu.VMEM((T, D), F32),
                            pltpu.SemaphoreType.DMA((2, N_EXPERTS))],
        ),
        out_shape=jax.ShapeDtypeStruct((B, S, D), F32),
        compiler_params=_params("arbitrary", "arbitrary"),
        name="combine_residual_ln2",
    )(offs, x1, pos_exp, g_exp, g2, lg, lb, ye_rows)


DFT_R = 128
K1_HALF = 40
K1_VALID = DFT_R // 2 + 1
K1_GROUPS = (K1_HALF // SUBLANES, -(-(K1_VALID - K1_HALF) // SUBLANES))
K1_SPEC = K1_HALF + K1_GROUPS[1] * SUBLANES
HY_CT = 128


@functools.lru_cache(maxsize=None)
def _dft_tables_np(S, two_sided):
    N = 2 * S
    half = np.arange(S // DFT_R)
    a_idx = np.concatenate([half, N // DFT_R - 1 - half]) if two_sided else half
    r = np.arange(2 * K1_HALF)
    k1 = np.arange(2)[:, None] * K1_HALF + r[None, :] % K1_HALF
    valid = (k1 < K1_VALID).astype(np.float64)
    n = DFT_R * a_idx[None, :] + np.arange(DFT_R)[:, None]
    theta = ((n[None, :, None, :] * k1[:, None, :, None]) % N) * (2.0 * math.pi / N)
    is_im = (r >= K1_HALF)[None, None, :, None]
    g = np.where(is_im, -np.sin(theta), np.cos(theta)) * valid[:, None, :, None]
    cw = np.where((k1 == 0) | (k1 == DFT_R // 2), 1.0, 2.0) * valid / N
    gi = np.transpose(g * cw[:, None, :, None], (0, 1, 3, 2))
    ph = ((np.arange(DFT_R)[:, None] * np.arange(DFT_R)[None, :]) % DFT_R) * (2.0 * math.pi / DFT_R)
    C, Sn = np.cos(ph), np.sin(ph)
    f3f = np.block([[C, Sn], [-Sn, C]])
    f3i = np.block([[C, -Sn], [Sn, C]])

    def pair_lanes(t):
        h, nb, nr, nc = t.shape
        return t.reshape(h, nb // 2, 2, nr, nc).transpose(0, 1, 3, 2, 4).reshape(h, nb // 2, nr, 2 * nc)

    return tuple(np.ascontiguousarray(t, dtype=np.float32) for t in (pair_lanes(g), pair_lanes(gi), f3f, f3i))


def _dft_tables(S, two_sided=False):
    return tuple(jnp.asarray(t).astype(BF16) for t in _dft_tables_np(S, two_sided))


def _swap01(x):
    return jnp.swapaxes(x, 0, 1)


def _block_diag_lanes(top, bot):
    return jnp.concatenate([jnp.concatenate([top, jnp.zeros((top.shape[0], bot.shape[1]), top.dtype)], axis=1),
                            jnp.concatenate([jnp.zeros((bot.shape[0], top.shape[1]), bot.dtype), bot], axis=1)], axis=0)


def _dft_stage1(load, gf_ref, hf, a_scr):
    ct = a_scr.shape[2]

    def body(g, carry):
        b0 = pl.multiple_of(g * SUBLANES, SUBLANES)
        zt = load(b0)
        for p in range(SUBLANES // 2):
            a2 = jnp.dot(gf_ref[hf, g * (SUBLANES // 2) + p], _block_diag_lanes(zt[2 * p], zt[2 * p + 1]),
                         preferred_element_type=F32)
            a_scr[b0 + 2 * p] = a2[:, :ct]
            a_scr[b0 + 2 * p + 1] = a2[:, ct:]
        return carry

    lax.fori_loop(0, DFT_R // SUBLANES, body, 0, unroll=4)


def _dft_stage3_operands(a_scr, r0):
    mr = _swap01(a_scr[:, pl.ds(r0, SUBLANES), :])
    mi = _swap01(a_scr[:, pl.ds(K1_HALF + r0, SUBLANES), :])
    return [jnp.concatenate([jnp.concatenate([mr[j], mi[j]], axis=0),
                             jnp.concatenate([mr[j + 1], mi[j + 1]], axis=0)], axis=1).astype(BF16)
            for j in range(0, SUBLANES, 2)]


def _filter_mlp_kernel(feat_ref, featx_ref, w1_ref, b1_ref, fr_ref, w2_ref, b2_ref, w3_ref, b3_ref, dec_ref,
                       of_ref, ob_ref):
    tl = feat_ref.shape[0]
    ncol = of_ref.shape[1]
    fr = fr_ref[...]

    def hidden(f):
        h = jnp.sin(fr * (jnp.dot(f, w1_ref[...], precision=HIGHEST, preferred_element_type=F32) + b1_ref[...]))
        return jnp.sin(fr * (jnp.dot(h, w2_ref[...], precision=HIGHEST, preferred_element_type=F32) + b2_ref[...]))

    def taps(h2, t, lo):
        out = _dot3(h2, w3_ref[:, lo:lo + ncol]) + b3_ref[:, lo:lo + ncol]
        return out * jnp.exp(-t * jnp.abs(dec_ref[:, lo:lo + ncol]))

    feats, featx = feat_ref[...], featx_ref[...]
    h2, h2x = hidden(feats), hidden(featx)
    of_ref[...] = taps(h2, feats[:, 0:1], 0)
    row = lax.broadcasted_iota(jnp.int32, (tl, 1), 0)
    last = row == tl - 1
    h2n = jnp.where(last, h2x[0:1, :], pltpu.roll(h2, tl - 1, 0))
    tn = jnp.where(last, featx[0:1, :], pltpu.roll(feats, tl - 1, 0))[:, 0:1]
    ob_ref[...] = jnp.where(last & (pl.program_id(0) == pl.num_programs(0) - 1), 0.0, taps(h2n, tn, ncol))


@functools.lru_cache(maxsize=None)
def _filter_features_np(L, nf):
    pos = np.arange(L + SUBLANES, dtype=np.float64)
    t = pos[:, None] / (L - 1)
    bands = (nf - 1) // 2
    freqs = np.linspace(1e-4, bands - 1, bands)
    phase = (2.0 * math.pi / L) * pos[:, None] * freqs[None, :]
    feats = np.concatenate([t, np.cos(phase), -np.sin(phase), np.zeros((L + SUBLANES, LANES - nf))], axis=-1)
    return np.ascontiguousarray(feats, dtype=np.float32)


def _filter_taps(L, w1, b1, freq, w2, b2, w3, b3, decay):
    nf = w1.shape[0]
    feats = jnp.asarray(_filter_features_np(L, nf))
    w1p = jnp.concatenate([w1, jnp.zeros((LANES - nf, w1.shape[1]), F32)], axis=0)
    fo = w1.shape[1]
    ncol = w3.shape[1] // 2
    tl = 512
    full = lambda shape: pl.BlockSpec(shape, lambda i: (0,) * len(shape))
    out = jax.ShapeDtypeStruct((L, ncol), F32)
    blk = pl.BlockSpec((tl, ncol), lambda i: (i, 0))
    return pl.pallas_call(
        _filter_mlp_kernel,
        grid=(L // tl,),
        in_specs=[pl.BlockSpec((tl, LANES), lambda i: (i, 0)),
                  pl.BlockSpec((SUBLANES, LANES), lambda i: ((i + 1) * (tl // SUBLANES), 0)),
                  full((LANES, fo)), full((1, fo)), full((1, fo)), full((fo, fo)), full((1, fo)),
                  full((fo, 2 * ncol)), full((1, 2 * ncol)), full((1, 2 * ncol))],
        out_specs=[blk, blk],
        out_shape=[out, out],
        compiler_params=_params("parallel"),
        name="hyena_filter_mlp",
    )(feats, feats, w1p, b1[None, :], freq[None, :], w2, b2[None, :], w3, b3[None, :], decay.reshape(1, 2 * ncol))


def _filter_spectrum_kernel(hf_ref, hb_ref, gf_ref, f3f_ref, o_ref, a_scr):
    ct = a_scr.shape[2]

    def load(b0):
        zf = _swap01(hf_ref[:, pl.ds(b0, SUBLANES), :])
        zb = _swap01(hb_ref[:, pl.ds(pl.multiple_of(DFT_R - SUBLANES - b0, SUBLANES), SUBLANES), :])
        return [jnp.concatenate([zf[j], zb[SUBLANES - 1 - j]], axis=0).astype(BF16) for j in range(SUBLANES)]

    for hf in range(2):
        _dft_stage1(load, gf_ref, hf, a_scr)

        def body(g, carry):
            r0 = pl.multiple_of(g * SUBLANES, SUBLANES)
            for p, m in enumerate(_dft_stage3_operands(a_scr, r0)):
                x2 = jnp.dot(f3f_ref[...], m, preferred_element_type=F32)
                for q in range(2):
                    o_ref[0, hf * K1_HALF + r0 + 2 * p + q] = x2[:, q * ct:(q + 1) * ct].astype(BF16)
            return carry

        lax.fori_loop(0, K1_GROUPS[hf], body, 0, unroll=True)


def _filter_spectrum(fwd3, bwd3, gf, f3f):
    na, nb, ncol = fwd3.shape
    C = ncol // 2
    nct = C // HY_CT
    ct = HY_CT
    taps = pl.BlockSpec((na, nb, ct), lambda o, j: (0, 0, o * nct + j))
    return pl.pallas_call(
        _filter_spectrum_kernel,
        grid=(2, nct),
        in_specs=[
            taps, taps,
            pl.BlockSpec(gf.shape, lambda o, j: (0, 0, 0, 0), pipeline_mode=pl.Buffered(1)),
            pl.BlockSpec(f3f.shape, lambda o, j: (0, 0), pipeline_mode=pl.Buffered(1)),
        ],
        out_specs=pl.BlockSpec((1, K1_SPEC, 2 * DFT_R, ct), lambda o, j: (o, 0, 0, j)),
        out_shape=jax.ShapeDtypeStruct((2, K1_SPEC, 2 * DFT_R, C), BF16),
        scratch_shapes=[pltpu.VMEM((DFT_R, 2 * K1_HALF, ct), F32)],
        compiler_params=_params("parallel", "parallel"),
        name="hyena_filter_spectrum",
    )(fwd3, bwd3, gf, f3f)


def _long_conv_kernel(z_ref, xg_ref, h_ref, bias_ref, gf_ref, f3f_ref, f3i_ref, gi_ref, o_ref,
                      a_scr, b_scr, zt_scr, yt_scr):
    ct = a_scr.shape[2]
    bias = bias_ref[0]

    def load_first(b0):
        zt = _swap01(z_ref[0, :, pl.ds(b0, SUBLANES), :]).astype(BF16)
        for j in range(SUBLANES):
            zt_scr[b0 + j] = zt[j]
        return zt

    for hf in range(2):
        _dft_stage1(load_first if hf == 0 else (lambda b0: [zt_scr[b0 + j] for j in range(SUBLANES)]),
                    gf_ref, hf, a_scr)
        ngrp = K1_GROUPS[hf]
        if ngrp * SUBLANES < K1_HALF:
            b_scr[ngrp * SUBLANES:] = jnp.zeros((K1_HALF - ngrp * SUBLANES,) + b_scr.shape[1:], F32)

        def spectral(g, carry):
            r0 = pl.multiple_of(g * SUBLANES, SUBLANES)
            for p, m in enumerate(_dft_stage3_operands(a_scr, r0)):
                x = jnp.dot(f3f_ref[...], m, preferred_element_type=F32)
                k1 = hf * K1_HALF + r0 + 2 * p
                h = jnp.concatenate([h_ref[0, k1], h_ref[0, k1 + 1]], axis=1).astype(F32)
                xr, xi, hr, hi = x[:DFT_R], x[DFT_R:], h[:DFT_R], h[DFT_R:]
                y = jnp.concatenate([xr * hr - xi * hi, xr * hi + xi * hr], axis=0).astype(BF16)
                bb = jnp.dot(f3i_ref[...], y, preferred_element_type=F32)
                b_scr[r0 + 2 * p] = bb[:, :ct]
                b_scr[r0 + 2 * p + 1] = bb[:, ct:]
            return carry

        lax.fori_loop(0, ngrp, spectral, 0, unroll=True)

        def inverse(g, carry):
            b0 = pl.multiple_of(g * SUBLANES, SUBLANES)
            br = _swap01(b_scr[:, pl.ds(b0, SUBLANES), :])
            bi = _swap01(b_scr[:, pl.ds(DFT_R + b0, SUBLANES), :])
            ys = []
            for p in range(SUBLANES // 2):
                bm = [jnp.concatenate([br[2 * p + q], bi[2 * p + q]], axis=0).astype(BF16) for q in range(2)]
                y2 = jnp.dot(gi_ref[hf, g * (SUBLANES // 2) + p], _block_diag_lanes(bm[0], bm[1]),
                             preferred_element_type=F32)
                ys += [y2[:, :ct], y2[:, ct:]]
            if hf == 0:
                for j in range(SUBLANES):
                    yt_scr[b0 + j] = ys[j]
            else:
                y = _swap01(jnp.stack([yt_scr[b0 + j] + ys[j] for j in range(SUBLANES)], axis=0))
                sl = (0, slice(None), pl.ds(b0, SUBLANES), slice(None))
                o_ref[sl] = xg_ref[sl] * (y + z_ref[sl] * bias)
            return carry

        lax.fori_loop(0, DFT_R // SUBLANES, inverse, 0, unroll=4)


def _long_conv(z_arr, z_off, xg_arr, xg_off, spec, order, bias3, gf, f3f, f3i, gi):
    B, na, nb, _ = z_arr.shape
    C = spec.shape[3]
    ct = HY_CT
    const = lambda arr: pl.BlockSpec(arr.shape, lambda j, b: (0,) * arr.ndim, pipeline_mode=pl.Buffered(1))
    return pl.pallas_call(
        _long_conv_kernel,
        grid=(C // ct, B),
        in_specs=[
            pl.BlockSpec((1, na, nb, ct), lambda j, b: (b, 0, 0, z_off + j)),
            pl.BlockSpec((1, na, nb, ct), lambda j, b: (b, 0, 0, xg_off + j)),
            pl.BlockSpec((1, K1_SPEC, 2 * DFT_R, ct), lambda j, b: (order, 0, 0, j), pipeline_mode=pl.Buffered(1)),
            pl.BlockSpec((1, 1, ct), lambda j, b: (order, 0, j)),
            const(gf), const(f3f), const(f3i), const(gi),
        ],
        out_specs=pl.BlockSpec((1, na, nb, ct), lambda j, b: (b, 0, 0, j)),
        out_shape=jax.ShapeDtypeStruct((B, na, nb, C), F32),
        scratch_shapes=[pltpu.VMEM((DFT_R, 2 * K1_HALF, ct), F32), pltpu.VMEM((K1_HALF, 2 * DFT_R, ct), F32),
                        pltpu.VMEM((nb, na, ct), BF16), pltpu.VMEM((nb, na, ct), F32)],
        compiler_params=_params("parallel", "parallel"),
        name="hyena_long_conv",
    )(z_arr, xg_arr, spec, bias3, gf, f3f, f3i, gi)


def _hyena_group(hy_conv, fw1, fb1, ffreq, fw2, fb2, fw3, fb3, decay, hyena_bias):
    B, S, _ = hy_conv.shape
    na = S // DFT_R
    gf, gi, f3f, f3i = _dft_tables(S)
    fwd, bwd = _filter_taps(S, fw1, fb1, ffreq, fw2, fb2, fw3, fb3, decay)
    spec = _filter_spectrum(fwd.reshape(na, DFT_R, -1), bwd.reshape(na, DFT_R, -1),
                            _dft_tables(S, two_sided=True)[0], f3f)
    u = hy_conv.reshape(B, na, DFT_R, -1)
    nct = HY_W // HY_CT
    bias3 = hyena_bias[:, None, :]
    z1 = _long_conv(u, 0, u, nct, spec, 0, bias3, gf, f3f, f3i, gi)
    return _long_conv(z1, 0, u, 2 * nct, spec, 1, bias3, gf, f3f, f3i, gi).reshape(B, S, HY_W)


def _rope_rot_cols(w):
    half = ROPE // 2
    return jnp.concatenate([-w[..., half:], w[..., :half]], axis=-1)


def _prep_weights(w_in, w_qb, w_kvb):
    D = w_in.shape[0]
    o = Q_RANK + KV_RANK
    w_kr = w_in[:, o:o + ROPE]
    z64 = jnp.zeros((D, NOPE), F32)
    z32 = jnp.zeros((D, LANES - NOPE - ROPE), F32)
    win_aug = jnp.concatenate([
        w_in[:, :o],
        z64, w_kr, z32,
        z64, _rope_rot_cols(w_kr), z32,
        w_in[:, o + ROPE:],
    ], axis=1).astype(BF16)
    wq = w_qb.reshape(Q_RANK, MLA_HEADS, NOPE + ROPE)
    zq = jnp.zeros((Q_RANK, MLA_HEADS, LANES - NOPE - ROPE), F32)
    wq_main = jnp.concatenate([wq, zq], axis=-1).reshape(Q_RANK, MLA_HEADS * LANES).astype(BF16)
    wq_rot = jnp.concatenate([jnp.zeros((Q_RANK, MLA_HEADS, NOPE), F32), _rope_rot_cols(wq[..., NOPE:]), zq],
                             axis=-1).reshape(Q_RANK, MLA_HEADS * LANES).astype(BF16)
    wkv = w_kvb.reshape(KV_RANK, MLA_HEADS, NOPE + V_DIM)
    wk = jnp.concatenate([wkv[..., :NOPE], jnp.zeros((KV_RANK, MLA_HEADS, LANES - NOPE), F32)],
                         axis=-1).reshape(KV_RANK, MLA_HEADS * LANES).astype(BF16)
    wv = wkv[..., NOPE:].reshape(KV_RANK, MLA_HEADS * V_DIM).astype(BF16)
    return win_aug, wq_main, wq_rot, wk, wv


def kernel(x, c, positions, w_ada, b_ada, w_in, q_norm_g, w_qb, kv_norm_g, w_kvb, conv_w, conv_b, filt_w1, filt_b1, filt_freq, filt_w2, filt_b2, filt_w3, filt_b3, hyena_decay, hyena_bias, attn_out_g, hyena_out_g, w_o, ln1_g, ln1_b, w_router, w_gate, w_up, w_down, ln2_g, ln2_b):
    B, S, D = x.shape
    l = 0
    mod = _ada_mod(c, w_ada[l], b_ada[l][None, :])
    sh1, sc1, g1, sh2, sc2, g2 = [m[:, None, :] for m in jnp.split(mod, 6, axis=-1)]

    win_aug, wq_main, wq_rot, wk, wv = _prep_weights(w_in[l], w_qb[l], w_kvb[l])
    cos_h, sin_h = _rope_tables(positions)
    pad = lambda n, v: jnp.full((B, n, S), v, F32)
    cos_t = jnp.concatenate([pad(NOPE, 1.0), cos_h, cos_h, pad(LANES - NOPE - ROPE, 1.0)], axis=1)
    sin_t = jnp.concatenate([pad(NOPE, 0.0), sin_h, sin_h, pad(LANES - NOPE - ROPE, 0.0)], axis=1)
    q, kt, v, hy_conv = _inproj(x, sc1, sh1, cos_t, sin_t, win_aug, q_norm_g[l][None, :], wq_main, wq_rot,
                                kv_norm_g[l][None, :], wk, wv, conv_w[l], conv_b[l])
    a = _attention(q, kt, v)

    hy = _hyena_group(hy_conv, filt_w1[l], filt_b1[l], filt_freq[l], filt_w2[l],
                      filt_b2[l], filt_w3[l], filt_b3[l], hyena_decay[l], hyena_bias[l])

    wr_pad = jnp.concatenate([w_router[l], jnp.zeros((D, LANES - N_EXPERTS), F32)], axis=1)
    x1, u2, aff_t = _outproj(a, hy, x, g1, sc2, sh2, attn_out_g[l][None, :], hyena_out_g[l][None, :],
                             w_o[l].astype(BF16), ln1_g[l][None, :], ln1_b[l][None, :], wr_pad)

    cap = EC_FACTOR * S // N_EXPERTS
    rpe = S // LANES
    aff2 = aff_t.reshape(B, N_EXPERTS * rpe, LANES)
    pos_sel, pos_all, gsel = _select(aff2, cap)
    row_start = pos_all.reshape(B, N_EXPERTS, rpe, LANES)[:, :, :, 0]

    def block_offsets(tokens):
        ends = jnp.full((B, N_EXPERTS, 1), cap, jnp.int32)
        return jnp.concatenate([row_start[:, :, ::tokens // LANES], ends], axis=-1).reshape(-1)

    xe = _gather(block_offsets(GATHER_TOK), u2, pos_sel, cap)
    ye = _moe_ffn(xe, w_gate[l], w_up[l], w_down[l])
    return _combine_ln2(block_offsets(CMB_TOK), x1, pos_sel.reshape(B, N_EXPERTS, S), gsel.reshape(B, N_EXPERTS, S), g2, ln2_g[l][None, :], ln2_b[l][None, :],
                        ye.reshape(N_EXPERTS * B * cap, D), cap)
```

```python
import functools
import math

import jax
import jax.numpy as jnp
import numpy as np
from jax import lax
from jax.experimental import pallas as pl
from jax.experimental.pallas import tpu as pltpu

F32 = jnp.float32
BF16 = jnp.bfloat16
HIGHEST = lax.Precision.HIGHEST

LANES = 128
SUBLANES = 8
VMEM_LIMIT = 56 * 1024 * 1024

MLA_HEADS = 8
NOPE = 64
ROPE = 32
V_DIM = 64
Q_RANK = 256
KV_RANK = 128
HY_W = 512
HY_COLS = 3 * HY_W
N_EXPERTS = 16
EC_FACTOR = 2
ROPE_THETA = 10000.0
EPS = 1e-5
DN_ALPHA = 2.0 ** 0.25
LOG2E = 1.4426950408889634

TOK_TILE = 512
ATT_TQ = 512
ATT_TK = 512
MOE_ROWS = 512


def _params(*sem):
    return pltpu.CompilerParams(dimension_semantics=sem, vmem_limit_bytes=VMEM_LIMIT)


def _ada_kernel(ct_ref, w_ref, b_ref, o_ref):
    ct = ct_ref[...]
    sc = ct * jax.nn.sigmoid(ct)
    w = w_ref[...]
    rows = [jnp.sum(w * sc[:, b:b + 1], axis=0, keepdims=True) for b in range(ct.shape[1])]
    o_ref[...] = jnp.concatenate(rows, axis=0) + b_ref[...]


def _ada_mod(c, w_ada, b_ada):
    B, D = c.shape
    n = w_ada.shape[1]
    tn = 1024
    return pl.pallas_call(
        _ada_kernel,
        grid=(n // tn,),
        in_specs=[
            pl.BlockSpec((D, B), lambda j: (0, 0)),
            pl.BlockSpec((D, tn), lambda j: (0, j)),
            pl.BlockSpec((1, tn), lambda j: (0, j)),
        ],
        out_specs=pl.BlockSpec((B, tn), lambda j: (0, j)),
        out_shape=jax.ShapeDtypeStruct((B, n), F32),
        compiler_params=_params("parallel"),
        name="ada_mod",
    )(c.T, w_ada, b_ada)


def _dot3(a, b):
    ah, bh = a.astype(BF16), b.astype(BF16)
    al = (a - ah.astype(F32)).astype(BF16)
    bl = (b - bh.astype(F32)).astype(BF16)
    dot = functools.partial(jnp.dot, preferred_element_type=F32)
    return dot(ah, bh) + (dot(ah, bl) + dot(al, bh))


def _rms(x, g):
    return x * lax.rsqrt(jnp.mean(x * x, axis=-1, keepdims=True) + EPS) * g


def _rope_kernel(pos_ref, freq_ref, cos_ref, sin_ref):
    ang = freq_ref[...] * pos_ref[0].astype(F32)
    cos, sin = jnp.cos(ang), jnp.sin(ang)
    S = ang.shape[1]
    fill = lambda n, v: jnp.full((n, S), v, F32)
    cos_ref[0] = jnp.concatenate([fill(NOPE, 1.0), cos, cos, fill(LANES - NOPE - ROPE, 1.0)], axis=0)
    sin_ref[0] = jnp.concatenate([fill(NOPE, 0.0), sin, sin, fill(LANES - NOPE - ROPE, 0.0)], axis=0)


def _rope_tables(positions):
    B, S = positions.shape
    half = ROPE // 2
    inv_freq = (ROPE_THETA ** (-jnp.arange(half, dtype=F32) / half))[:, None]
    out = jax.ShapeDtypeStruct((B, LANES, S), F32)
    blk = pl.BlockSpec((1, LANES, S), lambda b: (b, 0, 0))
    return pl.pallas_call(
        _rope_kernel,
        grid=(B,),
        in_specs=[pl.BlockSpec((1, 1, S), lambda b: (b, 0, 0)), pl.BlockSpec((half, 1), lambda b: (0, 0))],
        out_specs=[blk, blk],
        out_shape=[out, out],
        compiler_params=_params("parallel"),
        name="rope_tables",
    )(positions[:, None, :], inv_freq)


def _inproj_kernel(x_ref, xp_ref, xn_ref, sc_ref, sh_ref, cos_ref, sin_ref, win_ref, qg_ref, wq_ref, wqr_ref,
                   kvg_ref, wk_ref, wv_ref, cw_ref, cb_ref, q_ref, k_ref, v_ref, hy_ref):
    def modulate(xv):
        return (xv * (1.0 + sc_ref[0]) + sh_ref[0]).astype(BF16)

    u = modulate(x_ref[0])
    cos = cos_ref[0].T
    sin = sin_ref[0].T

    def proj(lo, hi):
        return jnp.dot(u, win_ref[:, lo:hi], preferred_element_type=F32)

    o = Q_RANK + 3 * KV_RANK
    hy = proj(o, o + HY_COLS)
    tm = hy.shape[0]
    i, last = pl.program_id(1), pl.num_programs(1) - 1
    edge = jnp.dot(modulate(jnp.concatenate([xp_ref[0], xn_ref[0]], axis=0)), win_ref[:, o:o + HY_COLS],
                   preferred_element_type=F32)
    prev = jnp.where(i == 0, 0.0, edge[SUBLANES - 1:SUBLANES, :])
    nxt = jnp.where(i == last, 0.0, edge[SUBLANES:SUBLANES + 1, :])
    row = lax.broadcasted_iota(jnp.int32, hy.shape, 0)
    up = jnp.where(row == 0, prev, pltpu.roll(hy, 1, 0))
    dn = jnp.where(row == tm - 1, nxt, pltpu.roll(hy, tm - 1, 0))
    hy_ref[0] = cb_ref[...] + cw_ref[0:1, :] * up + cw_ref[1:2, :] * hy + cw_ref[2:3, :] * dn

    cq = _rms(proj(0, Q_RANK), qg_ref[...]).astype(BF16)
    q_main = jnp.dot(cq, wq_ref[...], preferred_element_type=F32)
    q_rot = jnp.dot(cq, wqr_ref[...], preferred_element_type=F32)
    qscale = LOG2E * (NOPE + ROPE) ** -0.5
    cq_s = cos * qscale
    sq_s = sin * qscale
    for h in range(MLA_HEADS):
        sl = slice(h * LANES, (h + 1) * LANES)
        q_ref[0, h] = (q_main[:, sl] * cq_s + q_rot[:, sl] * sq_s).T.astype(BF16)

    o = Q_RANK
    ckv = _rms(proj(o, o + KV_RANK), kvg_ref[...]).astype(BF16)
    kr = proj(o + KV_RANK, o + 2 * KV_RANK) * cos + proj(o + 2 * KV_RANK, o + 3 * KV_RANK) * sin
    kn = jnp.dot(ckv, wk_ref[...], preferred_element_type=F32)
    for h in range(MLA_HEADS):
        k_ref[0, h] = (kn[:, h * LANES:(h + 1) * LANES] + kr).astype(BF16)
    v_ref[0] = jnp.dot(ckv, wv_ref[...], preferred_element_type=F32).T.astype(BF16)


def _inproj(x, sc1, sh1, cos_t, sin_t, win_aug, qg, wq, wqr, kvg, wk, wv, conv_w, conv_b):
    B, S, D = x.shape
    tm = TOK_TILE
    ncol = win_aug.shape[1]
    g8 = tm // SUBLANES
    n8 = S // SUBLANES
    full = lambda shape: pl.BlockSpec(shape, lambda b, i: (0,) * len(shape))
    return pl.pallas_call(
        _inproj_kernel,
        grid=(B, S // tm),
        in_specs=[
            pl.BlockSpec((1, tm, D), lambda b, i: (b, i, 0)),
            pl.BlockSpec((1, SUBLANES, D), lambda b, i: (b, jnp.maximum(i * g8 - 1, 0), 0)),
            pl.BlockSpec((1, SUBLANES, D), lambda b, i: (b, jnp.minimum((i + 1) * g8, n8 - 1), 0)),
            pl.BlockSpec((1, 1, D), lambda b, i: (b, 0, 0)),
            pl.BlockSpec((1, 1, D), lambda b, i: (b, 0, 0)),
            pl.BlockSpec((1, LANES, tm), lambda b, i: (b, 0, i)),
            pl.BlockSpec((1, LANES, tm), lambda b, i: (b, 0, i)),
            full((D, ncol)),
            full((1, Q_RANK)),
            full((Q_RANK, MLA_HEADS * LANES)),
            full((Q_RANK, MLA_HEADS * LANES)),
            full((1, KV_RANK)),
            full((KV_RANK, MLA_HEADS * LANES)),
            full((KV_RANK, MLA_HEADS * V_DIM)),
            full(conv_w.shape),
            full((1, HY_COLS)),
        ],
        out_specs=[
            pl.BlockSpec((1, MLA_HEADS, LANES, tm), lambda b, i: (b, 0, 0, i)),
            pl.BlockSpec((1, MLA_HEADS, tm, LANES), lambda b, i: (b, 0, i, 0)),
            pl.BlockSpec((1, MLA_HEADS * V_DIM, tm), lambda b, i: (b, 0, i)),
            pl.BlockSpec((1, tm, HY_COLS), lambda b, i: (b, i, 0)),
        ],
        out_shape=[
            jax.ShapeDtypeStruct((B, MLA_HEADS, LANES, S), BF16),
            jax.ShapeDtypeStruct((B, MLA_HEADS, S, LANES), BF16),
            jax.ShapeDtypeStruct((B, MLA_HEADS * V_DIM, S), BF16),
            jax.ShapeDtypeStruct((B, S, HY_COLS), F32),
        ],
        compiler_params=_params("parallel", "parallel"),
        name="in_proj",
    )(x, x, x, sc1, sh1, cos_t, sin_t, win_aug, qg, wq, wqr, kvg, wk, wv, conv_w, conv_b[None, :])


ATT_AUG = 16


def _attn_kernel(qt_ref, k_ref, vt_ref, o_ref, s_scr):
    S = k_ref.shape[2]
    tq = ATT_TQ
    nk = S // ATT_TK
    nq = S // tq
    ones_rows = jnp.where(lax.broadcasted_iota(jnp.int32, (ATT_AUG, ATT_TK), 0) == 0, 1.0, 0.0).astype(BF16)

    def scores(qi, c, slot):
        k0 = pl.multiple_of(c * ATT_TK, ATT_TK)
        q0 = pl.multiple_of(qi * tq, tq)
        for h in range(2):
            s_scr[slot, h] = jnp.dot(k_ref[0, h, pl.ds(k0, ATT_TK), :], qt_ref[0, h, :, pl.ds(q0, tq)],
                                     preferred_element_type=F32)

    def absorb(c, slot, carry):
        k0 = pl.multiple_of(c * ATT_TK, ATT_TK)
        out = []
        for h in range(2):
            m_prev, acc = carry[h]
            s = s_scr[slot, h]
            m_new = jnp.maximum(m_prev, jnp.max(s, axis=0, keepdims=True))
            alpha = jnp.exp2(m_prev - m_new)
            p = jnp.exp2(s - m_new).astype(BF16)
            v_aug = jnp.concatenate([vt_ref[0, h * V_DIM:(h + 1) * V_DIM, pl.ds(k0, ATT_TK)], ones_rows], axis=0)
            out.append((m_new, alpha * acc + jnp.dot(v_aug, p, preferred_element_type=F32)))
        return tuple(out)

    def query_tile(qi, _):
        def pair(i, carry):
            c0 = 2 * i
            scores(qi, c0 + 1, 1)
            carry = absorb(c0, 0, carry)
            wrap = c0 + 2 >= nk
            scores(jnp.where(wrap, jnp.minimum(qi + 1, nq - 1), qi), jnp.where(wrap, 0, c0 + 2), 0)
            return absorb(c0 + 1, 1, carry)

        init = tuple((jnp.full((1, tq), -jnp.inf, F32), jnp.zeros((V_DIM + ATT_AUG, tq), F32)) for _ in range(2))
        fin = lax.fori_loop(0, nk // 2, pair, init, unroll=4)
        o = jnp.concatenate([acc[:V_DIM] * (1.0 / acc[V_DIM:V_DIM + 1]) for (_, acc) in fin], axis=0)
        o_ref[0, pl.ds(pl.multiple_of(qi * tq, tq), tq), :] = o.T
        return 0

    scores(0, 0, 0)
    lax.fori_loop(0, nq, query_tile, 0)


def _attention(qt, k, vt):
    B, H, _, S = qt.shape
    return pl.pallas_call(
        _attn_kernel,
        grid=(B, H // 2),
        in_specs=[
            pl.BlockSpec((1, 2, LANES, S), lambda b, hp: (b, hp, 0, 0)),
            pl.BlockSpec((1, 2, S, LANES), lambda b, hp: (b, hp, 0, 0)),
            pl.BlockSpec((1, 2 * V_DIM, S), lambda b, hp: (b, hp, 0)),
        ],
        out_specs=pl.BlockSpec((1, S, 2 * V_DIM), lambda b, hp: (b, 0, hp)),
        out_shape=jax.ShapeDtypeStruct((B, S, H * V_DIM), F32),
        scratch_shapes=[pltpu.VMEM((2, 2, ATT_TK, ATT_TQ), F32)],
        compiler_params=_params("parallel", "parallel"),
        name="mla_attention",
    )(qt, k, vt)


def _layer_norm(x, g, b):
    mu = jnp.mean(x, axis=-1, keepdims=True)
    xc = x - mu
    var = jnp.mean(xc * xc, axis=-1, keepdims=True)
    return xc * lax.rsqrt(var + EPS) * g + b


def _outproj_kernel(a_ref, hy_ref, x_ref, g1_ref, sc2_ref, sh2_ref, ag_ref, hg_ref, wo_ref,
                    lg_ref, lb_ref, wr_ref, x1_ref, u2_ref, aff_ref):
    half = a_ref.shape[2]
    tm = a_ref.shape[1]
    nsplit = 2
    for r0 in range(0, tm, tm // nsplit):
        rs = slice(r0, r0 + tm // nsplit)
        an = _rms(a_ref[0, rs, :], ag_ref[...]).astype(BF16)
        hn = _rms(hy_ref[0, rs, :], hg_ref[...]).astype(BF16)
        mixed = (jnp.dot(an, wo_ref[:half, :], preferred_element_type=F32)
                 + jnp.dot(hn, wo_ref[half:, :], preferred_element_type=F32))
        x1 = _layer_norm(DN_ALPHA * x_ref[0, rs, :] + g1_ref[0] * mixed, lg_ref[...], lb_ref[...])
        x1_ref[0, rs, :] = x1
        u2 = x1 * (1.0 + sc2_ref[0]) + sh2_ref[0]
        u2_ref[0, rs, :] = u2.astype(BF16)
        logits = _dot3(u2, wr_ref[...])
        lane = lax.broadcasted_iota(jnp.int32, logits.shape, 1)
        logits = jnp.where(lane < N_EXPERTS, logits, -jnp.inf)
        e = jnp.exp(logits - jnp.max(logits, axis=-1, keepdims=True))
        aff = e / jnp.sum(e, axis=-1, keepdims=True)
        aff_ref[0, :, rs] = aff.T[:N_EXPERTS, :]


def _outproj(a, hy, x, g1, sc2, sh2, ag, hg, wo, lg, lb, wr_pad):
    B, S, D = x.shape
    tm = TOK_TILE
    half = a.shape[2]
    full = lambda shape: pl.BlockSpec(shape, lambda b, i: (0,) * len(shape))
    tok = lambda w: pl.BlockSpec((1, tm, w), lambda b, i: (b, i, 0))
    per_b = pl.BlockSpec((1, 1, D), lambda b, i: (b, 0, 0))
    return pl.pallas_call(
        _outproj_kernel,
        grid=(B, S // tm),
        in_specs=[tok(half), tok(half), tok(D), per_b, per_b, per_b,
                  full((1, half)), full((1, half)), full((D, D)), full((1, D)), full((1, D)),
                  full((D, LANES))],
        out_specs=[tok(D), tok(D), pl.BlockSpec((1, N_EXPERTS, tm), lambda b, i: (b, 0, i))],
        out_shape=[
            jax.ShapeDtypeStruct((B, S, D), F32),
            jax.ShapeDtypeStruct((B, S, D), BF16),
            jax.ShapeDtypeStruct((B, N_EXPERTS, S), F32),
        ],
        compiler_params=_params("parallel", "parallel"),
        name="out_proj_ln1_router",
    )(a, hy, x, g1, sc2, sh2, ag, hg, wo, lg, lb, wr_pad)


def _select_kernel(aff_ref, ones_ref, utri_ref, ltri_ref, pos_ref, posall_ref, g_ref, *, cap):
    aff = aff_ref[0]
    ones = ones_ref[...]

    def rowsum(mf):
        return jnp.dot(mf, ones, preferred_element_type=F32)

    def expert_total(mf):
        x = mf.astype(F32).reshape(N_EXPERTS, aff.shape[0] // N_EXPERTS, LANES)
        tot = jnp.sum(jnp.sum(x, axis=1, keepdims=True), axis=2, keepdims=True)
        return jnp.broadcast_to(tot, x.shape).reshape(aff.shape)

    def prefix(mf):
        return (jnp.dot(mf, utri_ref[...], preferred_element_type=F32)
                + jnp.dot(ltri_ref[...], rowsum(mf).astype(BF16), preferred_element_type=F32))

    def ind(m):
        return jnp.where(m, 1.0, 0.0).astype(BF16)

    def search(i, thr_bits):
        cand = thr_bits | jnp.left_shift(jnp.int32(1), 30 - i)
        enough = expert_total(ind(aff >= lax.bitcast_convert_type(cand, F32))) >= cap
        return jnp.where(enough, cand, thr_bits)

    thr = lax.bitcast_convert_type(lax.fori_loop(0, 31, search, jnp.zeros(aff.shape, jnp.int32)), F32)
    gt = aff > thr
    eq = aff == thr
    room = cap - expert_total(ind(gt))
    take_eq = jnp.where(eq, jnp.where(prefix(ind(eq)) < room, 1.0, 0.0), 0.0)
    sel = jnp.where(gt, 1.0, take_eq)
    pos = prefix(sel.astype(BF16)).astype(jnp.int32)
    posall_ref[0] = pos
    pos_ref[0] = jnp.where(sel > 0.0, pos, -1)
    g_ref[0] = jnp.where(sel > 0.0, aff, 0.0)


def _select(aff2, cap):
    B, R, _ = aff2.shape
    rpe = R // N_EXPERTS
    li = jnp.arange(LANES, dtype=jnp.int32)
    ri = jnp.arange(R, dtype=jnp.int32)
    same = (ri[:, None] // rpe) == (ri[None, :] // rpe)
    ones = jnp.ones((LANES, LANES), BF16)
    utri = (li[:, None] < li[None, :]).astype(BF16)
    ltri = (same & (ri[None, :] < ri[:, None])).astype(BF16)
    blk = pl.BlockSpec((1, R, LANES), lambda b: (b, 0, 0))
    full = lambda arr: pl.BlockSpec(arr.shape, lambda b: (0, 0))
    return pl.pallas_call(
        functools.partial(_select_kernel, cap=cap),
        grid=(B,),
        in_specs=[blk, full(ones), full(utri), full(ltri)],
        out_specs=[blk, blk, blk],
        out_shape=[jax.ShapeDtypeStruct((B, R, LANES), jnp.int32), jax.ShapeDtypeStruct((B, R, LANES), jnp.int32),
                   jax.ShapeDtypeStruct((B, R, LANES), F32)],
        compiler_params=_params("parallel"),
        name="expert_choice_select",
    )(aff2, ones, utri, ltri)


GATHER_TOK = 1024
GATHER_WIN = 160


def _gather_kernel(offs_ref, u_ref, pos_ref, xe_ref, acc, *, cap, nblk):
    b, e = pl.program_id(0), pl.program_id(1)
    base = (b * N_EXPERTS + e) * (nblk + 1)
    rows = GATHER_TOK // LANES
    acc[...] = jnp.zeros(acc.shape, F32)
    slot = lax.broadcasted_iota(jnp.int32, (GATHER_WIN, LANES), 0)

    def block(j, carry):
        off, end = offs_ref[base + j], offs_ref[base + j + 1]
        w = (off // SUBLANES) * SUBLANES
        prow = pos_ref[0, pl.ds(pl.multiple_of(j * rows, rows), rows), :]
        ublk = u_ref[0, pl.ds(pl.multiple_of(j * GATHER_TOK, GATHER_TOK), GATHER_TOK), :]

        def window(i, c):
            s0 = pl.multiple_of(w + i * GATHER_WIN, SUBLANES)
            onehot = jnp.concatenate(
                [jnp.where(slot == prow[r:r + 1, :] - s0, 1.0, 0.0).astype(BF16) for r in range(rows)], axis=1)
            acc[pl.ds(s0, GATHER_WIN), :] += jnp.dot(onehot, ublk, preferred_element_type=F32)
            return c

        window(0, 0)
        lax.fori_loop(1, (end - w + GATHER_WIN - 1) // GATHER_WIN, window, 0)
        return carry

    lax.fori_loop(0, nblk, block, 0, unroll=2)
    xe_ref[0] = acc[:cap, :].astype(BF16)


def _gather(offs, u2, pos_sel, cap):
    B, S, D = u2.shape
    rpe = S // LANES
    nblk = S // GATHER_TOK
    return pl.pallas_call(
        functools.partial(_gather_kernel, cap=cap, nblk=nblk),
        grid_spec=pltpu.PrefetchScalarGridSpec(
            num_scalar_prefetch=1,
            grid=(B, N_EXPERTS),
            in_specs=[pl.BlockSpec((1, S, D), lambda b, e, o: (b, 0, 0)),
                      pl.BlockSpec((1, rpe, LANES), lambda b, e, o: (b, e, 0))],
            out_specs=pl.BlockSpec((1, cap, D), lambda b, e, o: (e, b, 0)),
            scratch_shapes=[pltpu.VMEM((cap + GATHER_WIN + SUBLANES, D), F32)],
        ),
        out_shape=jax.ShapeDtypeStruct((N_EXPERTS, B * cap, D), BF16),
        compiler_params=_params("parallel", "parallel"),
        name="expert_gather",
    )(offs, u2, pos_sel)


def _moe_kernel(x_ref, wg_ref, wu_ref, wd_ref, o_ref, wgb, wub, wdb):
    @pl.when(pl.program_id(1) == 0)
    def _():
        wgb[...] = wg_ref[0].astype(BF16)
        wub[...] = wu_ref[0].astype(BF16)
        wdb[...] = wd_ref[0].astype(BF16)

    x = x_ref[0]
    hg = jnp.dot(x, wgb[...], preferred_element_type=F32)
    hu = jnp.dot(x, wub[...], preferred_element_type=F32)
    h = (hg * jax.nn.sigmoid(hg) * hu).astype(BF16)
    o_ref[0] = jnp.dot(h, wdb[...], preferred_element_type=F32).astype(BF16)


def _moe_ffn(xe, w_gate, w_up, w_down):
    E, R, D = xe.shape
    Fh = w_gate.shape[2]
    tr = MOE_ROWS
    return pl.pallas_call(
        _moe_kernel,
        grid=(E, R // tr),
        in_specs=[
            pl.BlockSpec((1, tr, D), lambda e, r: (e, r, 0)),
            pl.BlockSpec((1, D, Fh), lambda e, r: (e, 0, 0)),
            pl.BlockSpec((1, D, Fh), lambda e, r: (e, 0, 0)),
            pl.BlockSpec((1, Fh, D), lambda e, r: (e, 0, 0)),
        ],
        out_specs=pl.BlockSpec((1, tr, D), lambda e, r: (e, r, 0)),
        out_shape=jax.ShapeDtypeStruct((E, R, D), BF16),
        scratch_shapes=[pltpu.VMEM((D, Fh), BF16), pltpu.VMEM((D, Fh), BF16), pltpu.VMEM((Fh, D), BF16)],
        compiler_params=_params("parallel", "arbitrary"),
        name="moe_swiglu",
    )(xe, w_gate, w_up, w_down)


CMB_TOK = 512
CMB_WIN = 128
BF16_ROWS = 16


def _combine_kernel(offs_ref, x1_ref, pos_ref, g_ref, g2_ref, lg_ref, lb_ref, ye_hbm, o_ref,
                    rhs, lhs, acc, sem, *, cap, nblk, nbatch):
    b, i = pl.program_id(0), pl.program_id(1)
    W = CMB_WIN
    total = ye_hbm.shape[0]
    step = b * nblk + i
    cur_slot = step % 2
    row_e = lax.broadcasted_iota(jnp.int32, (N_EXPERTS, 1), 0)
    slot_w = lax.broadcasted_iota(jnp.int32, (W, CMB_TOK), 0)

    def windows(bb, ii):
        out = []
        for e in range(N_EXPERTS):
            k = (bb * N_EXPERTS + e) * (nblk + 1) + ii
            off, end = offs_ref[k], offs_ref[k + 1]
            lo = (off // BF16_ROWS) * BF16_ROWS
            out.append(((e * nbatch + bb) * cap, lo, jnp.where(end > off, (end - lo + W - 1) // W, 0)))
        return out

    def fetch_starts(win, p):
        return [jnp.minimum(rb + lo + p * W, total - W) for rb, lo, _ in win]

    def window_copy(e, start, s):
        return pltpu.make_async_copy(ye_hbm.at[pl.ds(pl.multiple_of(start, BF16_ROWS), W), :],
                                     rhs.at[s, pl.ds(e * W, W), :], sem.at[s, e])

    def start_fetch(win, p, s):
        for e, start in enumerate(fetch_starts(win, p)):
            window_copy(e, start, s).start()

    def absorb(win, p, s, first):
        starts = fetch_starts(win, p)
        lo_col = jnp.zeros((N_EXPERTS, 1), jnp.int32)
        shift_col = jnp.zeros((N_EXPERTS, 1), jnp.int32)
        for e, (rb, lo0, _) in enumerate(win):
            lo = lo0 + p * W
            lo_col = jnp.where(row_e == e, lo, lo_col)
            shift_col = jnp.where(row_e == e, rb + lo - starts[e], shift_col)
        d = pos_ref[0] - lo_col
        rel = jnp.where((d >= 0) & (d < W), d + shift_col, -1)
        g = g_ref[0]
        for e in range(N_EXPERTS):
            lhs[e * W:(e + 1) * W, :] = jnp.where(slot_w == rel[e:e + 1, :], g[e:e + 1, :], 0.0).astype(BF16)
        for e in range(N_EXPERTS):
            window_copy(e, starts[e], s).wait()
        part = lax.dot_general(lhs[...], rhs[s], (((0,), (0,)), ((), ())), preferred_element_type=F32)
        acc[...] = part if first else acc[...] + part

    cur = windows(b, i)

    @pl.when(step == 0)
    def _():
        start_fetch(cur, 0, 0)

    @pl.when(step + 1 < nbatch * nblk)
    def _():
        wrap = i + 1 == nblk
        start_fetch(windows(jnp.where(wrap, b + 1, b), jnp.where(wrap, 0, i + 1)), 0, 1 - cur_slot)

    absorb(cur, 0, cur_slot, True)

    def extra_pass(p, carry):
        start_fetch(cur, p, cur_slot)
        absorb(cur, p, cur_slot, False)
        return carry

    npass = functools.reduce(jnp.maximum, [n for _, _, n in cur])
    lax.fori_loop(1, npass, extra_pass, 0)
    o_ref[0] = _layer_norm(DN_ALPHA * x1_ref[0] + g2_ref[0] * acc[...], lg_ref[...], lb_ref[...])


def _combine_ln2(offs, x1, pos_exp, g_exp, g2, lg, lb, ye_rows, cap):
    B, S, D = x1.shape
    T = CMB_TOK
    nblk = S // T
    tok = lambda w: pl.BlockSpec((1, T, w), lambda b, i, o: (b, i, 0))
    vec = pl.BlockSpec((1, D), lambda b, i, o: (0, 0))
    exp_major = pl.BlockSpec((1, N_EXPERTS, T), lambda b, i, o: (b, 0, i))
    return pl.pallas_call(
        functools.partial(_combine_kernel, cap=cap, nblk=nblk, nbatch=B),
        grid_spec=pltpu.PrefetchScalarGridSpec(
            num_scalar_prefetch=1,
            grid=(B, nblk),
            in_specs=[tok(D), exp_major, exp_major,
                      pl.BlockSpec((1, 1, D), lambda b, i, o: (b, 0, 0)), vec, vec,
                      pl.BlockSpec(memory_space=pl.ANY)],
            out_specs=tok(D),
            scratch_shapes=[pltpu.VMEM((2, N_EXPERTS * CMB_WIN, D), BF16),
                            pltpu.VMEM((N_EXPERTS * CMB_WIN, T), BF16),
                            pltpu.VMEM((T, D), F32),
                            pltpu.SemaphoreType.DMA((2, N_EXPERTS))],
        ),
        out_shape=jax.ShapeDtypeStruct((B, S, D), F32),
        compiler_params=_params("arbitrary", "arbitrary"),
        name="combine_residual_ln2",
    )(offs, x1, pos_exp, g_exp, g2, lg, lb, ye_rows)


DFT_R = 128
K1_HALF = 40
K1_VALID = DFT_R // 2 + 1
K1_GROUPS = (K1_HALF // SUBLANES, -(-(K1_VALID - K1_HALF) // SUBLANES))
K1_SPEC = K1_HALF + K1_GROUPS[1] * SUBLANES
HY_CT = 128


@functools.lru_cache(maxsize=None)
def _dft_tables_np(S, two_sided):
    N = 2 * S
    half = np.arange(S // DFT_R)
    a_idx = np.concatenate([half, N // DFT_R - 1 - half]) if two_sided else half
    r = np.arange(2 * K1_HALF)
    k1 = np.arange(2)[:, None] * K1_HALF + r[None, :] % K1_HALF
    valid = (k1 < K1_VALID).astype(np.float64)
    n = DFT_R * a_idx[None, :] + np.arange(DFT_R)[:, None]
    theta = ((n[None, :, None, :] * k1[:, None, :, None]) % N) * (2.0 * math.pi / N)
    is_im = (r >= K1_HALF)[None, None, :, None]
    g = np.where(is_im, -np.sin(theta), np.cos(theta)) * valid[:, None, :, None]
    cw = np.where((k1 == 0) | (k1 == DFT_R // 2), 1.0, 2.0) * valid / N
    gi = np.transpose(g * cw[:, None, :, None], (0, 1, 3, 2))
    ph = ((np.arange(DFT_R)[:, None] * np.arange(DFT_R)[None, :]) % DFT_R) * (2.0 * math.pi / DFT_R)
    C, Sn = np.cos(ph), np.sin(ph)
    f3f = np.block([[C, Sn], [-Sn, C]])
    f3i = np.block([[C, -Sn], [Sn, C]])

    def pair_lanes(t):
        h, nb, nr, nc = t.shape
        return t.reshape(h, nb // 2, 2, nr, nc).transpose(0, 1, 3, 2, 4).reshape(h, nb // 2, nr, 2 * nc)

    return tuple(np.ascontiguousarray(t, dtype=np.float32) for t in (pair_lanes(g), pair_lanes(gi), f3f, f3i))


def _dft_tables(S, two_sided=False):
    return tuple(jnp.asarray(t).astype(BF16) for t in _dft_tables_np(S, two_sided))


def _swap01(x):
    return jnp.swapaxes(x, 0, 1)


def _block_diag_lanes(top, bot):
    return jnp.concatenate([jnp.concatenate([top, jnp.zeros((top.shape[0], bot.shape[1]), top.dtype)], axis=1),
                            jnp.concatenate([jnp.zeros((bot.shape[0], top.shape[1]), bot.dtype), bot], axis=1)], axis=0)


def _dft_stage1(load, gf_ref, hf, a_scr):
    ct = a_scr.shape[2]

    def body(g, carry):
        b0 = pl.multiple_of(g * SUBLANES, SUBLANES)
        zt = load(b0)
        for p in range(SUBLANES // 2):
            a2 = jnp.dot(gf_ref[hf, g * (SUBLANES // 2) + p], _block_diag_lanes(zt[2 * p], zt[2 * p + 1]),
                         preferred_element_type=F32)
            a_scr[b0 + 2 * p] = a2[:, :ct]
            a_scr[b0 + 2 * p + 1] = a2[:, ct:]
        return carry

    lax.fori_loop(0, DFT_R // SUBLANES, body, 0, unroll=4)


def _dft_stage3_operands(a_scr, r0):
    mr = _swap01(a_scr[:, pl.ds(r0, SUBLANES), :])
    mi = _swap01(a_scr[:, pl.ds(K1_HALF + r0, SUBLANES), :])
    return [jnp.concatenate([jnp.concatenate([mr[j], mi[j]], axis=0),
                             jnp.concatenate([mr[j + 1], mi[j + 1]], axis=0)], axis=1).astype(BF16)
            for j in range(0, SUBLANES, 2)]


def _filter_mlp_kernel(feat_ref, featx_ref, w1_ref, b1_ref, fr_ref, w2_ref, b2_ref, w3_ref, b3_ref, dec_ref,
                       of_ref, ob_ref):
    tl = feat_ref.shape[0]
    ncol = of_ref.shape[1]
    fr = fr_ref[...]

    def hidden(f):
        h = jnp.sin(fr * (jnp.dot(f, w1_ref[...], precision=HIGHEST, preferred_element_type=F32) + b1_ref[...]))
        return jnp.sin(fr * (jnp.dot(h, w2_ref[...], precision=HIGHEST, preferred_element_type=F32) + b2_ref[...]))

    def taps(h2, t, lo):
        out = _dot3(h2, w3_ref[:, lo:lo + ncol]) + b3_ref[:, lo:lo + ncol]
        return out * jnp.exp(-t * jnp.abs(dec_ref[:, lo:lo + ncol]))

    feats, featx = feat_ref[...], featx_ref[...]
    h2, h2x = hidden(feats), hidden(featx)
    of_ref[...] = taps(h2, feats[:, 0:1], 0)
    row = lax.broadcasted_iota(jnp.int32, (tl, 1), 0)
    last = row == tl - 1
    h2n = jnp.where(last, h2x[0:1, :], pltpu.roll(h2, tl - 1, 0))
    tn = jnp.where(last, featx[0:1, :], pltpu.roll(feats, tl - 1, 0))[:, 0:1]
    ob_ref[...] = jnp.where(last & (pl.program_id(0) == pl.num_programs(0) - 1), 0.0, taps(h2n, tn, ncol))


@functools.lru_cache(maxsize=None)
def _filter_features_np(L, nf):
    pos = np.arange(L + SUBLANES, dtype=np.float64)
    t = pos[:, None] / (L - 1)
    bands = (nf - 1) // 2
    freqs = np.linspace(1e-4, bands - 1, bands)
    phase = (2.0 * math.pi / L) * pos[:, None] * freqs[None, :]
    feats = np.concatenate([t, np.cos(phase), -np.sin(phase), np.zeros((L + SUBLANES, LANES - nf))], axis=-1)
    return np.ascontiguousarray(feats, dtype=np.float32)


def _filter_taps(L, w1, b1, freq, w2, b2, w3, b3, decay):
    nf = w1.shape[0]
    feats = jnp.asarray(_filter_features_np(L, nf))
    w1p = jnp.concatenate([w1, jnp.zeros((LANES - nf, w1.shape[1]), F32)], axis=0)
    fo = w1.shape[1]
    ncol = w3.shape[1] // 2
    tl = 512
    full = lambda shape: pl.BlockSpec(shape, lambda i: (0,) * len(shape))
    out = jax.ShapeDtypeStruct((L, ncol), F32)
    blk = pl.BlockSpec((tl, ncol), lambda i: (i, 0))
    return pl.pallas_call(
        _filter_mlp_kernel,
        grid=(L // tl,),
        in_specs=[pl.BlockSpec((tl, LANES), lambda i: (i, 0)),
                  pl.BlockSpec((SUBLANES, LANES), lambda i: ((i + 1) * (tl // SUBLANES), 0)),
                  full((LANES, fo)), full((1, fo)), full((1, fo)), full((fo, fo)), full((1, fo)),
                  full((fo, 2 * ncol)), full((1, 2 * ncol)), full((1, 2 * ncol))],
        out_specs=[blk, blk],
        out_shape=[out, out],
        compiler_params=_params("parallel"),
        name="hyena_filter_mlp",
    )(feats, feats, w1p, b1[None, :], freq[None, :], w2, b2[None, :], w3, b3[None, :], decay.reshape(1, 2 * ncol))


def _filter_spectrum_kernel(hf_ref, hb_ref, gf_ref, f3f_ref, o_ref, a_scr):
    ct = a_scr.shape[2]

    def load(b0):
        zf = _swap01(hf_ref[:, pl.ds(b0, SUBLANES), :])
        zb = _swap01(hb_ref[:, pl.ds(pl.multiple_of(DFT_R - SUBLANES - b0, SUBLANES), SUBLANES), :])
        return [jnp.concatenate([zf[j], zb[SUBLANES - 1 - j]], axis=0).astype(BF16) for j in range(SUBLANES)]

    for hf in range(2):
        _dft_stage1(load, gf_ref, hf, a_scr)

        def body(g, carry):
            r0 = pl.multiple_of(g * SUBLANES, SUBLANES)
            for p, m in enumerate(_dft_stage3_operands(a_scr, r0)):
                x2 = jnp.dot(f3f_ref[...], m, preferred_element_type=F32)
                for q in range(2):
                    o_ref[0, hf * K1_HALF + r0 + 2 * p + q] = x2[:, q * ct:(q + 1) * ct].astype(BF16)
            return carry

        lax.fori_loop(0, K1_GROUPS[hf], body, 0, unroll=True)


def _filter_spectrum(fwd3, bwd3, gf, f3f):
    na, nb, ncol = fwd3.shape
    C = ncol // 2
    nct = C // HY_CT
    ct = HY_CT
    taps = pl.BlockSpec((na, nb, ct), lambda o, j: (0, 0, o * nct + j))
    return pl.pallas_call(
        _filter_spectrum_kernel,
        grid=(2, nct),
        in_specs=[
            taps, taps,
            pl.BlockSpec(gf.shape, lambda o, j: (0, 0, 0, 0), pipeline_mode=pl.Buffered(1)),
            pl.BlockSpec(f3f.shape, lambda o, j: (0, 0), pipeline_mode=pl.Buffered(1)),
        ],
        out_specs=pl.BlockSpec((1, K1_SPEC, 2 * DFT_R, ct), lambda o, j: (o, 0, 0, j)),
        out_shape=jax.ShapeDtypeStruct((2, K1_SPEC, 2 * DFT_R, C), BF16),
        scratch_shapes=[pltpu.VMEM((DFT_R, 2 * K1_HALF, ct), F32)],
        compiler_params=_params("parallel", "parallel"),
        name="hyena_filter_spectrum",
    )(fwd3, bwd3, gf, f3f)


def _long_conv_kernel(z_ref, xg_ref, h_ref, bias_ref, gf_ref, f3f_ref, f3i_ref, gi_ref, o_ref,
                      a_scr, b_scr, zt_scr, yt_scr):
    ct = a_scr.shape[2]
    bias = bias_ref[0]

    def load_first(b0):
        zt = _swap01(z_ref[0, :, pl.ds(b0, SUBLANES), :]).astype(BF16)
        for j in range(SUBLANES):
            zt_scr[b0 + j] = zt[j]
        return zt

    for hf in range(2):
        _dft_stage1(load_first if hf == 0 else (lambda b0: [zt_scr[b0 + j] for j in range(SUBLANES)]),
                    gf_ref, hf, a_scr)
        ngrp = K1_GROUPS[hf]
        if ngrp * SUBLANES < K1_HALF:
            b_scr[ngrp * SUBLANES:] = jnp.zeros((K1_HALF - ngrp * SUBLANES,) + b_scr.shape[1:], F32)

        def spectral(g, carry):
            r0 = pl.multiple_of(g * SUBLANES, SUBLANES)
            for p, m in enumerate(_dft_stage3_operands(a_scr, r0)):
                x = jnp.dot(f3f_ref[...], m, preferred_element_type=F32)
                k1 = hf * K1_HALF + r0 + 2 * p
                h = jnp.concatenate([h_ref[0, k1], h_ref[0, k1 + 1]], axis=1).astype(F32)
                xr, xi, hr, hi = x[:DFT_R], x[DFT_R:], h[:DFT_R], h[DFT_R:]
                y = jnp.concatenate([xr * hr - xi * hi, xr * hi + xi * hr], axis=0).astype(BF16)
                bb = jnp.dot(f3i_ref[...], y, preferred_element_type=F32)
                b_scr[r0 + 2 * p] = bb[:, :ct]
                b_scr[r0 + 2 * p + 1] = bb[:, ct:]
            return carry

        lax.fori_loop(0, ngrp, spectral, 0, unroll=True)

        def inverse(g, carry):
            b0 = pl.multiple_of(g * SUBLANES, SUBLANES)
            br = _swap01(b_scr[:, pl.ds(b0, SUBLANES), :])
            bi = _swap01(b_scr[:, pl.ds(DFT_R + b0, SUBLANES), :])
            ys = []
            for p in range(SUBLANES // 2):
                bm = [jnp.concatenate([br[2 * p + q], bi[2 * p + q]], axis=0).astype(BF16) for q in range(2)]
                y2 = jnp.dot(gi_ref[hf, g * (SUBLANES // 2) + p], _block_diag_lanes(bm[0], bm[1]),
                             preferred_element_type=F32)
                ys += [y2[:, :ct], y2[:, ct:]]
            if hf == 0:
                for j in range(SUBLANES):
                    yt_scr[b0 + j] = ys[j]
            else:
                y = _swap01(jnp.stack([yt_scr[b0 + j] + ys[j] for j in range(SUBLANES)], axis=0))
                sl = (0, slice(None), pl.ds(b0, SUBLANES), slice(None))
                o_ref[sl] = xg_ref[sl] * (y + z_ref[sl] * bias)
            return carry

        lax.fori_loop(0, DFT_R // SUBLANES, inverse, 0, unroll=4)


def _long_conv(z_arr, z_off, xg_arr, xg_off, spec, order, bias3, gf, f3f, f3i, gi):
    B, na, nb, _ = z_arr.shape
    C = spec.shape[3]
    ct = HY_CT
    const = lambda arr: pl.BlockSpec(arr.shape, lambda j, b: (0,) * arr.ndim, pipeline_mode=pl.Buffered(1))
    return pl.pallas_call(
        _long_conv_kernel,
        grid=(C // ct, B),
        in_specs=[
            pl.BlockSpec((1, na, nb, ct), lambda j, b: (b, 0, 0, z_off + j)),
            pl.BlockSpec((1, na, nb, ct), lambda j, b: (b, 0, 0, xg_off + j)),
            pl.BlockSpec((1, K1_SPEC, 2 * DFT_R, ct), lambda j, b: (order, 0, 0, j), pipeline_mode=pl.Buffered(1)),
            pl.BlockSpec((1, 1, ct), lambda j, b: (order, 0, j)),
            const(gf), const(f3f), const(f3i), const(gi),
        ],
        out_specs=pl.BlockSpec((1, na, nb, ct), lambda j, b: (b, 0, 0, j)),
        out_shape=jax.ShapeDtypeStruct((B, na, nb, C), F32),
        scratch_shapes=[pltpu.VMEM((DFT_R, 2 * K1_HALF, ct), F32), pltpu.VMEM((K1_HALF, 2 * DFT_R, ct), F32),
                        pltpu.VMEM((nb, na, ct), BF16), pltpu.VMEM((nb, na, ct), F32)],
        compiler_params=_params("parallel", "parallel"),
        name="hyena_long_conv",
    )(z_arr, xg_arr, spec, bias3, gf, f3f, f3i, gi)


def _hyena_group(hy_conv, fw1, fb1, ffreq, fw2, fb2, fw3, fb3, decay, hyena_bias):
    B, S, _ = hy_conv.shape
    na = S // DFT_R
    gf, gi, f3f, f3i = _dft_tables(S)
    fwd, bwd = _filter_taps(S, fw1, fb1, ffreq, fw2, fb2, fw3, fb3, decay)
    spec = _filter_spectrum(fwd.reshape(na, DFT_R, -1), bwd.reshape(na, DFT_R, -1),
                            _dft_tables(S, two_sided=True)[0], f3f)
    u = hy_conv.reshape(B, na, DFT_R, -1)
    nct = HY_W // HY_CT
    bias3 = hyena_bias[:, None, :]
    z1 = _long_conv(u, 0, u, nct, spec, 0, bias3, gf, f3f, f3i, gi)
    return _long_conv(z1, 0, u, 2 * nct, spec, 1, bias3, gf, f3f, f3i, gi).reshape(B, S, HY_W)


def _rope_rot_cols(w):
    half = ROPE // 2
    return jnp.concatenate([-w[..., half:], w[..., :half]], axis=-1)


def _prep_weights(w_in, w_qb, w_kvb):
    D = w_in.shape[0]
    o = Q_RANK + KV_RANK
    w_kr = w_in[:, o:o + ROPE]
    z64 = jnp.zeros((D, NOPE), F32)
    z32 = jnp.zeros((D, LANES - NOPE - ROPE), F32)
    win_aug = jnp.concatenate([
        w_in[:, :o],
        z64, w_kr, z32,
        z64, _rope_rot_cols(w_kr), z32,
        w_in[:, o + ROPE:],
    ], axis=1).astype(BF16)
    wq = w_qb.reshape(Q_RANK, MLA_HEADS, NOPE + ROPE)
    zq = jnp.zeros((Q_RANK, MLA_HEADS, LANES - NOPE - ROPE), F32)
    wq_main = jnp.concatenate([wq, zq], axis=-1).reshape(Q_RANK, MLA_HEADS * LANES).astype(BF16)
    wq_rot = jnp.concatenate([jnp.zeros((Q_RANK, MLA_HEADS, NOPE), F32), _rope_rot_cols(wq[..., NOPE:]), zq],
                             axis=-1).reshape(Q_RANK, MLA_HEADS * LANES).astype(BF16)
    wkv = w_kvb.reshape(KV_RANK, MLA_HEADS, NOPE + V_DIM)
    wk = jnp.concatenate([wkv[..., :NOPE], jnp.zeros((KV_RANK, MLA_HEADS, LANES - NOPE), F32)],
                         axis=-1).reshape(KV_RANK, MLA_HEADS * LANES).astype(BF16)
    wv = wkv[..., NOPE:].reshape(KV_RANK, MLA_HEADS * V_DIM).astype(BF16)
    return win_aug, wq_main, wq_rot, wk, wv


def kernel(x, c, positions, w_ada, b_ada, w_in, q_norm_g, w_qb, kv_norm_g, w_kvb, conv_w, conv_b, filt_w1, filt_b1, filt_freq, filt_w2, filt_b2, filt_w3, filt_b3, hyena_decay, hyena_bias, attn_out_g, hyena_out_g, w_o, ln1_g, ln1_b, w_router, w_gate, w_up, w_down, ln2_g, ln2_b):
    B, S, D = x.shape
    l = 0
    mod = _ada_mod(c, w_ada[l], b_ada[l][None, :])
    sh1, sc1, g1, sh2, sc2, g2 = [m[:, None, :] for m in jnp.split(mod, 6, axis=-1)]

    win_aug, wq_main, wq_rot, wk, wv = _prep_weights(w_in[l], w_qb[l], w_kvb[l])
    cos_t, sin_t = _rope_tables(positions)
    q, kt, v, hy_conv = _inproj(x, sc1, sh1, cos_t, sin_t, win_aug, q_norm_g[l][None, :], wq_main, wq_rot,
                                kv_norm_g[l][None, :], wk, wv, conv_w[l], conv_b[l])
    a = _attention(q, kt, v)

    hy = _hyena_group(hy_conv, filt_w1[l], filt_b1[l], filt_freq[l], filt_w2[l],
                      filt_b2[l], filt_w3[l], filt_b3[l], hyena_decay[l], hyena_bias[l])

    wr_pad = jnp.concatenate([w_router[l], jnp.zeros((D, LANES - N_EXPERTS), F32)], axis=1)
    x1, u2, aff_t = _outproj(a, hy, x, g1, sc2, sh2, attn_out_g[l][None, :], hyena_out_g[l][None, :],
                             w_o[l].astype(BF16), ln1_g[l][None, :], ln1_b[l][None, :], wr_pad)

    cap = EC_FACTOR * S // N_EXPERTS
    rpe = S // LANES
    aff2 = aff_t.reshape(B, N_EXPERTS * rpe, LANES)
    pos_sel, pos_all, gsel = _select(aff2, cap)
    row_start = pos_all.reshape(B, N_EXPERTS, rpe, LANES)[:, :, :, 0]

    def block_offsets(tokens):
        ends = jnp.full((B, N_EXPERTS, 1), cap, jnp.int32)
        return jnp.concatenate([row_start[:, :, ::tokens // LANES], ends], axis=-1).reshape(-1)

    xe = _gather(block_offsets(GATHER_TOK), u2, pos_sel, cap)
    ye = _moe_ffn(xe, w_gate[l], w_up[l], w_down[l])
    return _combine_ln2(block_offsets(CMB_TOK), x1, pos_sel.reshape(B, N_EXPERTS, S), gsel.reshape(B, N_EXPERTS, S), g2, ln2_g[l][None, :], ln2_b[l][None, :],
                        ye.reshape(N_EXPERTS * B * cap, D), cap)
```

```python
import functools
import math

import jax
import jax.numpy as jnp
import numpy as np
from jax import lax
from jax.experimental import pallas as pl
from jax.experimental.pallas import tpu as pltpu

F32 = jnp.float32
BF16 = jnp.bfloat16
HIGHEST = lax.Precision.HIGHEST

LANES = 128
SUBLANES = 8
VMEM_LIMIT = 56 * 1024 * 1024

MLA_HEADS = 8
NOPE = 64
ROPE = 32
V_DIM = 64
Q_RANK = 256
KV_RANK = 128
HY_W = 512
HY_COLS = 3 * HY_W
N_EXPERTS = 16
EC_FACTOR = 2
ROPE_THETA = 10000.0
EPS = 1e-5
DN_ALPHA = 2.0 ** 0.25
LOG2E = 1.4426950408889634

TOK_TILE = 512
ATT_TQ = 512
ATT_TK = 512
MOE_ROWS = 512


def _params(*sem):
    return pltpu.CompilerParams(dimension_semantics=sem, vmem_limit_bytes=VMEM_LIMIT)


def _ada_kernel(ct_ref, w_ref, b_ref, o_ref):
    ct = ct_ref[...]
    sc = ct * jax.nn.sigmoid(ct)
    w = w_ref[...]
    rows = [jnp.sum(w * sc[:, b:b + 1], axis=0, keepdims=True) for b in range(ct.shape[1])]
    o_ref[...] = jnp.concatenate(rows, axis=0) + b_ref[...]


def _ada_mod(c, w_ada, b_ada):
    B, D = c.shape
    n = w_ada.shape[1]
    tn = 1024
    return pl.pallas_call(
        _ada_kernel,
        grid=(n // tn,),
        in_specs=[
            pl.BlockSpec((D, B), lambda j: (0, 0)),
            pl.BlockSpec((D, tn), lambda j: (0, j)),
            pl.BlockSpec((1, tn), lambda j: (0, j)),
        ],
        out_specs=pl.BlockSpec((B, tn), lambda j: (0, j)),
        out_shape=jax.ShapeDtypeStruct((B, n), F32),
        compiler_params=_params("parallel"),
        name="ada_mod",
    )(c.T, w_ada, b_ada)


def _dot3(a, b):
    ah, bh = a.astype(BF16), b.astype(BF16)
    al = (a - ah.astype(F32)).astype(BF16)
    bl = (b - bh.astype(F32)).astype(BF16)
    dot = functools.partial(jnp.dot, preferred_element_type=F32)
    return dot(ah, bh) + (dot(ah, bl) + dot(al, bh))


def _rms(x, g):
    return x * lax.rsqrt(jnp.mean(x * x, axis=-1, keepdims=True) + EPS) * g


def _rope_kernel(pos_ref, freq_ref, cos_ref, sin_ref):
    ang = freq_ref[...] * pos_ref[0].astype(F32)
    cos, sin = jnp.cos(ang), jnp.sin(ang)
    S = ang.shape[1]
    fill = lambda n, v: jnp.full((n, S), v, F32)
    cos_ref[0] = jnp.concatenate([fill(NOPE, 1.0), cos, cos, fill(LANES - NOPE - ROPE, 1.0)], axis=0)
    sin_ref[0] = jnp.concatenate([fill(NOPE, 0.0), sin, sin, fill(LANES - NOPE - ROPE, 0.0)], axis=0)


def _rope_tables(positions):
    B, S = positions.shape
    half = ROPE // 2
    inv_freq = (ROPE_THETA ** (-jnp.arange(half, dtype=F32) / half))[:, None]
    out = jax.ShapeDtypeStruct((B, LANES, S), F32)
    blk = pl.BlockSpec((1, LANES, S), lambda b: (b, 0, 0))
    return pl.pallas_call(
        _rope_kernel,
        grid=(B,),
        in_specs=[pl.BlockSpec((1, 1, S), lambda b: (b, 0, 0)), pl.BlockSpec((half, 1), lambda b: (0, 0))],
        out_specs=[blk, blk],
        out_shape=[out, out],
        compiler_params=_params("parallel"),
        name="rope_tables",
    )(positions[:, None, :], inv_freq)


def _inproj_kernel(x_ref, xp_ref, xn_ref, sc_ref, sh_ref, cos_ref, sin_ref, win_ref, qg_ref, wq_ref, wqr_ref,
                   kvg_ref, wk_ref, wv_ref, cw_ref, cb_ref, q_ref, k_ref, v_ref, hy_ref):
    def modulate(xv):
        return (xv * (1.0 + sc_ref[0]) + sh_ref[0]).astype(BF16)

    u = modulate(x_ref[0])
    cos = cos_ref[0].T
    sin = sin_ref[0].T

    def proj(lo, hi):
        return jnp.dot(u, win_ref[:, lo:hi], preferred_element_type=F32)

    o = Q_RANK + 3 * KV_RANK
    hy = proj(o, o + HY_COLS)
    tm = hy.shape[0]
    i, last = pl.program_id(1), pl.num_programs(1) - 1
    edge = jnp.dot(modulate(jnp.concatenate([xp_ref[0], xn_ref[0]], axis=0)), win_ref[:, o:o + HY_COLS],
                   preferred_element_type=F32)
    prev = jnp.where(i == 0, 0.0, edge[SUBLANES - 1:SUBLANES, :])
    nxt = jnp.where(i == last, 0.0, edge[SUBLANES:SUBLANES + 1, :])
    row = lax.broadcasted_iota(jnp.int32, hy.shape, 0)
    up = jnp.where(row == 0, prev, pltpu.roll(hy, 1, 0))
    dn = jnp.where(row == tm - 1, nxt, pltpu.roll(hy, tm - 1, 0))
    hy_ref[0] = cb_ref[...] + cw_ref[0:1, :] * up + cw_ref[1:2, :] * hy + cw_ref[2:3, :] * dn

    cq = _rms(proj(0, Q_RANK), qg_ref[...]).astype(BF16)
    q_main = jnp.dot(cq, wq_ref[...], preferred_element_type=F32)
    q_rot = jnp.dot(cq, wqr_ref[...], preferred_element_type=F32)
    qscale = LOG2E * (NOPE + ROPE) ** -0.5
    cq_s = cos * qscale
    sq_s = sin * qscale
    for h in range(MLA_HEADS):
        sl = slice(h * LANES, (h + 1) * LANES)
        q_ref[0, h] = (q_main[:, sl] * cq_s + q_rot[:, sl] * sq_s).T.astype(BF16)

    o = Q_RANK
    ckv = _rms(proj(o, o + KV_RANK), kvg_ref[...]).astype(BF16)
    kr = proj(o + KV_RANK, o + 2 * KV_RANK) * cos + proj(o + 2 * KV_RANK, o + 3 * KV_RANK) * sin
    kn = jnp.dot(ckv, wk_ref[...], preferred_element_type=F32)
    for h in range(MLA_HEADS):
        k_ref[0, h] = (kn[:, h * LANES:(h + 1) * LANES] + kr).astype(BF16)
    v_ref[0] = jnp.dot(ckv, wv_ref[...], preferred_element_type=F32).T.astype(BF16)


def _inproj(x, sc1, sh1, cos_t, sin_t, win_aug, qg, wq, wqr, kvg, wk, wv, conv_w, conv_b):
    B, S, D = x.shape
    tm = TOK_TILE
    ncol = win_aug.shape[1]
    g8 = tm // SUBLANES
    n8 = S // SUBLANES
    full = lambda shape: pl.BlockSpec(shape, lambda b, i: (0,) * len(shape))
    return pl.pallas_call(
        _inproj_kernel,
        grid=(B, S // tm),
        in_specs=[
            pl.BlockSpec((1, tm, D), lambda b, i: (b, i, 0)),
            pl.BlockSpec((1, SUBLANES, D), lambda b, i: (b, jnp.maximum(i * g8 - 1, 0), 0)),
            pl.BlockSpec((1, SUBLANES, D), lambda b, i: (b, jnp.minimum((i + 1) * g8, n8 - 1), 0)),
            pl.BlockSpec((1, 1, D), lambda b, i: (b, 0, 0)),
            pl.BlockSpec((1, 1, D), lambda b, i: (b, 0, 0)),
            pl.BlockSpec((1, LANES, tm), lambda b, i: (b, 0, i)),
            pl.BlockSpec((1, LANES, tm), lambda b, i: (b, 0, i)),
            full((D, ncol)),
            full((1, Q_RANK)),
            full((Q_RANK, MLA_HEADS * LANES)),
            full((Q_RANK, MLA_HEADS * LANES)),
            full((1, KV_RANK)),
            full((KV_RANK, MLA_HEADS * LANES)),
            full((KV_RANK, MLA_HEADS * V_DIM)),
            full(conv_w.shape),
            full((1, HY_COLS)),
        ],
        out_specs=[
            pl.BlockSpec((1, MLA_HEADS, LANES, tm), lambda b, i: (b, 0, 0, i)),
            pl.BlockSpec((1, MLA_HEADS, tm, LANES), lambda b, i: (b, 0, i, 0)),
            pl.BlockSpec((1, MLA_HEADS * V_DIM, tm), lambda b, i: (b, 0, i)),
            pl.BlockSpec((1, tm, HY_COLS), lambda b, i: (b, i, 0)),
        ],
        out_shape=[
            jax.ShapeDtypeStruct((B, MLA_HEADS, LANES, S), BF16),
            jax.ShapeDtypeStruct((B, MLA_HEADS, S, LANES), BF16),
            jax.ShapeDtypeStruct((B, MLA_HEADS * V_DIM, S), BF16),
            jax.ShapeDtypeStruct((B, S, HY_COLS), F32),
        ],
        compiler_params=_params("parallel", "parallel"),
        name="in_proj",
    )(x, x, x, sc1, sh1, cos_t, sin_t, win_aug, qg, wq, wqr, kvg, wk, wv, conv_w, conv_b[None, :])


ATT_AUG = 16


def _attn_kernel(qt_ref, k_ref, vt_ref, o_ref, s_scr):
    S = k_ref.shape[2]
    tq = ATT_TQ
    nk = S // ATT_TK
    nq = S // tq
    ones_rows = jnp.where(lax.broadcasted_iota(jnp.int32, (ATT_AUG, ATT_TK), 0) == 0, 1.0, 0.0).astype(BF16)

    def scores(qi, c, slot):
        k0 = pl.multiple_of(c * ATT_TK, ATT_TK)
        q0 = pl.multiple_of(qi * tq, tq)
        for h in range(2):
            s_scr[slot, h] = jnp.dot(k_ref[0, h, pl.ds(k0, ATT_TK), :], qt_ref[0, h, :, pl.ds(q0, tq)],
                                     preferred_element_type=F32)

    def absorb(c, slot, carry):
        k0 = pl.multiple_of(c * ATT_TK, ATT_TK)
        out = []
        for h in range(2):
            m_prev, acc = carry[h]
            s = s_scr[slot, h]
            m_new = jnp.maximum(m_prev, jnp.max(s, axis=0, keepdims=True))
            alpha = jnp.exp2(m_prev - m_new)
            p = jnp.exp2(s - m_new).astype(BF16)
            v_aug = jnp.concatenate([vt_ref[0, h * V_DIM:(h + 1) * V_DIM, pl.ds(k0, ATT_TK)], ones_rows], axis=0)
            out.append((m_new, alpha * acc + jnp.dot(v_aug, p, preferred_element_type=F32)))
        return tuple(out)

    def query_tile(qi, _):
        def pair(i, carry):
            c0 = 2 * i
            scores(qi, c0 + 1, 1)
            carry = absorb(c0, 0, carry)
            wrap = c0 + 2 >= nk
            scores(jnp.where(wrap, jnp.minimum(qi + 1, nq - 1), qi), jnp.where(wrap, 0, c0 + 2), 0)
            return absorb(c0 + 1, 1, carry)

        init = tuple((jnp.full((1, tq), -jnp.inf, F32), jnp.zeros((V_DIM + ATT_AUG, tq), F32)) for _ in range(2))
        fin = lax.fori_loop(0, nk // 2, pair, init, unroll=4)
        o = jnp.concatenate([acc[:V_DIM] * (1.0 / acc[V_DIM:V_DIM + 1]) for (_, acc) in fin], axis=0)
        o_ref[0, pl.ds(pl.multiple_of(qi * tq, tq), tq), :] = o.T
        return 0

    scores(0, 0, 0)
    lax.fori_loop(0, nq, query_tile, 0)


def _attention(qt, k, vt):
    B, H, _, S = qt.shape
    return pl.pallas_call(
        _attn_kernel,
        grid=(B, H // 2),
        in_specs=[
            pl.BlockSpec((1, 2, LANES, S), lambda b, hp: (b, hp, 0, 0)),
            pl.BlockSpec((1, 2, S, LANES), lambda b, hp: (b, hp, 0, 0)),
            pl.BlockSpec((1, 2 * V_DIM, S), lambda b, hp: (b, hp, 0)),
        ],
        out_specs=pl.BlockSpec((1, S, 2 * V_DIM), lambda b, hp: (b, 0, hp)),
        out_shape=jax.ShapeDtypeStruct((B, S, H * V_DIM), F32),
        scratch_shapes=[pltpu.VMEM((2, 2, ATT_TK, ATT_TQ), F32)],
        compiler_params=_params("parallel", "parallel"),
        name="mla_attention",
    )(qt, k, vt)


def _layer_norm(x, g, b):
    mu = jnp.mean(x, axis=-1, keepdims=True)
    xc = x - mu
    var = jnp.mean(xc * xc, axis=-1, keepdims=True)
    return xc * lax.rsqrt(var + EPS) * g + b


def _outproj_kernel(a_ref, hy_ref, x_ref, g1_ref, sc2_ref, sh2_ref, ag_ref, hg_ref, wo_ref,
                    lg_ref, lb_ref, wr_ref, x1_ref, u2_ref, aff_ref):
    half = a_ref.shape[2]
    tm = a_ref.shape[1]
    nsplit = 2
    for r0 in range(0, tm, tm // nsplit):
        rs = slice(r0, r0 + tm // nsplit)
        an = _rms(a_ref[0, rs, :], ag_ref[...]).astype(BF16)
        hn = _rms(hy_ref[0, rs, :], hg_ref[...]).astype(BF16)
        mixed = (jnp.dot(an, wo_ref[:half, :], preferred_element_type=F32)
                 + jnp.dot(hn, wo_ref[half:, :], preferred_element_type=F32))
        x1 = _layer_norm(DN_ALPHA * x_ref[0, rs, :] + g1_ref[0] * mixed, lg_ref[...], lb_ref[...])
        x1_ref[0, rs, :] = x1
        u2 = x1 * (1.0 + sc2_ref[0]) + sh2_ref[0]
        u2_ref[0, rs, :] = u2.astype(BF16)
        logits = _dot3(u2, wr_ref[...])
        lane = lax.broadcasted_iota(jnp.int32, logits.shape, 1)
        logits = jnp.where(lane < N_EXPERTS, logits, -jnp.inf)
        e = jnp.exp(logits - jnp.max(logits, axis=-1, keepdims=True))
        aff = e / jnp.sum(e, axis=-1, keepdims=True)
        aff_ref[0, :, rs] = aff.T[:N_EXPERTS, :]


def _outproj(a, hy, x, g1, sc2, sh2, ag, hg, wo, lg, lb, wr_pad):
    B, S, D = x.shape
    tm = TOK_TILE
    half = a.shape[2]
    full = lambda shape: pl.BlockSpec(shape, lambda b, i: (0,) * len(shape))
    tok = lambda w: pl.BlockSpec((1, tm, w), lambda b, i: (b, i, 0))
    per_b = pl.BlockSpec((1, 1, D), lambda b, i: (b, 0, 0))
    return pl.pallas_call(
        _outproj_kernel,
        grid=(B, S // tm),
        in_specs=[tok(half), tok(half), tok(D), per_b, per_b, per_b,
                  full((1, half)), full((1, half)), full((D, D)), full((1, D)), full((1, D)),
                  full((D, LANES))],
        out_specs=[tok(D), tok(D), pl.BlockSpec((1, N_EXPERTS, tm), lambda b, i: (b, 0, i))],
        out_shape=[
            jax.ShapeDtypeStruct((B, S, D), F32),
            jax.ShapeDtypeStruct((B, S, D), BF16),
            jax.ShapeDtypeStruct((B, N_EXPERTS, S), F32),
        ],
        compiler_params=_params("parallel", "parallel"),
        name="out_proj_ln1_router",
    )(a, hy, x, g1, sc2, sh2, ag, hg, wo, lg, lb, wr_pad)


def _select_kernel(aff_ref, ones_ref, utri_ref, ltri_ref, pos_ref, posall_ref, g_ref, *, cap):
    aff = aff_ref[0]
    ones = ones_ref[...]

    def rowsum(mf):
        return jnp.dot(mf, ones, preferred_element_type=F32)

    def expert_total(mf):
        x = mf.astype(F32).reshape(N_EXPERTS, aff.shape[0] // N_EXPERTS, LANES)
        tot = jnp.sum(jnp.sum(x, axis=1, keepdims=True), axis=2, keepdims=True)
        return jnp.broadcast_to(tot, x.shape).reshape(aff.shape)

    def prefix(mf):
        return (jnp.dot(mf, utri_ref[...], preferred_element_type=F32)
                + jnp.dot(ltri_ref[...], rowsum(mf).astype(BF16), preferred_element_type=F32))

    def ind(m):
        return jnp.where(m, 1.0, 0.0).astype(BF16)

    def search(i, thr_bits):
        cand = thr_bits | jnp.left_shift(jnp.int32(1), 30 - i)
        enough = expert_total(ind(aff >= lax.bitcast_convert_type(cand, F32))) >= cap
        return jnp.where(enough, cand, thr_bits)

    thr = lax.bitcast_convert_type(lax.fori_loop(0, 31, search, jnp.zeros(aff.shape, jnp.int32)), F32)
    gt = aff > thr
    eq = aff == thr
    room = cap - expert_total(ind(gt))
    take_eq = jnp.where(eq, jnp.where(prefix(ind(eq)) < room, 1.0, 0.0), 0.0)
    sel = jnp.where(gt, 1.0, take_eq)
    pos = prefix(sel.astype(BF16)).astype(jnp.int32)
    posall_ref[0] = pos
    pos_ref[0] = jnp.where(sel > 0.0, pos, -1)
    g_ref[0] = jnp.where(sel > 0.0, aff, 0.0)


def _select(aff2, cap):
    B, R, _ = aff2.shape
    rpe = R // N_EXPERTS
    li = jnp.arange(LANES, dtype=jnp.int32)
    ri = jnp.arange(R, dtype=jnp.int32)
    same = (ri[:, None] // rpe) == (ri[None, :] // rpe)
    ones = jnp.ones((LANES, LANES), BF16)
    utri = (li[:, None] < li[None, :]).astype(BF16)
    ltri = (same & (ri[None, :] < ri[:, None])).astype(BF16)
    blk = pl.BlockSpec((1, R, LANES), lambda b: (b, 0, 0))
    full = lambda arr: pl.BlockSpec(arr.shape, lambda b: (0, 0))
    return pl.pallas_call(
        functools.partial(_select_kernel, cap=cap),
        grid=(B,),
        in_specs=[blk, full(ones), full(utri), full(ltri)],
        out_specs=[blk, blk, blk],
        out_shape=[jax.ShapeDtypeStruct((B, R, LANES), jnp.int32), jax.ShapeDtypeStruct((B, R, LANES), jnp.int32),
                   jax.ShapeDtypeStruct((B, R, LANES), F32)],
        compiler_params=_params("parallel"),
        name="expert_choice_select",
    )(aff2, ones, utri, ltri)


GATHER_TOK = 1024
GATHER_WIN = 160


def _gather_kernel(offs_ref, u_ref, pos_ref, xe_ref, acc, *, cap, nblk):
    b, e = pl.program_id(0), pl.program_id(1)
    base = (b * N_EXPERTS + e) * (nblk + 1)
    rows = GATHER_TOK // LANES
    acc[...] = jnp.zeros(acc.shape, F32)
    slot = lax.broadcasted_iota(jnp.int32, (GATHER_WIN, LANES), 0)

    def block(j, first_window):
        off, end = offs_ref[base + j], offs_ref[base + j + 1]
        w = (off // SUBLANES) * SUBLANES

        def window(i, c):
            s0 = pl.multiple_of(w + i * GATHER_WIN, SUBLANES)
            prow = pos_ref[0, pl.ds(pl.multiple_of(j * rows, rows), rows), :]
            ublk = u_ref[0, pl.ds(pl.multiple_of(j * GATHER_TOK, GATHER_TOK), GATHER_TOK), :]
            onehot = jnp.concatenate(
                [jnp.where(slot == prow[r:r + 1, :] - s0, 1.0, 0.0).astype(BF16) for r in range(rows)], axis=1)
            acc[pl.ds(s0, GATHER_WIN), :] += jnp.dot(onehot, ublk, preferred_element_type=F32)
            return c

        if first_window:
            window(0, 0)
        else:
            lax.fori_loop(1, (end - w + GATHER_WIN - 1) // GATHER_WIN, window, 0)
        return 0

    lax.fori_loop(0, nblk, lambda j, c: block(j, True), 0, unroll=4)
    lax.fori_loop(0, nblk, lambda j, c: block(j, False), 0)
    xe_ref[0] = acc[:cap, :].astype(BF16)


def _gather(offs, u2, pos_sel, cap):
    B, S, D = u2.shape
    rpe = S // LANES
    nblk = S // GATHER_TOK
    return pl.pallas_call(
        functools.partial(_gather_kernel, cap=cap, nblk=nblk),
        grid_spec=pltpu.PrefetchScalarGridSpec(
            num_scalar_prefetch=1,
            grid=(B, N_EXPERTS),
            in_specs=[pl.BlockSpec((1, S, D), lambda b, e, o: (b, 0, 0)),
                      pl.BlockSpec((1, rpe, LANES), lambda b, e, o: (b, e, 0))],
            out_specs=pl.BlockSpec((1, cap, D), lambda b, e, o: (e, b, 0)),
            scratch_shapes=[pltpu.VMEM((cap + GATHER_WIN + SUBLANES, D), F32)],
        ),
        out_shape=jax.ShapeDtypeStruct((N_EXPERTS, B * cap, D), BF16),
        compiler_params=_params("parallel", "parallel"),
        name="expert_gather",
    )(offs, u2, pos_sel)


def _moe_kernel(x_ref, wg_ref, wu_ref, wd_ref, o_ref, wgb, wub, wdb):
    @pl.when(pl.program_id(1) == 0)
    def _():
        wgb[...] = wg_ref[0].astype(BF16)
        wub[...] = wu_ref[0].astype(BF16)
        wdb[...] = wd_ref[0].astype(BF16)

    x = x_ref[0]
    hg = jnp.dot(x, wgb[...], preferred_element_type=F32)
    hu = jnp.dot(x, wub[...], preferred_element_type=F32)
    h = (hg * jax.nn.sigmoid(hg) * hu).astype(BF16)
    o_ref[0] = jnp.dot(h, wdb[...], preferred_element_type=F32).astype(BF16)


def _moe_ffn(xe, w_gate, w_up, w_down):
    E, R, D = xe.shape
    Fh = w_gate.shape[2]
    tr = MOE_ROWS
    return pl.pallas_call(
        _moe_kernel,
        grid=(E, R // tr),
        in_specs=[
            pl.BlockSpec((1, tr, D), lambda e, r: (e, r, 0)),
            pl.BlockSpec((1, D, Fh), lambda e, r: (e, 0, 0)),
            pl.BlockSpec((1, D, Fh), lambda e, r: (e, 0, 0)),
            pl.BlockSpec((1, Fh, D), lambda e, r: (e, 0, 0)),
        ],
        out_specs=pl.BlockSpec((1, tr, D), lambda e, r: (e, r, 0)),
        out_shape=jax.ShapeDtypeStruct((E, R, D), BF16),
        scratch_shapes=[pltpu.VMEM((D, Fh), BF16), pltpu.VMEM((D, Fh), BF16), pltpu.VMEM((Fh, D), BF16)],
        compiler_params=_params("parallel", "arbitrary"),
        name="moe_swiglu",
    )(xe, w_gate, w_up, w_down)


CMB_TOK = 512
CMB_WIN = 128
BF16_ROWS = 16


def _combine_kernel(offs_ref, x1_ref, pos_ref, g_ref, g2_ref, lg_ref, lb_ref, ye_hbm, o_ref,
                    rhs, lhs, acc, sem, *, cap, nblk, nbatch):
    b, i = pl.program_id(0), pl.program_id(1)
    W = CMB_WIN
    total = ye_hbm.shape[0]
    step = b * nblk + i
    cur_slot = step % 2
    row_e = lax.broadcasted_iota(jnp.int32, (N_EXPERTS, 1), 0)
    slot_w = lax.broadcasted_iota(jnp.int32, (W, CMB_TOK), 0)

    def windows(bb, ii):
        out = []
        for e in range(N_EXPERTS):
            k = (bb * N_EXPERTS + e) * (nblk + 1) + ii
            off, end = offs_ref[k], offs_ref[k + 1]
            lo = (off // BF16_ROWS) * BF16_ROWS
            out.append(((e * nbatch + bb) * cap, lo, jnp.where(end > off, (end - lo + W - 1) // W, 0)))
        return out

    def fetch_starts(win, p):
        return [jnp.minimum(rb + lo + p * W, total - W) for rb, lo, _ in win]

    def window_copy(e, start, s):
        return pltpu.make_async_copy(ye_hbm.at[pl.ds(pl.multiple_of(start, BF16_ROWS), W), :],
                                     rhs.at[s, pl.ds(e * W, W), :], sem.at[s, e])

    def start_fetch(win, p, s):
        for e, start in enumerate(fetch_starts(win, p)):
            window_copy(e, start, s).start()

    def absorb(win, p, s, first):
        starts = fetch_starts(win, p)
        lo_col = jnp.zeros((N_EXPERTS, 1), jnp.int32)
        shift_col = jnp.zeros((N_EXPERTS, 1), jnp.int32)
        for e, (rb, lo0, _) in enumerate(win):
            lo = lo0 + p * W
            lo_col = jnp.where(row_e == e, lo, lo_col)
            shift_col = jnp.where(row_e == e, rb + lo - starts[e], shift_col)
        d = pos_ref[0] - lo_col
        rel = jnp.where((d >= 0) & (d < W), d + shift_col, -1)
        g = g_ref[0]
        for e in range(N_EXPERTS):
            lhs[e * W:(e + 1) * W, :] = jnp.where(slot_w == rel[e:e + 1, :], g[e:e + 1, :], 0.0).astype(BF16)
        for e in range(N_EXPERTS):
            window_copy(e, starts[e], s).wait()
        part = lax.dot_general(lhs[...], rhs[s], (((0,), (0,)), ((), ())), preferred_element_type=F32)
        acc[...] = part if first else acc[...] + part

    cur = windows(b, i)

    @pl.when(step == 0)
    def _():
        start_fetch(cur, 0, 0)

    @pl.when(step + 1 < nbatch * nblk)
    def _():
        wrap = i + 1 == nblk
        start_fetch(windows(jnp.where(wrap, b + 1, b), jnp.where(wrap, 0, i + 1)), 0, 1 - cur_slot)

    absorb(cur, 0, cur_slot, True)

    def extra_pass(p, carry):
        start_fetch(cur, p, cur_slot)
        absorb(cur, p, cur_slot, False)
        return carry

    npass = functools.reduce(jnp.maximum, [n for _, _, n in cur])
    lax.fori_loop(1, npass, extra_pass, 0)
    o_ref[0] = _layer_norm(DN_ALPHA * x1_ref[0] + g2_ref[0] * acc[...], lg_ref[...], lb_ref[...])


def _combine_ln2(offs, x1, pos_exp, g_exp, g2, lg, lb, ye_rows, cap):
    B, S, D = x1.shape
    T = CMB_TOK
    nblk = S // T
    tok = lambda w: pl.BlockSpec((1, T, w), lambda b, i, o: (b, i, 0))
    vec = pl.BlockSpec((1, D), lambda b, i, o: (0, 0))
    exp_major = pl.BlockSpec((1, N_EXPERTS, T), lambda b, i, o: (b, 0, i))
    return pl.pallas_call(
        functools.partial(_combine_kernel, cap=cap, nblk=nblk, nbatch=B),
        grid_spec=pltpu.PrefetchScalarGridSpec(
            num_scalar_prefetch=1,
            grid=(B, nblk),
            in_specs=[tok(D), exp_major, exp_major,
                      pl.BlockSpec((1, 1, D), lambda b, i, o: (b, 0, 0)), vec, vec,
                      pl.BlockSpec(memory_space=pl.ANY)],
            out_specs=tok(D),
            scratch_shapes=[pltpu.VMEM((2, N_EXPERTS * CMB_WIN, D), BF16),
                            pltpu.VMEM((N_EXPERTS * CMB_WIN, T), BF16),
                            pltpu.VMEM((T, D), F32),
                            pltpu.SemaphoreType.DMA((2, N_EXPERTS))],
        ),
        out_shape=jax.ShapeDtypeStruct((B, S, D), F32),
        compiler_params=_params("arbitrary", "arbitrary"),
        name="combine_residual_ln2",
    )(offs, x1, pos_exp, g_exp, g2, lg, lb, ye_rows)


DFT_R = 128
K1_HALF = 40
K1_VALID = DFT_R // 2 + 1
K1_GROUPS = (K1_HALF // SUBLANES, -(-(K1_VALID - K1_HALF) // SUBLANES))
K1_SPEC = K1_HALF + K1_GROUPS[1] * SUBLANES
HY_CT = 128


@functools.lru_cache(maxsize=None)
def _dft_tables_np(S, two_sided):
    N = 2 * S
    half = np.arange(S // DFT_R)
    a_idx = np.concatenate([half, N // DFT_R - 1 - half]) if two_sided else half
    r = np.arange(2 * K1_HALF)
    k1 = np.arange(2)[:, None] * K1_HALF + r[None, :] % K1_HALF
    valid = (k1 < K1_VALID).astype(np.float64)
    n = DFT_R * a_idx[None, :] + np.arange(DFT_R)[:, None]
    theta = ((n[None, :, None, :] * k1[:, None, :, None]) % N) * (2.0 * math.pi / N)
    is_im = (r >= K1_HALF)[None, None, :, None]
    g = np.where(is_im, -np.sin(theta), np.cos(theta)) * valid[:, None, :, None]
    cw = np.where((k1 == 0) | (k1 == DFT_R // 2), 1.0, 2.0) * valid / N
    gi = np.transpose(g * cw[:, None, :, None], (0, 1, 3, 2))
    ph = ((np.arange(DFT_R)[:, None] * np.arange(DFT_R)[None, :]) % DFT_R) * (2.0 * math.pi / DFT_R)
    C, Sn = np.cos(ph), np.sin(ph)
    f3f = np.block([[C, Sn], [-Sn, C]])
    f3i = np.block([[C, -Sn], [Sn, C]])

    def pair_lanes(t):
        h, nb, nr, nc = t.shape
        return t.reshape(h, nb // 2, 2, nr, nc).transpose(0, 1, 3, 2, 4).reshape(h, nb // 2, nr, 2 * nc)

    return tuple(np.ascontiguousarray(t, dtype=np.float32) for t in (pair_lanes(g), pair_lanes(gi), f3f, f3i))


def _dft_tables(S, two_sided=False):
    return tuple(jnp.asarray(t).astype(BF16) for t in _dft_tables_np(S, two_sided))


def _swap01(x):
    return jnp.swapaxes(x, 0, 1)


def _block_diag_lanes(top, bot):
    return jnp.concatenate([jnp.concatenate([top, jnp.zeros((top.shape[0], bot.shape[1]), top.dtype)], axis=1),
                            jnp.concatenate([jnp.zeros((bot.shape[0], top.shape[1]), bot.dtype), bot], axis=1)], axis=0)


def _dft_stage1(load, gf_ref, hf, a_scr):
    ct = a_scr.shape[2]

    def body(g, carry):
        b0 = pl.multiple_of(g * SUBLANES, SUBLANES)
        zt = load(b0)
        for p in range(SUBLANES // 2):
            a2 = jnp.dot(gf_ref[hf, g * (SUBLANES // 2) + p], _block_diag_lanes(zt[2 * p], zt[2 * p + 1]),
                         preferred_element_type=F32)
            a_scr[b0 + 2 * p] = a2[:, :ct]
            a_scr[b0 + 2 * p + 1] = a2[:, ct:]
        return carry

    lax.fori_loop(0, DFT_R // SUBLANES, body, 0, unroll=4)


def _dft_stage3_operands(a_scr, r0):
    mr = _swap01(a_scr[:, pl.ds(r0, SUBLANES), :])
    mi = _swap01(a_scr[:, pl.ds(K1_HALF + r0, SUBLANES), :])
    return [jnp.concatenate([jnp.concatenate([mr[j], mi[j]], axis=0),
                             jnp.concatenate([mr[j + 1], mi[j + 1]], axis=0)], axis=1).astype(BF16)
            for j in range(0, SUBLANES, 2)]


def _filter_mlp_kernel(feat_ref, featx_ref, w1_ref, b1_ref, fr_ref, w2_ref, b2_ref, w3_ref, b3_ref, dec_ref,
                       of_ref, ob_ref):
    tl = feat_ref.shape[0]
    ncol = of_ref.shape[1]
    fr = fr_ref[...]

    def hidden(f):
        h = jnp.sin(fr * (jnp.dot(f, w1_ref[...], precision=HIGHEST, preferred_element_type=F32) + b1_ref[...]))
        return jnp.sin(fr * (jnp.dot(h, w2_ref[...], precision=HIGHEST, preferred_element_type=F32) + b2_ref[...]))

    def taps(h2, t, lo):
        out = _dot3(h2, w3_ref[:, lo:lo + ncol]) + b3_ref[:, lo:lo + ncol]
        return out * jnp.exp(-t * jnp.abs(dec_ref[:, lo:lo + ncol]))

    feats, featx = feat_ref[...], featx_ref[...]
    h2, h2x = hidden(feats), hidden(featx)
    of_ref[...] = taps(h2, feats[:, 0:1], 0)
    row = lax.broadcasted_iota(jnp.int32, (tl, 1), 0)
    last = row == tl - 1
    h2n = jnp.where(last, h2x[0:1, :], pltpu.roll(h2, tl - 1, 0))
    tn = jnp.where(last, featx[0:1, :], pltpu.roll(feats, tl - 1, 0))[:, 0:1]
    ob_ref[...] = jnp.where(last & (pl.program_id(0) == pl.num_programs(0) - 1), 0.0, taps(h2n, tn, ncol))


@functools.lru_cache(maxsize=None)
def _filter_features_np(L, nf):
    pos = np.arange(L + SUBLANES, dtype=np.float64)
    t = pos[:, None] / (L - 1)
    bands = (nf - 1) // 2
    freqs = np.linspace(1e-4, bands - 1, bands)
    phase = (2.0 * math.pi / L) * pos[:, None] * freqs[None, :]
    feats = np.concatenate([t, np.cos(phase), -np.sin(phase), np.zeros((L + SUBLANES, LANES - nf))], axis=-1)
    return np.ascontiguousarray(feats, dtype=np.float32)


def _filter_taps(L, w1, b1, freq, w2, b2, w3, b3, decay):
    nf = w1.shape[0]
    feats = jnp.asarray(_filter_features_np(L, nf))
    w1p = jnp.concatenate([w1, jnp.zeros((LANES - nf, w1.shape[1]), F32)], axis=0)
    fo = w1.shape[1]
    ncol = w3.shape[1] // 2
    tl = 512
    full = lambda shape: pl.BlockSpec(shape, lambda i: (0,) * len(shape))
    out = jax.ShapeDtypeStruct((L, ncol), F32)
    blk = pl.BlockSpec((tl, ncol), lambda i: (i, 0))
    return pl.pallas_call(
        _filter_mlp_kernel,
        grid=(L // tl,),
        in_specs=[pl.BlockSpec((tl, LANES), lambda i: (i, 0)),
                  pl.BlockSpec((SUBLANES, LANES), lambda i: ((i + 1) * (tl // SUBLANES), 0)),
                  full((LANES, fo)), full((1, fo)), full((1, fo)), full((fo, fo)), full((1, fo)),
                  full((fo, 2 * ncol)), full((1, 2 * ncol)), full((1, 2 * ncol))],
        out_specs=[blk, blk],
        out_shape=[out, out],
        compiler_params=_params("parallel"),
        name="hyena_filter_mlp",
    )(feats, feats, w1p, b1[None, :], freq[None, :], w2, b2[None, :], w3, b3[None, :], decay.reshape(1, 2 * ncol))


def _filter_spectrum_kernel(hf_ref, hb_ref, gf_ref, f3f_ref, o_ref, a_scr):
    ct = a_scr.shape[2]

    def load(b0):
        zf = _swap01(hf_ref[:, pl.ds(b0, SUBLANES), :])
        zb = _swap01(hb_ref[:, pl.ds(pl.multiple_of(DFT_R - SUBLANES - b0, SUBLANES), SUBLANES), :])
        return [jnp.concatenate([zf[j], zb[SUBLANES - 1 - j]], axis=0).astype(BF16) for j in range(SUBLANES)]

    for hf in range(2):
        _dft_stage1(load, gf_ref, hf, a_scr)

        def body(g, carry):
            r0 = pl.multiple_of(g * SUBLANES, SUBLANES)
            for p, m in enumerate(_dft_stage3_operands(a_scr, r0)):
                x2 = jnp.dot(f3f_ref[...], m, preferred_element_type=F32)
                for q in range(2):
                    o_ref[0, hf * K1_HALF + r0 + 2 * p + q] = x2[:, q * ct:(q + 1) * ct].astype(BF16)
            return carry

        lax.fori_loop(0, K1_GROUPS[hf], body, 0, unroll=True)


def _filter_spectrum(fwd3, bwd3, gf, f3f):
    na, nb, ncol = fwd3.shape
    C = ncol // 2
    nct = C // HY_CT
    ct = HY_CT
    taps = pl.BlockSpec((na, nb, ct), lambda o, j: (0, 0, o * nct + j))
    return pl.pallas_call(
        _filter_spectrum_kernel,
        grid=(2, nct),
        in_specs=[
            taps, taps,
            pl.BlockSpec(gf.shape, lambda o, j: (0, 0, 0, 0), pipeline_mode=pl.Buffered(1)),
            pl.BlockSpec(f3f.shape, lambda o, j: (0, 0), pipeline_mode=pl.Buffered(1)),
        ],
        out_specs=pl.BlockSpec((1, K1_SPEC, 2 * DFT_R, ct), lambda o, j: (o, 0, 0, j)),
        out_shape=jax.ShapeDtypeStruct((2, K1_SPEC, 2 * DFT_R, C), BF16),
        scratch_shapes=[pltpu.VMEM((DFT_R, 2 * K1_HALF, ct), F32)],
        compiler_params=_params("parallel", "parallel"),
        name="hyena_filter_spectrum",
    )(fwd3, bwd3, gf, f3f)


def _long_conv_kernel(z_ref, xg_ref, h_ref, bias_ref, gf_ref, f3f_ref, f3i_ref, gi_ref, o_ref,
                      a_scr, b_scr, zt_scr, yt_scr):
    ct = a_scr.shape[2]
    bias = bias_ref[0]

    def load_first(b0):
        zt = _swap01(z_ref[0, :, pl.ds(b0, SUBLANES), :]).astype(BF16)
        for j in range(SUBLANES):
            zt_scr[b0 + j] = zt[j]
        return zt

    for hf in range(2):
        _dft_stage1(load_first if hf == 0 else (lambda b0: [zt_scr[b0 + j] for j in range(SUBLANES)]),
                    gf_ref, hf, a_scr)
        ngrp = K1_GROUPS[hf]
        if ngrp * SUBLANES < K1_HALF:
            b_scr[ngrp * SUBLANES:] = jnp.zeros((K1_HALF - ngrp * SUBLANES,) + b_scr.shape[1:], F32)

        def spectral(g, carry):
            r0 = pl.multiple_of(g * SUBLANES, SUBLANES)
            for p, m in enumerate(_dft_stage3_operands(a_scr, r0)):
                x = jnp.dot(f3f_ref[...], m, preferred_element_type=F32)
                k1 = hf * K1_HALF + r0 + 2 * p
                h = jnp.concatenate([h_ref[0, k1], h_ref[0, k1 + 1]], axis=1).astype(F32)
                xr, xi, hr, hi = x[:DFT_R], x[DFT_R:], h[:DFT_R], h[DFT_R:]
                y = jnp.concatenate([xr * hr - xi * hi, xr * hi + xi * hr], axis=0).astype(BF16)
                bb = jnp.dot(f3i_ref[...], y, preferred_element_type=F32)
                b_scr[r0 + 2 * p] = bb[:, :ct]
                b_scr[r0 + 2 * p + 1] = bb[:, ct:]
            return carry

        lax.fori_loop(0, ngrp, spectral, 0, unroll=True)

        def inverse(g, carry):
            b0 = pl.multiple_of(g * SUBLANES, SUBLANES)
            br = _swap01(b_scr[:, pl.ds(b0, SUBLANES), :])
            bi = _swap01(b_scr[:, pl.ds(DFT_R + b0, SUBLANES), :])
            ys = []
            for p in range(SUBLANES // 2):
                bm = [jnp.concatenate([br[2 * p + q], bi[2 * p + q]], axis=0).astype(BF16) for q in range(2)]
                y2 = jnp.dot(gi_ref[hf, g * (SUBLANES // 2) + p], _block_diag_lanes(bm[0], bm[1]),
                             preferred_element_type=F32)
                ys += [y2[:, :ct], y2[:, ct:]]
            if hf == 0:
                for j in range(SUBLANES):
                    yt_scr[b0 + j] = ys[j]
            else:
                y = _swap01(jnp.stack([yt_scr[b0 + j] + ys[j] for j in range(SUBLANES)], axis=0))
                sl = (0, slice(None), pl.ds(b0, SUBLANES), slice(None))
                o_ref[sl] = xg_ref[sl] * (y + z_ref[sl] * bias)
            return carry

        lax.fori_loop(0, DFT_R // SUBLANES, inverse, 0, unroll=4)


def _long_conv(z_arr, z_off, xg_arr, xg_off, spec, order, bias3, gf, f3f, f3i, gi):
    B, na, nb, _ = z_arr.shape
    C = spec.shape[3]
    ct = HY_CT
    const = lambda arr: pl.BlockSpec(arr.shape, lambda j, b: (0,) * arr.ndim, pipeline_mode=pl.Buffered(1))
    return pl.pallas_call(
        _long_conv_kernel,
        grid=(C // ct, B),
        in_specs=[
            pl.BlockSpec((1, na, nb, ct), lambda j, b: (b, 0, 0, z_off + j)),
            pl.BlockSpec((1, na, nb, ct), lambda j, b: (b, 0, 0, xg_off + j)),
            pl.BlockSpec((1, K1_SPEC, 2 * DFT_R, ct), lambda j, b: (order, 0, 0, j), pipeline_mode=pl.Buffered(1)),
            pl.BlockSpec((1, 1, ct), lambda j, b: (order, 0, j)),
            const(gf), const(f3f), const(f3i), const(gi),
        ],
        out_specs=pl.BlockSpec((1, na, nb, ct), lambda j, b: (b, 0, 0, j)),
        out_shape=jax.ShapeDtypeStruct((B, na, nb, C), F32),
        scratch_shapes=[pltpu.VMEM((DFT_R, 2 * K1_HALF, ct), F32), pltpu.VMEM((K1_HALF, 2 * DFT_R, ct), F32),
                        pltpu.VMEM((nb, na, ct), BF16), pltpu.VMEM((nb, na, ct), F32)],
        compiler_params=_params("parallel", "parallel"),
        name="hyena_long_conv",
    )(z_arr, xg_arr, spec, bias3, gf, f3f, f3i, gi)


def _hyena_group(hy_conv, fw1, fb1, ffreq, fw2, fb2, fw3, fb3, decay, hyena_bias):
    B, S, _ = hy_conv.shape
    na = S // DFT_R
    gf, gi, f3f, f3i = _dft_tables(S)
    fwd, bwd = _filter_taps(S, fw1, fb1, ffreq, fw2, fb2, fw3, fb3, decay)
    spec = _filter_spectrum(fwd.reshape(na, DFT_R, -1), bwd.reshape(na, DFT_R, -1),
                            _dft_tables(S, two_sided=True)[0], f3f)
    u = hy_conv.reshape(B, na, DFT_R, -1)
    nct = HY_W // HY_CT
    bias3 = hyena_bias[:, None, :]
    z1 = _long_conv(u, 0, u, nct, spec, 0, bias3, gf, f3f, f3i, gi)
    return _long_conv(z1, 0, u, 2 * nct, spec, 1, bias3, gf, f3f, f3i, gi).reshape(B, S, HY_W)


def _rope_rot_cols(w):
    half = ROPE // 2
    return jnp.concatenate([-w[..., half:], w[..., :half]], axis=-1)


def _prep_weights(w_in, w_qb, w_kvb):
    D = w_in.shape[0]
    o = Q_RANK + KV_RANK
    w_kr = w_in[:, o:o + ROPE]
    z64 = jnp.zeros((D, NOPE), F32)
    z32 = jnp.zeros((D, LANES - NOPE - ROPE), F32)
    win_aug = jnp.concatenate([
        w_in[:, :o],
        z64, w_kr, z32,
        z64, _rope_rot_cols(w_kr), z32,
        w_in[:, o + ROPE:],
    ], axis=1).astype(BF16)
    wq = w_qb.reshape(Q_RANK, MLA_HEADS, NOPE + ROPE)
    zq = jnp.zeros((Q_RANK, MLA_HEADS, LANES - NOPE - ROPE), F32)
    wq_main = jnp.concatenate([wq, zq], axis=-1).reshape(Q_RANK, MLA_HEADS * LANES).astype(BF16)
    wq_rot = jnp.concatenate([jnp.zeros((Q_RANK, MLA_HEADS, NOPE), F32), _rope_rot_cols(wq[..., NOPE:]), zq],
                             axis=-1).reshape(Q_RANK, MLA_HEADS * LANES).astype(BF16)
    wkv = w_kvb.reshape(KV_RANK, MLA_HEADS, NOPE + V_DIM)
    wk = jnp.concatenate([wkv[..., :NOPE], jnp.zeros((KV_RANK, MLA_HEADS, LANES - NOPE), F32)],
                         axis=-1).reshape(KV_RANK, MLA_HEADS * LANES).astype(BF16)
    wv = wkv[..., NOPE:].reshape(KV_RANK, MLA_HEADS * V_DIM).astype(BF16)
    return win_aug, wq_main, wq_rot, wk, wv


def kernel(x, c, positions, w_ada, b_ada, w_in, q_norm_g, w_qb, kv_norm_g, w_kvb, conv_w, conv_b, filt_w1, filt_b1, filt_freq, filt_w2, filt_b2, filt_w3, filt_b3, hyena_decay, hyena_bias, attn_out_g, hyena_out_g, w_o, ln1_g, ln1_b, w_router, w_gate, w_up, w_down, ln2_g, ln2_b):
    B, S, D = x.shape
    l = 0
    mod = _ada_mod(c, w_ada[l], b_ada[l][None, :])
    sh1, sc1, g1, sh2, sc2, g2 = [m[:, None, :] for m in jnp.split(mod, 6, axis=-1)]

    win_aug, wq_main, wq_rot, wk, wv = _prep_weights(w_in[l], w_qb[l], w_kvb[l])
    cos_t, sin_t = _rope_tables(positions)
    q, kt, v, hy_conv = _inproj(x, sc1, sh1, cos_t, sin_t, win_aug, q_norm_g[l][None, :], wq_main, wq_rot,
                                kv_norm_g[l][None, :], wk, wv, conv_w[l], conv_b[l])
    a = _attention(q, kt, v)

    hy = _hyena_group(hy_conv, filt_w1[l], filt_b1[l], filt_freq[l], filt_w2[l],
                      filt_b2[l], filt_w3[l], filt_b3[l], hyena_decay[l], hyena_bias[l])

    wr_pad = jnp.concatenate([w_router[l], jnp.zeros((D, LANES - N_EXPERTS), F32)], axis=1)
    x1, u2, aff_t = _outproj(a, hy, x, g1, sc2, sh2, attn_out_g[l][None, :], hyena_out_g[l][None, :],
                             w_o[l].astype(BF16), ln1_g[l][None, :], ln1_b[l][None, :], wr_pad)

    cap = EC_FACTOR * S // N_EXPERTS
    rpe = S // LANES
    aff2 = aff_t.reshape(B, N_EXPERTS * rpe, LANES)
    pos_sel, pos_all, gsel = _select(aff2, cap)
    row_start = pos_all.reshape(B, N_EXPERTS, rpe, LANES)[:, :, :, 0]

    def block_offsets(tokens):
        ends = jnp.full((B, N_EXPERTS, 1), cap, jnp.int32)
        return jnp.concatenate([row_start[:, :, ::tokens // LANES], ends], axis=-1).reshape(-1)

    xe = _gather(block_offsets(GATHER_TOK), u2, pos_sel, cap)
    ye = _moe_ffn(xe, w_gate[l], w_up[l], w_down[l])
    return _combine_ln2(block_offsets(CMB_TOK), x1, pos_sel.reshape(B, N_EXPERTS, S), gsel.reshape(B, N_EXPERTS, S), g2, ln2_g[l][None, :], ln2_b[l][None, :],
                        ye.reshape(N_EXPERTS * B * cap, D), cap)
```

```python
import functools
import math

import jax
import jax.numpy as jnp
import numpy as np
from jax import lax
from jax.experimental import pallas as pl
from jax.experimental.pallas import tpu as pltpu

F32 = jnp.float32
BF16 = jnp.bfloat16
HIGHEST = lax.Precision.HIGHEST

LANES = 128
SUBLANES = 8
VMEM_LIMIT = 56 * 1024 * 1024

MLA_HEADS = 8
NOPE = 64
ROPE = 32
V_DIM = 64
Q_RANK = 256
KV_RANK = 128
HY_W = 512
HY_COLS = 3 * HY_W
N_EXPERTS = 16
EC_FACTOR = 2
ROPE_THETA = 10000.0
EPS = 1e-5
DN_ALPHA = 2.0 ** 0.25
LOG2E = 1.4426950408889634

TOK_TILE = 512
ATT_TQ = 512
ATT_TK = 512
MOE_ROWS = 512


def _params(*sem):
    return pltpu.CompilerParams(dimension_semantics=sem, vmem_limit_bytes=VMEM_LIMIT)


def _ada_kernel(ct_ref, w_ref, b_ref, o_ref):
    ct = ct_ref[...]
    sc = ct * jax.nn.sigmoid(ct)
    w = w_ref[...]
    rows = [jnp.sum(w * sc[:, b:b + 1], axis=0, keepdims=True) for b in range(ct.shape[1])]
    o_ref[...] = jnp.concatenate(rows, axis=0) + b_ref[...]


def _ada_mod(c, w_ada, b_ada):
    B, D = c.shape
    n = w_ada.shape[1]
    tn = 1024
    return pl.pallas_call(
        _ada_kernel,
        grid=(n // tn,),
        in_specs=[
            pl.BlockSpec((D, B), lambda j: (0, 0)),
            pl.BlockSpec((D, tn), lambda j: (0, j)),
            pl.BlockSpec((1, tn), lambda j: (0, j)),
        ],
        out_specs=pl.BlockSpec((B, tn), lambda j: (0, j)),
        out_shape=jax.ShapeDtypeStruct((B, n), F32),
        compiler_params=_params("parallel"),
        name="ada_mod",
    )(c.T, w_ada, b_ada)


def _dot3(a, b):
    ah, bh = a.astype(BF16), b.astype(BF16)
    al = (a - ah.astype(F32)).astype(BF16)
    bl = (b - bh.astype(F32)).astype(BF16)
    dot = functools.partial(jnp.dot, preferred_element_type=F32)
    return dot(ah, bh) + (dot(ah, bl) + dot(al, bh))


def _rms(x, g):
    return x * lax.rsqrt(jnp.mean(x * x, axis=-1, keepdims=True) + EPS) * g


def _rope_kernel(pos_ref, freq_ref, cos_ref, sin_ref):
    ang = freq_ref[...] * pos_ref[0].astype(F32)
    cos, sin = jnp.cos(ang), jnp.sin(ang)
    S = ang.shape[1]
    fill = lambda n, v: jnp.full((n, S), v, F32)
    cos_ref[0] = jnp.concatenate([fill(NOPE, 1.0), cos, cos, fill(LANES - NOPE - ROPE, 1.0)], axis=0)
    sin_ref[0] = jnp.concatenate([fill(NOPE, 0.0), sin, sin, fill(LANES - NOPE - ROPE, 0.0)], axis=0)


def _rope_tables(positions):
    B, S = positions.shape
    half = ROPE // 2
    inv_freq = (ROPE_THETA ** (-jnp.arange(half, dtype=F32) / half))[:, None]
    out = jax.ShapeDtypeStruct((B, LANES, S), F32)
    blk = pl.BlockSpec((1, LANES, S), lambda b: (b, 0, 0))
    return pl.pallas_call(
        _rope_kernel,
        grid=(B,),
        in_specs=[pl.BlockSpec((1, 1, S), lambda b: (b, 0, 0)), pl.BlockSpec((half, 1), lambda b: (0, 0))],
        out_specs=[blk, blk],
        out_shape=[out, out],
        compiler_params=_params("parallel"),
        name="rope_tables",
    )(positions[:, None, :], inv_freq)


def _inproj_kernel(x_ref, xp_ref, xn_ref, sc_ref, sh_ref, cos_ref, sin_ref, win_ref, qg_ref, wq_ref, wqr_ref,
                   kvg_ref, wk_ref, wv_ref, cw_ref, cb_ref, q_ref, k_ref, v_ref, hy_ref):
    def modulate(xv):
        return (xv * (1.0 + sc_ref[0]) + sh_ref[0]).astype(BF16)

    u = modulate(x_ref[0])
    cos = cos_ref[0].T
    sin = sin_ref[0].T

    def proj(lo, hi):
        return jnp.dot(u, win_ref[:, lo:hi], preferred_element_type=F32)

    o = Q_RANK + 3 * KV_RANK
    hy = proj(o, o + HY_COLS)
    tm = hy.shape[0]
    i, last = pl.program_id(1), pl.num_programs(1) - 1
    edge = jnp.dot(modulate(jnp.concatenate([xp_ref[0], xn_ref[0]], axis=0)), win_ref[:, o:o + HY_COLS],
                   preferred_element_type=F32)
    prev = jnp.where(i == 0, 0.0, edge[SUBLANES - 1:SUBLANES, :])
    nxt = jnp.where(i == last, 0.0, edge[SUBLANES:SUBLANES + 1, :])
    row = lax.broadcasted_iota(jnp.int32, hy.shape, 0)
    up = jnp.where(row == 0, prev, pltpu.roll(hy, 1, 0))
    dn = jnp.where(row == tm - 1, nxt, pltpu.roll(hy, tm - 1, 0))
    hy_ref[0] = cb_ref[...] + cw_ref[0:1, :] * up + cw_ref[1:2, :] * hy + cw_ref[2:3, :] * dn

    cq = _rms(proj(0, Q_RANK), qg_ref[...]).astype(BF16)
    q_main = jnp.dot(cq, wq_ref[...], preferred_element_type=F32)
    q_rot = jnp.dot(cq, wqr_ref[...], preferred_element_type=F32)
    qscale = LOG2E * (NOPE + ROPE) ** -0.5
    cq_s = cos * qscale
    sq_s = sin * qscale
    for h in range(MLA_HEADS):
        sl = slice(h * LANES, (h + 1) * LANES)
        q_ref[0, h] = (q_main[:, sl] * cq_s + q_rot[:, sl] * sq_s).T.astype(BF16)

    o = Q_RANK
    ckv = _rms(proj(o, o + KV_RANK), kvg_ref[...]).astype(BF16)
    kr = proj(o + KV_RANK, o + 2 * KV_RANK) * cos + proj(o + 2 * KV_RANK, o + 3 * KV_RANK) * sin
    kn = jnp.dot(ckv, wk_ref[...], preferred_element_type=F32)
    for h in range(MLA_HEADS):
        k_ref[0, h] = (kn[:, h * LANES:(h + 1) * LANES] + kr).astype(BF16)
    v_ref[0] = jnp.dot(ckv, wv_ref[...], preferred_element_type=F32).T.astype(BF16)


def _inproj(x, sc1, sh1, cos_t, sin_t, win_aug, qg, wq, wqr, kvg, wk, wv, conv_w, conv_b):
    B, S, D = x.shape
    tm = TOK_TILE
    ncol = win_aug.shape[1]
    g8 = tm // SUBLANES
    n8 = S // SUBLANES
    full = lambda shape: pl.BlockSpec(shape, lambda b, i: (0,) * len(shape))
    return pl.pallas_call(
        _inproj_kernel,
        grid=(B, S // tm),
        in_specs=[
            pl.BlockSpec((1, tm, D), lambda b, i: (b, i, 0)),
            pl.BlockSpec((1, SUBLANES, D), lambda b, i: (b, jnp.maximum(i * g8 - 1, 0), 0)),
            pl.BlockSpec((1, SUBLANES, D), lambda b, i: (b, jnp.minimum((i + 1) * g8, n8 - 1), 0)),
            pl.BlockSpec((1, 1, D), lambda b, i: (b, 0, 0)),
            pl.BlockSpec((1, 1, D), lambda b, i: (b, 0, 0)),
            pl.BlockSpec((1, LANES, tm), lambda b, i: (b, 0, i)),
            pl.BlockSpec((1, LANES, tm), lambda b, i: (b, 0, i)),
            full((D, ncol)),
            full((1, Q_RANK)),
            full((Q_RANK, MLA_HEADS * LANES)),
            full((Q_RANK, MLA_HEADS * LANES)),
            full((1, KV_RANK)),
            full((KV_RANK, MLA_HEADS * LANES)),
            full((KV_RANK, MLA_HEADS * V_DIM)),
            full(conv_w.shape),
            full((1, HY_COLS)),
        ],
        out_specs=[
            pl.BlockSpec((1, MLA_HEADS, LANES, tm), lambda b, i: (b, 0, 0, i)),
            pl.BlockSpec((1, MLA_HEADS, tm, LANES), lambda b, i: (b, 0, i, 0)),
            pl.BlockSpec((1, MLA_HEADS * V_DIM, tm), lambda b, i: (b, 0, i)),
            pl.BlockSpec((1, tm, HY_COLS), lambda b, i: (b, i, 0)),
        ],
        out_shape=[
            jax.ShapeDtypeStruct((B, MLA_HEADS, LANES, S), BF16),
            jax.ShapeDtypeStruct((B, MLA_HEADS, S, LANES), BF16),
            jax.ShapeDtypeStruct((B, MLA_HEADS * V_DIM, S), BF16),
            jax.ShapeDtypeStruct((B, S, HY_COLS), F32),
        ],
        compiler_params=_params("parallel", "parallel"),
        name="in_proj",
    )(x, x, x, sc1, sh1, cos_t, sin_t, win_aug, qg, wq, wqr, kvg, wk, wv, conv_w, conv_b[None, :])


ATT_AUG = 16


def _attn_kernel(qt_ref, k_ref, vt_ref, o_ref, s_scr, cmax_scr):
    S = k_ref.shape[2]
    tq = ATT_TQ
    nk = S // ATT_TK
    nq = S // tq
    ones_rows = jnp.where(lax.broadcasted_iota(jnp.int32, (ATT_AUG, ATT_TK), 0) == 0, 1.0, 0.0).astype(BF16)

    def scores(qi, c, slot):
        k0 = pl.multiple_of(c * ATT_TK, ATT_TK)
        q0 = pl.multiple_of(qi * tq, tq)
        for h in range(2):
            s = jnp.dot(k_ref[0, h, pl.ds(k0, ATT_TK), :], qt_ref[0, h, :, pl.ds(q0, tq)],
                        preferred_element_type=F32)
            s_scr[slot, h] = s
            cmax_scr[slot, h] = jnp.max(s, axis=0, keepdims=True)

    def absorb(c, slot, carry):
        k0 = pl.multiple_of(c * ATT_TK, ATT_TK)
        out = []
        for h in range(2):
            m_prev, acc = carry[h]
            s = s_scr[slot, h]
            m_new = jnp.maximum(m_prev, cmax_scr[slot, h])
            alpha = jnp.exp2(m_prev - m_new)
            p = jnp.exp2(s - m_new).astype(BF16)
            v_aug = jnp.concatenate([vt_ref[0, h * V_DIM:(h + 1) * V_DIM, pl.ds(k0, ATT_TK)], ones_rows], axis=0)
            out.append((m_new, alpha * acc + jnp.dot(v_aug, p, preferred_element_type=F32)))
        return tuple(out)

    def query_tile(qi, _):
        def pair(i, carry):
            c0 = 2 * i
            scores(qi, c0 + 1, 1)
            carry = absorb(c0, 0, carry)
            wrap = c0 + 2 >= nk
            scores(jnp.where(wrap, jnp.minimum(qi + 1, nq - 1), qi), jnp.where(wrap, 0, c0 + 2), 0)
            return absorb(c0 + 1, 1, carry)

        init = tuple((jnp.full((1, tq), -jnp.inf, F32), jnp.zeros((V_DIM + ATT_AUG, tq), F32)) for _ in range(2))
        fin = lax.fori_loop(0, nk // 2, pair, init, unroll=4)
        o = jnp.concatenate([acc[:V_DIM] * (1.0 / acc[V_DIM:V_DIM + 1]) for (_, acc) in fin], axis=0)
        o_ref[0, pl.ds(pl.multiple_of(qi * tq, tq), tq), :] = o.T
        return 0

    scores(0, 0, 0)
    lax.fori_loop(0, nq, query_tile, 0)


def _attention(qt, k, vt):
    B, H, _, S = qt.shape
    return pl.pallas_call(
        _attn_kernel,
        grid=(B, H // 2),
        in_specs=[
            pl.BlockSpec((1, 2, LANES, S), lambda b, hp: (b, hp, 0, 0)),
            pl.BlockSpec((1, 2, S, LANES), lambda b, hp: (b, hp, 0, 0)),
            pl.BlockSpec((1, 2 * V_DIM, S), lambda b, hp: (b, hp, 0)),
        ],
        out_specs=pl.BlockSpec((1, S, 2 * V_DIM), lambda b, hp: (b, 0, hp)),
        out_shape=jax.ShapeDtypeStruct((B, S, H * V_DIM), F32),
        scratch_shapes=[pltpu.VMEM((2, 2, ATT_TK, ATT_TQ), F32), pltpu.VMEM((2, 2, 1, ATT_TQ), F32)],
        compiler_params=_params("parallel", "parallel"),
        name="mla_attention",
    )(qt, k, vt)


def _layer_norm(x, g, b):
    mu = jnp.mean(x, axis=-1, keepdims=True)
    xc = x - mu
    var = jnp.mean(xc * xc, axis=-1, keepdims=True)
    return xc * lax.rsqrt(var + EPS) * g + b


def _outproj_kernel(a_ref, hy_ref, x_ref, g1_ref, sc2_ref, sh2_ref, ag_ref, hg_ref, wo_ref,
                    lg_ref, lb_ref, wr_ref, x1_ref, u2_ref, aff_ref):
    half = a_ref.shape[2]
    tm = a_ref.shape[1]
    nsplit = 2
    for r0 in range(0, tm, tm // nsplit):
        rs = slice(r0, r0 + tm // nsplit)
        an = _rms(a_ref[0, rs, :], ag_ref[...]).astype(BF16)
        hn = _rms(hy_ref[0, rs, :], hg_ref[...]).astype(BF16)
        mixed = (jnp.dot(an, wo_ref[:half, :], preferred_element_type=F32)
                 + jnp.dot(hn, wo_ref[half:, :], preferred_element_type=F32))
        x1 = _layer_norm(DN_ALPHA * x_ref[0, rs, :] + g1_ref[0] * mixed, lg_ref[...], lb_ref[...])
        x1_ref[0, rs, :] = x1
        u2 = x1 * (1.0 + sc2_ref[0]) + sh2_ref[0]
        u2_ref[0, rs, :] = u2.astype(BF16)
        logits = _dot3(u2, wr_ref[...])
        lane = lax.broadcasted_iota(jnp.int32, logits.shape, 1)
        logits = jnp.where(lane < N_EXPERTS, logits, -jnp.inf)
        e = jnp.exp(logits - jnp.max(logits, axis=-1, keepdims=True))
        aff = e / jnp.sum(e, axis=-1, keepdims=True)
        aff_ref[0, :, rs] = aff.T[:N_EXPERTS, :]


def _outproj(a, hy, x, g1, sc2, sh2, ag, hg, wo, lg, lb, wr_pad):
    B, S, D = x.shape
    tm = TOK_TILE
    half = a.shape[2]
    full = lambda shape: pl.BlockSpec(shape, lambda b, i: (0,) * len(shape))
    tok = lambda w: pl.BlockSpec((1, tm, w), lambda b, i: (b, i, 0))
    per_b = pl.BlockSpec((1, 1, D), lambda b, i: (b, 0, 0))
    return pl.pallas_call(
        _outproj_kernel,
        grid=(B, S // tm),
        in_specs=[tok(half), tok(half), tok(D), per_b, per_b, per_b,
                  full((1, half)), full((1, half)), full((D, D)), full((1, D)), full((1, D)),
                  full((D, LANES))],
        out_specs=[tok(D), tok(D), pl.BlockSpec((1, N_EXPERTS, tm), lambda b, i: (b, 0, i))],
        out_shape=[
            jax.ShapeDtypeStruct((B, S, D), F32),
            jax.ShapeDtypeStruct((B, S, D), BF16),
            jax.ShapeDtypeStruct((B, N_EXPERTS, S), F32),
        ],
        compiler_params=_params("parallel", "parallel"),
        name="out_proj_ln1_router",
    )(a, hy, x, g1, sc2, sh2, ag, hg, wo, lg, lb, wr_pad)


def _select_kernel(aff_ref, ones_ref, utri_ref, ltri_ref, pos_ref, posall_ref, g_ref, *, cap):
    aff = aff_ref[0]
    ones = ones_ref[...]

    def rowsum(mf):
        return jnp.dot(mf, ones, preferred_element_type=F32)

    def expert_total(mf):
        x = mf.astype(F32).reshape(N_EXPERTS, aff.shape[0] // N_EXPERTS, LANES)
        tot = jnp.sum(jnp.sum(x, axis=1, keepdims=True), axis=2, keepdims=True)
        return jnp.broadcast_to(tot, x.shape).reshape(aff.shape)

    def prefix(mf):
        return (jnp.dot(mf, utri_ref[...], preferred_element_type=F32)
                + jnp.dot(ltri_ref[...], rowsum(mf).astype(BF16), preferred_element_type=F32))

    def ind(m):
        return jnp.where(m, 1.0, 0.0).astype(BF16)

    def search(i, thr_bits):
        cand = thr_bits | jnp.left_shift(jnp.int32(1), 30 - i)
        enough = expert_total(ind(aff >= lax.bitcast_convert_type(cand, F32))) >= cap
        return jnp.where(enough, cand, thr_bits)

    thr = lax.bitcast_convert_type(lax.fori_loop(0, 31, search, jnp.zeros(aff.shape, jnp.int32)), F32)
    gt = aff > thr
    eq = aff == thr
    room = cap - expert_total(ind(gt))
    take_eq = jnp.where(eq, jnp.where(prefix(ind(eq)) < room, 1.0, 0.0), 0.0)
    sel = jnp.where(gt, 1.0, take_eq)
    pos = prefix(sel.astype(BF16)).astype(jnp.int32)
    posall_ref[0] = pos
    pos_ref[0] = jnp.where(sel > 0.0, pos, -1)
    g_ref[0] = jnp.where(sel > 0.0, aff, 0.0)


def _select(aff2, cap):
    B, R, _ = aff2.shape
    rpe = R // N_EXPERTS
    li = jnp.arange(LANES, dtype=jnp.int32)
    ri = jnp.arange(R, dtype=jnp.int32)
    same = (ri[:, None] // rpe) == (ri[None, :] // rpe)
    ones = jnp.ones((LANES, LANES), BF16)
    utri = (li[:, None] < li[None, :]).astype(BF16)
    ltri = (same & (ri[None, :] < ri[:, None])).astype(BF16)
    blk = pl.BlockSpec((1, R, LANES), lambda b: (b, 0, 0))
    full = lambda arr: pl.BlockSpec(arr.shape, lambda b: (0, 0))
    return pl.pallas_call(
        functools.partial(_select_kernel, cap=cap),
        grid=(B,),
        in_specs=[blk, full(ones), full(utri), full(ltri)],
        out_specs=[blk, blk, blk],
        out_shape=[jax.ShapeDtypeStruct((B, R, LANES), jnp.int32), jax.ShapeDtypeStruct((B, R, LANES), jnp.int32),
                   jax.ShapeDtypeStruct((B, R, LANES), F32)],
        compiler_params=_params("parallel"),
        name="expert_choice_select",
    )(aff2, ones, utri, ltri)


GATHER_TOK = 1024
GATHER_WIN = 160


def _gather_kernel(offs_ref, u_ref, pos_ref, xe_ref, acc, *, cap, nblk):
    b, e = pl.program_id(0), pl.program_id(1)
    base = (b * N_EXPERTS + e) * (nblk + 1)
    rows = GATHER_TOK // LANES
    acc[...] = jnp.zeros(acc.shape, F32)
    slot = lax.broadcasted_iota(jnp.int32, (GATHER_WIN, LANES), 0)

    def block(j, first_window):
        off, end = offs_ref[base + j], offs_ref[base + j + 1]
        w = (off // SUBLANES) * SUBLANES

        def window(i, c):
            s0 = pl.multiple_of(w + i * GATHER_WIN, SUBLANES)
            prow = pos_ref[0, pl.ds(pl.multiple_of(j * rows, rows), rows), :]
            ublk = u_ref[0, pl.ds(pl.multiple_of(j * GATHER_TOK, GATHER_TOK), GATHER_TOK), :]
            onehot = jnp.concatenate(
                [jnp.where(slot == prow[r:r + 1, :] - s0, 1.0, 0.0).astype(BF16) for r in range(rows)], axis=1)
            acc[pl.ds(s0, GATHER_WIN), :] += jnp.dot(onehot, ublk, preferred_element_type=F32)
            return c

        if first_window:
            window(0, 0)
        else:
            lax.fori_loop(1, (end - w + GATHER_WIN - 1) // GATHER_WIN, window, 0)
        return 0

    lax.fori_loop(0, nblk, lambda j, c: block(j, True), 0, unroll=4)
    lax.fori_loop(0, nblk, lambda j, c: block(j, False), 0)
    xe_ref[0] = acc[:cap, :].astype(BF16)


def _gather(offs, u2, pos_sel, cap):
    B, S, D = u2.shape
    rpe = S // LANES
    nblk = S // GATHER_TOK
    return pl.pallas_call(
        functools.partial(_gather_kernel, cap=cap, nblk=nblk),
        grid_spec=pltpu.PrefetchScalarGridSpec(
            num_scalar_prefetch=1,
            grid=(B, N_EXPERTS),
            in_specs=[pl.BlockSpec((1, S, D), lambda b, e, o: (b, 0, 0)),
                      pl.BlockSpec((1, rpe, LANES), lambda b, e, o: (b, e, 0))],
            out_specs=pl.BlockSpec((1, cap, D), lambda b, e, o: (e, b, 0)),
            scratch_shapes=[pltpu.VMEM((cap + GATHER_WIN + SUBLANES, D), F32)],
        ),
        out_shape=jax.ShapeDtypeStruct((N_EXPERTS, B * cap, D), BF16),
        compiler_params=_params("parallel", "parallel"),
        name="expert_gather",
    )(offs, u2, pos_sel)


def _moe_kernel(x_ref, wg_ref, wu_ref, wd_ref, o_ref, wgb, wub, wdb):
    @pl.when(pl.program_id(1) == 0)
    def _():
        wgb[...] = wg_ref[0].astype(BF16)
        wub[...] = wu_ref[0].astype(BF16)
        wdb[...] = wd_ref[0].astype(BF16)

    x = x_ref[0]
    hg = jnp.dot(x, wgb[...], preferred_element_type=F32)
    hu = jnp.dot(x, wub[...], preferred_element_type=F32)
    h = (hg * jax.nn.sigmoid(hg) * hu).astype(BF16)
    o_ref[0] = jnp.dot(h, wdb[...], preferred_element_type=F32).astype(BF16)


def _moe_ffn(xe, w_gate, w_up, w_down):
    E, R, D = xe.shape
    Fh = w_gate.shape[2]
    tr = MOE_ROWS
    return pl.pallas_call(
        _moe_kernel,
        grid=(E, R // tr),
        in_specs=[
            pl.BlockSpec((1, tr, D), lambda e, r: (e, r, 0)),
            pl.BlockSpec((1, D, Fh), lambda e, r: (e, 0, 0)),
            pl.BlockSpec((1, D, Fh), lambda e, r: (e, 0, 0)),
            pl.BlockSpec((1, Fh, D), lambda e, r: (e, 0, 0)),
        ],
        out_specs=pl.BlockSpec((1, tr, D), lambda e, r: (e, r, 0)),
        out_shape=jax.ShapeDtypeStruct((E, R, D), BF16),
        scratch_shapes=[pltpu.VMEM((D, Fh), BF16), pltpu.VMEM((D, Fh), BF16), pltpu.VMEM((Fh, D), BF16)],
        compiler_params=_params("parallel", "arbitrary"),
        name="moe_swiglu",
    )(xe, w_gate, w_up, w_down)


CMB_TOK = 512
CMB_WIN = 128
BF16_ROWS = 16


def _combine_kernel(offs_ref, x1_ref, pos_ref, g_ref, g2_ref, lg_ref, lb_ref, ye_hbm, o_ref,
                    rhs, lhs, acc, sem, *, cap, nblk, nbatch):
    b, i = pl.program_id(0), pl.program_id(1)
    W = CMB_WIN
    total = ye_hbm.shape[0]
    step = b * nblk + i
    cur_slot = step % 2
    row_e = lax.broadcasted_iota(jnp.int32, (N_EXPERTS, 1), 0)
    slot_w = lax.broadcasted_iota(jnp.int32, (W, CMB_TOK), 0)

    def windows(bb, ii):
        out = []
        for e in range(N_EXPERTS):
            k = (bb * N_EXPERTS + e) * (nblk + 1) + ii
            off, end = offs_ref[k], offs_ref[k + 1]
            lo = (off // BF16_ROWS) * BF16_ROWS
            out.append(((e * nbatch + bb) * cap, lo, jnp.where(end > off, (end - lo + W - 1) // W, 0)))
        return out

    def fetch_starts(win, p):
        return [jnp.minimum(rb + lo + p * W, total - W) for rb, lo, _ in win]

    def window_copy(e, start, s):
        return pltpu.make_async_copy(ye_hbm.at[pl.ds(pl.multiple_of(start, BF16_ROWS), W), :],
                                     rhs.at[s, pl.ds(e * W, W), :], sem.at[s, e])

    def start_fetch(win, p, s):
        for e, start in enumerate(fetch_starts(win, p)):
            window_copy(e, start, s).start()

    def absorb(win, p, s, first):
        starts = fetch_starts(win, p)
        lo_col = jnp.zeros((N_EXPERTS, 1), jnp.int32)
        shift_col = jnp.zeros((N_EXPERTS, 1), jnp.int32)
        for e, (rb, lo0, _) in enumerate(win):
            lo = lo0 + p * W
            lo_col = jnp.where(row_e == e, lo, lo_col)
            shift_col = jnp.where(row_e == e, rb + lo - starts[e], shift_col)
        d = pos_ref[0] - lo_col
        rel = jnp.where((d >= 0) & (d < W), d + shift_col, -1)
        g = g_ref[0]
        for e in range(N_EXPERTS):
            lhs[e * W:(e + 1) * W, :] = jnp.where(slot_w == rel[e:e + 1, :], g[e:e + 1, :], 0.0).astype(BF16)
        for e in range(N_EXPERTS):
            window_copy(e, starts[e], s).wait()
        part = lax.dot_general(lhs[...], rhs[s], (((0,), (0,)), ((), ())), preferred_element_type=F32)
        acc[...] = part if first else acc[...] + part

    cur = windows(b, i)

    @pl.when(step == 0)
    def _():
        start_fetch(cur, 0, 0)

    @pl.when(step + 1 < nbatch * nblk)
    def _():
        wrap = i + 1 == nblk
        start_fetch(windows(jnp.where(wrap, b + 1, b), jnp.where(wrap, 0, i + 1)), 0, 1 - cur_slot)

    absorb(cur, 0, cur_slot, True)

    def extra_pass(p, carry):
        start_fetch(cur, p, cur_slot)
        absorb(cur, p, cur_slot, False)
        return carry

    npass = functools.reduce(jnp.maximum, [n for _, _, n in cur])
    lax.fori_loop(1, npass, extra_pass, 0)
    o_ref[0] = _layer_norm(DN_ALPHA * x1_ref[0] + g2_ref[0] * acc[...], lg_ref[...], lb_ref[...])


def _combine_ln2(offs, x1, pos_exp, g_exp, g2, lg, lb, ye_rows, cap):
    B, S, D = x1.shape
    T = CMB_TOK
    nblk = S // T
    tok = lambda w: pl.BlockSpec((1, T, w), lambda b, i, o: (b, i, 0))
    vec = pl.BlockSpec((1, D), lambda b, i, o: (0, 0))
    exp_major = pl.BlockSpec((1, N_EXPERTS, T), lambda b, i, o: (b, 0, i))
    return pl.pallas_call(
        functools.partial(_combine_kernel, cap=cap, nblk=nblk, nbatch=B),
        grid_spec=pltpu.PrefetchScalarGridSpec(
            num_scalar_prefetch=1,
            grid=(B, nblk),
            in_specs=[tok(D), exp_major, exp_major,
                      pl.BlockSpec((1, 1, D), lambda b, i, o: (b, 0, 0)), vec, vec,
                      pl.BlockSpec(memory_space=pl.ANY)],
            out_specs=tok(D),
            scratch_shapes=[pltpu.VMEM((2, N_EXPERTS * CMB_WIN, D), BF16),
                            pltpu.VMEM((N_EXPERTS * CMB_WIN, T), BF16),
                            pltpu.VMEM((T, D), F32),
                            pltpu.SemaphoreType.DMA((2, N_EXPERTS))],
        ),
        out_shape=jax.ShapeDtypeStruct((B, S, D), F32),
        compiler_params=_params("arbitrary", "arbitrary"),
        name="combine_residual_ln2",
    )(offs, x1, pos_exp, g_exp, g2, lg, lb, ye_rows)


DFT_R = 128
K1_HALF = 40
K1_VALID = DFT_R // 2 + 1
K1_GROUPS = (K1_HALF // SUBLANES, -(-(K1_VALID - K1_HALF) // SUBLANES))
K1_SPEC = K1_HALF + K1_GROUPS[1] * SUBLANES
HY_CT = 128


@functools.lru_cache(maxsize=None)
def _dft_tables_np(S, two_sided):
    N = 2 * S
    half = np.arange(S // DFT_R)
    a_idx = np.concatenate([half, N // DFT_R - 1 - half]) if two_sided else half
    r = np.arange(2 * K1_HALF)
    k1 = np.arange(2)[:, None] * K1_HALF + r[None, :] % K1_HALF
    valid = (k1 < K1_VALID).astype(np.float64)
    n = DFT_R * a_idx[None, :] + np.arange(DFT_R)[:, None]
    theta = ((n[None, :, None, :] * k1[:, None, :, None]) % N) * (2.0 * math.pi / N)
    is_im = (r >= K1_HALF)[None, None, :, None]
    g = np.where(is_im, -np.sin(theta), np.cos(theta)) * valid[:, None, :, None]
    cw = np.where((k1 == 0) | (k1 == DFT_R // 2), 1.0, 2.0) * valid / N
    gi = np.transpose(g * cw[:, None, :, None], (0, 1, 3, 2))
    ph = ((np.arange(DFT_R)[:, None] * np.arange(DFT_R)[None, :]) % DFT_R) * (2.0 * math.pi / DFT_R)
    C, Sn = np.cos(ph), np.sin(ph)
    f3f = np.block([[C, Sn], [-Sn, C]])
    f3i = np.block([[C, -Sn], [Sn, C]])

    def pair_lanes(t):
        h, nb, nr, nc = t.shape
        return t.reshape(h, nb // 2, 2, nr, nc).transpose(0, 1, 3, 2, 4).reshape(h, nb // 2, nr, 2 * nc)

    return tuple(np.ascontiguousarray(t, dtype=np.float32) for t in (pair_lanes(g), pair_lanes(gi), f3f, f3i))


def _dft_tables(S, two_sided=False):
    return tuple(jnp.asarray(t).astype(BF16) for t in _dft_tables_np(S, two_sided))


def _swap01(x):
    return jnp.swapaxes(x, 0, 1)


def _block_diag_lanes(top, bot):
    return jnp.concatenate([jnp.concatenate([top, jnp.zeros((top.shape[0], bot.shape[1]), top.dtype)], axis=1),
                            jnp.concatenate([jnp.zeros((bot.shape[0], top.shape[1]), bot.dtype), bot], axis=1)], axis=0)


def _dft_stage1(load, gf_ref, hf, a_scr):
    ct = a_scr.shape[2]

    def body(g, carry):
        b0 = pl.multiple_of(g * SUBLANES, SUBLANES)
        zt = load(b0)
        for p in range(SUBLANES // 2):
            a2 = jnp.dot(gf_ref[hf, g * (SUBLANES // 2) + p], _block_diag_lanes(zt[2 * p], zt[2 * p + 1]),
                         preferred_element_type=F32)
            a_scr[b0 + 2 * p] = a2[:, :ct]
            a_scr[b0 + 2 * p + 1] = a2[:, ct:]
        return carry

    lax.fori_loop(0, DFT_R // SUBLANES, body, 0, unroll=4)


def _dft_stage3_operands(a_scr, r0):
    mr = _swap01(a_scr[:, pl.ds(r0, SUBLANES), :])
    mi = _swap01(a_scr[:, pl.ds(K1_HALF + r0, SUBLANES), :])
    return [jnp.concatenate([jnp.concatenate([mr[j], mi[j]], axis=0),
                             jnp.concatenate([mr[j + 1], mi[j + 1]], axis=0)], axis=1).astype(BF16)
            for j in range(0, SUBLANES, 2)]


def _filter_mlp_kernel(feat_ref, featx_ref, w1_ref, b1_ref, fr_ref, w2_ref, b2_ref, w3_ref, b3_ref, dec_ref,
                       of_ref, ob_ref):
    tl = feat_ref.shape[0]
    ncol = of_ref.shape[1]
    fr = fr_ref[...]

    def hidden(f):
        h = jnp.sin(fr * (jnp.dot(f, w1_ref[...], precision=HIGHEST, preferred_element_type=F32) + b1_ref[...]))
        return jnp.sin(fr * (jnp.dot(h, w2_ref[...], precision=HIGHEST, preferred_element_type=F32) + b2_ref[...]))

    def taps(h2, t, lo):
        out = _dot3(h2, w3_ref[:, lo:lo + ncol]) + b3_ref[:, lo:lo + ncol]
        return out * jnp.exp(-t * jnp.abs(dec_ref[:, lo:lo + ncol]))

    feats, featx = feat_ref[...], featx_ref[...]
    h2, h2x = hidden(feats), hidden(featx)
    of_ref[...] = taps(h2, feats[:, 0:1], 0)
    row = lax.broadcasted_iota(jnp.int32, (tl, 1), 0)
    last = row == tl - 1
    h2n = jnp.where(last, h2x[0:1, :], pltpu.roll(h2, tl - 1, 0))
    tn = jnp.where(last, featx[0:1, :], pltpu.roll(feats, tl - 1, 0))[:, 0:1]
    ob_ref[...] = jnp.where(last & (pl.program_id(0) == pl.num_programs(0) - 1), 0.0, taps(h2n, tn, ncol))


@functools.lru_cache(maxsize=None)
def _filter_features_np(L, nf):
    pos = np.arange(L + SUBLANES, dtype=np.float64)
    t = pos[:, None] / (L - 1)
    bands = (nf - 1) // 2
    freqs = np.linspace(1e-4, bands - 1, bands)
    phase = (2.0 * math.pi / L) * pos[:, None] * freqs[None, :]
    feats = np.concatenate([t, np.cos(phase), -np.sin(phase), np.zeros((L + SUBLANES, LANES - nf))], axis=-1)
    return np.ascontiguousarray(feats, dtype=np.float32)


def _filter_taps(L, w1, b1, freq, w2, b2, w3, b3, decay):
    nf = w1.shape[0]
    feats = jnp.asarray(_filter_features_np(L, nf))
    w1p = jnp.concatenate([w1, jnp.zeros((LANES - nf, w1.shape[1]), F32)], axis=0)
    fo = w1.shape[1]
    ncol = w3.shape[1] // 2
    tl = 512
    full = lambda shape: pl.BlockSpec(shape, lambda i: (0,) * len(shape))
    out = jax.ShapeDtypeStruct((L, ncol), F32)
    blk = pl.BlockSpec((tl, ncol), lambda i: (i, 0))
    return pl.pallas_call(
        _filter_mlp_kernel,
        grid=(L // tl,),
        in_specs=[pl.BlockSpec((tl, LANES), lambda i: (i, 0)),
                  pl.BlockSpec((SUBLANES, LANES), lambda i: ((i + 1) * (tl // SUBLANES), 0)),
                  full((LANES, fo)), full((1, fo)), full((1, fo)), full((fo, fo)), full((1, fo)),
                  full((fo, 2 * ncol)), full((1, 2 * ncol)), full((1, 2 * ncol))],
        out_specs=[blk, blk],
        out_shape=[out, out],
        compiler_params=_params("parallel"),
        name="hyena_filter_mlp",
    )(feats, feats, w1p, b1[None, :], freq[None, :], w2, b2[None, :], w3, b3[None, :], decay.reshape(1, 2 * ncol))


def _filter_spectrum_kernel(hf_ref, hb_ref, gf_ref, f3f_ref, o_ref, a_scr):
    ct = a_scr.shape[2]

    def load(b0):
        zf = _swap01(hf_ref[:, pl.ds(b0, SUBLANES), :])
        zb = _swap01(hb_ref[:, pl.ds(pl.multiple_of(DFT_R - SUBLANES - b0, SUBLANES), SUBLANES), :])
        return [jnp.concatenate([zf[j], zb[SUBLANES - 1 - j]], axis=0).astype(BF16) for j in range(SUBLANES)]

    for hf in range(2):
        _dft_stage1(load, gf_ref, hf, a_scr)

        def body(g, carry):
            r0 = pl.multiple_of(g * SUBLANES, SUBLANES)
            for p, m in enumerate(_dft_stage3_operands(a_scr, r0)):
                x2 = jnp.dot(f3f_ref[...], m, preferred_element_type=F32)
                for q in range(2):
                    o_ref[0, hf * K1_HALF + r0 + 2 * p + q] = x2[:, q * ct:(q + 1) * ct].astype(BF16)
            return carry

        lax.fori_loop(0, K1_GROUPS[hf], body, 0, unroll=True)


def _filter_spectrum(fwd3, bwd3, gf, f3f):
    na, nb, ncol = fwd3.shape
    C = ncol // 2
    nct = C // HY_CT
    ct = HY_CT
    taps = pl.BlockSpec((na, nb, ct), lambda o, j: (0, 0, o * nct + j))
    return pl.pallas_call(
        _filter_spectrum_kernel,
        grid=(2, nct),
        in_specs=[
            taps, taps,
            pl.BlockSpec(gf.shape, lambda o, j: (0, 0, 0, 0), pipeline_mode=pl.Buffered(1)),
            pl.BlockSpec(f3f.shape, lambda o, j: (0, 0), pipeline_mode=pl.Buffered(1)),
        ],
        out_specs=pl.BlockSpec((1, K1_SPEC, 2 * DFT_R, ct), lambda o, j: (o, 0, 0, j)),
        out_shape=jax.ShapeDtypeStruct((2, K1_SPEC, 2 * DFT_R, C), BF16),
        scratch_shapes=[pltpu.VMEM((DFT_R, 2 * K1_HALF, ct), F32)],
        compiler_params=_params("parallel", "parallel"),
        name="hyena_filter_spectrum",
    )(fwd3, bwd3, gf, f3f)


def _long_conv_kernel(z_ref, xg_ref, h_ref, bias_ref, gf_ref, f3f_ref, f3i_ref, gi_ref, o_ref,
                      a_scr, b_scr, zt_scr, yt_scr):
    ct = a_scr.shape[2]
    bias = bias_ref[0]

    def load_first(b0):
        zt = _swap01(z_ref[0, :, pl.ds(b0, SUBLANES), :]).astype(BF16)
        for j in range(SUBLANES):
            zt_scr[b0 + j] = zt[j]
        return zt

    for hf in range(2):
        _dft_stage1(load_first if hf == 0 else (lambda b0: [zt_scr[b0 + j] for j in range(SUBLANES)]),
                    gf_ref, hf, a_scr)
        ngrp = K1_GROUPS[hf]
        if ngrp * SUBLANES < K1_HALF:
            b_scr[ngrp * SUBLANES:] = jnp.zeros((K1_HALF - ngrp * SUBLANES,) + b_scr.shape[1:], F32)

        def spectral(g, carry):
            r0 = pl.multiple_of(g * SUBLANES, SUBLANES)
            for p, m in enumerate(_dft_stage3_operands(a_scr, r0)):
                x = jnp.dot(f3f_ref[...], m, preferred_element_type=F32)
                k1 = hf * K1_HALF + r0 + 2 * p
                h = jnp.concatenate([h_ref[0, k1], h_ref[0, k1 + 1]], axis=1).astype(F32)
                xr, xi, hr, hi = x[:DFT_R], x[DFT_R:], h[:DFT_R], h[DFT_R:]
                y = jnp.concatenate([xr * hr - xi * hi, xr * hi + xi * hr], axis=0).astype(BF16)
                bb = jnp.dot(f3i_ref[...], y, preferred_element_type=F32)
                b_scr[r0 + 2 * p] = bb[:, :ct]
                b_scr[r0 + 2 * p + 1] = bb[:, ct:]
            return carry

        lax.fori_loop(0, ngrp, spectral, 0, unroll=True)

        def inverse(g, carry):
            b0 = pl.multiple_of(g * SUBLANES, SUBLANES)
            br = _swap01(b_scr[:, pl.ds(b0, SUBLANES), :])
            bi = _swap01(b_scr[:, pl.ds(DFT_R + b0, SUBLANES), :])
            ys = []
            for p in range(SUBLANES // 2):
                bm = [jnp.concatenate([br[2 * p + q], bi[2 * p + q]], axis=0).astype(BF16) for q in range(2)]
                y2 = jnp.dot(gi_ref[hf, g * (SUBLANES // 2) + p], _block_diag_lanes(bm[0], bm[1]),
                             preferred_element_type=F32)
                ys += [y2[:, :ct], y2[:, ct:]]
            if hf == 0:
                for j in range(SUBLANES):
                    yt_scr[b0 + j] = ys[j]
            else:
                y = _swap01(jnp.stack([yt_scr[b0 + j] + ys[j] for j in range(SUBLANES)], axis=0))
                sl = (0, slice(None), pl.ds(b0, SUBLANES), slice(None))
                o_ref[sl] = xg_ref[sl] * (y + z_ref[sl] * bias)
            return carry

        lax.fori_loop(0, DFT_R // SUBLANES, inverse, 0, unroll=4)


def _long_conv(z_arr, z_off, xg_arr, xg_off, spec, order, bias3, gf, f3f, f3i, gi):
    B, na, nb, _ = z_arr.shape
    C = spec.shape[3]
    ct = HY_CT
    const = lambda arr: pl.BlockSpec(arr.shape, lambda j, b: (0,) * arr.ndim, pipeline_mode=pl.Buffered(1))
    return pl.pallas_call(
        _long_conv_kernel,
        grid=(C // ct, B),
        in_specs=[
            pl.BlockSpec((1, na, nb, ct), lambda j, b: (b, 0, 0, z_off + j)),
            pl.BlockSpec((1, na, nb, ct), lambda j, b: (b, 0, 0, xg_off + j)),
            pl.BlockSpec((1, K1_SPEC, 2 * DFT_R, ct), lambda j, b: (order, 0, 0, j), pipeline_mode=pl.Buffered(1)),
            pl.BlockSpec((1, 1, ct), lambda j, b: (order, 0, j)),
            const(gf), const(f3f), const(f3i), const(gi),
        ],
        out_specs=pl.BlockSpec((1, na, nb, ct), lambda j, b: (b, 0, 0, j)),
        out_shape=jax.ShapeDtypeStruct((B, na, nb, C), F32),
        scratch_shapes=[pltpu.VMEM((DFT_R, 2 * K1_HALF, ct), F32), pltpu.VMEM((K1_HALF, 2 * DFT_R, ct), F32),
                        pltpu.VMEM((nb, na, ct), BF16), pltpu.VMEM((nb, na, ct), F32)],
        compiler_params=_params("parallel", "parallel"),
        name="hyena_long_conv",
    )(z_arr, xg_arr, spec, bias3, gf, f3f, f3i, gi)


def _hyena_group(hy_conv, fw1, fb1, ffreq, fw2, fb2, fw3, fb3, decay, hyena_bias):
    B, S, _ = hy_conv.shape
    na = S // DFT_R
    gf, gi, f3f, f3i = _dft_tables(S)
    fwd, bwd = _filter_taps(S, fw1, fb1, ffreq, fw2, fb2, fw3, fb3, decay)
    spec = _filter_spectrum(fwd.reshape(na, DFT_R, -1), bwd.reshape(na, DFT_R, -1),
                            _dft_tables(S, two_sided=True)[0], f3f)
    u = hy_conv.reshape(B, na, DFT_R, -1)
    nct = HY_W // HY_CT
    bias3 = hyena_bias[:, None, :]
    z1 = _long_conv(u, 0, u, nct, spec, 0, bias3, gf, f3f, f3i, gi)
    return _long_conv(z1, 0, u, 2 * nct, spec, 1, bias3, gf, f3f, f3i, gi).reshape(B, S, HY_W)


def _rope_rot_cols(w):
    half = ROPE // 2
    return jnp.concatenate([-w[..., half:], w[..., :half]], axis=-1)


def _prep_weights(w_in, w_qb, w_kvb):
    D = w_in.shape[0]
    o = Q_RANK + KV_RANK
    w_kr = w_in[:, o:o + ROPE]
    z64 = jnp.zeros((D, NOPE), F32)
    z32 = jnp.zeros((D, LANES - NOPE - ROPE), F32)
    win_aug = jnp.concatenate([
        w_in[:, :o],
        z64, w_kr, z32,
        z64, _rope_rot_cols(w_kr), z32,
        w_in[:, o + ROPE:],
    ], axis=1).astype(BF16)
    wq = w_qb.reshape(Q_RANK, MLA_HEADS, NOPE + ROPE)
    zq = jnp.zeros((Q_RANK, MLA_HEADS, LANES - NOPE - ROPE), F32)
    wq_main = jnp.concatenate([wq, zq], axis=-1).reshape(Q_RANK, MLA_HEADS * LANES).astype(BF16)
    wq_rot = jnp.concatenate([jnp.zeros((Q_RANK, MLA_HEADS, NOPE), F32), _rope_rot_cols(wq[..., NOPE:]), zq],
                             axis=-1).reshape(Q_RANK, MLA_HEADS * LANES).astype(BF16)
    wkv = w_kvb.reshape(KV_RANK, MLA_HEADS, NOPE + V_DIM)
    wk = jnp.concatenate([wkv[..., :NOPE], jnp.zeros((KV_RANK, MLA_HEADS, LANES - NOPE), F32)],
                         axis=-1).reshape(KV_RANK, MLA_HEADS * LANES).astype(BF16)
    wv = wkv[..., NOPE:].reshape(KV_RANK, MLA_HEADS * V_DIM).astype(BF16)
    return win_aug, wq_main, wq_rot, wk, wv


def kernel(x, c, positions, w_ada, b_ada, w_in, q_norm_g, w_qb, kv_norm_g, w_kvb, conv_w, conv_b, filt_w1, filt_b1, filt_freq, filt_w2, filt_b2, filt_w3, filt_b3, hyena_decay, hyena_bias, attn_out_g, hyena_out_g, w_o, ln1_g, ln1_b, w_router, w_gate, w_up, w_down, ln2_g, ln2_b):
    B, S, D = x.shape
    l = 0
    mod = _ada_mod(c, w_ada[l], b_ada[l][None, :])
    sh1, sc1, g1, sh2, sc2, g2 = [m[:, None, :] for m in jnp.split(mod, 6, axis=-1)]

    win_aug, wq_main, wq_rot, wk, wv = _prep_weights(w_in[l], w_qb[l], w_kvb[l])
    cos_t, sin_t = _rope_tables(positions)
    q, kt, v, hy_conv = _inproj(x, sc1, sh1, cos_t, sin_t, win_aug, q_norm_g[l][None, :], wq_main, wq_rot,
                                kv_norm_g[l][None, :], wk, wv, conv_w[l], conv_b[l])
    a = _attention(q, kt, v)

    hy = _hyena_group(hy_conv, filt_w1[l], filt_b1[l], filt_freq[l], filt_w2[l],
                      filt_b2[l], filt_w3[l], filt_b3[l], hyena_decay[l], hyena_bias[l])

    wr_pad = jnp.concatenate([w_router[l], jnp.zeros((D, LANES - N_EXPERTS), F32)], axis=1)
    x1, u2, aff_t = _outproj(a, hy, x, g1, sc2, sh2, attn_out_g[l][None, :], hyena_out_g[l][None, :],
                             w_o[l].astype(BF16), ln1_g[l][None, :], ln1_b[l][None, :], wr_pad)

    cap = EC_FACTOR * S // N_EXPERTS
    rpe = S // LANES
    aff2 = aff_t.reshape(B, N_EXPERTS * rpe, LANES)
    pos_sel, pos_all, gsel = _select(aff2, cap)
    row_start = pos_all.reshape(B, N_EXPERTS, rpe, LANES)[:, :, :, 0]

    def block_offsets(tokens):
        ends = jnp.full((B, N_EXPERTS, 1), cap, jnp.int32)
        return jnp.concatenate([row_start[:, :, ::tokens // LANES], ends], axis=-1).reshape(-1)

    xe = _gather(block_offsets(GATHER_TOK), u2, pos_sel, cap)
    ye = _moe_ffn(xe, w_gate[l], w_up[l], w_down[l])
    return _combine_ln2(block_offsets(CMB_TOK), x1, pos_sel.reshape(B, N_EXPERTS, S), gsel.reshape(B, N_EXPERTS, S), g2, ln2_g[l][None, :], ln2_b[l][None, :],
                        ye.reshape(N_EXPERTS * B * cap, D), cap)
```

```python
import functools
import math

import jax
import jax.numpy as jnp
import numpy as np
from jax import lax
from jax.experimental import pallas as pl
from jax.experimental.pallas import tpu as pltpu

F32 = jnp.float32
BF16 = jnp.bfloat16
HIGHEST = lax.Precision.HIGHEST

LANES = 128
SUBLANES = 8
VMEM_LIMIT = 56 * 1024 * 1024

MLA_HEADS = 8
NOPE = 64
ROPE = 32
V_DIM = 64
Q_RANK = 256
KV_RANK = 128
HY_W = 512
HY_COLS = 3 * HY_W
N_EXPERTS = 16
EC_FACTOR = 2
ROPE_THETA = 10000.0
EPS = 1e-5
DN_ALPHA = 2.0 ** 0.25
LOG2E = 1.4426950408889634

TOK_TILE = 512
ATT_TQ = 512
ATT_TK = 512
MOE_ROWS = 512


def _params(*sem):
    return pltpu.CompilerParams(dimension_semantics=sem, vmem_limit_bytes=VMEM_LIMIT)


def _ada_kernel(ct_ref, w_ref, b_ref, o_ref):
    ct = ct_ref[...]
    sc = ct * jax.nn.sigmoid(ct)
    w = w_ref[...]
    rows = [jnp.sum(w * sc[:, b:b + 1], axis=0, keepdims=True) for b in range(ct.shape[1])]
    o_ref[...] = jnp.concatenate(rows, axis=0) + b_ref[...]


def _ada_mod(c, w_ada, b_ada):
    B, D = c.shape
    n = w_ada.shape[1]
    tn = 1024
    return pl.pallas_call(
        _ada_kernel,
        grid=(n // tn,),
        in_specs=[
            pl.BlockSpec((D, B), lambda j: (0, 0)),
            pl.BlockSpec((D, tn), lambda j: (0, j)),
            pl.BlockSpec((1, tn), lambda j: (0, j)),
        ],
        out_specs=pl.BlockSpec((B, tn), lambda j: (0, j)),
        out_shape=jax.ShapeDtypeStruct((B, n), F32),
        compiler_params=_params("parallel"),
        name="ada_mod",
    )(c.T, w_ada, b_ada)


def _dot3(a, b):
    ah, bh = a.astype(BF16), b.astype(BF16)
    al = (a - ah.astype(F32)).astype(BF16)
    bl = (b - bh.astype(F32)).astype(BF16)
    dot = functools.partial(jnp.dot, preferred_element_type=F32)
    return dot(ah, bh) + (dot(ah, bl) + dot(al, bh))


def _rms(x, g):
    return x * lax.rsqrt(jnp.mean(x * x, axis=-1, keepdims=True) + EPS) * g


def _rope_kernel(pos_ref, freq_ref, cos_ref, sin_ref):
    ang = freq_ref[...] * pos_ref[0].astype(F32)
    cos, sin = jnp.cos(ang), jnp.sin(ang)
    S = ang.shape[1]
    fill = lambda n, v: jnp.full((n, S), v, F32)
    cos_ref[0] = jnp.concatenate([fill(NOPE, 1.0), cos, cos, fill(LANES - NOPE - ROPE, 1.0)], axis=0)
    sin_ref[0] = jnp.concatenate([fill(NOPE, 0.0), sin, sin, fill(LANES - NOPE - ROPE, 0.0)], axis=0)


def _rope_tables(positions):
    B, S = positions.shape
    half = ROPE // 2
    inv_freq = (ROPE_THETA ** (-jnp.arange(half, dtype=F32) / half))[:, None]
    out = jax.ShapeDtypeStruct((B, LANES, S), F32)
    blk = pl.BlockSpec((1, LANES, S), lambda b: (b, 0, 0))
    return pl.pallas_call(
        _rope_kernel,
        grid=(B,),
        in_specs=[pl.BlockSpec((1, 1, S), lambda b: (b, 0, 0)), pl.BlockSpec((half, 1), lambda b: (0, 0))],
        out_specs=[blk, blk],
        out_shape=[out, out],
        compiler_params=_params("parallel"),
        name="rope_tables",
    )(positions[:, None, :], inv_freq)


def _inproj_kernel(x_ref, xp_ref, xn_ref, sc_ref, sh_ref, cos_ref, sin_ref, win_ref, qg_ref, wq_ref, wqr_ref,
                   kvg_ref, wk_ref, wv_ref, cw_ref, cb_ref, q_ref, k_ref, v_ref, hy_ref):
    def modulate(xv):
        return (xv * (1.0 + sc_ref[0]) + sh_ref[0]).astype(BF16)

    u = modulate(x_ref[0])
    cos = cos_ref[0].T
    sin = sin_ref[0].T

    def proj(lo, hi):
        return jnp.dot(u, win_ref[:, lo:hi], preferred_element_type=F32)

    o = Q_RANK + 3 * KV_RANK
    hy = proj(o, o + HY_COLS)
    tm = hy.shape[0]
    i, last = pl.program_id(1), pl.num_programs(1) - 1
    edge = jnp.dot(modulate(jnp.concatenate([xp_ref[0], xn_ref[0]], axis=0)), win_ref[:, o:o + HY_COLS],
                   preferred_element_type=F32)
    prev = jnp.where(i == 0, 0.0, edge[SUBLANES - 1:SUBLANES, :])
    nxt = jnp.where(i == last, 0.0, edge[SUBLANES:SUBLANES + 1, :])
    row = lax.broadcasted_iota(jnp.int32, hy.shape, 0)
    up = jnp.where(row == 0, prev, pltpu.roll(hy, 1, 0))
    dn = jnp.where(row == tm - 1, nxt, pltpu.roll(hy, tm - 1, 0))
    hy_ref[0] = cb_ref[...] + cw_ref[0:1, :] * up + cw_ref[1:2, :] * hy + cw_ref[2:3, :] * dn

    cq = _rms(proj(0, Q_RANK), qg_ref[...]).astype(BF16)
    q_main = jnp.dot(cq, wq_ref[...], preferred_element_type=F32)
    q_rot = jnp.dot(cq, wqr_ref[...], preferred_element_type=F32)
    qscale = LOG2E * (NOPE + ROPE) ** -0.5
    cq_s = cos * qscale
    sq_s = sin * qscale
    for h in range(MLA_HEADS):
        sl = slice(h * LANES, (h + 1) * LANES)
        q_ref[0, h] = (q_main[:, sl] * cq_s + q_rot[:, sl] * sq_s).T.astype(BF16)

    o = Q_RANK
    ckv = _rms(proj(o, o + KV_RANK), kvg_ref[...]).astype(BF16)
    kr = proj(o + KV_RANK, o + 2 * KV_RANK) * cos + proj(o + 2 * KV_RANK, o + 3 * KV_RANK) * sin
    kn = jnp.dot(ckv, wk_ref[...], preferred_element_type=F32)
    for h in range(MLA_HEADS):
        k_ref[0, h] = (kn[:, h * LANES:(h + 1) * LANES] + kr).astype(BF16)
    v_ref[0] = jnp.dot(ckv, wv_ref[...], preferred_element_type=F32).T.astype(BF16)


def _inproj(x, sc1, sh1, cos_t, sin_t, win_aug, qg, wq, wqr, kvg, wk, wv, conv_w, conv_b):
    B, S, D = x.shape
    tm = TOK_TILE
    ncol = win_aug.shape[1]
    g8 = tm // SUBLANES
    n8 = S // SUBLANES
    full = lambda shape: pl.BlockSpec(shape, lambda b, i: (0,) * len(shape))
    return pl.pallas_call(
        _inproj_kernel,
        grid=(B, S // tm),
        in_specs=[
            pl.BlockSpec((1, tm, D), lambda b, i: (b, i, 0)),
            pl.BlockSpec((1, SUBLANES, D), lambda b, i: (b, jnp.maximum(i * g8 - 1, 0), 0)),
            pl.BlockSpec((1, SUBLANES, D), lambda b, i: (b, jnp.minimum((i + 1) * g8, n8 - 1), 0)),
            pl.BlockSpec((1, 1, D), lambda b, i: (b, 0, 0)),
            pl.BlockSpec((1, 1, D), lambda b, i: (b, 0, 0)),
            pl.BlockSpec((1, LANES, tm), lambda b, i: (b, 0, i)),
            pl.BlockSpec((1, LANES, tm), lambda b, i: (b, 0, i)),
            full((D, ncol)),
            full((1, Q_RANK)),
            full((Q_RANK, MLA_HEADS * LANES)),
            full((Q_RANK, MLA_HEADS * LANES)),
            full((1, KV_RANK)),
            full((KV_RANK, MLA_HEADS * LANES)),
            full((KV_RANK, MLA_HEADS * V_DIM)),
            full(conv_w.shape),
            full((1, HY_COLS)),
        ],
        out_specs=[
            pl.BlockSpec((1, MLA_HEADS, LANES, tm), lambda b, i: (b, 0, 0, i)),
            pl.BlockSpec((1, MLA_HEADS, tm, LANES), lambda b, i: (b, 0, i, 0)),
            pl.BlockSpec((1, MLA_HEADS * V_DIM, tm), lambda b, i: (b, 0, i)),
            pl.BlockSpec((1, tm, HY_COLS), lambda b, i: (b, i, 0)),
        ],
        out_shape=[
            jax.ShapeDtypeStruct((B, MLA_HEADS, LANES, S), BF16),
            jax.ShapeDtypeStruct((B, MLA_HEADS, S, LANES), BF16),
            jax.ShapeDtypeStruct((B, MLA_HEADS * V_DIM, S), BF16),
            jax.ShapeDtypeStruct((B, S, HY_COLS), F32),
        ],
        compiler_params=_params("parallel", "parallel"),
        name="in_proj",
    )(x, x, x, sc1, sh1, cos_t, sin_t, win_aug, qg, wq, wqr, kvg, wk, wv, conv_w, conv_b[None, :])


ATT_AUG = 16


def _attn_kernel(qt_ref, k_ref, vt_ref, o_ref, s_scr, cmax_scr):
    S = k_ref.shape[2]
    tq = ATT_TQ
    nk = S // ATT_TK
    nq = S // tq
    ones_rows = jnp.where(lax.broadcasted_iota(jnp.int32, (ATT_AUG, ATT_TK), 0) == 0, 1.0, 0.0).astype(BF16)

    def scores(qi, c, slot):
        k0 = pl.multiple_of(c * ATT_TK, ATT_TK)
        q0 = pl.multiple_of(qi * tq, tq)
        for h in range(2):
            s = jnp.dot(k_ref[0, h, pl.ds(k0, ATT_TK), :], qt_ref[0, h, :, pl.ds(q0, tq)],
                        preferred_element_type=F32)
            s_scr[slot, h] = s
            cmax_scr[slot, h] = jnp.max(s, axis=0, keepdims=True)

    def absorb(c, slot, carry):
        k0 = pl.multiple_of(c * ATT_TK, ATT_TK)
        out = []
        for h in range(2):
            m_prev, acc = carry[h]
            s = s_scr[slot, h]
            m_new = jnp.maximum(m_prev, cmax_scr[slot, h])
            alpha = jnp.exp2(m_prev - m_new)
            p = jnp.exp2(s - m_new).astype(BF16)
            v_aug = jnp.concatenate([vt_ref[0, h * V_DIM:(h + 1) * V_DIM, pl.ds(k0, ATT_TK)], ones_rows], axis=0)
            out.append((m_new, alpha * acc + jnp.dot(v_aug, p, preferred_element_type=F32)))
        return tuple(out)

    def query_tile(qi, _):
        def pair(i, carry):
            c0 = 2 * i
            scores(qi, c0 + 1, 1)
            carry = absorb(c0, 0, carry)
            wrap = c0 + 2 >= nk
            scores(jnp.where(wrap, jnp.minimum(qi + 1, nq - 1), qi), jnp.where(wrap, 0, c0 + 2), 0)
            return absorb(c0 + 1, 1, carry)

        init = tuple((jnp.full((1, tq), -jnp.inf, F32), jnp.zeros((V_DIM + ATT_AUG, tq), F32)) for _ in range(2))
        fin = lax.fori_loop(0, nk // 2, pair, init, unroll=8)
        o = jnp.concatenate([acc[:V_DIM] * (1.0 / acc[V_DIM:V_DIM + 1]) for (_, acc) in fin], axis=0)
        o_ref[0, pl.ds(pl.multiple_of(qi * tq, tq), tq), :] = o.T
        return 0

    scores(0, 0, 0)
    lax.fori_loop(0, nq, query_tile, 0)


def _attention(qt, k, vt):
    B, H, _, S = qt.shape
    return pl.pallas_call(
        _attn_kernel,
        grid=(B, H // 2),
        in_specs=[
            pl.BlockSpec((1, 2, LANES, S), lambda b, hp: (b, hp, 0, 0)),
            pl.BlockSpec((1, 2, S, LANES), lambda b, hp: (b, hp, 0, 0)),
            pl.BlockSpec((1, 2 * V_DIM, S), lambda b, hp: (b, hp, 0)),
        ],
        out_specs=pl.BlockSpec((1, S, 2 * V_DIM), lambda b, hp: (b, 0, hp)),
        out_shape=jax.ShapeDtypeStruct((B, S, H * V_DIM), F32),
        scratch_shapes=[pltpu.VMEM((2, 2, ATT_TK, ATT_TQ), F32), pltpu.VMEM((2, 2, 1, ATT_TQ), F32)],
        compiler_params=_params("parallel", "parallel"),
        name="mla_attention",
    )(qt, k, vt)


def _layer_norm(x, g, b):
    mu = jnp.mean(x, axis=-1, keepdims=True)
    xc = x - mu
    var = jnp.mean(xc * xc, axis=-1, keepdims=True)
    return xc * lax.rsqrt(var + EPS) * g + b


def _outproj_kernel(a_ref, hy_ref, x_ref, g1_ref, sc2_ref, sh2_ref, ag_ref, hg_ref, wo_ref,
                    lg_ref, lb_ref, wr_ref, x1_ref, u2_ref, aff_ref):
    half = a_ref.shape[2]
    tm = a_ref.shape[1]
    nsplit = 2
    for r0 in range(0, tm, tm // nsplit):
        rs = slice(r0, r0 + tm // nsplit)
        an = _rms(a_ref[0, rs, :], ag_ref[...]).astype(BF16)
        hn = _rms(hy_ref[0, rs, :], hg_ref[...]).astype(BF16)
        mixed = (jnp.dot(an, wo_ref[:half, :], preferred_element_type=F32)
                 + jnp.dot(hn, wo_ref[half:, :], preferred_element_type=F32))
        x1 = _layer_norm(DN_ALPHA * x_ref[0, rs, :] + g1_ref[0] * mixed, lg_ref[...], lb_ref[...])
        x1_ref[0, rs, :] = x1
        u2 = x1 * (1.0 + sc2_ref[0]) + sh2_ref[0]
        u2_ref[0, rs, :] = u2.astype(BF16)
        logits = _dot3(u2, wr_ref[...])
        lane = lax.broadcasted_iota(jnp.int32, logits.shape, 1)
        logits = jnp.where(lane < N_EXPERTS, logits, -jnp.inf)
        e = jnp.exp(logits - jnp.max(logits, axis=-1, keepdims=True))
        aff = e / jnp.sum(e, axis=-1, keepdims=True)
        aff_ref[0, :, rs] = aff.T[:N_EXPERTS, :]


def _outproj(a, hy, x, g1, sc2, sh2, ag, hg, wo, lg, lb, wr_pad):
    B, S, D = x.shape
    tm = TOK_TILE
    half = a.shape[2]
    full = lambda shape: pl.BlockSpec(shape, lambda b, i: (0,) * len(shape))
    tok = lambda w: pl.BlockSpec((1, tm, w), lambda b, i: (b, i, 0))
    per_b = pl.BlockSpec((1, 1, D), lambda b, i: (b, 0, 0))
    return pl.pallas_call(
        _outproj_kernel,
        grid=(B, S // tm),
        in_specs=[tok(half), tok(half), tok(D), per_b, per_b, per_b,
                  full((1, half)), full((1, half)), full((D, D)), full((1, D)), full((1, D)),
                  full((D, LANES))],
        out_specs=[tok(D), tok(D), pl.BlockSpec((1, N_EXPERTS, tm), lambda b, i: (b, 0, i))],
        out_shape=[
            jax.ShapeDtypeStruct((B, S, D), F32),
            jax.ShapeDtypeStruct((B, S, D), BF16),
            jax.ShapeDtypeStruct((B, N_EXPERTS, S), F32),
        ],
        compiler_params=_params("parallel", "parallel"),
        name="out_proj_ln1_router",
    )(a, hy, x, g1, sc2, sh2, ag, hg, wo, lg, lb, wr_pad)


def _select_kernel(aff_ref, ones_ref, utri_ref, ltri_ref, pos_ref, posall_ref, g_ref, *, cap):
    aff = aff_ref[0]
    ones = ones_ref[...]

    def rowsum(mf):
        return jnp.dot(mf, ones, preferred_element_type=F32)

    def expert_total(mf):
        x = mf.astype(F32).reshape(N_EXPERTS, aff.shape[0] // N_EXPERTS, LANES)
        tot = jnp.sum(jnp.sum(x, axis=1, keepdims=True), axis=2, keepdims=True)
        return jnp.broadcast_to(tot, x.shape).reshape(aff.shape)

    def prefix(mf):
        return (jnp.dot(mf, utri_ref[...], preferred_element_type=F32)
                + jnp.dot(ltri_ref[...], rowsum(mf).astype(BF16), preferred_element_type=F32))

    def ind(m):
        return jnp.where(m, 1.0, 0.0).astype(BF16)

    def search(i, thr_bits):
        cand = thr_bits | jnp.left_shift(jnp.int32(1), 30 - i)
        enough = expert_total(ind(aff >= lax.bitcast_convert_type(cand, F32))) >= cap
        return jnp.where(enough, cand, thr_bits)

    thr = lax.bitcast_convert_type(lax.fori_loop(0, 31, search, jnp.zeros(aff.shape, jnp.int32)), F32)
    gt = aff > thr
    eq = aff == thr
    room = cap - expert_total(ind(gt))
    take_eq = jnp.where(eq, jnp.where(prefix(ind(eq)) < room, 1.0, 0.0), 0.0)
    sel = jnp.where(gt, 1.0, take_eq)
    pos = prefix(sel.astype(BF16)).astype(jnp.int32)
    posall_ref[0] = pos
    pos_ref[0] = jnp.where(sel > 0.0, pos, -1)
    g_ref[0] = jnp.where(sel > 0.0, aff, 0.0)


def _select(aff2, cap):
    B, R, _ = aff2.shape
    rpe = R // N_EXPERTS
    li = jnp.arange(LANES, dtype=jnp.int32)
    ri = jnp.arange(R, dtype=jnp.int32)
    same = (ri[:, None] // rpe) == (ri[None, :] // rpe)
    ones = jnp.ones((LANES, LANES), BF16)
    utri = (li[:, None] < li[None, :]).astype(BF16)
    ltri = (same & (ri[None, :] < ri[:, None])).astype(BF16)
    blk = pl.BlockSpec((1, R, LANES), lambda b: (b, 0, 0))
    full = lambda arr: pl.BlockSpec(arr.shape, lambda b: (0, 0))
    return pl.pallas_call(
        functools.partial(_select_kernel, cap=cap),
        grid=(B,),
        in_specs=[blk, full(ones), full(utri), full(ltri)],
        out_specs=[blk, blk, blk],
        out_shape=[jax.ShapeDtypeStruct((B, R, LANES), jnp.int32), jax.ShapeDtypeStruct((B, R, LANES), jnp.int32),
                   jax.ShapeDtypeStruct((B, R, LANES), F32)],
        compiler_params=_params("parallel"),
        name="expert_choice_select",
    )(aff2, ones, utri, ltri)


GATHER_TOK = 1024
GATHER_WIN = 160


def _gather_kernel(offs_ref, u_ref, pos_ref, xe_ref, acc, *, cap, nblk):
    b, e = pl.program_id(0), pl.program_id(1)
    base = (b * N_EXPERTS + e) * (nblk + 1)
    rows = GATHER_TOK // LANES
    acc[...] = jnp.zeros(acc.shape, F32)
    slot = lax.broadcasted_iota(jnp.int32, (GATHER_WIN, LANES), 0)

    def block(j, first_window):
        off, end = offs_ref[base + j], offs_ref[base + j + 1]
        w = (off // SUBLANES) * SUBLANES

        def window(i, c):
            s0 = pl.multiple_of(w + i * GATHER_WIN, SUBLANES)
            prow = pos_ref[0, pl.ds(pl.multiple_of(j * rows, rows), rows), :]
            ublk = u_ref[0, pl.ds(pl.multiple_of(j * GATHER_TOK, GATHER_TOK), GATHER_TOK), :]
            onehot = jnp.concatenate(
                [jnp.where(slot == prow[r:r + 1, :] - s0, 1.0, 0.0).astype(BF16) for r in range(rows)], axis=1)
            acc[pl.ds(s0, GATHER_WIN), :] += jnp.dot(onehot, ublk, preferred_element_type=F32)
            return c

        if first_window:
            window(0, 0)
        else:
            lax.fori_loop(1, (end - w + GATHER_WIN - 1) // GATHER_WIN, window, 0)
        return 0

    lax.fori_loop(0, nblk, lambda j, c: block(j, True), 0, unroll=4)
    lax.fori_loop(0, nblk, lambda j, c: block(j, False), 0)
    xe_ref[0] = acc[:cap, :].astype(BF16)


def _gather(offs, u2, pos_sel, cap):
    B, S, D = u2.shape
    rpe = S // LANES
    nblk = S // GATHER_TOK
    return pl.pallas_call(
        functools.partial(_gather_kernel, cap=cap, nblk=nblk),
        grid_spec=pltpu.PrefetchScalarGridSpec(
            num_scalar_prefetch=1,
            grid=(B, N_EXPERTS),
            in_specs=[pl.BlockSpec((1, S, D), lambda b, e, o: (b, 0, 0)),
                      pl.BlockSpec((1, rpe, LANES), lambda b, e, o: (b, e, 0))],
            out_specs=pl.BlockSpec((1, cap, D), lambda b, e, o: (e, b, 0)),
            scratch_shapes=[pltpu.VMEM((cap + GATHER_WIN + SUBLANES, D), F32)],
        ),
        out_shape=jax.ShapeDtypeStruct((N_EXPERTS, B * cap, D), BF16),
        compiler_params=_params("parallel", "parallel"),
        name="expert_gather",
    )(offs, u2, pos_sel)


def _moe_kernel(x_ref, wg_ref, wu_ref, wd_ref, o_ref, wgb, wub, wdb):
    @pl.when(pl.program_id(1) == 0)
    def _():
        wgb[...] = wg_ref[0].astype(BF16)
        wub[...] = wu_ref[0].astype(BF16)
        wdb[...] = wd_ref[0].astype(BF16)

    x = x_ref[0]
    hg = jnp.dot(x, wgb[...], preferred_element_type=F32)
    hu = jnp.dot(x, wub[...], preferred_element_type=F32)
    h = (hg * jax.nn.sigmoid(hg) * hu).astype(BF16)
    o_ref[0] = jnp.dot(h, wdb[...], preferred_element_type=F32).astype(BF16)


def _moe_ffn(xe, w_gate, w_up, w_down):
    E, R, D = xe.shape
    Fh = w_gate.shape[2]
    tr = MOE_ROWS
    return pl.pallas_call(
        _moe_kernel,
        grid=(E, R // tr),
        in_specs=[
            pl.BlockSpec((1, tr, D), lambda e, r: (e, r, 0)),
            pl.BlockSpec((1, D, Fh), lambda e, r: (e, 0, 0)),
            pl.BlockSpec((1, D, Fh), lambda e, r: (e, 0, 0)),
            pl.BlockSpec((1, Fh, D), lambda e, r: (e, 0, 0)),
        ],
        out_specs=pl.BlockSpec((1, tr, D), lambda e, r: (e, r, 0)),
        out_shape=jax.ShapeDtypeStruct((E, R, D), BF16),
        scratch_shapes=[pltpu.VMEM((D, Fh), BF16), pltpu.VMEM((D, Fh), BF16), pltpu.VMEM((Fh, D), BF16)],
        compiler_params=_params("parallel", "arbitrary"),
        name="moe_swiglu",
    )(xe, w_gate, w_up, w_down)


CMB_TOK = 512
CMB_WIN = 128
BF16_ROWS = 16


def _combine_kernel(offs_ref, x1_ref, pos_ref, g_ref, g2_ref, lg_ref, lb_ref, ye_hbm, o_ref,
                    rhs, lhs, acc, sem, *, cap, nblk, nbatch):
    b, i = pl.program_id(0), pl.program_id(1)
    W = CMB_WIN
    total = ye_hbm.shape[0]
    step = b * nblk + i
    cur_slot = step % 2
    row_e = lax.broadcasted_iota(jnp.int32, (N_EXPERTS, 1), 0)
    slot_w = lax.broadcasted_iota(jnp.int32, (W, CMB_TOK), 0)

    def windows(bb, ii):
        out = []
        for e in range(N_EXPERTS):
            k = (bb * N_EXPERTS + e) * (nblk + 1) + ii
            off, end = offs_ref[k], offs_ref[k + 1]
            lo = (off // BF16_ROWS) * BF16_ROWS
            out.append(((e * nbatch + bb) * cap, lo, jnp.where(end > off, (end - lo + W - 1) // W, 0)))
        return out

    def fetch_starts(win, p):
        return [jnp.minimum(rb + lo + p * W, total - W) for rb, lo, _ in win]

    def window_copy(e, start, s):
        return pltpu.make_async_copy(ye_hbm.at[pl.ds(pl.multiple_of(start, BF16_ROWS), W), :],
                                     rhs.at[s, pl.ds(e * W, W), :], sem.at[s, e])

    def start_fetch(win, p, s):
        for e, start in enumerate(fetch_starts(win, p)):
            window_copy(e, start, s).start()

    def absorb(win, p, s, first):
        starts = fetch_starts(win, p)
        lo_col = jnp.zeros((N_EXPERTS, 1), jnp.int32)
        shift_col = jnp.zeros((N_EXPERTS, 1), jnp.int32)
        for e, (rb, lo0, _) in enumerate(win):
            lo = lo0 + p * W
            lo_col = jnp.where(row_e == e, lo, lo_col)
            shift_col = jnp.where(row_e == e, rb + lo - starts[e], shift_col)
        d = pos_ref[0] - lo_col
        rel = jnp.where((d >= 0) & (d < W), d + shift_col, -1)
        g = g_ref[0]
        for e in range(N_EXPERTS):
            lhs[e * W:(e + 1) * W, :] = jnp.where(slot_w == rel[e:e + 1, :], g[e:e + 1, :], 0.0).astype(BF16)
        for e in range(N_EXPERTS):
            window_copy(e, starts[e], s).wait()
        part = lax.dot_general(lhs[...], rhs[s], (((0,), (0,)), ((), ())), preferred_element_type=F32)
        acc[...] = part if first else acc[...] + part

    cur = windows(b, i)

    @pl.when(step == 0)
    def _():
        start_fetch(cur, 0, 0)

    @pl.when(step + 1 < nbatch * nblk)
    def _():
        wrap = i + 1 == nblk
        start_fetch(windows(jnp.where(wrap, b + 1, b), jnp.where(wrap, 0, i + 1)), 0, 1 - cur_slot)

    absorb(cur, 0, cur_slot, True)

    def extra_pass(p, carry):
        start_fetch(cur, p, cur_slot)
        absorb(cur, p, cur_slot, False)
        return carry

    npass = functools.reduce(jnp.maximum, [n for _, _, n in cur])
    lax.fori_loop(1, npass, extra_pass, 0)
    o_ref[0] = _layer_norm(DN_ALPHA * x1_ref[0] + g2_ref[0] * acc[...], lg_ref[...], lb_ref[...])


def _combine_ln2(offs, x1, pos_exp, g_exp, g2, lg, lb, ye_rows, cap):
    B, S, D = x1.shape
    T = CMB_TOK
    nblk = S // T
    tok = lambda w: pl.BlockSpec((1, T, w), lambda b, i, o: (b, i, 0))
    vec = pl.BlockSpec((1, D), lambda b, i, o: (0, 0))
    exp_major = pl.BlockSpec((1, N_EXPERTS, T), lambda b, i, o: (b, 0, i))
    return pl.pallas_call(
        functools.partial(_combine_kernel, cap=cap, nblk=nblk, nbatch=B),
        grid_spec=pltpu.PrefetchScalarGridSpec(
            num_scalar_prefetch=1,
            grid=(B, nblk),
            in_specs=[tok(D), exp_major, exp_major,
                      pl.BlockSpec((1, 1, D), lambda b, i, o: (b, 0, 0)), vec, vec,
                      pl.BlockSpec(memory_space=pl.ANY)],
            out_specs=tok(D),
            scratch_shapes=[pltpu.VMEM((2, N_EXPERTS * CMB_WIN, D), BF16),
                            pltpu.VMEM((N_EXPERTS * CMB_WIN, T), BF16),
                            pltpu.VMEM((T, D), F32),
                            pltpu.SemaphoreType.DMA((2, N_EXPERTS))],
        ),
        out_shape=jax.ShapeDtypeStruct((B, S, D), F32),
        compiler_params=_params("arbitrary", "arbitrary"),
        name="combine_residual_ln2",
    )(offs, x1, pos_exp, g_exp, g2, lg, lb, ye_rows)


DFT_R = 128
K1_HALF = 40
K1_VALID = DFT_R // 2 + 1
K1_GROUPS = (K1_HALF // SUBLANES, -(-(K1_VALID - K1_HALF) // SUBLANES))
K1_SPEC = K1_HALF + K1_GROUPS[1] * SUBLANES
HY_CT = 128


@functools.lru_cache(maxsize=None)
def _dft_tables_np(S, two_sided):
    N = 2 * S
    half = np.arange(S // DFT_R)
    a_idx = np.concatenate([half, N // DFT_R - 1 - half]) if two_sided else half
    r = np.arange(2 * K1_HALF)
    k1 = np.arange(2)[:, None] * K1_HALF + r[None, :] % K1_HALF
    valid = (k1 < K1_VALID).astype(np.float64)
    n = DFT_R * a_idx[None, :] + np.arange(DFT_R)[:, None]
    theta = ((n[None, :, None, :] * k1[:, None, :, None]) % N) * (2.0 * math.pi / N)
    is_im = (r >= K1_HALF)[None, None, :, None]
    g = np.where(is_im, -np.sin(theta), np.cos(theta)) * valid[:, None, :, None]
    cw = np.where((k1 == 0) | (k1 == DFT_R // 2), 1.0, 2.0) * valid / N
    gi = np.transpose(g * cw[:, None, :, None], (0, 1, 3, 2))
    ph = ((np.arange(DFT_R)[:, None] * np.arange(DFT_R)[None, :]) % DFT_R) * (2.0 * math.pi / DFT_R)
    C, Sn = np.cos(ph), np.sin(ph)
    f3f = np.block([[C, Sn], [-Sn, C]])
    f3i = np.block([[C, -Sn], [Sn, C]])

    def pair_lanes(t):
        h, nb, nr, nc = t.shape
        return t.reshape(h, nb // 2, 2, nr, nc).transpose(0, 1, 3, 2, 4).reshape(h, nb // 2, nr, 2 * nc)

    return tuple(np.ascontiguousarray(t, dtype=np.float32) for t in (pair_lanes(g), pair_lanes(gi), f3f, f3i))


def _dft_tables(S, two_sided=False):
    return tuple(jnp.asarray(t).astype(BF16) for t in _dft_tables_np(S, two_sided))


def _swap01(x):
    return jnp.swapaxes(x, 0, 1)


def _block_diag_lanes(top, bot):
    return jnp.concatenate([jnp.concatenate([top, jnp.zeros((top.shape[0], bot.shape[1]), top.dtype)], axis=1),
                            jnp.concatenate([jnp.zeros((bot.shape[0], top.shape[1]), bot.dtype), bot], axis=1)], axis=0)


def _dft_stage1(load, gf_ref, hf, a_scr):
    ct = a_scr.shape[2]

    def body(g, carry):
        b0 = pl.multiple_of(g * SUBLANES, SUBLANES)
        zt = load(b0)
        for p in range(SUBLANES // 2):
            a2 = jnp.dot(gf_ref[hf, g * (SUBLANES // 2) + p], _block_diag_lanes(zt[2 * p], zt[2 * p + 1]),
                         preferred_element_type=F32)
            a_scr[b0 + 2 * p] = a2[:, :ct]
            a_scr[b0 + 2 * p + 1] = a2[:, ct:]
        return carry

    lax.fori_loop(0, DFT_R // SUBLANES, body, 0, unroll=4)


def _dft_stage3_operands(a_scr, r0):
    mr = _swap01(a_scr[:, pl.ds(r0, SUBLANES), :])
    mi = _swap01(a_scr[:, pl.ds(K1_HALF + r0, SUBLANES), :])
    return [jnp.concatenate([jnp.concatenate([mr[j], mi[j]], axis=0),
                             jnp.concatenate([mr[j + 1], mi[j + 1]], axis=0)], axis=1).astype(BF16)
            for j in range(0, SUBLANES, 2)]


def _filter_mlp_kernel(feat_ref, featx_ref, w1_ref, b1_ref, fr_ref, w2_ref, b2_ref, w3_ref, b3_ref, dec_ref,
                       of_ref, ob_ref):
    tl = feat_ref.shape[0]
    ncol = of_ref.shape[1]
    fr = fr_ref[...]

    def hidden(f):
        h = jnp.sin(fr * (jnp.dot(f, w1_ref[...], precision=HIGHEST, preferred_element_type=F32) + b1_ref[...]))
        return jnp.sin(fr * (jnp.dot(h, w2_ref[...], precision=HIGHEST, preferred_element_type=F32) + b2_ref[...]))

    def taps(h2, t, lo):
        out = _dot3(h2, w3_ref[:, lo:lo + ncol]) + b3_ref[:, lo:lo + ncol]
        return out * jnp.exp(-t * jnp.abs(dec_ref[:, lo:lo + ncol]))

    feats, featx = feat_ref[...], featx_ref[...]
    h2, h2x = hidden(feats), hidden(featx)
    of_ref[...] = taps(h2, feats[:, 0:1], 0)
    row = lax.broadcasted_iota(jnp.int32, (tl, 1), 0)
    last = row == tl - 1
    h2n = jnp.where(last, h2x[0:1, :], pltpu.roll(h2, tl - 1, 0))
    tn = jnp.where(last, featx[0:1, :], pltpu.roll(feats, tl - 1, 0))[:, 0:1]
    ob_ref[...] = jnp.where(last & (pl.program_id(0) == pl.num_programs(0) - 1), 0.0, taps(h2n, tn, ncol))


@functools.lru_cache(maxsize=None)
def _filter_features_np(L, nf):
    pos = np.arange(L + SUBLANES, dtype=np.float64)
    t = pos[:, None] / (L - 1)
    bands = (nf - 1) // 2
    freqs = np.linspace(1e-4, bands - 1, bands)
    phase = (2.0 * math.pi / L) * pos[:, None] * freqs[None, :]
    feats = np.concatenate([t, np.cos(phase), -np.sin(phase), np.zeros((L + SUBLANES, LANES - nf))], axis=-1)
    return np.ascontiguousarray(feats, dtype=np.float32)


def _filter_taps(L, w1, b1, freq, w2, b2, w3, b3, decay):
    nf = w1.shape[0]
    feats = jnp.asarray(_filter_features_np(L, nf))
    w1p = jnp.concatenate([w1, jnp.zeros((LANES - nf, w1.shape[1]), F32)], axis=0)
    fo = w1.shape[1]
    ncol = w3.shape[1] // 2
    tl = 512
    full = lambda shape: pl.BlockSpec(shape, lambda i: (0,) * len(shape))
    out = jax.ShapeDtypeStruct((L, ncol), F32)
    blk = pl.BlockSpec((tl, ncol), lambda i: (i, 0))
    return pl.pallas_call(
        _filter_mlp_kernel,
        grid=(L // tl,),
        in_specs=[pl.BlockSpec((tl, LANES), lambda i: (i, 0)),
                  pl.BlockSpec((SUBLANES, LANES), lambda i: ((i + 1) * (tl // SUBLANES), 0)),
                  full((LANES, fo)), full((1, fo)), full((1, fo)), full((fo, fo)), full((1, fo)),
                  full((fo, 2 * ncol)), full((1, 2 * ncol)), full((1, 2 * ncol))],
        out_specs=[blk, blk],
        out_shape=[out, out],
        compiler_params=_params("parallel"),
        name="hyena_filter_mlp",
    )(feats, feats, w1p, b1[None, :], freq[None, :], w2, b2[None, :], w3, b3[None, :], decay.reshape(1, 2 * ncol))


def _filter_spectrum_kernel(hf_ref, hb_ref, gf_ref, f3f_ref, o_ref, a_scr):
    ct = a_scr.shape[2]

    def load(b0):
        zf = _swap01(hf_ref[:, pl.ds(b0, SUBLANES), :])
        zb = _swap01(hb_ref[:, pl.ds(pl.multiple_of(DFT_R - SUBLANES - b0, SUBLANES), SUBLANES), :])
        return [jnp.concatenate([zf[j], zb[SUBLANES - 1 - j]], axis=0).astype(BF16) for j in range(SUBLANES)]

    for hf in range(2):
        _dft_stage1(load, gf_ref, hf, a_scr)

        def body(g, carry):
            r0 = pl.multiple_of(g * SUBLANES, SUBLANES)
            for p, m in enumerate(_dft_stage3_operands(a_scr, r0)):
                x2 = jnp.dot(f3f_ref[...], m, preferred_element_type=F32)
                for q in range(2):
                    o_ref[0, hf * K1_HALF + r0 + 2 * p + q] = x2[:, q * ct:(q + 1) * ct].astype(BF16)
            return carry

        lax.fori_loop(0, K1_GROUPS[hf], body, 0, unroll=True)


def _filter_spectrum(fwd3, bwd3, gf, f3f):
    na, nb, ncol = fwd3.shape
    C = ncol // 2
    nct = C // HY_CT
    ct = HY_CT
    taps = pl.BlockSpec((na, nb, ct), lambda o, j: (0, 0, o * nct + j))
    return pl.pallas_call(
        _filter_spectrum_kernel,
        grid=(2, nct),
        in_specs=[
            taps, taps,
            pl.BlockSpec(gf.shape, lambda o, j: (0, 0, 0, 0), pipeline_mode=pl.Buffered(1)),
            pl.BlockSpec(f3f.shape, lambda o, j: (0, 0), pipeline_mode=pl.Buffered(1)),
        ],
        out_specs=pl.BlockSpec((1, K1_SPEC, 2 * DFT_R, ct), lambda o, j: (o, 0, 0, j)),
        out_shape=jax.ShapeDtypeStruct((2, K1_SPEC, 2 * DFT_R, C), BF16),
        scratch_shapes=[pltpu.VMEM((DFT_R, 2 * K1_HALF, ct), F32)],
        compiler_params=_params("parallel", "parallel"),
        name="hyena_filter_spectrum",
    )(fwd3, bwd3, gf, f3f)


def _long_conv_kernel(z_ref, xg_ref, h_ref, bias_ref, gf_ref, f3f_ref, f3i_ref, gi_ref, o_ref,
                      a_scr, b_scr, zt_scr, yt_scr):
    ct = a_scr.shape[2]
    bias = bias_ref[0]

    def load_first(b0):
        zt = _swap01(z_ref[0, :, pl.ds(b0, SUBLANES), :]).astype(BF16)
        for j in range(SUBLANES):
            zt_scr[b0 + j] = zt[j]
        return zt

    for hf in range(2):
        _dft_stage1(load_first if hf == 0 else (lambda b0: [zt_scr[b0 + j] for j in range(SUBLANES)]),
                    gf_ref, hf, a_scr)
        ngrp = K1_GROUPS[hf]
        if ngrp * SUBLANES < K1_HALF:
            b_scr[ngrp * SUBLANES:] = jnp.zeros((K1_HALF - ngrp * SUBLANES,) + b_scr.shape[1:], F32)

        def spectral(g, carry):
            r0 = pl.multiple_of(g * SUBLANES, SUBLANES)
            for p, m in enumerate(_dft_stage3_operands(a_scr, r0)):
                x = jnp.dot(f3f_ref[...], m, preferred_element_type=F32)
                k1 = hf * K1_HALF + r0 + 2 * p
                h = jnp.concatenate([h_ref[0, k1], h_ref[0, k1 + 1]], axis=1).astype(F32)
                xr, xi, hr, hi = x[:DFT_R], x[DFT_R:], h[:DFT_R], h[DFT_R:]
                y = jnp.concatenate([xr * hr - xi * hi, xr * hi + xi * hr], axis=0).astype(BF16)
                bb = jnp.dot(f3i_ref[...], y, preferred_element_type=F32)
                b_scr[r0 + 2 * p] = bb[:, :ct]
                b_scr[r0 + 2 * p + 1] = bb[:, ct:]
            return carry

        lax.fori_loop(0, ngrp, spectral, 0, unroll=True)

        def inverse(g, carry):
            b0 = pl.multiple_of(g * SUBLANES, SUBLANES)
            br = _swap01(b_scr[:, pl.ds(b0, SUBLANES), :])
            bi = _swap01(b_scr[:, pl.ds(DFT_R + b0, SUBLANES), :])
            ys = []
            for p in range(SUBLANES // 2):
                bm = [jnp.concatenate([br[2 * p + q], bi[2 * p + q]], axis=0).astype(BF16) for q in range(2)]
                y2 = jnp.dot(gi_ref[hf, g * (SUBLANES // 2) + p], _block_diag_lanes(bm[0], bm[1]),
                             preferred_element_type=F32)
                ys += [y2[:, :ct], y2[:, ct:]]
            if hf == 0:
                for j in range(SUBLANES):
                    yt_scr[b0 + j] = ys[j]
            else:
                y = _swap01(jnp.stack([yt_scr[b0 + j] + ys[j] for j in range(SUBLANES)], axis=0))
                sl = (0, slice(None), pl.ds(b0, SUBLANES), slice(None))
                o_ref[sl] = xg_ref[sl] * (y + z_ref[sl] * bias)
            return carry

        lax.fori_loop(0, DFT_R // SUBLANES, inverse, 0, unroll=4)


def _long_conv(z_arr, z_off, xg_arr, xg_off, spec, order, bias3, gf, f3f, f3i, gi):
    B, na, nb, _ = z_arr.shape
    C = spec.shape[3]
    ct = HY_CT
    const = lambda arr: pl.BlockSpec(arr.shape, lambda j, b: (0,) * arr.ndim, pipeline_mode=pl.Buffered(1))
    return pl.pallas_call(
        _long_conv_kernel,
        grid=(C // ct, B),
        in_specs=[
            pl.BlockSpec((1, na, nb, ct), lambda j, b: (b, 0, 0, z_off + j)),
            pl.BlockSpec((1, na, nb, ct), lambda j, b: (b, 0, 0, xg_off + j)),
            pl.BlockSpec((1, K1_SPEC, 2 * DFT_R, ct), lambda j, b: (order, 0, 0, j), pipeline_mode=pl.Buffered(1)),
            pl.BlockSpec((1, 1, ct), lambda j, b: (order, 0, j)),
            const(gf), const(f3f), const(f3i), const(gi),
        ],
        out_specs=pl.BlockSpec((1, na, nb, ct), lambda j, b: (b, 0, 0, j)),
        out_shape=jax.ShapeDtypeStruct((B, na, nb, C), F32),
        scratch_shapes=[pltpu.VMEM((DFT_R, 2 * K1_HALF, ct), F32), pltpu.VMEM((K1_HALF, 2 * DFT_R, ct), F32),
                        pltpu.VMEM((nb, na, ct), BF16), pltpu.VMEM((nb, na, ct), F32)],
        compiler_params=_params("parallel", "parallel"),
        name="hyena_long_conv",
    )(z_arr, xg_arr, spec, bias3, gf, f3f, f3i, gi)


def _hyena_group(hy_conv, fw1, fb1, ffreq, fw2, fb2, fw3, fb3, decay, hyena_bias):
    B, S, _ = hy_conv.shape
    na = S // DFT_R
    gf, gi, f3f, f3i = _dft_tables(S)
    fwd, bwd = _filter_taps(S, fw1, fb1, ffreq, fw2, fb2, fw3, fb3, decay)
    spec = _filter_spectrum(fwd.reshape(na, DFT_R, -1), bwd.reshape(na, DFT_R, -1),
                            _dft_tables(S, two_sided=True)[0], f3f)
    u = hy_conv.reshape(B, na, DFT_R, -1)
    nct = HY_W // HY_CT
    bias3 = hyena_bias[:, None, :]
    z1 = _long_conv(u, 0, u, nct, spec, 0, bias3, gf, f3f, f3i, gi)
    return _long_conv(z1, 0, u, 2 * nct, spec, 1, bias3, gf, f3f, f3i, gi).reshape(B, S, HY_W)


def _rope_rot_cols(w):
    half = ROPE // 2
    return jnp.concatenate([-w[..., half:], w[..., :half]], axis=-1)


def _prep_weights(w_in, w_qb, w_kvb):
    D = w_in.shape[0]
    o = Q_RANK + KV_RANK
    w_kr = w_in[:, o:o + ROPE]
    z64 = jnp.zeros((D, NOPE), F32)
    z32 = jnp.zeros((D, LANES - NOPE - ROPE), F32)
    win_aug = jnp.concatenate([
        w_in[:, :o],
        z64, w_kr, z32,
        z64, _rope_rot_cols(w_kr), z32,
        w_in[:, o + ROPE:],
    ], axis=1).astype(BF16)
    wq = w_qb.reshape(Q_RANK, MLA_HEADS, NOPE + ROPE)
    zq = jnp.zeros((Q_RANK, MLA_HEADS, LANES - NOPE - ROPE), F32)
    wq_main = jnp.concatenate([wq, zq], axis=-1).reshape(Q_RANK, MLA_HEADS * LANES).astype(BF16)
    wq_rot = jnp.concatenate([jnp.zeros((Q_RANK, MLA_HEADS, NOPE), F32), _rope_rot_cols(wq[..., NOPE:]), zq],
                             axis=-1).reshape(Q_RANK, MLA_HEADS * LANES).astype(BF16)
    wkv = w_kvb.reshape(KV_RANK, MLA_HEADS, NOPE + V_DIM)
    wk = jnp.concatenate([wkv[..., :NOPE], jnp.zeros((KV_RANK, MLA_HEADS, LANES - NOPE), F32)],
                         axis=-1).reshape(KV_RANK, MLA_HEADS * LANES).astype(BF16)
    wv = wkv[..., NOPE:].reshape(KV_RANK, MLA_HEADS * V_DIM).astype(BF16)
    return win_aug, wq_main, wq_rot, wk, wv


def kernel(x, c, positions, w_ada, b_ada, w_in, q_norm_g, w_qb, kv_norm_g, w_kvb, conv_w, conv_b, filt_w1, filt_b1, filt_freq, filt_w2, filt_b2, filt_w3, filt_b3, hyena_decay, hyena_bias, attn_out_g, hyena_out_g, w_o, ln1_g, ln1_b, w_router, w_gate, w_up, w_down, ln2_g, ln2_b):
    B, S, D = x.shape
    l = 0
    mod = _ada_mod(c, w_ada[l], b_ada[l][None, :])
    sh1, sc1, g1, sh2, sc2, g2 = [m[:, None, :] for m in jnp.split(mod, 6, axis=-1)]

    win_aug, wq_main, wq_rot, wk, wv = _prep_weights(w_in[l], w_qb[l], w_kvb[l])
    cos_t, sin_t = _rope_tables(positions)
    q, kt, v, hy_conv = _inproj(x, sc1, sh1, cos_t, sin_t, win_aug, q_norm_g[l][None, :], wq_main, wq_rot,
                                kv_norm_g[l][None, :], wk, wv, conv_w[l], conv_b[l])
    a = _attention(q, kt, v)

    hy = _hyena_group(hy_conv, filt_w1[l], filt_b1[l], filt_freq[l], filt_w2[l],
                      filt_b2[l], filt_w3[l], filt_b3[l], hyena_decay[l], hyena_bias[l])

    wr_pad = jnp.concatenate([w_router[l], jnp.zeros((D, LANES - N_EXPERTS), F32)], axis=1)
    x1, u2, aff_t = _outproj(a, hy, x, g1, sc2, sh2, attn_out_g[l][None, :], hyena_out_g[l][None, :],
                             w_o[l].astype(BF16), ln1_g[l][None, :], ln1_b[l][None, :], wr_pad)

    cap = EC_FACTOR * S // N_EXPERTS
    rpe = S // LANES
    aff2 = aff_t.reshape(B, N_EXPERTS * rpe, LANES)
    pos_sel, pos_all, gsel = _select(aff2, cap)
    row_start = pos_all.reshape(B, N_EXPERTS, rpe, LANES)[:, :, :, 0]

    def block_offsets(tokens):
        ends = jnp.full((B, N_EXPERTS, 1), cap, jnp.int32)
        return jnp.concatenate([row_start[:, :, ::tokens // LANES], ends], axis=-1).reshape(-1)

    xe = _gather(block_offsets(GATHER_TOK), u2, pos_sel, cap)
    ye = _moe_ffn(xe, w_gate[l], w_up[l], w_down[l])
    return _combine_ln2(block_offsets(CMB_TOK), x1, pos_sel.reshape(B, N_EXPERTS, S), gsel.reshape(B, N_EXPERTS, S), g2, ln2_g[l][None, :], ln2_b[l][None, :],
                        ye.reshape(N_EXPERTS * B * cap, D), cap)
```
